```python
import jax, jax.numpy as jnp
from jax import lax
import numpy as np

D_MODEL = 2048
BATCH = 2
SEQ = 8192
DEPTH = 2

CONV_DIM = 512
CONV_WIDTH = 31
FOX_HEADS = 8
FOX_HEAD_DIM = 128
FOX_DIM = FOX_HEADS * FOX_HEAD_DIM
FOX_BLOCK = 128
GLA_HEADS = 4
GLA_DK = 64
GLA_DV = 128
GLA_KDIM = GLA_HEADS * GLA_DK
GLA_VDIM = GLA_HEADS * GLA_DV
GLA_RANK = 16
GLA_TAU = 16.0
GLA_CHUNK = 64
N_BRANCH = 3
N_EXPERTS = 64
N_GROUPS = 8
TOPK_GROUPS = 4
TOP_K = 8
EXPERT_DIM = 512
SHARED_DIM = 512
ROUTED_SCALE = 2.5
MOE_BLOCK = 256
EPS = 1e-6
FORGET_BIAS_INIT = 3.0

COL_SIZES = (2 * CONV_DIM,
             FOX_DIM, FOX_DIM, FOX_DIM,
             FOX_HEADS,
             GLA_KDIM, GLA_KDIM, GLA_VDIM,
             GLA_VDIM,
             GLA_RANK,
             N_BRANCH * D_MODEL)
N_COLS = sum(COL_SIZES)
FOX_F_OFFSET = 2 * CONV_DIM + 3 * FOX_DIM
COL_SPLITS = [int(v) for v in np.cumsum(COL_SIZES)[:-1]]

kernel_name = 'hybrid_conv_fox_gla_moe_adaln'


def rms_norm(x, g):
    xf = x.astype(jnp.float32)
    y = xf * lax.rsqrt(jnp.mean(xf * xf, axis=-1, keepdims=True) + EPS) * g
    return y.astype(x.dtype)


def layer_norm(x, g, b):
    xf = x.astype(jnp.float32)
    mu = jnp.mean(xf, axis=-1, keepdims=True)
    var = jnp.mean(jnp.square(xf - mu), axis=-1, keepdims=True)
    return ((xf - mu) * lax.rsqrt(var + EPS) * g + b).astype(x.dtype)


def conv_branch(u, conv_w, conv_b, ln_g, ln_b):
    a, g = jnp.split(u, 2, axis=-1)
    y = a * jax.nn.sigmoid(g)
    y = lax.conv_general_dilated(y, conv_w, window_strides=(1,), padding=[(CONV_WIDTH - 1, 0)],
                                 dimension_numbers=('NWC', 'WIO', 'NWC'),
                                 feature_group_count=CONV_DIM) + conv_b
    y = layer_norm(y, ln_g, ln_b)
    return jax.nn.silu(y)


def fox_attention(q, k, v, f_logit):
    B, S, _ = q.shape
    nb = S // FOX_BLOCK

    def heads(t):
        return t.reshape(B, S, FOX_HEADS, FOX_HEAD_DIM).transpose(0, 2, 1, 3)

    q, k, v = heads(q), heads(k), heads(v)
    log_f = jax.nn.log_sigmoid(f_logit.astype(jnp.float32)).transpose(0, 2, 1)
    cum = jnp.cumsum(log_f, axis=-1)
    q_blocks = q.reshape(B, FOX_HEADS, nb, FOX_BLOCK, FOX_HEAD_DIM).transpose(2, 0, 1, 3, 4)
    c_blocks = cum.reshape(B, FOX_HEADS, nb, FOX_BLOCK).transpose(2, 0, 1, 3)
    kpos = jnp.arange(S)
    scale = FOX_HEAD_DIM ** -0.5

    def block(args):
        i, qi, ci = args
        qpos = i * FOX_BLOCK + jnp.arange(FOX_BLOCK)
        s = jnp.einsum('bhqd,bhkd->bhqk', qi, k, preferred_element_type=jnp.float32) * scale
        s = s + ci[..., :, None] - cum[..., None, :]
        s = jnp.where(kpos[None, :] <= qpos[:, None], s, -jnp.inf)
        p = jax.nn.softmax(s, axis=-1)
        return jnp.einsum('bhqk,bhkd->bhqd', p.astype(v.dtype), v)

    o = lax.map(block, (jnp.arange(nb), q_blocks, c_blocks))
    return o.transpose(1, 0, 3, 2, 4).reshape(B, S, FOX_DIM)


def gla_attention(q, k, v, r, a_lr, wa, ba, norm_g):
    B, S, _ = q.shape
    n = S // GLA_CHUNK
    out_dtype = v.dtype
    log_a = jax.nn.log_sigmoid((a_lr @ wa + ba).astype(jnp.float32)) / GLA_TAU

    def chunks(t, d):
        t = t.astype(jnp.float32).reshape(B, n, GLA_CHUNK, GLA_HEADS, d)
        return t.transpose(1, 0, 3, 2, 4)

    qc = chunks(q * GLA_DK ** -0.5, GLA_DK)
    kc, vc, ac = chunks(k, GLA_DK), chunks(v, GLA_DV), chunks(log_a, GLA_DK)
    mask = jnp.tril(jnp.ones((GLA_CHUNK, GLA_CHUNK), bool))[None, None, :, :, None]

    def step(state, xs):
        qi, ki, vi, ai = xs
        b = jnp.cumsum(ai, axis=2)
        diff = b[:, :, :, None, :] - b[:, :, None, :, :]
        decay = jnp.exp(jnp.where(mask, diff, -jnp.inf))
        attn = jnp.einsum('bhtd,bhsd,bhtsd->bhts', qi, ki, decay)
        o = jnp.einsum('bhts,bhsv->bhtv', attn, vi) + \
            jnp.einsum('bhtd,bhdv->bhtv', qi * jnp.exp(b), state)
        b_last = b[:, :, -1, :]
        k_dec = ki * jnp.exp(b_last[:, :, None, :] - b)
        state = state * jnp.exp(b_last)[..., None] + jnp.einsum('bhsd,bhsv->bhdv', k_dec, vi)
        return state, o

    state0 = jnp.zeros((B, GLA_HEADS, GLA_DK, GLA_DV), jnp.float32)
    _, o = lax.scan(step, state0, (qc, kc, vc, ac))
    o = o.transpose(1, 0, 3, 2, 4).reshape(B, S, GLA_HEADS, GLA_DV)
    o = o * lax.rsqrt(jnp.mean(o * o, axis=-1, keepdims=True) + EPS)
    o = o.reshape(B, S, GLA_VDIM) * norm_g
    return (o * jax.nn.silu(r.astype(jnp.float32))).astype(out_dtype)


def hybrid_mixer(h, w_in, b_in, conv_w, conv_b, conv_ln_g, conv_ln_b, gla_wa, gla_ba,
                 gla_norm_g, w_branch_a, w_branch_b, w_branch_c, w_out):
    z = h @ w_in + b_in
    (u_conv, q_f, k_f, v_f, f_logit, q_g, k_g, v_g, r_g, a_lr, gate_logit) = \
        jnp.split(z, COL_SPLITS, axis=-1)
    y_a = conv_branch(u_conv, conv_w, conv_b, conv_ln_g, conv_ln_b) @ w_branch_a
    y_b = fox_attention(q_f, k_f, v_f, f_logit) @ w_branch_b
    y_c = gla_attention(q_g, k_g, v_g, r_g, a_lr, gla_wa, gla_ba, gla_norm_g) @ w_branch_c
    g_a, g_b, g_c = jnp.split(jax.nn.sigmoid(gate_logit), N_BRANCH, axis=-1)
    merged = g_a * y_a + g_b * y_b + g_c * y_c
    return merged @ w_out


def moe_ffn(h, w_router, e_bias, w1, w3, w2, ws1, ws3, ws2):
    B, S, D = h.shape
    T = B * S
    xf = h.reshape(T, D)
    scores = jax.nn.sigmoid((xf @ w_router).astype(jnp.float32))
    biased = scores + e_bias.astype(jnp.float32)
    per_group = N_EXPERTS // N_GROUPS
    grp_score = lax.top_k(biased.reshape(T, N_GROUPS, per_group), 2)[0].sum(-1)
    _, top_g = lax.top_k(grp_score, TOPK_GROUPS)
    gmask = jnp.any(top_g[..., None] == jnp.arange(N_GROUPS), axis=1)
    masked = jnp.where(jnp.repeat(gmask, per_group, axis=-1), biased, -jnp.inf)
    _, top_e = lax.top_k(masked, TOP_K)
    w = jnp.take_along_axis(scores, top_e, axis=-1)
    w = w / jnp.sum(w, axis=-1, keepdims=True) * ROUTED_SCALE

    TK = T * TOP_K
    n_blocks = -(-TK // MOE_BLOCK) + N_EXPERTS
    n_rows = n_blocks * MOE_BLOCK
    eid = top_e.reshape(TK).astype(jnp.int32)
    tok = jnp.repeat(jnp.arange(T, dtype=jnp.int32), TOP_K)
    gw = w.reshape(TK)
    order = jnp.argsort(eid, stable=True)
    s_eid, s_tok, s_gw = eid[order], tok[order], gw[order]
    counts = jnp.zeros((N_EXPERTS,), jnp.int32).at[eid].add(1)
    starts = jnp.cumsum(counts) - counts
    padded = (counts + MOE_BLOCK - 1) // MOE_BLOCK * MOE_BLOCK
    pends = jnp.cumsum(padded)
    pstarts = pends - padded
    dest = pstarts[s_eid] + jnp.arange(TK, dtype=jnp.int32) - starts[s_eid]
    row_tok = jnp.full((n_rows,), T, jnp.int32).at[dest].set(s_tok)
    row_gw = jnp.zeros((n_rows,), jnp.float32).at[dest].set(s_gw)
    block_start = jnp.arange(n_blocks, dtype=jnp.int32) * MOE_BLOCK
    block_e = jnp.minimum(jnp.searchsorted(pends, block_start, side='right'), N_EXPERTS - 1)
    x_pad = jnp.concatenate([xf, jnp.zeros((1, D), xf.dtype)], axis=0)

    def expert_block(acc, xs):
        e, rt, rg = xs
        xb = x_pad[rt]
        hb = jax.nn.silu(xb @ w1[e]) * (xb @ w3[e])
        yb = (hb @ w2[e]) * rg[:, None].astype(xb.dtype)
        return acc.at[rt].add(yb), None

    routed, _ = lax.scan(expert_block, jnp.zeros_like(x_pad),
                         (block_e, row_tok.reshape(n_blocks, MOE_BLOCK),
                          row_gw.reshape(n_blocks, MOE_BLOCK)))
    shared = (jax.nn.silu(xf @ ws1) * (xf @ ws3)) @ ws2
    return (routed[:T] + shared).reshape(B, S, D)


def setup_inputs(seed: int = 0) -> dict:
    key = jax.random.key(seed)
    ks = jax.random.split(key, 32)
    L, D = DEPTH, D_MODEL

    def nrm(k, shape, fan_in, gain=1.0):
        return jax.random.normal(k, shape, jnp.float32) * (gain * fan_in ** -0.5)

    def gain_vec(k, shape):
        return 1.0 + 0.02 * jax.random.normal(k, shape, jnp.float32)

    def bias(k, shape):
        return 0.02 * jax.random.normal(k, shape, jnp.float32)

    b_in = bias(ks[6], (L, N_COLS)).at[:, FOX_F_OFFSET:FOX_F_OFFSET + FOX_HEADS].add(FORGET_BIAS_INIT)
    return {
        'x': jax.random.normal(ks[0], (BATCH, SEQ, D), jnp.float32),
        'c': jax.random.normal(ks[1], (BATCH, D), jnp.float32),
        'norm1_g': gain_vec(ks[2], (L, D)),
        'ada_w': nrm(ks[3], (L, D, 6 * D), D, 0.5),
        'ada_b': bias(ks[4], (L, 6 * D)),
        'w_in': nrm(ks[5], (L, D, N_COLS), D),
        'b_in': b_in,
        'conv_w': nrm(ks[7], (L, CONV_WIDTH, 1, CONV_DIM), CONV_WIDTH),
        'conv_b': bias(ks[8], (L, CONV_DIM)),
        'conv_ln_g': gain_vec(ks[9], (L, CONV_DIM)),
        'conv_ln_b': bias(ks[10], (L, CONV_DIM)),
        'gla_wa': nrm(ks[11], (L, GLA_RANK, GLA_KDIM), GLA_RANK),
        'gla_ba': bias(ks[12], (L, GLA_KDIM)),
        'gla_norm_g': gain_vec(ks[13], (L, GLA_VDIM)),
        'w_branch_a': nrm(ks[14], (L, CONV_DIM, D), CONV_DIM),
        'w_branch_b': nrm(ks[15], (L, FOX_DIM, D), FOX_DIM),
        'w_branch_c': nrm(ks[16], (L, GLA_VDIM, D), GLA_VDIM),
        'w_out': nrm(ks[17], (L, D, D), D),
        'norm2_g': gain_vec(ks[18], (L, D)),
        'w_router': nrm(ks[19], (L, D, N_EXPERTS), D),
        'e_bias': 0.01 * jax.random.normal(ks[20], (L, N_EXPERTS), jnp.float32),
        'w1': nrm(ks[21], (L, N_EXPERTS, D, EXPERT_DIM), D),
        'w3': nrm(ks[22], (L, N_EXPERTS, D, EXPERT_DIM), D),
        'w2': nrm(ks[23], (L, N_EXPERTS, EXPERT_DIM, D), EXPERT_DIM),
        'ws1': nrm(ks[24], (L, D, SHARED_DIM), D),
        'ws3': nrm(ks[25], (L, D, SHARED_DIM), D),
        'ws2': nrm(ks[26], (L, SHARED_DIM, D), SHARED_DIM),
        'final_g': gain_vec(ks[27], (D,)),
    }


def reference(x, c, norm1_g, ada_w, ada_b, w_in, b_in, conv_w, conv_b, conv_ln_g, conv_ln_b,
              gla_wa, gla_ba, gla_norm_g, w_branch_a, w_branch_b, w_branch_c, w_out,
              norm2_g, w_router, e_bias, w1, w3, w2, ws1, ws3, ws2, final_g):
    c_act = jax.nn.silu(c)
    for l in range(DEPTH):
        mod = (c_act @ ada_w[l] + ada_b[l])[:, None, :]
        sh1, sc1, g1, sh2, sc2, g2 = jnp.split(mod, 6, axis=-1)
        h = rms_norm(x, norm1_g[l]) * (1.0 + sc1) + sh1
        x = x + g1 * hybrid_mixer(h, w_in[l], b_in[l], conv_w[l], conv_b[l], conv_ln_g[l],
                                  conv_ln_b[l], gla_wa[l], gla_ba[l], gla_norm_g[l],
                                  w_branch_a[l], w_branch_b[l], w_branch_c[l], w_out[l])
        h = rms_norm(x, norm2_g[l]) * (1.0 + sc2) + sh2
        x = x + g2 * moe_ffn(h, w_router[l], e_bias[l], w1[l], w3[l], w2[l],
                             ws1[l], ws3[l], ws2[l])
    return rms_norm(x, final_g)
```

```python
import functools

import numpy as np
import jax
import jax.numpy as jnp
from jax import lax
from jax.experimental import pallas as pl
from jax.experimental.pallas import tpu as pltpu

F32 = jnp.float32
BF16 = jnp.bfloat16
U32 = jnp.uint32
I32 = jnp.int32

D_MODEL = 2048
CONV_DIM = 512
CONV_WIDTH = 31
FOX_HEADS = 8
FOX_HEAD_DIM = 128
FOX_DIM = FOX_HEADS * FOX_HEAD_DIM
GLA_HEADS = 4
GLA_DK = 64
GLA_DV = 128
GLA_KDIM = GLA_HEADS * GLA_DK
GLA_VDIM = GLA_HEADS * GLA_DV
GLA_RANK = 16
GLA_TAU = 16.0
GLA_CHUNK = 64
GLA_SUB = 8
N_EXPERTS = 64
N_GROUPS = 8
TOPK_GROUPS = 4
TOP_K = 8
EXPERT_DIM = 512
SHARED_DIM = 512
ROUTED_SCALE = 2.5
MOE_BLOCK = 256
EPS = 1e-6

LANES = 128
V7X_VMEM_LIMIT = 56 * 1024 * 1024

Z_GATE = 0
Z_CONV = 48
Z_QF = 56
Z_KF = 64
Z_VF = 72
Z_QG = 80
Z_KG = 82
Z_VG = 84
Z_RG = 88
Z_COLS = 92 * LANES
ZS_F = 0
ZS_A = 8
NEG = -1e30


def _cparams(n_axes, vmem=V7X_VMEM_LIMIT):
    return pltpu.CompilerParams(dimension_semantics=("arbitrary",) * n_axes, vmem_limit_bytes=vmem)


def _log_sigmoid(x):
    return jnp.minimum(x, 0.0) - jnp.log(1.0 + jnp.exp(-jnp.abs(x)))


def _silu(x):
    return x * jax.nn.sigmoid(x)


def _split3(x):
    hi = x.astype(BF16)
    r1 = x - hi.astype(F32)
    mid = r1.astype(BF16)
    lo = (r1 - mid.astype(F32)).astype(BF16)
    return hi, mid, lo


def _dot(a, b):
    return jnp.dot(a, b, preferred_element_type=F32)


def _dot_nt(a, b):
    return lax.dot_general(a, b, (((1,), (1,)), ((), ())), preferred_element_type=F32)


def _dot_tn(a, b):
    return lax.dot_general(a, b, (((0,), (0,)), ((), ())), preferred_element_type=F32)


def _lmul_exact(l01, x):
    hi, mid, lo = _split3(x)
    return _dot(l01, hi) + _dot(l01, mid) + _dot(l01, lo)


def _dot_f32(a, b):
    a0, a1, a2 = _split3(a)
    b0, b1, b2 = _split3(b)
    return (_dot(a0, b0) + _dot(a0, b1) + _dot(a1, b0)) + (_dot(a0, b2) + _dot(a1, b1) + _dot(a2, b0))


def _ada_kernel(ct_ref, w_ref, b_ref, o_ref):
    ct = ct_ref[...]
    act = _silu(ct)
    w = w_ref[...]
    rows = [jnp.sum(w * act[:, b:b + 1], axis=0, keepdims=True) for b in range(ct.shape[1])]
    o_ref[...] = jnp.concatenate(rows, axis=0) + b_ref[...]


def _ada_mod(c, ada_w, ada_b):
    n_l, d, n = ada_w.shape
    bsz = c.shape[0]
    tn = 512
    return pl.pallas_call(
        _ada_kernel,
        grid=(n_l, n // tn),
        in_specs=[pl.BlockSpec((d, bsz), lambda l, j: (0, 0)),
                  pl.BlockSpec((None, d, tn), lambda l, j: (l, 0, j)),
                  pl.BlockSpec((None, 1, tn), lambda l, j: (l, 0, j))],
        out_specs=pl.BlockSpec((None, bsz, tn), lambda l, j: (l, 0, j)),
        out_shape=jax.ShapeDtypeStruct((n_l, bsz, n), F32),
        compiler_params=_cparams(2),
        name="ada_mod",
    )(c.T, ada_w, ada_b.reshape(n_l, 1, n))


def _modulated_norm(x, g, sc, sh):
    ms = jnp.mean(x * x, axis=-1, keepdims=True)
    return (x * lax.rsqrt(ms + EPS) * g) * (1.0 + sc) + sh


def _inproj_kernel(x_ref, g_ref, sc_ref, sh_ref, w_ref, b_ref, ws_ref, bs_ref, z_ref, zs_ref, h_scr):
    @pl.when(pl.program_id(1) == 0)
    def _():
        h = _modulated_norm(x_ref[...], g_ref[...], sc_ref[...], sh_ref[...]).astype(BF16)
        h_scr[...] = h
        zs_ref[...] = _dot(h, ws_ref[...]) + bs_ref[...]

    z_ref[...] = (_dot(h_scr[...], w_ref[...]) + b_ref[...]).astype(BF16)


def _inproj(x2, g, sc, sh, w, b, ws, bs, seq):
    t, d = x2.shape
    tm, tn = 1024, 512
    per_b = seq // tm
    vec = lambda: pl.BlockSpec((None, 1, d), lambda i, j: (i // per_b, 0, 0))
    return pl.pallas_call(
        _inproj_kernel,
        grid=(t // tm, Z_COLS // tn),
        in_specs=[pl.BlockSpec((tm, d), lambda i, j: (i, 0)),
                  pl.BlockSpec((1, d), lambda i, j: (0, 0)),
                  vec(), vec(),
                  pl.BlockSpec((d, tn), lambda i, j: (0, j)),
                  pl.BlockSpec((1, tn), lambda i, j: (0, j)),
                  pl.BlockSpec((d, LANES), lambda i, j: (0, 0)),
                  pl.BlockSpec((1, LANES), lambda i, j: (0, 0))],
        out_specs=[pl.BlockSpec((tm, tn), lambda i, j: (i, j)),
                   pl.BlockSpec((tm, LANES), lambda i, j: (i, 0))],
        out_shape=[jax.ShapeDtypeStruct((t, Z_COLS), BF16), jax.ShapeDtypeStruct((t, LANES), F32)],
        scratch_shapes=[pltpu.VMEM((tm, d), BF16)],
        compiler_params=_cparams(2),
        name="inproj",
    )(x2, g, sc, sh, w, b, ws, bs)


def _prep_kernel(zs_ref, ltri_ref, lblk_ref, pq_ref, pk_ref, cq_ref, ck_ref, wa_ref, ba_ref,
                 aq_ref, ak_ref, bc_ref, carry_ref):
    @pl.when(pl.program_id(1) == 0)
    def _():
        carry_ref[...] = jnp.zeros_like(carry_ref)

    zs = zs_ref[...]
    cum = _lmul_exact(ltri_ref[...], _log_sigmoid(zs)) + carry_ref[...]
    carry_ref[...] = cum[cum.shape[0] - 1:, :]
    hi, mid, lo = _split3(cum)
    aq = cq_ref[...] + _dot(hi, pq_ref[0]) + _dot(mid, pq_ref[1]) + _dot(lo, pq_ref[2])
    ak = ck_ref[...] + _dot(hi, pk_ref[0]) + _dot(mid, pk_ref[1]) + _dot(lo, pk_ref[2])
    aq_ref[...] = aq.astype(BF16)
    ak_ref[...] = ak.astype(BF16)
    la = _log_sigmoid(_dot_f32(zs, wa_ref[...]) + ba_ref[...]) * (1.0 / GLA_TAU)
    bc_ref[...] = _lmul_exact(lblk_ref[...], la)


def _prep_consts(ts):
    r = np.arange(ts)
    ltri = (r[:, None] >= r[None, :]).astype(np.float32)
    lblk = ltri * (r[:, None] // GLA_CHUNK == r[None, :] // GLA_CHUNK)
    pq = np.zeros((3, LANES, FOX_DIM), np.float32)
    pk = np.zeros((3, LANES, FOX_DIM), np.float32)
    cq = np.zeros((1, FOX_DIM), np.float32)
    ck = np.zeros((1, FOX_DIM), np.float32)
    for h in range(FOX_HEADS):
        for p in range(3):
            pq[p, ZS_F + h, h * FOX_HEAD_DIM + p] = 1.0
            pk[p, ZS_F + h, h * FOX_HEAD_DIM + 3 + p] = -1.0
            cq[0, h * FOX_HEAD_DIM + 3 + p] = 1.0
            ck[0, h * FOX_HEAD_DIM + p] = 1.0
    return (jnp.asarray(ltri, BF16), jnp.asarray(lblk, BF16), jnp.asarray(pq, BF16), jnp.asarray(pk, BF16),
            jnp.asarray(cq), jnp.asarray(ck))


def _prep(zs, wa_pad, ba, seq):
    t = zs.shape[0]
    ts = 256
    per_b = seq // ts
    ltri, lblk, pq, pk, cq, ck = _prep_consts(ts)
    full = lambda a: pl.BlockSpec(a.shape, lambda b, s: (0,) * a.ndim)
    row = lambda n: pl.BlockSpec((ts, n), lambda b, s: (b * per_b + s, 0))
    return pl.pallas_call(
        _prep_kernel,
        grid=(t // seq, per_b),
        in_specs=[row(LANES), full(ltri), full(lblk), full(pq), full(pk), full(cq), full(ck),
                  full(wa_pad), full(ba)],
        out_specs=[row(FOX_DIM), row(FOX_DIM), row(GLA_KDIM)],
        out_shape=[jax.ShapeDtypeStruct((t, FOX_DIM), BF16), jax.ShapeDtypeStruct((t, FOX_DIM), BF16),
                   jax.ShapeDtypeStruct((t, GLA_KDIM), F32)],
        scratch_shapes=[pltpu.VMEM((1, LANES), F32)],
        compiler_params=_cparams(2),
        name="prep",
    )(zs, ltri, lblk, pq, pk, cq, ck, wa_pad, ba)


CONV_HIST = 32


def _conv_kernel(u_ref, w_ref, cb_ref, lg_ref, lb_ref, o_ref, ypad_ref):
    ts = u_ref.shape[0]

    @pl.when(pl.program_id(1) == 0)
    def _():
        ypad_ref[0:CONV_HIST, :] = jnp.zeros((CONV_HIST, CONV_DIM), F32)

    u = u_ref[...].astype(F32)
    ypad_ref[CONV_HIST:CONV_HIST + ts, :] = u[:, :CONV_DIM] * jax.nn.sigmoid(u[:, CONV_DIM:])
    acc = jnp.zeros((ts, CONV_DIM), F32) + cb_ref[...]
    first = CONV_HIST - (CONV_WIDTH - 1)
    for j in range(CONV_WIDTH):
        acc = acc + w_ref[j:j + 1, :] * ypad_ref[first + j:first + j + ts, :]
    ypad_ref[0:CONV_HIST, :] = ypad_ref[ts:ts + CONV_HIST, :]
    mu = jnp.mean(acc, axis=-1, keepdims=True)
    cen = acc - mu
    var = jnp.mean(cen * cen, axis=-1, keepdims=True)
    y = cen * lax.rsqrt(var + EPS) * lg_ref[...] + lb_ref[...]
    o_ref[...] = _silu(y).astype(BF16)


def _conv(z, w, cb, lg, lb, seq):
    t = z.shape[0]
    ts = 512
    per_b = seq // ts
    full = lambda a: pl.BlockSpec(a.shape, lambda b, s: (0,) * a.ndim)
    return pl.pallas_call(
        _conv_kernel,
        grid=(t // seq, per_b),
        in_specs=[pl.BlockSpec((ts, 2 * CONV_DIM), lambda b, s: (b * per_b + s, Z_CONV * LANES // (2 * CONV_DIM))),
                  full(w), full(cb), full(lg), full(lb)],
        out_specs=pl.BlockSpec((ts, CONV_DIM), lambda b, s: (b * per_b + s, 0)),
        out_shape=jax.ShapeDtypeStruct((t, CONV_DIM), BF16),
        scratch_shapes=[pltpu.VMEM((ts + CONV_HIST, CONV_DIM), F32)],
        compiler_params=_cparams(2),
        name="conv",
    )(z, w, cb, lg, lb)


def _fox_kernel(q_ref, aq_ref, k_ref, ak_ref, v_ref, o_ref, m_ref, acc_ref, *, blk):
    i = pl.program_id(2)
    qq = jnp.concatenate([q_ref[...], aq_ref[...]], axis=1)
    ones = jnp.ones((blk, FOX_HEAD_DIM), BF16)
    m_ref[...] = jnp.full(m_ref.shape, NEG, F32)
    acc_ref[...] = jnp.zeros(acc_ref.shape, F32)

    def step(j, masked):
        off = pl.multiple_of(j * blk, blk)
        kk = jnp.concatenate([k_ref[pl.ds(off, blk), :], ak_ref[pl.ds(off, blk), :]], axis=1)
        s = _dot_nt(qq, kk)
        if masked:
            r = lax.broadcasted_iota(I32, (blk, blk), 0)
            c = lax.broadcasted_iota(I32, (blk, blk), 1)
            s = jnp.where(c <= r, s, NEG)
        m_old = m_ref[...]
        m_new = jnp.maximum(m_old, jnp.max(s, axis=1, keepdims=True))
        p = jnp.exp(s - m_new).astype(BF16)
        vv = jnp.concatenate([v_ref[pl.ds(off, blk), :], ones], axis=1)
        acc_ref[...] = acc_ref[...] * jnp.exp(m_old - m_new) + _dot(p, vv)
        m_ref[...] = m_new

    def body(j, carry):
        step(j, False)
        return carry

    lax.fori_loop(0, i, body, 0)
    step(i, True)
    acc = acc_ref[...]
    o_ref[...] = (acc[:, :FOX_HEAD_DIM] / acc[:, FOX_HEAD_DIM:]).astype(BF16)


def _fox(z, aq, ak, seq):
    t = z.shape[0]
    blk = min(512, seq)
    nq = seq // blk
    qspec = lambda base: pl.BlockSpec((blk, LANES), lambda b, h, i: (b * nq + i, base + h))
    kspec = lambda base: pl.BlockSpec((seq, LANES), lambda b, h, i: (b, base + h))
    return pl.pallas_call(
        functools.partial(_fox_kernel, blk=blk),
        grid=(t // seq, FOX_HEADS, nq),
        in_specs=[qspec(Z_QF), qspec(0), kspec(Z_KF), kspec(0), kspec(Z_VF)],
        out_specs=qspec(0),
        out_shape=jax.ShapeDtypeStruct((t, FOX_DIM), BF16),
        scratch_shapes=[pltpu.VMEM((blk, 1), F32), pltpu.VMEM((blk, 2 * FOX_HEAD_DIM), F32)],
        compiler_params=_cparams(3),
        name="fox",
    )(z, aq, z, ak, z)


def _gla_chunk(q, k, v, b, st, e3):
    c, nsub = GLA_CHUNK, GLA_CHUNK // GLA_SUB
    row = lax.broadcasted_iota(I32, (c, GLA_KDIM), 0)
    rblk, rloc = row // GLA_SUB, row % GLA_SUB
    lane_k = lax.broadcasted_iota(I32, (c, GLA_KDIM), 1)
    khead, kpos = lane_k // GLA_DK, lane_k % GLA_DK
    b3 = b.reshape(nsub, GLA_SUB, GLA_KDIM)
    k3 = k.reshape(nsub, GLA_SUB, GLA_KDIM)
    last = b3[:, GLA_SUB - 1:GLA_SUB, :]
    bcast = lambda a3, i: jnp.broadcast_to(a3[:, i:i + 1, :], (nsub, GLA_SUB, GLA_KDIM)).reshape(c, GLA_KDIM)

    prev = jnp.concatenate([last[:1], last[:nsub - 1]], axis=0)
    refq = jnp.broadcast_to(prev, (nsub, GLA_SUB, GLA_KDIM)).reshape(c, GLA_KDIM)
    qt = q * jnp.exp(jnp.minimum(b - refq, 0.0))
    qcat = jnp.concatenate([jnp.where(rblk == i, qt, 0.0) for i in range(1, nsub)], axis=1).astype(BF16)
    kts = []
    for i in range(1, nsub):
        kt = k * jnp.exp(jnp.minimum(last[i - 1] - b, 0.0))
        kts.append(jnp.where(row < GLA_SUB * i, kt, 0.0))
    kcat = jnp.concatenate(
        [jnp.concatenate([jnp.where(khead == h, kt, 0.0) for kt in kts], axis=1) for h in range(GLA_HEADS)],
        axis=0).astype(BF16)
    a_all = _dot_nt(qcat, kcat)

    for sl in range(GLA_SUB):
        p = q * bcast(k3, sl) * jnp.exp(jnp.minimum(b - bcast(b3, sl), 0.0))
        w = _dot(p.astype(BF16), e3)
        hit = (kpos == rblk * GLA_SUB + sl) & (rloc >= sl)
        a_all = a_all + jnp.where(hit, w, 0.0)

    lane_v = lax.broadcasted_iota(I32, (c, GLA_VDIM), 1) // GLA_DV
    vbd = jnp.concatenate([jnp.where(lane_v == h, v, jnp.zeros_like(v)) for h in range(GLA_HEADS)], axis=0)
    qe = (q * jnp.exp(b)).astype(BF16)
    o = _dot(a_all.astype(BF16), vbd) + _dot_nt(qe, st.astype(BF16))

    blast = b[c - 1:c, :]
    kd = (k * jnp.exp(blast - b)).astype(BF16)
    shead_v = lax.broadcasted_iota(I32, (GLA_VDIM, GLA_KDIM), 0) // GLA_DV
    shead_k = lax.broadcasted_iota(I32, (GLA_VDIM, GLA_KDIM), 1) // GLA_DK
    st_new = jnp.where(shead_v == shead_k, st * jnp.exp(blast) + _dot_tn(v, kd), 0.0)
    return o, st_new


def _gla_kernel(q_ref, k_ref, v_ref, r_ref, b_ref, ng_ref, e3_ref, o_ref, st_ref):
    @pl.when(pl.program_id(1) == 0)
    def _():
        st_ref[...] = jnp.zeros_like(st_ref)

    e3 = e3_ref[...]
    ng = ng_ref[...]

    def body(ci, carry):
        rows = pl.ds(pl.multiple_of(ci * GLA_CHUNK, GLA_CHUNK), GLA_CHUNK)
        o, st_new = _gla_chunk(q_ref[rows, :].astype(F32), k_ref[rows, :].astype(F32), v_ref[rows, :],
                               b_ref[rows, :], st_ref[...], e3)
        st_ref[...] = st_new
        outs = []
        for h in range(GLA_HEADS):
            oh = o[:, h * GLA_DV:(h + 1) * GLA_DV]
            outs.append(oh * lax.rsqrt(jnp.mean(oh * oh, axis=-1, keepdims=True) + EPS))
        y = jnp.concatenate(outs, axis=1) * ng * _silu(r_ref[rows, :].astype(F32))
        o_ref[rows, :] = y.astype(BF16)
        return carry

    lax.fori_loop(0, q_ref.shape[0] // GLA_CHUNK, body, 0)


def _gla(z, bc, ng, seq):
    t = z.shape[0]
    ts = min(512, seq)
    per_b = seq // ts
    lane = np.arange(GLA_KDIM)
    e3 = jnp.asarray((lane[:, None] // GLA_DK == lane[None, :] // GLA_DK).astype(np.float32), BF16)
    spec = lambda n, unit: pl.BlockSpec((ts, n), lambda b, s: (b * per_b + s, unit))
    full = lambda a: pl.BlockSpec(a.shape, lambda b, s: (0,) * a.ndim)
    return pl.pallas_call(
        _gla_kernel,
        grid=(t // seq, per_b),
        in_specs=[spec(GLA_KDIM, Z_QG * LANES // GLA_KDIM), spec(GLA_KDIM, Z_KG * LANES // GLA_KDIM),
                  spec(GLA_VDIM, Z_VG * LANES // GLA_VDIM), spec(GLA_VDIM, Z_RG * LANES // GLA_VDIM),
                  spec(GLA_KDIM, 0), full(ng), full(e3)],
        out_specs=spec(GLA_VDIM, 0),
        out_shape=jax.ShapeDtypeStruct((t, GLA_VDIM), BF16),
        scratch_shapes=[pltpu.VMEM((GLA_VDIM, GLA_KDIM), F32)],
        compiler_params=_cparams(2),
        name="gla",
    )(z, z, z, z, bc, ng, e3)


def _merge_kernel(x_ref, g1_ref, ya_ref, yb_ref, yc_ref, ga_ref, gb_ref, gc_ref,
                  wa_ref, wb_ref, wc_ref, wo_ref, o_ref):
    gate = lambda r: jax.nn.sigmoid(r[...].astype(F32))
    merged = (gate(ga_ref) * _dot(ya_ref[...], wa_ref[...]) + gate(gb_ref) * _dot(yb_ref[...], wb_ref[...])
              + gate(gc_ref) * _dot(yc_ref[...], wc_ref[...]))
    o_ref[...] = x_ref[...] + g1_ref[...] * _dot(merged.astype(BF16), wo_ref[...])


def _merge(x2, g1, ya, yb, yc, z, wa, wb, wc, wo, seq):
    t, d = x2.shape
    tm = 256
    per_b = seq // tm
    full = lambda a: pl.BlockSpec(a.shape, lambda i: (0,) * a.ndim)
    row = lambda n, unit=0: pl.BlockSpec((tm, n), lambda i: (i, unit))
    return pl.pallas_call(
        _merge_kernel,
        grid=(t // tm,),
        in_specs=[row(d), pl.BlockSpec((None, 1, d), lambda i: (i // per_b, 0, 0)),
                  row(CONV_DIM), row(FOX_DIM), row(GLA_VDIM), row(d, 0), row(d, 1), row(d, 2),
                  full(wa), full(wb), full(wc), full(wo)],
        out_specs=row(d),
        out_shape=jax.ShapeDtypeStruct((t, d), F32),
        compiler_params=_cparams(1),
        name="merge",
    )(x2, g1, ya, yb, yc, z, z, z, wa, wb, wc, wo)


def _pack_halves(y):
    n = y.shape[1] // 2
    lo = pltpu.bitcast(y[:, :n].astype(BF16).astype(F32), U32)
    hi = pltpu.bitcast(y[:, n:].astype(BF16).astype(F32), U32)
    return (hi & jnp.uint32(0xFFFF0000)) | (lo >> 16)


def _unpack_halves(p):
    lo = pltpu.bitcast(p << 16, F32)
    hi = pltpu.bitcast(p & jnp.uint32(0xFFFF0000), F32)
    return lo, hi


def _first_index(hit, idx, n):
    return jnp.min(jnp.where(hit, idx, n), axis=0, keepdims=True)


def _moe_pre_kernel(x_ref, g_ref, sc_ref, sh_ref, g2_ref, wrh_ref, wrl_ref, eb_ref, ws1_ref, ws3_ref, ws2_ref,
                    u_ref, hp_ref, base_ref, te_ref, tw_ref, rk_ref, cnt_ref, carry_ref):
    @pl.when(pl.program_id(0) == 0)
    def _():
        carry_ref[...] = jnp.zeros_like(carry_ref)

    x = x_ref[...]
    h = _modulated_norm(x, g_ref[...], sc_ref[...], sh_ref[...])
    hb = h.astype(BF16)
    hp_ref[...] = _pack_halves(h)

    hs = (_silu(_dot(hb, ws1_ref[...])) * _dot(hb, ws3_ref[...])).astype(BF16)
    base_ref[...] = x + g2_ref[...] * _dot(hs, ws2_ref[...])

    hlo = (h - hb.astype(F32)).astype(BF16)
    logits = _dot_nt(wrh_ref[...], hb) + _dot_nt(wrh_ref[...], hlo) + _dot_nt(wrl_ref[...], hb)
    scores = jax.nn.sigmoid(logits)
    biased = scores + eb_ref[...]
    tm = x.shape[0]
    per_group = N_EXPERTS // N_GROUPS
    sub = lax.broadcasted_iota(I32, (per_group, tm), 0).astype(F32)
    gscore = []
    for g in range(N_GROUPS):
        blk = biased[g * per_group:(g + 1) * per_group, :]
        m1 = jnp.max(blk, axis=0, keepdims=True)
        rest = jnp.where(sub == _first_index(blk == m1, sub, float(per_group)), -jnp.inf, blk)
        gscore.append(m1 + jnp.max(rest, axis=0, keepdims=True))
    gs = jnp.concatenate(gscore, axis=0)
    gidx = lax.broadcasted_iota(I32, (N_GROUPS, tm), 0).astype(F32)
    gsel = jnp.zeros((N_GROUPS, tm), F32)
    for _ in range(TOPK_GROUPS):
        m = jnp.max(gs, axis=0, keepdims=True)
        hit = gidx == _first_index(gs == m, gidx, float(N_GROUPS))
        gsel = jnp.where(hit, 1.0, gsel)
        gs = jnp.where(hit, -jnp.inf, gs)
    cand = jnp.concatenate(
        [jnp.where(gsel[g:g + 1, :] > 0.5, biased[g * per_group:(g + 1) * per_group, :], -jnp.inf)
         for g in range(N_GROUPS)], axis=0)
    eidx = lax.broadcasted_iota(I32, (N_EXPERTS, tm), 0).astype(F32)
    hits, tops, tws = [], [], []
    sel = jnp.zeros((N_EXPERTS, tm), F32)
    for _ in range(TOP_K):
        m = jnp.max(cand, axis=0, keepdims=True)
        e = _first_index(cand == m, eidx, float(N_EXPERTS))
        hit = eidx == e
        hits.append(hit)
        tops.append(e)
        tws.append(jnp.sum(jnp.where(hit, scores, 0.0), axis=0, keepdims=True))
        cand = jnp.where(hit, -jnp.inf, cand)
        sel = jnp.where(hit, 1.0, sel)
    te_ref[...] = jnp.concatenate(tops, axis=0).astype(I32)
    tw = jnp.concatenate(tws, axis=0)
    tw_ref[...] = tw / jnp.sum(tw, axis=0, keepdims=True) * ROUTED_SCALE

    before = _dot(sel.astype(BF16), u_ref[...]) + carry_ref[...]
    rk_ref[...] = jnp.concatenate(
        [jnp.sum(jnp.where(hit, before, 0.0), axis=0, keepdims=True) for hit in hits], axis=0).astype(I32)
    carry = carry_ref[...] + jnp.sum(sel, axis=1, keepdims=True)
    carry_ref[...] = carry
    cnt_ref[...] = jnp.broadcast_to(carry, cnt_ref.shape).astype(I32)


def _moe_pre(x2, g, sc, sh, g2, wrh, wrl, eb, ws1, ws3, ws2, seq):
    t, d = x2.shape
    tm = 256
    per_b = seq // tm
    r = np.arange(tm)
    u = jnp.asarray((r[:, None] < r[None, :]).astype(np.float32), BF16)
    full = lambda a: pl.BlockSpec(a.shape, lambda i: (0,) * a.ndim)
    vec = lambda: pl.BlockSpec((None, 1, d), lambda i: (i // per_b, 0, 0))
    colk = lambda: pl.BlockSpec((TOP_K, tm), lambda i: (0, i))
    return pl.pallas_call(
        _moe_pre_kernel,
        grid=(t // tm,),
        in_specs=[pl.BlockSpec((tm, d), lambda i: (i, 0)), full(g), vec(), vec(), vec(),
                  full(wrh), full(wrl), full(eb), full(ws1), full(ws3), full(ws2), full(u)],
        out_specs=[pl.BlockSpec((tm, d // 2), lambda i: (i, 0)), pl.BlockSpec((tm, d), lambda i: (i, 0)),
                   colk(), colk(), colk(), pl.BlockSpec((N_EXPERTS, LANES), lambda i: (0, 0))],
        out_shape=[jax.ShapeDtypeStruct((t, d // 2), U32), jax.ShapeDtypeStruct((t, d), F32),
                   jax.ShapeDtypeStruct((TOP_K, t), I32), jax.ShapeDtypeStruct((TOP_K, t), F32),
                   jax.ShapeDtypeStruct((TOP_K, t), I32), jax.ShapeDtypeStruct((N_EXPERTS, LANES), I32)],
        scratch_shapes=[pltpu.VMEM((N_EXPERTS, 1), F32)],
        compiler_params=_cparams(1),
        name="moe_pre",
    )(x2, g, sc, sh, g2, wrh, wrl, eb, ws1, ws3, ws2, u)


def _row_copy(src_hbm, src_row, dst_buf, slot, dst_row, sem):
    return pltpu.make_async_copy(src_hbm.at[pl.ds(src_row, 1)], dst_buf.at[slot, pl.ds(dst_row, 1)], sem.at[slot])


def _row_store(src_buf, slot, src_row, dst_hbm, dst_row, sem):
    return pltpu.make_async_copy(src_buf.at[slot, pl.ds(src_row, 1)], dst_hbm.at[pl.ds(dst_row, 1)], sem.at[slot])


def _experts_kernel(be_ref, nu_ref, tok_ref, tokn_ref, dst_ref, hp_hbm, w1_ref, w3_ref, w2_ref, out_hbm,
                    xbuf, ybuf, w1b, w3b, w2b, gsem, ssem):
    i = pl.program_id(0)
    n_used = nu_ref[0]
    slot = i % 2

    def gather(idx_ref, s, start):
        def body(r, carry):
            cp = _row_copy(hp_hbm, idx_ref[0, 0, r], xbuf, s, r, gsem)
            cp.start() if start else cp.wait()
            return carry
        lax.fori_loop(0, MOE_BLOCK, body, 0)

    def scatter(s, start):
        def body(r, carry):
            cp = _row_store(ybuf, s, r, out_hbm, dst_ref[0, 0, r] if start else 0, ssem)
            cp.start() if start else cp.wait()
            return carry
        lax.fori_loop(0, MOE_BLOCK, body, 0)

    @pl.when(i < n_used)
    def _():
        @pl.when(i == 0)
        def _():
            gather(tok_ref, 0, True)

        @pl.when(i + 1 < n_used)
        def _():
            gather(tokn_ref, 1 - slot, True)

        gather(tok_ref, slot, False)

        @pl.when((i == 0) | (be_ref[i] != be_ref[jnp.maximum(i - 1, 0)]))
        def _():
            w1b[...] = w1_ref[...].astype(BF16)
            w3b[...] = w3_ref[...].astype(BF16)
            w2b[...] = w2_ref[...].astype(BF16)

        lo, hi = _unpack_halves(xbuf[slot])
        xb = jnp.concatenate([lo.astype(BF16), hi.astype(BF16)], axis=1)
        hid = (_silu(_dot(xb, w1b[...])) * _dot(xb, w3b[...])).astype(BF16)
        y = _dot(hid, w2b[...])

        @pl.when(i >= 2)
        def _():
            scatter(slot, False)

        ybuf[slot] = _pack_halves(y)
        scatter(slot, True)

        @pl.when(i == n_used - 1)
        def _():
            @pl.when(i >= 1)
            def _():
                scatter(1 - slot, False)
            scatter(slot, False)


def _experts(block_e, n_used, row_tok, row_dst, hp, w1, w3, w2, n_out_rows):
    nb = row_tok.shape[0]
    d2 = hp.shape[1]
    idx = lambda f: pl.BlockSpec((1, 1, MOE_BLOCK), f, memory_space=pltpu.SMEM)
    wspec = lambda a: pl.BlockSpec((None,) + a.shape[1:], lambda i, be, nu: (be[i], 0, 0))
    return pl.pallas_call(
        _experts_kernel,
        grid_spec=pltpu.PrefetchScalarGridSpec(
            num_scalar_prefetch=2,
            grid=(nb,),
            in_specs=[idx(lambda i, be, nu: (i, 0, 0)),
                      idx(lambda i, be, nu: (jnp.minimum(i + 1, nb - 1), 0, 0)),
                      idx(lambda i, be, nu: (i, 0, 0)),
                      pl.BlockSpec(memory_space=pl.ANY),
                      wspec(w1), wspec(w3), wspec(w2)],
            out_specs=pl.BlockSpec(memory_space=pl.ANY),
            scratch_shapes=[pltpu.VMEM((2, MOE_BLOCK, d2), U32), pltpu.VMEM((2, MOE_BLOCK, d2), U32),
                            pltpu.VMEM(w1.shape[1:], BF16), pltpu.VMEM(w3.shape[1:], BF16),
                            pltpu.VMEM(w2.shape[1:], BF16),
                            pltpu.SemaphoreType.DMA((2,)), pltpu.SemaphoreType.DMA((2,))]),
        out_shape=jax.ShapeDtypeStruct((n_out_rows, d2), U32),
        compiler_params=_cparams(1),
        name="experts",
    )(block_e, n_used, row_tok, row_tok, row_dst, hp, w1, w3, w2)


def _combine_kernel(base_ref, g2_ref, w_ref, s_ref, fg_ref, o_ref, *, final):
    w = w_ref[...]
    acc_lo = acc_hi = None
    for k in range(TOP_K):
        lo, hi = _unpack_halves(s_ref[k])
        wk = w[:, k:k + 1]
        acc_lo = wk * lo if acc_lo is None else acc_lo + wk * lo
        acc_hi = wk * hi if acc_hi is None else acc_hi + wk * hi
    x = base_ref[...] + g2_ref[...] * jnp.concatenate([acc_lo, acc_hi], axis=1)
    if final:
        x = x * lax.rsqrt(jnp.mean(x * x, axis=-1, keepdims=True) + EPS) * fg_ref[...]
    o_ref[...] = x


def _combine(base, g2, w, slots3, fg, seq, final):
    t, d = base.shape
    tm = 256
    per_b = seq // tm
    return pl.pallas_call(
        functools.partial(_combine_kernel, final=final),
        grid=(t // tm,),
        in_specs=[pl.BlockSpec((tm, d), lambda i: (i, 0)),
                  pl.BlockSpec((None, 1, d), lambda i: (i // per_b, 0, 0)),
                  pl.BlockSpec((tm, TOP_K), lambda i: (i, 0)),
                  pl.BlockSpec((TOP_K, tm, d // 2), lambda i: (0, i, 0)),
                  pl.BlockSpec((1, d), lambda i: (0, 0))],
        out_specs=pl.BlockSpec((tm, d), lambda i: (i, 0)),
        out_shape=jax.ShapeDtypeStruct((t, d), F32),
        compiler_params=_cparams(1),
        name="combine",
    )(base, g2, w, slots3, fg)


TRASH_ROWS = 64


def _dispatch_tables(top_e, rank, counts, t):
    tk = TOP_K * t
    n_blocks = tk // MOE_BLOCK + N_EXPERTS
    n_rows = n_blocks * MOE_BLOCK
    plane = t + TRASH_ROWS
    padded = (counts + MOE_BLOCK - 1) // MOE_BLOCK * MOE_BLOCK
    pends = jnp.cumsum(padded)
    pstarts = pends - padded
    dest = (pstarts[top_e] + rank).reshape(tk)
    row_pair = jnp.full((n_rows,), -1, I32).at[dest].set(jnp.arange(tk, dtype=I32))
    valid = row_pair >= 0
    r = jnp.arange(n_rows, dtype=I32)
    pos, par = r % MOE_BLOCK, (r // MOE_BLOCK) % 2
    trash = (pos % TOP_K) * plane + t + pos // TOP_K + par * (TRASH_ROWS // 2)
    row_tok = jnp.where(valid, row_pair % t, 0)
    row_dst = jnp.where(valid, (row_pair // t) * plane + row_pair % t, trash)
    n_used = (pends[-1] // MOE_BLOCK).astype(I32)
    starts = jnp.arange(n_blocks, dtype=I32) * MOE_BLOCK
    block_e = jnp.minimum(jnp.searchsorted(pends, starts, side="right"), N_EXPERTS - 1).astype(I32)
    block_e = jnp.where(jnp.arange(n_blocks) < n_used, block_e, block_e[jnp.maximum(n_used - 1, 0)])
    return (block_e, n_used.reshape(1), row_tok.reshape(n_blocks, 1, MOE_BLOCK),
            row_dst.reshape(n_blocks, 1, MOE_BLOCK), plane)


def _layout_inproj(w_in, b_in):
    o = np.cumsum([0, 2 * CONV_DIM, FOX_DIM, FOX_DIM, FOX_DIM, FOX_HEADS, GLA_KDIM, GLA_KDIM, GLA_VDIM, GLA_VDIM,
                   GLA_RANK, 3 * D_MODEL])
    seg = lambda a, i: a[..., int(o[i]):int(o[i + 1])]

    def main(a):
        return jnp.concatenate([seg(a, 10), seg(a, 0), seg(a, 1) * FOX_HEAD_DIM ** -0.5, seg(a, 2), seg(a, 3),
                                seg(a, 5) * GLA_DK ** -0.5, seg(a, 6), seg(a, 7), seg(a, 8)], axis=-1)

    def small(a):
        pad = jnp.zeros(a.shape[:-1] + (LANES - FOX_HEADS - GLA_RANK,), a.dtype)
        return jnp.concatenate([seg(a, 4), seg(a, 9), pad], axis=-1)

    return main(w_in).astype(BF16), main(b_in)[None, :], small(w_in).astype(BF16), small(b_in)[None, :]


def kernel(x, c, norm1_g, ada_w, ada_b, w_in, b_in, conv_w, conv_b, conv_ln_g, conv_ln_b, gla_wa, gla_ba,
           gla_norm_g, w_branch_a, w_branch_b, w_branch_c, w_out, norm2_g, w_router, e_bias, w1, w3, w2,
           ws1, ws3, ws2, final_g):
    bsz, seq, d = x.shape
    t = bsz * seq
    depth = ada_w.shape[0]
    mod = _ada_mod(c, ada_w, ada_b).reshape(depth, bsz, 6, 1, d)
    x2 = x.reshape(t, d)
    row = lambda a: a.reshape(1, -1)
    for l in range(depth):
        sh1, sc1, g1, sh2, sc2, g2 = (mod[l, :, i] for i in range(6))
        w, b, ws, bs = _layout_inproj(w_in[l], b_in[l])
        z, zs = _inproj(x2, row(norm1_g[l]), sc1, sh1, w, b, ws, bs, seq)
        wa_pad = jnp.zeros((LANES, GLA_KDIM), F32).at[ZS_A:ZS_A + GLA_RANK].set(gla_wa[l])
        aq, ak, bc = _prep(zs, wa_pad, row(gla_ba[l]), seq)
        ya = _conv(z, conv_w[l].reshape(CONV_WIDTH, CONV_DIM), row(conv_b[l]), row(conv_ln_g[l]),
                   row(conv_ln_b[l]), seq)
        yb = _fox(z, aq, ak, seq)
        yc = _gla(z, bc, row(gla_norm_g[l]), seq)
        x2 = _merge(x2, g1, ya, yb, yc, z, w_branch_a[l].astype(BF16), w_branch_b[l].astype(BF16),
                    w_branch_c[l].astype(BF16), w_out[l].astype(BF16), seq)

        wr_t = w_router[l].T
        wrh = wr_t.astype(BF16)
        wrl = (wr_t - wrh.astype(F32)).astype(BF16)
        hp, base, top_e, top_w, rank, counts = _moe_pre(
            x2, row(norm2_g[l]), sc2, sh2, g2, wrh, wrl, e_bias[l].reshape(N_EXPERTS, 1),
            ws1[l].astype(BF16), ws3[l].astype(BF16), ws2[l].astype(BF16), seq)
        block_e, n_used, row_tok, row_dst, plane = _dispatch_tables(top_e, rank, counts[:, 0], t)
        slots = _experts(block_e, n_used, row_tok, row_dst, hp, w1[l], w3[l], w2[l], TOP_K * plane)
        x2 = _combine(base, g2, top_w.T, slots.reshape(TOP_K, plane, d // 2), row(final_g), seq,
                      final=(l == depth - 1))
    return x2.reshape(bsz, seq, d)
```

```python
import functools

import numpy as np
import jax
import jax.numpy as jnp
from jax import lax
from jax.experimental import pallas as pl
from jax.experimental.pallas import tpu as pltpu

F32 = jnp.float32
BF16 = jnp.bfloat16
U32 = jnp.uint32
I32 = jnp.int32

D_MODEL = 2048
CONV_DIM = 512
CONV_WIDTH = 31
FOX_HEADS = 8
FOX_HEAD_DIM = 128
FOX_DIM = FOX_HEADS * FOX_HEAD_DIM
GLA_HEADS = 4
GLA_DK = 64
GLA_DV = 128
GLA_KDIM = GLA_HEADS * GLA_DK
GLA_VDIM = GLA_HEADS * GLA_DV
GLA_RANK = 16
GLA_TAU = 16.0
GLA_CHUNK = 64
GLA_SUB = 8
N_EXPERTS = 64
N_GROUPS = 8
TOPK_GROUPS = 4
TOP_K = 8
EXPERT_DIM = 512
SHARED_DIM = 512
ROUTED_SCALE = 2.5
MOE_BLOCK = 256
EPS = 1e-6

LANES = 128
V7X_VMEM_LIMIT = 56 * 1024 * 1024

Z_GATE = 0
Z_CONV = 48
Z_QF = 56
Z_KF = 64
Z_VF = 72
Z_QG = 80
Z_KG = 82
Z_VG = 84
Z_RG = 88
Z_COLS = 92 * LANES
ZS_F = 0
ZS_A = 8
NEG = -1e30
LOG2E = 1.4426950408889634


def _cparams(n_axes, vmem=V7X_VMEM_LIMIT):
    return pltpu.CompilerParams(dimension_semantics=("arbitrary",) * n_axes, vmem_limit_bytes=vmem)


def _log_sigmoid(x):
    return jnp.minimum(x, 0.0) - jnp.log(1.0 + jnp.exp(-jnp.abs(x)))


def _silu(x):
    return x * jax.nn.sigmoid(x)


def _split3(x):
    hi = x.astype(BF16)
    r1 = x - hi.astype(F32)
    mid = r1.astype(BF16)
    lo = (r1 - mid.astype(F32)).astype(BF16)
    return hi, mid, lo


def _dot(a, b):
    return jnp.dot(a, b, preferred_element_type=F32)


def _dot_nt(a, b):
    return lax.dot_general(a, b, (((1,), (1,)), ((), ())), preferred_element_type=F32)


def _dot_tn(a, b):
    return lax.dot_general(a, b, (((0,), (0,)), ((), ())), preferred_element_type=F32)


def _lmul_exact(l01, x):
    hi, mid, lo = _split3(x)
    return _dot(l01, hi) + _dot(l01, mid) + _dot(l01, lo)


def _dot_f32(a, b):
    a0, a1, a2 = _split3(a)
    b0, b1, b2 = _split3(b)
    return (_dot(a0, b0) + _dot(a0, b1) + _dot(a1, b0)) + (_dot(a0, b2) + _dot(a1, b1) + _dot(a2, b0))


def _ada_kernel(ct_ref, w_ref, b_ref, o_ref):
    ct = ct_ref[...]
    act = _silu(ct)
    w = w_ref[...]
    rows = [jnp.sum(w * act[:, b:b + 1], axis=0, keepdims=True) for b in range(ct.shape[1])]
    o_ref[...] = jnp.concatenate(rows, axis=0) + b_ref[...]


def _ada_mod(c, ada_w, ada_b):
    n_l, d, n = ada_w.shape
    bsz = c.shape[0]
    tn = 512
    return pl.pallas_call(
        _ada_kernel,
        grid=(n_l, n // tn),
        in_specs=[pl.BlockSpec((d, bsz), lambda l, j: (0, 0)),
                  pl.BlockSpec((None, d, tn), lambda l, j: (l, 0, j)),
                  pl.BlockSpec((None, 1, tn), lambda l, j: (l, 0, j))],
        out_specs=pl.BlockSpec((None, bsz, tn), lambda l, j: (l, 0, j)),
        out_shape=jax.ShapeDtypeStruct((n_l, bsz, n), F32),
        compiler_params=_cparams(2),
        name="ada_mod",
    )(c.T, ada_w, ada_b.reshape(n_l, 1, n))


def _modulated_norm(x, g, sc, sh):
    ms = jnp.mean(x * x, axis=-1, keepdims=True)
    return (x * lax.rsqrt(ms + EPS) * g) * (1.0 + sc) + sh


def _inproj_kernel(x_ref, g_ref, sc_ref, sh_ref, w_ref, b_ref, ws_ref, bs_ref, z_ref, zs_ref, h_scr):
    @pl.when(pl.program_id(1) == 0)
    def _():
        h = _modulated_norm(x_ref[...], g_ref[...], sc_ref[...], sh_ref[...]).astype(BF16)
        h_scr[...] = h
        zs_ref[...] = _dot(h, ws_ref[...]) + bs_ref[...]

    z_ref[...] = (_dot(h_scr[...], w_ref[...]) + b_ref[...]).astype(BF16)


def _inproj(x2, g, sc, sh, w, b, ws, bs, seq):
    t, d = x2.shape
    tm, tn = 1024, 512
    per_b = seq // tm
    vec = lambda: pl.BlockSpec((None, 1, d), lambda i, j: (i // per_b, 0, 0))
    return pl.pallas_call(
        _inproj_kernel,
        grid=(t // tm, Z_COLS // tn),
        in_specs=[pl.BlockSpec((tm, d), lambda i, j: (i, 0)),
                  pl.BlockSpec((1, d), lambda i, j: (0, 0)),
                  vec(), vec(),
                  pl.BlockSpec((d, tn), lambda i, j: (0, j)),
                  pl.BlockSpec((1, tn), lambda i, j: (0, j)),
                  pl.BlockSpec((d, LANES), lambda i, j: (0, 0)),
                  pl.BlockSpec((1, LANES), lambda i, j: (0, 0))],
        out_specs=[pl.BlockSpec((tm, tn), lambda i, j: (i, j)),
                   pl.BlockSpec((tm, LANES), lambda i, j: (i, 0))],
        out_shape=[jax.ShapeDtypeStruct((t, Z_COLS), BF16), jax.ShapeDtypeStruct((t, LANES), F32)],
        scratch_shapes=[pltpu.VMEM((tm, d), BF16)],
        compiler_params=_cparams(2),
        name="inproj",
    )(x2, g, sc, sh, w, b, ws, bs)


def _prep_kernel(zs_ref, ltri_ref, lblk_ref, pq_ref, pk_ref, cq_ref, ck_ref, wa_ref, ba_ref,
                 aq_ref, ak_ref, bc_ref, carry_ref):
    @pl.when(pl.program_id(1) == 0)
    def _():
        carry_ref[...] = jnp.zeros_like(carry_ref)

    zs = zs_ref[...]
    cum = _lmul_exact(ltri_ref[...], _log_sigmoid(zs)) + carry_ref[...]
    carry_ref[...] = cum[cum.shape[0] - 1:, :]
    hi, mid, lo = _split3(cum * LOG2E)
    aq = cq_ref[...] + _dot(hi, pq_ref[0]) + _dot(mid, pq_ref[1]) + _dot(lo, pq_ref[2])
    ak = ck_ref[...] + _dot(hi, pk_ref[0]) + _dot(mid, pk_ref[1]) + _dot(lo, pk_ref[2])
    aq_ref[...] = aq.astype(BF16)
    ak_ref[...] = ak.astype(BF16)
    la = _log_sigmoid(_dot_f32(zs, wa_ref[...]) + ba_ref[...]) * (1.0 / GLA_TAU)
    bc_ref[...] = _lmul_exact(lblk_ref[...], la)


def _prep_consts(ts):
    r = np.arange(ts)
    ltri = (r[:, None] >= r[None, :]).astype(np.float32)
    lblk = ltri * (r[:, None] // GLA_CHUNK == r[None, :] // GLA_CHUNK)
    pq = np.zeros((3, LANES, FOX_DIM), np.float32)
    pk = np.zeros((3, LANES, FOX_DIM), np.float32)
    cq = np.zeros((1, FOX_DIM), np.float32)
    ck = np.zeros((1, FOX_DIM), np.float32)
    for h in range(FOX_HEADS):
        for p in range(3):
            pq[p, ZS_F + h, h * FOX_HEAD_DIM + p] = 1.0
            pk[p, ZS_F + h, h * FOX_HEAD_DIM + 3 + p] = -1.0
            cq[0, h * FOX_HEAD_DIM + 3 + p] = 1.0
            ck[0, h * FOX_HEAD_DIM + p] = 1.0
    return (jnp.asarray(ltri, BF16), jnp.asarray(lblk, BF16), jnp.asarray(pq, BF16), jnp.asarray(pk, BF16),
            jnp.asarray(cq), jnp.asarray(ck))


def _prep(zs, wa_pad, ba, seq):
    t = zs.shape[0]
    ts = 256
    per_b = seq // ts
    ltri, lblk, pq, pk, cq, ck = _prep_consts(ts)
    full = lambda a: pl.BlockSpec(a.shape, lambda b, s: (0,) * a.ndim)
    row = lambda n: pl.BlockSpec((ts, n), lambda b, s: (b * per_b + s, 0))
    return pl.pallas_call(
        _prep_kernel,
        grid=(t // seq, per_b),
        in_specs=[row(LANES), full(ltri), full(lblk), full(pq), full(pk), full(cq), full(ck),
                  full(wa_pad), full(ba)],
        out_specs=[row(FOX_DIM), row(FOX_DIM), row(GLA_KDIM)],
        out_shape=[jax.ShapeDtypeStruct((t, FOX_DIM), BF16), jax.ShapeDtypeStruct((t, FOX_DIM), BF16),
                   jax.ShapeDtypeStruct((t, GLA_KDIM), F32)],
        scratch_shapes=[pltpu.VMEM((1, LANES), F32)],
        compiler_params=_cparams(2),
        name="prep",
    )(zs, ltri, lblk, pq, pk, cq, ck, wa_pad, ba)


CONV_HIST = 32


def _conv_kernel(u_ref, w_ref, cb_ref, lg_ref, lb_ref, o_ref, ypad_ref):
    ts = u_ref.shape[0]

    @pl.when(pl.program_id(1) == 0)
    def _():
        ypad_ref[0:CONV_HIST, :] = jnp.zeros((CONV_HIST, CONV_DIM), F32)

    u = u_ref[...].astype(F32)
    ypad_ref[CONV_HIST:CONV_HIST + ts, :] = u[:, :CONV_DIM] * jax.nn.sigmoid(u[:, CONV_DIM:])
    acc = jnp.zeros((ts, CONV_DIM), F32) + cb_ref[...]
    first = CONV_HIST - (CONV_WIDTH - 1)
    for j in range(CONV_WIDTH):
        acc = acc + w_ref[j:j + 1, :] * ypad_ref[first + j:first + j + ts, :]
    ypad_ref[0:CONV_HIST, :] = ypad_ref[ts:ts + CONV_HIST, :]
    mu = jnp.mean(acc, axis=-1, keepdims=True)
    cen = acc - mu
    var = jnp.mean(cen * cen, axis=-1, keepdims=True)
    y = cen * lax.rsqrt(var + EPS) * lg_ref[...] + lb_ref[...]
    o_ref[...] = _silu(y).astype(BF16)


def _conv(z, w, cb, lg, lb, seq):
    t = z.shape[0]
    ts = 512
    per_b = seq // ts
    full = lambda a: pl.BlockSpec(a.shape, lambda b, s: (0,) * a.ndim)
    return pl.pallas_call(
        _conv_kernel,
        grid=(t // seq, per_b),
        in_specs=[pl.BlockSpec((ts, 2 * CONV_DIM), lambda b, s: (b * per_b + s, Z_CONV * LANES // (2 * CONV_DIM))),
                  full(w), full(cb), full(lg), full(lb)],
        out_specs=pl.BlockSpec((ts, CONV_DIM), lambda b, s: (b * per_b + s, 0)),
        out_shape=jax.ShapeDtypeStruct((t, CONV_DIM), BF16),
        scratch_shapes=[pltpu.VMEM((ts + CONV_HIST, CONV_DIM), F32)],
        compiler_params=_cparams(2),
        name="conv",
    )(z, w, cb, lg, lb)


def _fox_kernel(q_ref, aq_ref, k_ref, ak_ref, v_ref, o_ref, kk_ref, vt_ref, m_ref, acc_ref, *, blk):
    i = pl.program_id(2)
    dh = FOX_HEAD_DIM

    @pl.when(i == 0)
    def _():
        def fill(cb, carry):
            rows = pl.ds(pl.multiple_of(cb * blk, blk), blk)
            kk_ref[rows, 0:dh] = k_ref[rows, :]
            kk_ref[rows, dh:2 * dh] = ak_ref[rows, :]
            vt_ref[cb, 0:dh, :] = v_ref[rows, :].astype(F32).T.astype(BF16)
            vt_ref[cb, dh:2 * dh, :] = jnp.ones((dh, blk), BF16)
            return carry
        lax.fori_loop(0, k_ref.shape[0] // blk, fill, 0)

    qq = jnp.concatenate([q_ref[...], aq_ref[...]], axis=1)
    m_ref[...] = jnp.full(m_ref.shape, NEG, F32)
    acc_ref[...] = jnp.zeros(acc_ref.shape, F32)

    def step(j, masked):
        rows = pl.ds(pl.multiple_of(j * blk, blk), blk)
        st = _dot_nt(kk_ref[rows, :], qq)
        if masked:
            kv = lax.broadcasted_iota(I32, (blk, blk), 0)
            qi = lax.broadcasted_iota(I32, (blk, blk), 1)
            st = jnp.where(kv <= qi, st, NEG)
        m_old = m_ref[...]
        m_new = jnp.maximum(m_old, jnp.max(st, axis=0, keepdims=True))
        p = jnp.exp2(st - m_new).astype(BF16)
        acc_ref[...] = acc_ref[...] * jnp.exp2(m_old - m_new) + _dot(vt_ref[j], p)
        m_ref[...] = m_new

    def body(j, carry):
        step(j, False)
        return carry

    lax.fori_loop(0, i, body, 0)
    step(i, True)
    acc = acc_ref[...]
    o_ref[...] = (acc[:dh, :] / acc[dh:, :]).T.astype(BF16)


def _fox(z, aq, ak, seq):
    t = z.shape[0]
    blk = min(512, seq)
    nq = seq // blk
    qspec = lambda base: pl.BlockSpec((blk, LANES), lambda b, h, i: (b * nq + i, base + h))
    kspec = lambda base: pl.BlockSpec((seq, LANES), lambda b, h, i: (b, base + h))
    return pl.pallas_call(
        functools.partial(_fox_kernel, blk=blk),
        grid=(t // seq, FOX_HEADS, nq),
        in_specs=[qspec(Z_QF), qspec(0), kspec(Z_KF), kspec(0), kspec(Z_VF)],
        out_specs=qspec(0),
        out_shape=jax.ShapeDtypeStruct((t, FOX_DIM), BF16),
        scratch_shapes=[pltpu.VMEM((seq, 2 * FOX_HEAD_DIM), BF16), pltpu.VMEM((nq, 2 * FOX_HEAD_DIM, blk), BF16),
                        pltpu.VMEM((1, blk), F32), pltpu.VMEM((2 * FOX_HEAD_DIM, blk), F32)],
        compiler_params=_cparams(3),
        name="fox",
    )(z, aq, z, ak, z)


def _gla_chunk(q, k, v, b, st, e3):
    c, nsub = GLA_CHUNK, GLA_CHUNK // GLA_SUB
    row = lax.broadcasted_iota(I32, (c, GLA_KDIM), 0)
    rblk, rloc = row // GLA_SUB, row % GLA_SUB
    lane_k = lax.broadcasted_iota(I32, (c, GLA_KDIM), 1)
    khead, kpos = lane_k // GLA_DK, lane_k % GLA_DK
    b3 = b.reshape(nsub, GLA_SUB, GLA_KDIM)
    k3 = k.reshape(nsub, GLA_SUB, GLA_KDIM)
    last = b3[:, GLA_SUB - 1:GLA_SUB, :]
    bcast = lambda a3, i: jnp.broadcast_to(a3[:, i:i + 1, :], (nsub, GLA_SUB, GLA_KDIM)).reshape(c, GLA_KDIM)

    prev = jnp.concatenate([last[:1], last[:nsub - 1]], axis=0)
    refq = jnp.broadcast_to(prev, (nsub, GLA_SUB, GLA_KDIM)).reshape(c, GLA_KDIM)
    qt = q * jnp.exp(jnp.minimum(b - refq, 0.0))
    qcat = jnp.concatenate([jnp.where(rblk == i, qt, 0.0) for i in range(1, nsub)], axis=1).astype(BF16)
    kts = []
    for i in range(1, nsub):
        kt = k * jnp.exp(jnp.minimum(last[i - 1] - b, 0.0))
        kts.append(jnp.where(row < GLA_SUB * i, kt, 0.0))
    kcat = jnp.concatenate(
        [jnp.concatenate([jnp.where(khead == h, kt, 0.0) for kt in kts], axis=1) for h in range(GLA_HEADS)],
        axis=0).astype(BF16)
    a_all = _dot_nt(qcat, kcat)

    for sl in range(GLA_SUB):
        p = q * bcast(k3, sl) * jnp.exp(jnp.minimum(b - bcast(b3, sl), 0.0))
        w = _dot(p.astype(BF16), e3)
        hit = (kpos == rblk * GLA_SUB + sl) & (rloc >= sl)
        a_all = a_all + jnp.where(hit, w, 0.0)

    lane_v = lax.broadcasted_iota(I32, (c, GLA_VDIM), 1) // GLA_DV
    vbd = jnp.concatenate([jnp.where(lane_v == h, v, jnp.zeros_like(v)) for h in range(GLA_HEADS)], axis=0)
    qe = (q * jnp.exp(b)).astype(BF16)
    o = _dot(a_all.astype(BF16), vbd) + _dot_nt(qe, st.astype(BF16))

    blast = b[c - 1:c, :]
    kd = (k * jnp.exp(blast - b)).astype(BF16)
    shead_v = lax.broadcasted_iota(I32, (GLA_VDIM, GLA_KDIM), 0) // GLA_DV
    shead_k = lax.broadcasted_iota(I32, (GLA_VDIM, GLA_KDIM), 1) // GLA_DK
    st_new = jnp.where(shead_v == shead_k, st * jnp.exp(blast) + _dot_tn(v, kd), 0.0)
    return o, st_new


def _gla_kernel(q_ref, k_ref, v_ref, r_ref, b_ref, ng_ref, e3_ref, o_ref, st_ref):
    @pl.when(pl.program_id(1) == 0)
    def _():
        st_ref[...] = jnp.zeros_like(st_ref)

    e3 = e3_ref[...]
    ng = ng_ref[...]

    def body(ci, carry):
        rows = pl.ds(pl.multiple_of(ci * GLA_CHUNK, GLA_CHUNK), GLA_CHUNK)
        o, st_new = _gla_chunk(q_ref[rows, :].astype(F32), k_ref[rows, :].astype(F32), v_ref[rows, :],
                               b_ref[rows, :], st_ref[...], e3)
        st_ref[...] = st_new
        outs = []
        for h in range(GLA_HEADS):
            oh = o[:, h * GLA_DV:(h + 1) * GLA_DV]
            outs.append(oh * lax.rsqrt(jnp.mean(oh * oh, axis=-1, keepdims=True) + EPS))
        y = jnp.concatenate(outs, axis=1) * ng * _silu(r_ref[rows, :].astype(F32))
        o_ref[rows, :] = y.astype(BF16)
        return carry

    lax.fori_loop(0, q_ref.shape[0] // GLA_CHUNK, body, 0)


def _gla(z, bc, ng, seq):
    t = z.shape[0]
    ts = min(512, seq)
    per_b = seq // ts
    lane = np.arange(GLA_KDIM)
    e3 = jnp.asarray((lane[:, None] // GLA_DK == lane[None, :] // GLA_DK).astype(np.float32), BF16)
    spec = lambda n, unit: pl.BlockSpec((ts, n), lambda b, s: (b * per_b + s, unit))
    full = lambda a: pl.BlockSpec(a.shape, lambda b, s: (0,) * a.ndim)
    return pl.pallas_call(
        _gla_kernel,
        grid=(t // seq, per_b),
        in_specs=[spec(GLA_KDIM, Z_QG * LANES // GLA_KDIM), spec(GLA_KDIM, Z_KG * LANES // GLA_KDIM),
                  spec(GLA_VDIM, Z_VG * LANES // GLA_VDIM), spec(GLA_VDIM, Z_RG * LANES // GLA_VDIM),
                  spec(GLA_KDIM, 0), full(ng), full(e3)],
        out_specs=spec(GLA_VDIM, 0),
        out_shape=jax.ShapeDtypeStruct((t, GLA_VDIM), BF16),
        scratch_shapes=[pltpu.VMEM((GLA_VDIM, GLA_KDIM), F32)],
        compiler_params=_cparams(2),
        name="gla",
    )(z, z, z, z, bc, ng, e3)


def _merge_kernel(x_ref, g1_ref, ya_ref, yb_ref, yc_ref, ga_ref, gb_ref, gc_ref,
                  wa_ref, wb_ref, wc_ref, wo_ref, o_ref):
    gate = lambda r: jax.nn.sigmoid(r[...].astype(F32))
    merged = (gate(ga_ref) * _dot(ya_ref[...], wa_ref[...]) + gate(gb_ref) * _dot(yb_ref[...], wb_ref[...])
              + gate(gc_ref) * _dot(yc_ref[...], wc_ref[...]))
    o_ref[...] = x_ref[...] + g1_ref[...] * _dot(merged.astype(BF16), wo_ref[...])


def _merge(x2, g1, ya, yb, yc, z, wa, wb, wc, wo, seq):
    t, d = x2.shape
    tm = 256
    per_b = seq // tm
    full = lambda a: pl.BlockSpec(a.shape, lambda i: (0,) * a.ndim)
    row = lambda n, unit=0: pl.BlockSpec((tm, n), lambda i: (i, unit))
    return pl.pallas_call(
        _merge_kernel,
        grid=(t // tm,),
        in_specs=[row(d), pl.BlockSpec((None, 1, d), lambda i: (i // per_b, 0, 0)),
                  row(CONV_DIM), row(FOX_DIM), row(GLA_VDIM), row(d, 0), row(d, 1), row(d, 2),
                  full(wa), full(wb), full(wc), full(wo)],
        out_specs=row(d),
        out_shape=jax.ShapeDtypeStruct((t, d), F32),
        compiler_params=_cparams(1),
        name="merge",
    )(x2, g1, ya, yb, yc, z, z, z, wa, wb, wc, wo)


def _pack_halves(y):
    n = y.shape[1] // 2
    lo = pltpu.bitcast(y[:, :n].astype(BF16).astype(F32), U32)
    hi = pltpu.bitcast(y[:, n:].astype(BF16).astype(F32), U32)
    return (hi & jnp.uint32(0xFFFF0000)) | (lo >> 16)


def _unpack_halves(p):
    lo = pltpu.bitcast(p << 16, F32)
    hi = pltpu.bitcast(p & jnp.uint32(0xFFFF0000), F32)
    return lo, hi


def _first_index(hit, idx, n):
    return jnp.min(jnp.where(hit, idx, n), axis=0, keepdims=True)


def _moe_pre_kernel(x_ref, g_ref, sc_ref, sh_ref, g2_ref, wrh_ref, wrl_ref, eb_ref, ws1_ref, ws3_ref, ws2_ref,
                    u_ref, hp_ref, base_ref, te_ref, tw_ref, rk_ref, cnt_ref, carry_ref):
    @pl.when(pl.program_id(0) == 0)
    def _():
        carry_ref[...] = jnp.zeros_like(carry_ref)

    x = x_ref[...]
    h = _modulated_norm(x, g_ref[...], sc_ref[...], sh_ref[...])
    hb = h.astype(BF16)
    hp_ref[...] = _pack_halves(h)

    hs = (_silu(_dot(hb, ws1_ref[...])) * _dot(hb, ws3_ref[...])).astype(BF16)
    base_ref[...] = x + g2_ref[...] * _dot(hs, ws2_ref[...])

    hlo = (h - hb.astype(F32)).astype(BF16)
    logits = _dot_nt(wrh_ref[...], hb) + _dot_nt(wrh_ref[...], hlo) + _dot_nt(wrl_ref[...], hb)
    scores = jax.nn.sigmoid(logits)
    biased = scores + eb_ref[...]
    tm = x.shape[0]
    per_group = N_EXPERTS // N_GROUPS
    sub = lax.broadcasted_iota(I32, (per_group, tm), 0).astype(F32)
    gscore = []
    for g in range(N_GROUPS):
        blk = biased[g * per_group:(g + 1) * per_group, :]
        m1 = jnp.max(blk, axis=0, keepdims=True)
        rest = jnp.where(sub == _first_index(blk == m1, sub, float(per_group)), -jnp.inf, blk)
        gscore.append(m1 + jnp.max(rest, axis=0, keepdims=True))
    gs = jnp.concatenate(gscore, axis=0)
    gidx = lax.broadcasted_iota(I32, (N_GROUPS, tm), 0).astype(F32)
    gsel = jnp.zeros((N_GROUPS, tm), F32)
    for _ in range(TOPK_GROUPS):
        m = jnp.max(gs, axis=0, keepdims=True)
        hit = gidx == _first_index(gs == m, gidx, float(N_GROUPS))
        gsel = jnp.where(hit, 1.0, gsel)
        gs = jnp.where(hit, -jnp.inf, gs)
    cand = jnp.concatenate(
        [jnp.where(gsel[g:g + 1, :] > 0.5, biased[g * per_group:(g + 1) * per_group, :], -jnp.inf)
         for g in range(N_GROUPS)], axis=0)
    eidx = lax.broadcasted_iota(I32, (N_EXPERTS, tm), 0).astype(F32)
    hits, tops, tws = [], [], []
    sel = jnp.zeros((N_EXPERTS, tm), F32)
    for _ in range(TOP_K):
        m = jnp.max(cand, axis=0, keepdims=True)
        e = _first_index(cand == m, eidx, float(N_EXPERTS))
        hit = eidx == e
        hits.append(hit)
        tops.append(e)
        tws.append(jnp.sum(jnp.where(hit, scores, 0.0), axis=0, keepdims=True))
        cand = jnp.where(hit, -jnp.inf, cand)
        sel = jnp.where(hit, 1.0, sel)
    te_ref[...] = jnp.concatenate(tops, axis=0).astype(I32)
    tw = jnp.concatenate(tws, axis=0)
    tw_ref[...] = tw / jnp.sum(tw, axis=0, keepdims=True) * ROUTED_SCALE

    before = _dot(sel.astype(BF16), u_ref[...]) + carry_ref[...]
    rk_ref[...] = jnp.concatenate(
        [jnp.sum(jnp.where(hit, before, 0.0), axis=0, keepdims=True) for hit in hits], axis=0).astype(I32)
    carry = carry_ref[...] + jnp.sum(sel, axis=1, keepdims=True)
    carry_ref[...] = carry
    cnt_ref[...] = jnp.broadcast_to(carry, cnt_ref.shape).astype(I32)


def _moe_pre(x2, g, sc, sh, g2, wrh, wrl, eb, ws1, ws3, ws2, seq):
    t, d = x2.shape
    tm = 256
    per_b = seq // tm
    r = np.arange(tm)
    u = jnp.asarray((r[:, None] < r[None, :]).astype(np.float32), BF16)
    full = lambda a: pl.BlockSpec(a.shape, lambda i: (0,) * a.ndim)
    vec = lambda: pl.BlockSpec((None, 1, d), lambda i: (i // per_b, 0, 0))
    colk = lambda: pl.BlockSpec((TOP_K, tm), lambda i: (0, i))
    return pl.pallas_call(
        _moe_pre_kernel,
        grid=(t // tm,),
        in_specs=[pl.BlockSpec((tm, d), lambda i: (i, 0)), full(g), vec(), vec(), vec(),
                  full(wrh), full(wrl), full(eb), full(ws1), full(ws3), full(ws2), full(u)],
        out_specs=[pl.BlockSpec((tm, d // 2), lambda i: (i, 0)), pl.BlockSpec((tm, d), lambda i: (i, 0)),
                   colk(), colk(), colk(), pl.BlockSpec((N_EXPERTS, LANES), lambda i: (0, 0))],
        out_shape=[jax.ShapeDtypeStruct((t, d // 2), U32), jax.ShapeDtypeStruct((t, d), F32),
                   jax.ShapeDtypeStruct((TOP_K, t), I32), jax.ShapeDtypeStruct((TOP_K, t), F32),
                   jax.ShapeDtypeStruct((TOP_K, t), I32), jax.ShapeDtypeStruct((N_EXPERTS, LANES), I32)],
        scratch_shapes=[pltpu.VMEM((N_EXPERTS, 1), F32)],
        compiler_params=_cparams(1),
        name="moe_pre",
    )(x2, g, sc, sh, g2, wrh, wrl, eb, ws1, ws3, ws2, u)


def _row_copy(src_hbm, src_row, dst_buf, slot, dst_row, sem):
    return pltpu.make_async_copy(src_hbm.at[pl.ds(src_row, 1)], dst_buf.at[slot, pl.ds(dst_row, 1)], sem.at[slot])


def _row_store(src_buf, slot, src_row, dst_hbm, dst_row, sem):
    return pltpu.make_async_copy(src_buf.at[slot, pl.ds(src_row, 1)], dst_hbm.at[pl.ds(dst_row, 1)], sem.at[slot])


def _experts_kernel(be_ref, tok_ref, tokn_ref, dstp_ref, dstl_ref, hp_hbm, w1_ref, w3_ref, w2_ref, out_hbm,
                    xbuf, ybuf, w1b, w3b, w2b, gsem, ssem):
    s = pl.program_id(0)
    last = pl.num_programs(0) - 1
    slot = s % 2

    def gather_start(idx_ref, sl):
        for r in range(MOE_BLOCK):
            _row_copy(hp_hbm, idx_ref[0, 0, r], xbuf, sl, r, gsem).start()

    def gather_wait(sl):
        for r in range(MOE_BLOCK):
            _row_copy(hp_hbm, 0, xbuf, sl, r, gsem).wait()

    def scatter_start(idx_ref, sl):
        for r in range(MOE_BLOCK):
            _row_store(ybuf, sl, r, out_hbm, idx_ref[0, 0, r], ssem).start()

    def scatter_wait(sl):
        for r in range(MOE_BLOCK):
            _row_store(ybuf, sl, r, out_hbm, 0, ssem).wait()

    @pl.when(s == 0)
    def _():
        gather_start(tok_ref, 0)

    @pl.when(s >= 2)
    def _():
        scatter_wait(slot)

    @pl.when((s == 0) | (be_ref[s] != be_ref[jnp.maximum(s - 1, 0)]))
    def _():
        w1b[...] = w1_ref[...].astype(BF16)
        w3b[...] = w3_ref[...].astype(BF16)
        w2b[...] = w2_ref[...].astype(BF16)

    def main(scatter_prev):
        gather_wait(slot)
        gather_start(tokn_ref, 1 - slot)
        if scatter_prev:
            scatter_start(dstp_ref, 1 - slot)
        lo, hi = _unpack_halves(xbuf[slot])
        xb = jnp.concatenate([lo.astype(BF16), hi.astype(BF16)], axis=1)
        hid = (_silu(_dot(xb, w1b[...])) * _dot(xb, w3b[...])).astype(BF16)
        ybuf[slot] = _pack_halves(_dot(hid, w2b[...]))

    pl.when(s == 0)(functools.partial(main, False))
    pl.when(s > 0)(functools.partial(main, True))

    @pl.when(s == last)
    def _():
        scatter_start(dstl_ref, slot)
        gather_wait(1 - slot)
        scatter_wait(1 - slot)
        scatter_wait(slot)


def _experts(layer, block_e, row_tok, row_dst, hp, w1, w3, w2, n_out_rows):
    nb = row_tok.shape[0]
    d2 = hp.shape[1]
    idx = lambda f: pl.BlockSpec((1, 1, MOE_BLOCK), f, memory_space=pltpu.SMEM)
    wspec = lambda a: pl.BlockSpec((None, None) + a.shape[2:], lambda s, be: (layer, be[s], 0, 0))
    return pl.pallas_call(
        _experts_kernel,
        grid_spec=pltpu.PrefetchScalarGridSpec(
            num_scalar_prefetch=1,
            grid=(nb,),
            in_specs=[idx(lambda s, be: (s, 0, 0)),
                      idx(lambda s, be: (jnp.minimum(s + 1, nb - 1), 0, 0)),
                      idx(lambda s, be: (jnp.maximum(s - 1, 0), 0, 0)),
                      idx(lambda s, be: (nb - 1, 0, 0)),
                      pl.BlockSpec(memory_space=pl.ANY),
                      wspec(w1), wspec(w3), wspec(w2)],
            out_specs=pl.BlockSpec(memory_space=pl.ANY),
            scratch_shapes=[pltpu.VMEM((2, MOE_BLOCK, d2), U32), pltpu.VMEM((2, MOE_BLOCK, d2), U32),
                            pltpu.VMEM(w1.shape[2:], BF16), pltpu.VMEM(w3.shape[2:], BF16),
                            pltpu.VMEM(w2.shape[2:], BF16),
                            pltpu.SemaphoreType.DMA((2,)), pltpu.SemaphoreType.DMA((2,))]),
        out_shape=jax.ShapeDtypeStruct((n_out_rows, d2), U32),
        compiler_params=_cparams(1),
        name="experts",
    )(block_e, row_tok, row_tok, row_dst, row_dst, hp, w1, w3, w2)


def _combine_kernel(base_ref, g2_ref, w_ref, s_ref, fg_ref, o_ref, *, final):
    w = w_ref[...]
    acc_lo = acc_hi = None
    for k in range(TOP_K):
        lo, hi = _unpack_halves(s_ref[k])
        wk = w[:, k:k + 1]
        acc_lo = wk * lo if acc_lo is None else acc_lo + wk * lo
        acc_hi = wk * hi if acc_hi is None else acc_hi + wk * hi
    x = base_ref[...] + g2_ref[...] * jnp.concatenate([acc_lo, acc_hi], axis=1)
    if final:
        x = x * lax.rsqrt(jnp.mean(x * x, axis=-1, keepdims=True) + EPS) * fg_ref[...]
    o_ref[...] = x


def _combine(base, g2, w, slots3, fg, seq, final):
    t, d = base.shape
    tm = 256
    per_b = seq // tm
    return pl.pallas_call(
        functools.partial(_combine_kernel, final=final),
        grid=(t // tm,),
        in_specs=[pl.BlockSpec((tm, d), lambda i: (i, 0)),
                  pl.BlockSpec((None, 1, d), lambda i: (i // per_b, 0, 0)),
                  pl.BlockSpec((tm, TOP_K), lambda i: (i, 0)),
                  pl.BlockSpec((TOP_K, tm, d // 2), lambda i: (0, i, 0)),
                  pl.BlockSpec((1, d), lambda i: (0, 0))],
        out_specs=pl.BlockSpec((tm, d), lambda i: (i, 0)),
        out_shape=jax.ShapeDtypeStruct((t, d), F32),
        compiler_params=_cparams(1),
        name="combine",
    )(base, g2, w, slots3, fg)


TRASH_ROWS = 64


def _dispatch_tables(top_e, rank, counts, t):
    tk = TOP_K * t
    n_blocks = tk // MOE_BLOCK + N_EXPERTS
    n_rows = n_blocks * MOE_BLOCK
    plane = t + TRASH_ROWS
    padded = (counts + MOE_BLOCK - 1) // MOE_BLOCK * MOE_BLOCK
    pends = jnp.cumsum(padded)
    pstarts = pends - padded
    experts = jnp.arange(N_EXPERTS, dtype=I32)
    pstart_of = jnp.sum(jnp.where(top_e[..., None] == experts, pstarts, 0), axis=-1)
    dest = (pstart_of + rank).reshape(tk)
    row_pair = jnp.full((n_rows,), -1, I32).at[dest].set(jnp.arange(tk, dtype=I32))
    valid = row_pair >= 0
    r = jnp.arange(n_rows, dtype=I32)
    pos, par = r % MOE_BLOCK, (r // MOE_BLOCK) % 2
    trash = (pos % TOP_K) * plane + t + pos // TOP_K + par * (TRASH_ROWS // 2)
    row_tok = jnp.where(valid, row_pair % t, 0)
    row_dst = jnp.where(valid, (row_pair // t) * plane + row_pair % t, trash)
    starts = jnp.arange(n_blocks, dtype=I32) * MOE_BLOCK
    block_e = jnp.sum((pends[None, :] <= starts[:, None]).astype(I32), axis=1)
    block_e = jnp.minimum(block_e, jnp.max(jnp.where(starts < pends[-1], block_e, 0)))
    return block_e, row_tok.reshape(n_blocks, 1, MOE_BLOCK), row_dst.reshape(n_blocks, 1, MOE_BLOCK), plane


def _layout_inproj(w_in, b_in):
    o = np.cumsum([0, 2 * CONV_DIM, FOX_DIM, FOX_DIM, FOX_DIM, FOX_HEADS, GLA_KDIM, GLA_KDIM, GLA_VDIM, GLA_VDIM,
                   GLA_RANK, 3 * D_MODEL])
    seg = lambda a, i: a[..., int(o[i]):int(o[i + 1])]

    def main(a):
        return jnp.concatenate([seg(a, 10), seg(a, 0), seg(a, 1) * (FOX_HEAD_DIM ** -0.5 * LOG2E), seg(a, 2), seg(a, 3),
                                seg(a, 5) * GLA_DK ** -0.5, seg(a, 6), seg(a, 7), seg(a, 8)], axis=-1)

    def small(a):
        pad = jnp.zeros(a.shape[:-1] + (LANES - FOX_HEADS - GLA_RANK,), a.dtype)
        return jnp.concatenate([seg(a, 4), seg(a, 9), pad], axis=-1)

    return main(w_in).astype(BF16), main(b_in)[None, :], small(w_in).astype(BF16), small(b_in)[None, :]


def kernel(x, c, norm1_g, ada_w, ada_b, w_in, b_in, conv_w, conv_b, conv_ln_g, conv_ln_b, gla_wa, gla_ba,
           gla_norm_g, w_branch_a, w_branch_b, w_branch_c, w_out, norm2_g, w_router, e_bias, w1, w3, w2,
           ws1, ws3, ws2, final_g):
    bsz, seq, d = x.shape
    t = bsz * seq
    depth = ada_w.shape[0]
    mod = _ada_mod(c, ada_w, ada_b).reshape(depth, bsz, 6, 1, d)
    x2 = x.reshape(t, d)
    row = lambda a: a.reshape(1, -1)
    for l in range(depth):
        sh1, sc1, g1, sh2, sc2, g2 = (mod[l, :, i] for i in range(6))
        w, b, ws, bs = _layout_inproj(w_in[l], b_in[l])
        z, zs = _inproj(x2, row(norm1_g[l]), sc1, sh1, w, b, ws, bs, seq)
        wa_pad = jnp.zeros((LANES, GLA_KDIM), F32).at[ZS_A:ZS_A + GLA_RANK].set(gla_wa[l])
        aq, ak, bc = _prep(zs, wa_pad, row(gla_ba[l]), seq)
        ya = _conv(z, conv_w[l].reshape(CONV_WIDTH, CONV_DIM), row(conv_b[l]), row(conv_ln_g[l]),
                   row(conv_ln_b[l]), seq)
        yb = _fox(z, aq, ak, seq)
        yc = _gla(z, bc, row(gla_norm_g[l]), seq)
        x2 = _merge(x2, g1, ya, yb, yc, z, w_branch_a[l].astype(BF16), w_branch_b[l].astype(BF16),
                    w_branch_c[l].astype(BF16), w_out[l].astype(BF16), seq)

        wr_t = w_router[l].T
        wrh = wr_t.astype(BF16)
        wrl = (wr_t - wrh.astype(F32)).astype(BF16)
        hp, base, top_e, top_w, rank, counts = _moe_pre(
            x2, row(norm2_g[l]), sc2, sh2, g2, wrh, wrl, e_bias[l].reshape(N_EXPERTS, 1),
            ws1[l].astype(BF16), ws3[l].astype(BF16), ws2[l].astype(BF16), seq)
        block_e, row_tok, row_dst, plane = _dispatch_tables(top_e, rank, counts[:, 0], t)
        slots = _experts(l, block_e, row_tok, row_dst, hp, w1, w3, w2, TOP_K * plane)
        x2 = _combine(base, g2, top_w.T, slots.reshape(TOP_K, plane, d // 2), row(final_g), seq,
                      final=(l == depth - 1))
    return x2.reshape(bsz, seq, d)
```

```python
import functools

import numpy as np
import jax
import jax.numpy as jnp
from jax import lax
from jax.experimental import pallas as pl
from jax.experimental.pallas import tpu as pltpu

F32 = jnp.float32
BF16 = jnp.bfloat16
U32 = jnp.uint32
I32 = jnp.int32

D_MODEL = 2048
CONV_DIM = 512
CONV_WIDTH = 31
FOX_HEADS = 8
FOX_HEAD_DIM = 128
FOX_DIM = FOX_HEADS * FOX_HEAD_DIM
GLA_HEADS = 4
GLA_DK = 64
GLA_DV = 128
GLA_KDIM = GLA_HEADS * GLA_DK
GLA_VDIM = GLA_HEADS * GLA_DV
GLA_RANK = 16
GLA_TAU = 16.0
GLA_CHUNK = 64
GLA_SUB = 8
N_EXPERTS = 64
N_GROUPS = 8
TOPK_GROUPS = 4
TOP_K = 8
EXPERT_DIM = 512
SHARED_DIM = 512
ROUTED_SCALE = 2.5
MOE_BLOCK = 256
EPS = 1e-6

LANES = 128
V7X_VMEM_LIMIT = 56 * 1024 * 1024

Z_GATE = 0
Z_CONV = 48
Z_QF = 56
Z_KF = 64
Z_VF = 72
Z_QG = 80
Z_KG = 82
Z_VG = 84
Z_RG = 88
Z_COLS = 92 * LANES
ZS_F = 0
ZS_A = 8
NEG = -1e30
LOG2E = 1.4426950408889634


def _cparams(n_axes, vmem=V7X_VMEM_LIMIT):
    return pltpu.CompilerParams(dimension_semantics=("arbitrary",) * n_axes, vmem_limit_bytes=vmem)


def _log_sigmoid(x):
    return jnp.minimum(x, 0.0) - jnp.log(1.0 + jnp.exp(-jnp.abs(x)))


def _silu(x):
    return x * jax.nn.sigmoid(x)


def _split3(x):
    hi = x.astype(BF16)
    r1 = x - hi.astype(F32)
    mid = r1.astype(BF16)
    lo = (r1 - mid.astype(F32)).astype(BF16)
    return hi, mid, lo


def _dot(a, b):
    return jnp.dot(a, b, preferred_element_type=F32)


def _dot_nt(a, b):
    return lax.dot_general(a, b, (((1,), (1,)), ((), ())), preferred_element_type=F32)


def _dot_tn(a, b):
    return lax.dot_general(a, b, (((0,), (0,)), ((), ())), preferred_element_type=F32)


def _lmul_exact(l01, x):
    hi, mid, lo = _split3(x)
    return _dot(l01, hi) + _dot(l01, mid) + _dot(l01, lo)


def _dot_f32(a, b):
    a0, a1, a2 = _split3(a)
    b0, b1, b2 = _split3(b)
    return (_dot(a0, b0) + _dot(a0, b1) + _dot(a1, b0)) + (_dot(a0, b2) + _dot(a1, b1) + _dot(a2, b0))


def _ada_kernel(ct_ref, w_ref, b_ref, o_ref):
    ct = ct_ref[...]
    act = _silu(ct)
    w = w_ref[...]
    rows = [jnp.sum(w * act[:, b:b + 1], axis=0, keepdims=True) for b in range(ct.shape[1])]
    o_ref[...] = jnp.concatenate(rows, axis=0) + b_ref[...]


def _ada_mod(c, ada_w, ada_b):
    n_l, d, n = ada_w.shape
    bsz = c.shape[0]
    tn = 512
    return pl.pallas_call(
        _ada_kernel,
        grid=(n_l, n // tn),
        in_specs=[pl.BlockSpec((d, bsz), lambda l, j: (0, 0)),
                  pl.BlockSpec((None, d, tn), lambda l, j: (l, 0, j)),
                  pl.BlockSpec((None, 1, tn), lambda l, j: (l, 0, j))],
        out_specs=pl.BlockSpec((None, bsz, tn), lambda l, j: (l, 0, j)),
        out_shape=jax.ShapeDtypeStruct((n_l, bsz, n), F32),
        compiler_params=_cparams(2),
        name="ada_mod",
    )(c.T, ada_w, ada_b.reshape(n_l, 1, n))


def _modulated_norm(x, g, sc, sh):
    ms = jnp.mean(x * x, axis=-1, keepdims=True)
    return (x * lax.rsqrt(ms + EPS) * g) * (1.0 + sc) + sh


def _inproj_kernel(x_ref, g_ref, sc_ref, sh_ref, w_ref, b_ref, ws_ref, bs_ref, z_ref, zs_ref, h_scr):
    @pl.when(pl.program_id(1) == 0)
    def _():
        h = _modulated_norm(x_ref[...], g_ref[...], sc_ref[...], sh_ref[...]).astype(BF16)
        h_scr[...] = h
        zs_ref[...] = _dot(h, ws_ref[...]) + bs_ref[...]

    z_ref[...] = (_dot(h_scr[...], w_ref[...]) + b_ref[...]).astype(BF16)


def _inproj(x2, g, sc, sh, w, b, ws, bs, seq):
    t, d = x2.shape
    tm, tn = 1024, 512
    per_b = seq // tm
    vec = lambda: pl.BlockSpec((None, 1, d), lambda i, j: (i // per_b, 0, 0))
    return pl.pallas_call(
        _inproj_kernel,
        grid=(t // tm, Z_COLS // tn),
        in_specs=[pl.BlockSpec((tm, d), lambda i, j: (i, 0)),
                  pl.BlockSpec((1, d), lambda i, j: (0, 0)),
                  vec(), vec(),
                  pl.BlockSpec((d, tn), lambda i, j: (0, j)),
                  pl.BlockSpec((1, tn), lambda i, j: (0, j)),
                  pl.BlockSpec((d, LANES), lambda i, j: (0, 0)),
                  pl.BlockSpec((1, LANES), lambda i, j: (0, 0))],
        out_specs=[pl.BlockSpec((tm, tn), lambda i, j: (i, j)),
                   pl.BlockSpec((tm, LANES), lambda i, j: (i, 0))],
        out_shape=[jax.ShapeDtypeStruct((t, Z_COLS), BF16), jax.ShapeDtypeStruct((t, LANES), F32)],
        scratch_shapes=[pltpu.VMEM((tm, d), BF16)],
        compiler_params=_cparams(2),
        name="inproj",
    )(x2, g, sc, sh, w, b, ws, bs)


def _prep_kernel(zs_ref, ltri_ref, lblk_ref, pq_ref, pk_ref, cq_ref, ck_ref, wa_ref, ba_ref,
                 aq_ref, ak_ref, bc_ref, carry_ref):
    @pl.when(pl.program_id(1) == 0)
    def _():
        carry_ref[...] = jnp.zeros_like(carry_ref)

    zs = zs_ref[...]
    cum = _lmul_exact(ltri_ref[...], _log_sigmoid(zs)) + carry_ref[...]
    carry_ref[...] = cum[cum.shape[0] - 1:, :]
    hi, mid, lo = _split3(cum * LOG2E)
    aq = cq_ref[...] + _dot(hi, pq_ref[0]) + _dot(mid, pq_ref[1]) + _dot(lo, pq_ref[2])
    ak = ck_ref[...] + _dot(hi, pk_ref[0]) + _dot(mid, pk_ref[1]) + _dot(lo, pk_ref[2])
    aq_ref[...] = aq.astype(BF16)
    ak_ref[...] = ak.astype(BF16)
    la = _log_sigmoid(_dot_f32(zs, wa_ref[...]) + ba_ref[...]) * (1.0 / GLA_TAU)
    bc_ref[...] = _lmul_exact(lblk_ref[...], la)


def _prep_consts(ts):
    r = np.arange(ts)
    ltri = (r[:, None] >= r[None, :]).astype(np.float32)
    lblk = ltri * (r[:, None] // GLA_CHUNK == r[None, :] // GLA_CHUNK)
    pq = np.zeros((3, LANES, FOX_DIM), np.float32)
    pk = np.zeros((3, LANES, FOX_DIM), np.float32)
    cq = np.zeros((1, FOX_DIM), np.float32)
    ck = np.zeros((1, FOX_DIM), np.float32)
    for h in range(FOX_HEADS):
        for p in range(3):
            pq[p, ZS_F + h, h * FOX_HEAD_DIM + p] = 1.0
            pk[p, ZS_F + h, h * FOX_HEAD_DIM + 3 + p] = -1.0
            cq[0, h * FOX_HEAD_DIM + 3 + p] = 1.0
            ck[0, h * FOX_HEAD_DIM + p] = 1.0
    return (jnp.asarray(ltri, BF16), jnp.asarray(lblk, BF16), jnp.asarray(pq, BF16), jnp.asarray(pk, BF16),
            jnp.asarray(cq), jnp.asarray(ck))


def _prep(zs, wa_pad, ba, seq):
    t = zs.shape[0]
    ts = 256
    per_b = seq // ts
    ltri, lblk, pq, pk, cq, ck = _prep_consts(ts)
    full = lambda a: pl.BlockSpec(a.shape, lambda b, s: (0,) * a.ndim)
    row = lambda n: pl.BlockSpec((ts, n), lambda b, s: (b * per_b + s, 0))
    return pl.pallas_call(
        _prep_kernel,
        grid=(t // seq, per_b),
        in_specs=[row(LANES), full(ltri), full(lblk), full(pq), full(pk), full(cq), full(ck),
                  full(wa_pad), full(ba)],
        out_specs=[row(FOX_DIM), row(FOX_DIM), row(GLA_KDIM)],
        out_shape=[jax.ShapeDtypeStruct((t, FOX_DIM), BF16), jax.ShapeDtypeStruct((t, FOX_DIM), BF16),
                   jax.ShapeDtypeStruct((t, GLA_KDIM), F32)],
        scratch_shapes=[pltpu.VMEM((1, LANES), F32)],
        compiler_params=_cparams(2),
        name="prep",
    )(zs, ltri, lblk, pq, pk, cq, ck, wa_pad, ba)


CONV_HIST = 32


def _conv_kernel(u_ref, w_ref, cb_ref, lg_ref, lb_ref, o_ref, ypad_ref):
    ts = u_ref.shape[0]

    @pl.when(pl.program_id(1) == 0)
    def _():
        ypad_ref[0:CONV_HIST, :] = jnp.zeros((CONV_HIST, CONV_DIM), F32)

    u = u_ref[...].astype(F32)
    ypad_ref[CONV_HIST:CONV_HIST + ts, :] = u[:, :CONV_DIM] * jax.nn.sigmoid(u[:, CONV_DIM:])
    acc = jnp.zeros((ts, CONV_DIM), F32) + cb_ref[...]
    first = CONV_HIST - (CONV_WIDTH - 1)
    for j in range(CONV_WIDTH):
        acc = acc + w_ref[j:j + 1, :] * ypad_ref[first + j:first + j + ts, :]
    ypad_ref[0:CONV_HIST, :] = ypad_ref[ts:ts + CONV_HIST, :]
    mu = jnp.mean(acc, axis=-1, keepdims=True)
    cen = acc - mu
    var = jnp.mean(cen * cen, axis=-1, keepdims=True)
    y = cen * lax.rsqrt(var + EPS) * lg_ref[...] + lb_ref[...]
    o_ref[...] = _silu(y).astype(BF16)


def _conv(z, w, cb, lg, lb, seq):
    t = z.shape[0]
    ts = 512
    per_b = seq // ts
    full = lambda a: pl.BlockSpec(a.shape, lambda b, s: (0,) * a.ndim)
    return pl.pallas_call(
        _conv_kernel,
        grid=(t // seq, per_b),
        in_specs=[pl.BlockSpec((ts, 2 * CONV_DIM), lambda b, s: (b * per_b + s, Z_CONV * LANES // (2 * CONV_DIM))),
                  full(w), full(cb), full(lg), full(lb)],
        out_specs=pl.BlockSpec((ts, CONV_DIM), lambda b, s: (b * per_b + s, 0)),
        out_shape=jax.ShapeDtypeStruct((t, CONV_DIM), BF16),
        scratch_shapes=[pltpu.VMEM((ts + CONV_HIST, CONV_DIM), F32)],
        compiler_params=_cparams(2),
        name="conv",
    )(z, w, cb, lg, lb)


FOX_HEADS_PER_STEP = 2


def _fox_kernel(q_ref, aq_ref, k_ref, ak_ref, v_ref, o_ref, kk_ref, vt_ref, m_ref, acc_ref, *, blk):
    i = pl.program_id(2)
    dh = FOX_HEAD_DIM
    heads = range(FOX_HEADS_PER_STEP)
    col = lambda h: slice(h * dh, (h + 1) * dh)

    @pl.when(i == 0)
    def _():
        def fill(cb, carry):
            rows = pl.ds(pl.multiple_of(cb * blk, blk), blk)
            for h in heads:
                kk_ref[h, rows, 0:dh] = k_ref[rows, col(h)]
                kk_ref[h, rows, dh:2 * dh] = ak_ref[rows, col(h)]
                vt_ref[h, cb, 0:dh, :] = v_ref[rows, col(h)].astype(F32).T.astype(BF16)
                vt_ref[h, cb, dh:2 * dh, :] = jnp.ones((dh, blk), BF16)
            return carry
        lax.fori_loop(0, k_ref.shape[0] // blk, fill, 0)

    qq = [jnp.concatenate([q_ref[:, col(h)], aq_ref[:, col(h)]], axis=1) for h in heads]
    m_ref[...] = jnp.full(m_ref.shape, NEG, F32)
    acc_ref[...] = jnp.zeros(acc_ref.shape, F32)

    def step(j, masked):
        rows = pl.ds(pl.multiple_of(j * blk, blk), blk)
        for h in heads:
            st = _dot_nt(kk_ref[h, rows, :], qq[h])
            if masked:
                kv = lax.broadcasted_iota(I32, (blk, blk), 0)
                qi = lax.broadcasted_iota(I32, (blk, blk), 1)
                st = jnp.where(kv <= qi, st, NEG)
            m_old = m_ref[h]
            m_new = jnp.maximum(m_old, jnp.max(st, axis=0, keepdims=True))
            p = jnp.exp2(st - m_new).astype(BF16)
            acc_ref[h] = acc_ref[h] * jnp.exp2(m_old - m_new) + _dot(vt_ref[h, j], p)
            m_ref[h] = m_new

    def body(j, carry):
        step(j, False)
        return carry

    lax.fori_loop(0, i, body, 0)
    step(i, True)
    for h in heads:
        acc = acc_ref[h]
        o_ref[:, col(h)] = (acc[:dh, :] / acc[dh:, :]).T.astype(BF16)


def _fox(z, aq, ak, seq):
    t = z.shape[0]
    blk = min(512, seq)
    nq = seq // blk
    hps = FOX_HEADS_PER_STEP
    width = hps * FOX_HEAD_DIM
    unit = lambda base: base * LANES // width
    qspec = lambda base: pl.BlockSpec((blk, width), lambda b, g, i: (b * nq + i, unit(base) + g))
    kspec = lambda base: pl.BlockSpec((seq, width), lambda b, g, i: (b, unit(base) + g))
    return pl.pallas_call(
        functools.partial(_fox_kernel, blk=blk),
        grid=(t // seq, FOX_HEADS // hps, nq),
        in_specs=[qspec(Z_QF), qspec(0), kspec(Z_KF), kspec(0), kspec(Z_VF)],
        out_specs=qspec(0),
        out_shape=jax.ShapeDtypeStruct((t, FOX_DIM), BF16),
        scratch_shapes=[pltpu.VMEM((hps, seq, 2 * FOX_HEAD_DIM), BF16),
                        pltpu.VMEM((hps, nq, 2 * FOX_HEAD_DIM, blk), BF16),
                        pltpu.VMEM((hps, 1, blk), F32), pltpu.VMEM((hps, 2 * FOX_HEAD_DIM, blk), F32)],
        compiler_params=_cparams(3),
        name="fox",
    )(z, aq, z, ak, z)


def _gla_chunk(q, k, v, b, st, e3):
    c, nsub = GLA_CHUNK, GLA_CHUNK // GLA_SUB
    row = lax.broadcasted_iota(I32, (c, GLA_KDIM), 0)
    rblk, rloc = row // GLA_SUB, row % GLA_SUB
    lane_k = lax.broadcasted_iota(I32, (c, GLA_KDIM), 1)
    khead, kpos = lane_k // GLA_DK, lane_k % GLA_DK
    b3 = b.reshape(nsub, GLA_SUB, GLA_KDIM)
    k3 = k.reshape(nsub, GLA_SUB, GLA_KDIM)
    last = b3[:, GLA_SUB - 1:GLA_SUB, :]
    bcast = lambda a3, i: jnp.broadcast_to(a3[:, i:i + 1, :], (nsub, GLA_SUB, GLA_KDIM)).reshape(c, GLA_KDIM)

    prev = jnp.concatenate([last[:1], last[:nsub - 1]], axis=0)
    refq = jnp.broadcast_to(prev, (nsub, GLA_SUB, GLA_KDIM)).reshape(c, GLA_KDIM)
    qt = q * jnp.exp(jnp.minimum(b - refq, 0.0))
    qcat = jnp.concatenate([jnp.where(rblk == i, qt, 0.0) for i in range(1, nsub)], axis=1).astype(BF16)
    kts = []
    for i in range(1, nsub):
        kt = k * jnp.exp(jnp.minimum(last[i - 1] - b, 0.0))
        kts.append(jnp.where(row < GLA_SUB * i, kt, 0.0))
    kcat = jnp.concatenate(
        [jnp.concatenate([jnp.where(khead == h, kt, 0.0) for kt in kts], axis=1) for h in range(GLA_HEADS)],
        axis=0).astype(BF16)
    a_all = _dot_nt(qcat, kcat)

    for sl in range(GLA_SUB):
        p = q * bcast(k3, sl) * jnp.exp(jnp.minimum(b - bcast(b3, sl), 0.0))
        w = _dot(p.astype(BF16), e3)
        hit = (kpos == rblk * GLA_SUB + sl) & (rloc >= sl)
        a_all = a_all + jnp.where(hit, w, 0.0)

    lane_v = lax.broadcasted_iota(I32, (c, GLA_VDIM), 1) // GLA_DV
    vbd = jnp.concatenate([jnp.where(lane_v == h, v, jnp.zeros_like(v)) for h in range(GLA_HEADS)], axis=0)
    qe = (q * jnp.exp(b)).astype(BF16)
    o = _dot(a_all.astype(BF16), vbd) + _dot_nt(qe, st.astype(BF16))

    blast = b[c - 1:c, :]
    kd = (k * jnp.exp(blast - b)).astype(BF16)
    shead_v = lax.broadcasted_iota(I32, (GLA_VDIM, GLA_KDIM), 0) // GLA_DV
    shead_k = lax.broadcasted_iota(I32, (GLA_VDIM, GLA_KDIM), 1) // GLA_DK
    st_new = jnp.where(shead_v == shead_k, st * jnp.exp(blast) + _dot_tn(v, kd), 0.0)
    return o, st_new


def _gla_kernel(q_ref, k_ref, v_ref, r_ref, b_ref, ng_ref, e3_ref, o_ref, st_ref):
    @pl.when(pl.program_id(1) == 0)
    def _():
        st_ref[...] = jnp.zeros_like(st_ref)

    e3 = e3_ref[...]
    ng = ng_ref[...]

    def body(ci, carry):
        rows = pl.ds(pl.multiple_of(ci * GLA_CHUNK, GLA_CHUNK), GLA_CHUNK)
        o, st_new = _gla_chunk(q_ref[rows, :].astype(F32), k_ref[rows, :].astype(F32), v_ref[rows, :],
                               b_ref[rows, :], st_ref[...], e3)
        st_ref[...] = st_new
        outs = []
        for h in range(GLA_HEADS):
            oh = o[:, h * GLA_DV:(h + 1) * GLA_DV]
            outs.append(oh * lax.rsqrt(jnp.mean(oh * oh, axis=-1, keepdims=True) + EPS))
        y = jnp.concatenate(outs, axis=1) * ng * _silu(r_ref[rows, :].astype(F32))
        o_ref[rows, :] = y.astype(BF16)
        return carry

    lax.fori_loop(0, q_ref.shape[0] // GLA_CHUNK, body, 0)


def _gla(z, bc, ng, seq):
    t = z.shape[0]
    ts = min(512, seq)
    per_b = seq // ts
    lane = np.arange(GLA_KDIM)
    e3 = jnp.asarray((lane[:, None] // GLA_DK == lane[None, :] // GLA_DK).astype(np.float32), BF16)
    spec = lambda n, unit: pl.BlockSpec((ts, n), lambda b, s: (b * per_b + s, unit))
    full = lambda a: pl.BlockSpec(a.shape, lambda b, s: (0,) * a.ndim)
    return pl.pallas_call(
        _gla_kernel,
        grid=(t // seq, per_b),
        in_specs=[spec(GLA_KDIM, Z_QG * LANES // GLA_KDIM), spec(GLA_KDIM, Z_KG * LANES // GLA_KDIM),
                  spec(GLA_VDIM, Z_VG * LANES // GLA_VDIM), spec(GLA_VDIM, Z_RG * LANES // GLA_VDIM),
                  spec(GLA_KDIM, 0), full(ng), full(e3)],
        out_specs=spec(GLA_VDIM, 0),
        out_shape=jax.ShapeDtypeStruct((t, GLA_VDIM), BF16),
        scratch_shapes=[pltpu.VMEM((GLA_VDIM, GLA_KDIM), F32)],
        compiler_params=_cparams(2),
        name="gla",
    )(z, z, z, z, bc, ng, e3)


def _merge_kernel(x_ref, g1_ref, ya_ref, yb_ref, yc_ref, ga_ref, gb_ref, gc_ref,
                  wa_ref, wb_ref, wc_ref, wo_ref, o_ref):
    gate = lambda r: jax.nn.sigmoid(r[...].astype(F32))
    merged = (gate(ga_ref) * _dot(ya_ref[...], wa_ref[...]) + gate(gb_ref) * _dot(yb_ref[...], wb_ref[...])
              + gate(gc_ref) * _dot(yc_ref[...], wc_ref[...]))
    o_ref[...] = x_ref[...] + g1_ref[...] * _dot(merged.astype(BF16), wo_ref[...])


def _merge(x2, g1, ya, yb, yc, z, wa, wb, wc, wo, seq):
    t, d = x2.shape
    tm = 256
    per_b = seq // tm
    full = lambda a: pl.BlockSpec(a.shape, lambda i: (0,) * a.ndim)
    row = lambda n, unit=0: pl.BlockSpec((tm, n), lambda i: (i, unit))
    return pl.pallas_call(
        _merge_kernel,
        grid=(t // tm,),
        in_specs=[row(d), pl.BlockSpec((None, 1, d), lambda i: (i // per_b, 0, 0)),
                  row(CONV_DIM), row(FOX_DIM), row(GLA_VDIM), row(d, 0), row(d, 1), row(d, 2),
                  full(wa), full(wb), full(wc), full(wo)],
        out_specs=row(d),
        out_shape=jax.ShapeDtypeStruct((t, d), F32),
        compiler_params=_cparams(1),
        name="merge",
    )(x2, g1, ya, yb, yc, z, z, z, wa, wb, wc, wo)


def _pack_halves(y):
    n = y.shape[1] // 2
    lo = pltpu.bitcast(y[:, :n].astype(BF16).astype(F32), U32)
    hi = pltpu.bitcast(y[:, n:].astype(BF16).astype(F32), U32)
    return (hi & jnp.uint32(0xFFFF0000)) | (lo >> 16)


def _unpack_halves(p):
    lo = pltpu.bitcast(p << 16, F32)
    hi = pltpu.bitcast(p & jnp.uint32(0xFFFF0000), F32)
    return lo, hi


def _first_index(hit, idx, n):
    return jnp.min(jnp.where(hit, idx, n), axis=0, keepdims=True)


def _moe_pre_kernel(x_ref, g_ref, sc_ref, sh_ref, g2_ref, wrh_ref, wrl_ref, eb_ref, ws1_ref, ws3_ref, ws2_ref,
                    u_ref, hp_ref, base_ref, te_ref, tw_ref, rk_ref, cnt_ref, carry_ref):
    @pl.when(pl.program_id(0) == 0)
    def _():
        carry_ref[...] = jnp.zeros_like(carry_ref)

    x = x_ref[...]
    h = _modulated_norm(x, g_ref[...], sc_ref[...], sh_ref[...])
    hb = h.astype(BF16)
    _rows_to_tiles(hp_ref, (), _pack_halves(h))

    hs = (_silu(_dot(hb, ws1_ref[...])) * _dot(hb, ws3_ref[...])).astype(BF16)
    base_ref[...] = x + g2_ref[...] * _dot(hs, ws2_ref[...])

    hlo = (h - hb.astype(F32)).astype(BF16)
    logits = _dot_nt(wrh_ref[...], hb) + _dot_nt(wrh_ref[...], hlo) + _dot_nt(wrl_ref[...], hb)
    scores = jax.nn.sigmoid(logits)
    biased = scores + eb_ref[...]
    tm = x.shape[0]
    per_group = N_EXPERTS // N_GROUPS
    sub = lax.broadcasted_iota(I32, (per_group, tm), 0).astype(F32)
    gscore = []
    for g in range(N_GROUPS):
        blk = biased[g * per_group:(g + 1) * per_group, :]
        m1 = jnp.max(blk, axis=0, keepdims=True)
        rest = jnp.where(sub == _first_index(blk == m1, sub, float(per_group)), -jnp.inf, blk)
        gscore.append(m1 + jnp.max(rest, axis=0, keepdims=True))
    gs = jnp.concatenate(gscore, axis=0)
    gidx = lax.broadcasted_iota(I32, (N_GROUPS, tm), 0).astype(F32)
    gsel = jnp.zeros((N_GROUPS, tm), F32)
    for _ in range(TOPK_GROUPS):
        m = jnp.max(gs, axis=0, keepdims=True)
        hit = gidx == _first_index(gs == m, gidx, float(N_GROUPS))
        gsel = jnp.where(hit, 1.0, gsel)
        gs = jnp.where(hit, -jnp.inf, gs)
    cand = jnp.concatenate(
        [jnp.where(gsel[g:g + 1, :] > 0.5, biased[g * per_group:(g + 1) * per_group, :], -jnp.inf)
         for g in range(N_GROUPS)], axis=0)
    eidx = lax.broadcasted_iota(I32, (N_EXPERTS, tm), 0).astype(F32)
    hits, tops, tws = [], [], []
    sel = jnp.zeros((N_EXPERTS, tm), F32)
    for _ in range(TOP_K):
        m = jnp.max(cand, axis=0, keepdims=True)
        e = _first_index(cand == m, eidx, float(N_EXPERTS))
        hit = eidx == e
        hits.append(hit)
        tops.append(e)
        tws.append(jnp.sum(jnp.where(hit, scores, 0.0), axis=0, keepdims=True))
        cand = jnp.where(hit, -jnp.inf, cand)
        sel = jnp.where(hit, 1.0, sel)
    te_ref[...] = jnp.concatenate(tops, axis=0).astype(I32)
    tw = jnp.concatenate(tws, axis=0)
    tw_ref[...] = tw / jnp.sum(tw, axis=0, keepdims=True) * ROUTED_SCALE

    before = _dot(sel.astype(BF16), u_ref[...]) + carry_ref[...]
    rk_ref[...] = jnp.concatenate(
        [jnp.sum(jnp.where(hit, before, 0.0), axis=0, keepdims=True) for hit in hits], axis=0).astype(I32)
    carry = carry_ref[...] + jnp.sum(sel, axis=1, keepdims=True)
    carry_ref[...] = carry
    cnt_ref[...] = jnp.broadcast_to(carry, cnt_ref.shape).astype(I32)


def _moe_pre(x2, g, sc, sh, g2, wrh, wrl, eb, ws1, ws3, ws2, seq):
    t, d = x2.shape
    tm = 256
    per_b = seq // tm
    r = np.arange(tm)
    u = jnp.asarray((r[:, None] < r[None, :]).astype(np.float32), BF16)
    full = lambda a: pl.BlockSpec(a.shape, lambda i: (0,) * a.ndim)
    vec = lambda: pl.BlockSpec((None, 1, d), lambda i: (i // per_b, 0, 0))
    colk = lambda: pl.BlockSpec((TOP_K, tm), lambda i: (0, i))
    return pl.pallas_call(
        _moe_pre_kernel,
        grid=(t // tm,),
        in_specs=[pl.BlockSpec((tm, d), lambda i: (i, 0)), full(g), vec(), vec(), vec(),
                  full(wrh), full(wrl), full(eb), full(ws1), full(ws3), full(ws2), full(u)],
        out_specs=[pl.BlockSpec((tm * ROW_TILE, LANES), lambda i: (i, 0)), pl.BlockSpec((tm, d), lambda i: (i, 0)),
                   colk(), colk(), colk(), pl.BlockSpec((N_EXPERTS, LANES), lambda i: (0, 0))],
        out_shape=[jax.ShapeDtypeStruct((t * ROW_TILE, LANES), U32), jax.ShapeDtypeStruct((t, d), F32),
                   jax.ShapeDtypeStruct((TOP_K, t), I32), jax.ShapeDtypeStruct((TOP_K, t), F32),
                   jax.ShapeDtypeStruct((TOP_K, t), I32), jax.ShapeDtypeStruct((N_EXPERTS, LANES), I32)],
        scratch_shapes=[pltpu.VMEM((N_EXPERTS, 1), F32)],
        compiler_params=_cparams(1),
        name="moe_pre",
    )(x2, g, sc, sh, g2, wrh, wrl, eb, ws1, ws3, ws2, u)


ROW_TILE = 8


def _row_copy(src_hbm, src_row, dst_buf, slot, r, sem):
    src = src_hbm.at[pl.ds(pl.multiple_of(src_row, ROW_TILE), ROW_TILE)]
    return pltpu.make_async_copy(src, dst_buf.at[slot, pl.ds(ROW_TILE * r, ROW_TILE)], sem.at[slot])


def _row_store(src_buf, slot, r, dst_hbm, dst_row, sem):
    dst = dst_hbm.at[pl.ds(pl.multiple_of(dst_row, ROW_TILE), ROW_TILE)]
    return pltpu.make_async_copy(src_buf.at[slot, pl.ds(ROW_TILE * r, ROW_TILE)], dst, sem.at[slot])


def _tiles_to_rows(ref, lead, m):
    return jnp.concatenate([ref[lead + (pl.ds(j, m, stride=ROW_TILE), slice(None))] for j in range(ROW_TILE)], axis=1)


def _rows_to_tiles(ref, lead, val):
    m = val.shape[0]
    for j in range(ROW_TILE):
        ref[lead + (pl.ds(j, m, stride=ROW_TILE), slice(None))] = val[:, j * LANES:(j + 1) * LANES]


def _experts_kernel(be_ref, tok_ref, tokn_ref, dstp_ref, dstl_ref, hp_hbm, w1_ref, w3_ref, w2_ref, out_hbm,
                    xbuf, ybuf, w1b, w3b, w2b, gsem, ssem, *, nb):
    s = pl.program_id(0)
    odd = s % 2 == 1
    even = jnp.logical_not(odd)

    def gather_start(idx_ref, sl):
        for r in range(MOE_BLOCK):
            _row_copy(hp_hbm, idx_ref[0, 0, r], xbuf, sl, r, gsem).start(priority=1)

    def gather_wait(sl):
        for r in range(MOE_BLOCK):
            _row_copy(hp_hbm, 0, xbuf, sl, r, gsem).wait()

    def scatter_start(idx_ref, sl):
        for r in range(MOE_BLOCK):
            _row_store(ybuf, sl, r, out_hbm, idx_ref[0, 0, r], ssem).start(priority=r % 2)

    def scatter_wait(sl):
        for r in range(MOE_BLOCK):
            _row_store(ybuf, sl, r, out_hbm, 0, ssem).wait()

    @pl.when(s == 0)
    def _():
        gather_start(tok_ref, 0)

    pl.when((s >= 2) & even)(functools.partial(scatter_wait, 0))
    pl.when((s >= 2) & odd)(functools.partial(scatter_wait, 1))

    @pl.when((s == 0) | (be_ref[s] != be_ref[jnp.maximum(s - 1, 0)]))
    def _():
        w1b[...] = w1_ref[...].astype(BF16)
        w3b[...] = w3_ref[...].astype(BF16)
        w2b[...] = w2_ref[...].astype(BF16)

    def main(sl, scatter_prev):
        gather_wait(sl)
        gather_start(tokn_ref, 1 - sl)
        if scatter_prev:
            scatter_start(dstp_ref, 1 - sl)
        lo, hi = _unpack_halves(_tiles_to_rows(xbuf, (sl,), MOE_BLOCK))
        xb = jnp.concatenate([lo.astype(BF16), hi.astype(BF16)], axis=1)
        hid = (_silu(_dot(xb, w1b[...])) * _dot(xb, w3b[...])).astype(BF16)
        _rows_to_tiles(ybuf, (sl,), _pack_halves(_dot(hid, w2b[...])))

    pl.when(s == 0)(functools.partial(main, 0, False))
    pl.when((s > 0) & even)(functools.partial(main, 0, True))
    pl.when(odd)(functools.partial(main, 1, True))

    @pl.when(s == nb - 1)
    def _():
        sl = (nb - 1) % 2
        scatter_start(dstl_ref, sl)
        gather_wait(1 - sl)
        scatter_wait(1 - sl)
        scatter_wait(sl)


def _experts(layer, block_e, row_tok, row_dst, hp, w1, w3, w2, n_out_rows):
    nb = row_tok.shape[0]
    buf = pltpu.VMEM((2, MOE_BLOCK * ROW_TILE, LANES), U32)
    idx = lambda f: pl.BlockSpec((1, 1, MOE_BLOCK), f, memory_space=pltpu.SMEM)
    wspec = lambda a: pl.BlockSpec((None, None) + a.shape[2:], lambda s, be: (layer, be[s], 0, 0))
    return pl.pallas_call(
        functools.partial(_experts_kernel, nb=nb),
        grid_spec=pltpu.PrefetchScalarGridSpec(
            num_scalar_prefetch=1,
            grid=(nb,),
            in_specs=[idx(lambda s, be: (s, 0, 0)),
                      idx(lambda s, be: (jnp.minimum(s + 1, nb - 1), 0, 0)),
                      idx(lambda s, be: (jnp.maximum(s - 1, 0), 0, 0)),
                      idx(lambda s, be: (nb - 1, 0, 0)),
                      pl.BlockSpec(memory_space=pl.ANY),
                      wspec(w1), wspec(w3), wspec(w2)],
            out_specs=pl.BlockSpec(memory_space=pl.ANY),
            scratch_shapes=[buf, buf,
                            pltpu.VMEM(w1.shape[2:], BF16), pltpu.VMEM(w3.shape[2:], BF16),
                            pltpu.VMEM(w2.shape[2:], BF16),
                            pltpu.SemaphoreType.DMA((2,)), pltpu.SemaphoreType.DMA((2,))]),
        out_shape=jax.ShapeDtypeStruct((n_out_rows * ROW_TILE, LANES), U32),
        compiler_params=_cparams(1),
        name="experts",
    )(block_e, row_tok, row_tok, row_dst, row_dst, hp, w1, w3, w2)


def _combine_kernel(base_ref, g2_ref, w_ref, s_ref, fg_ref, o_ref, *, final):
    w = w_ref[...]
    acc_lo = acc_hi = None
    for k in range(TOP_K):
        lo, hi = _unpack_halves(_tiles_to_rows(s_ref, (k,), w.shape[0]))
        wk = w[:, k:k + 1]
        acc_lo = wk * lo if acc_lo is None else acc_lo + wk * lo
        acc_hi = wk * hi if acc_hi is None else acc_hi + wk * hi
    x = base_ref[...] + g2_ref[...] * jnp.concatenate([acc_lo, acc_hi], axis=1)
    if final:
        x = x * lax.rsqrt(jnp.mean(x * x, axis=-1, keepdims=True) + EPS) * fg_ref[...]
    o_ref[...] = x


def _combine(base, g2, w, slots3, fg, seq, final):
    t, d = base.shape
    tm = 256
    per_b = seq // tm
    return pl.pallas_call(
        functools.partial(_combine_kernel, final=final),
        grid=(t // tm,),
        in_specs=[pl.BlockSpec((tm, d), lambda i: (i, 0)),
                  pl.BlockSpec((None, 1, d), lambda i: (i // per_b, 0, 0)),
                  pl.BlockSpec((tm, TOP_K), lambda i: (i, 0)),
                  pl.BlockSpec((TOP_K, tm * ROW_TILE, LANES), lambda i: (0, i, 0)),
                  pl.BlockSpec((1, d), lambda i: (0, 0))],
        out_specs=pl.BlockSpec((tm, d), lambda i: (i, 0)),
        out_shape=jax.ShapeDtypeStruct((t, d), F32),
        compiler_params=_cparams(1),
        name="combine",
    )(base, g2, w, slots3, fg)


TRASH_ROWS = 64


def _dispatch_tables(top_e, rank, counts, t):
    tk = TOP_K * t
    n_blocks = tk // MOE_BLOCK + N_EXPERTS
    n_rows = n_blocks * MOE_BLOCK
    plane = t + TRASH_ROWS
    padded = (counts + MOE_BLOCK - 1) // MOE_BLOCK * MOE_BLOCK
    pends = jnp.cumsum(padded)
    pstarts = pends - padded
    experts = jnp.arange(N_EXPERTS, dtype=I32)
    pstart_of = jnp.sum(jnp.where(top_e[..., None] == experts, pstarts, 0), axis=-1)
    dest = (pstart_of + rank).reshape(tk)
    row_pair = jnp.full((n_rows,), -1, I32).at[dest].set(jnp.arange(tk, dtype=I32))
    valid = row_pair >= 0
    r = jnp.arange(n_rows, dtype=I32)
    pos, par = r % MOE_BLOCK, (r // MOE_BLOCK) % 2
    trash = (pos % TOP_K) * plane + t + pos // TOP_K + par * (TRASH_ROWS // 2)
    row_tok = jnp.where(valid, row_pair % t, 0)
    row_dst = jnp.where(valid, (row_pair // t) * plane + row_pair % t, trash)
    starts = jnp.arange(n_blocks, dtype=I32) * MOE_BLOCK
    block_e = jnp.sum((pends[None, :] <= starts[:, None]).astype(I32), axis=1)
    block_e = jnp.minimum(block_e, jnp.max(jnp.where(starts < pends[-1], block_e, 0)))
    tiles = lambda rows: (rows * ROW_TILE).reshape(n_blocks, 1, MOE_BLOCK)
    return block_e, tiles(row_tok), tiles(row_dst), plane


def _layout_inproj(w_in, b_in):
    o = np.cumsum([0, 2 * CONV_DIM, FOX_DIM, FOX_DIM, FOX_DIM, FOX_HEADS, GLA_KDIM, GLA_KDIM, GLA_VDIM, GLA_VDIM,
                   GLA_RANK, 3 * D_MODEL])
    seg = lambda a, i: a[..., int(o[i]):int(o[i + 1])]

    def main(a):
        return jnp.concatenate([seg(a, 10), seg(a, 0), seg(a, 1) * (FOX_HEAD_DIM ** -0.5 * LOG2E), seg(a, 2), seg(a, 3),
                                seg(a, 5) * GLA_DK ** -0.5, seg(a, 6), seg(a, 7), seg(a, 8)], axis=-1)

    def small(a):
        pad = jnp.zeros(a.shape[:-1] + (LANES - FOX_HEADS - GLA_RANK,), a.dtype)
        return jnp.concatenate([seg(a, 4), seg(a, 9), pad], axis=-1)

    return main(w_in).astype(BF16), main(b_in)[None, :], small(w_in).astype(BF16), small(b_in)[None, :]


def kernel(x, c, norm1_g, ada_w, ada_b, w_in, b_in, conv_w, conv_b, conv_ln_g, conv_ln_b, gla_wa, gla_ba,
           gla_norm_g, w_branch_a, w_branch_b, w_branch_c, w_out, norm2_g, w_router, e_bias, w1, w3, w2,
           ws1, ws3, ws2, final_g):
    bsz, seq, d = x.shape
    t = bsz * seq
    depth = ada_w.shape[0]
    mod = _ada_mod(c, ada_w, ada_b).reshape(depth, bsz, 6, 1, d)
    x2 = x.reshape(t, d)
    row = lambda a: a.reshape(1, -1)
    for l in range(depth):
        sh1, sc1, g1, sh2, sc2, g2 = (mod[l, :, i] for i in range(6))
        w, b, ws, bs = _layout_inproj(w_in[l], b_in[l])
        z, zs = _inproj(x2, row(norm1_g[l]), sc1, sh1, w, b, ws, bs, seq)
        wa_pad = jnp.zeros((LANES, GLA_KDIM), F32).at[ZS_A:ZS_A + GLA_RANK].set(gla_wa[l])
        aq, ak, bc = _prep(zs, wa_pad, row(gla_ba[l]), seq)
        ya = _conv(z, conv_w[l].reshape(CONV_WIDTH, CONV_DIM), row(conv_b[l]), row(conv_ln_g[l]),
                   row(conv_ln_b[l]), seq)
        yb = _fox(z, aq, ak, seq)
        yc = _gla(z, bc, row(gla_norm_g[l]), seq)
        x2 = _merge(x2, g1, ya, yb, yc, z, w_branch_a[l].astype(BF16), w_branch_b[l].astype(BF16),
                    w_branch_c[l].astype(BF16), w_out[l].astype(BF16), seq)

        wr_t = w_router[l].T
        wrh = wr_t.astype(BF16)
        wrl = (wr_t - wrh.astype(F32)).astype(BF16)
        hp, base, top_e, top_w, rank, counts = _moe_pre(
            x2, row(norm2_g[l]), sc2, sh2, g2, wrh, wrl, e_bias[l].reshape(N_EXPERTS, 1),
            ws1[l].astype(BF16), ws3[l].astype(BF16), ws2[l].astype(BF16), seq)
        block_e, row_tok, row_dst, plane = _dispatch_tables(top_e, rank, counts[:, 0], t)
        slots = _experts(l, block_e, row_tok, row_dst, hp, w1, w3, w2, TOP_K * plane)
        x2 = _combine(base, g2, top_w.T, slots.reshape(TOP_K, plane * ROW_TILE, LANES), row(final_g), seq,
                      final=(l == depth - 1))
    return x2.reshape(bsz, seq, d)
```

```python
import functools

import numpy as np
import jax
import jax.numpy as jnp
from jax import lax
from jax.experimental import pallas as pl
from jax.experimental.pallas import tpu as pltpu

F32 = jnp.float32
BF16 = jnp.bfloat16
U32 = jnp.uint32
I32 = jnp.int32

D_MODEL = 2048
CONV_DIM = 512
CONV_WIDTH = 31
FOX_HEADS = 8
FOX_HEAD_DIM = 128
FOX_DIM = FOX_HEADS * FOX_HEAD_DIM
GLA_HEADS = 4
GLA_DK = 64
GLA_DV = 128
GLA_KDIM = GLA_HEADS * GLA_DK
GLA_VDIM = GLA_HEADS * GLA_DV
GLA_RANK = 16
GLA_TAU = 16.0
GLA_CHUNK = 64
GLA_SUB = 8
N_EXPERTS = 64
N_GROUPS = 8
TOPK_GROUPS = 4
TOP_K = 8
EXPERT_DIM = 512
SHARED_DIM = 512
ROUTED_SCALE = 2.5
MOE_BLOCK = 256
EPS = 1e-6

LANES = 128
V7X_VMEM_LIMIT = 56 * 1024 * 1024

Z_GATE = 0
Z_CONV = 48
Z_QF = 56
Z_KF = 64
Z_VF = 72
Z_QG = 80
Z_KG = 82
Z_VG = 84
Z_RG = 88
Z_COLS = 92 * LANES
ZS_F = 0
ZS_A = 8
NEG = -1e30
LOG2E = 1.4426950408889634


def _cparams(n_axes, vmem=V7X_VMEM_LIMIT, flags=None):
    return pltpu.CompilerParams(dimension_semantics=("arbitrary",) * n_axes, vmem_limit_bytes=vmem, flags=flags)


def _log_sigmoid(x):
    return jnp.minimum(x, 0.0) - jnp.log(1.0 + jnp.exp(-jnp.abs(x)))


def _silu(x):
    return x * jax.nn.sigmoid(x)


def _split3(x):
    hi = x.astype(BF16)
    r1 = x - hi.astype(F32)
    mid = r1.astype(BF16)
    lo = (r1 - mid.astype(F32)).astype(BF16)
    return hi, mid, lo


def _dot(a, b):
    return jnp.dot(a, b, preferred_element_type=F32)


def _dot_nt(a, b):
    return lax.dot_general(a, b, (((1,), (1,)), ((), ())), preferred_element_type=F32)


def _dot_tn(a, b):
    return lax.dot_general(a, b, (((0,), (0,)), ((), ())), preferred_element_type=F32)


def _lmul_exact(l01, x):
    hi, mid, lo = _split3(x)
    return _dot(l01, hi) + _dot(l01, mid) + _dot(l01, lo)


def _dot_f32(a, b):
    a0, a1, a2 = _split3(a)
    b0, b1, b2 = _split3(b)
    return (_dot(a0, b0) + _dot(a0, b1) + _dot(a1, b0)) + (_dot(a0, b2) + _dot(a1, b1) + _dot(a2, b0))


def _ada_kernel(ct_ref, w_ref, b_ref, o_ref):
    ct = ct_ref[...]
    act = _silu(ct)
    w = w_ref[...]
    rows = [jnp.sum(w * act[:, b:b + 1], axis=0, keepdims=True) for b in range(ct.shape[1])]
    o_ref[...] = jnp.concatenate(rows, axis=0) + b_ref[...]


def _ada_mod(c, ada_w, ada_b):
    n_l, d, n = ada_w.shape
    bsz = c.shape[0]
    tn = 512
    return pl.pallas_call(
        _ada_kernel,
        grid=(n_l, n // tn),
        in_specs=[pl.BlockSpec((d, bsz), lambda l, j: (0, 0)),
                  pl.BlockSpec((None, d, tn), lambda l, j: (l, 0, j)),
                  pl.BlockSpec((None, 1, tn), lambda l, j: (l, 0, j))],
        out_specs=pl.BlockSpec((None, bsz, tn), lambda l, j: (l, 0, j)),
        out_shape=jax.ShapeDtypeStruct((n_l, bsz, n), F32),
        compiler_params=_cparams(2),
        name="ada_mod",
    )(c.T, ada_w, ada_b.reshape(n_l, 1, n))


def _modulated_norm(x, g, sc, sh):
    ms = jnp.mean(x * x, axis=-1, keepdims=True)
    return (x * lax.rsqrt(ms + EPS) * g) * (1.0 + sc) + sh


def _inproj_kernel(x_ref, g_ref, sc_ref, sh_ref, w_ref, b_ref, ws_ref, bs_ref, z_ref, zs_ref, h_scr):
    @pl.when(pl.program_id(1) == 0)
    def _():
        h = _modulated_norm(x_ref[...], g_ref[...], sc_ref[...], sh_ref[...]).astype(BF16)
        h_scr[...] = h
        zs_ref[...] = _dot(h, ws_ref[...]) + bs_ref[...]

    z_ref[...] = (_dot(h_scr[...], w_ref[...]) + b_ref[...]).astype(BF16)


def _inproj(x2, g, sc, sh, w, b, ws, bs, seq):
    t, d = x2.shape
    tm, tn = 1024, 512
    per_b = seq // tm
    vec = lambda: pl.BlockSpec((None, 1, d), lambda i, j: (i // per_b, 0, 0))
    return pl.pallas_call(
        _inproj_kernel,
        grid=(t // tm, Z_COLS // tn),
        in_specs=[pl.BlockSpec((tm, d), lambda i, j: (i, 0)),
                  pl.BlockSpec((1, d), lambda i, j: (0, 0)),
                  vec(), vec(),
                  pl.BlockSpec((d, tn), lambda i, j: (0, j)),
                  pl.BlockSpec((1, tn), lambda i, j: (0, j)),
                  pl.BlockSpec((d, LANES), lambda i, j: (0, 0)),
                  pl.BlockSpec((1, LANES), lambda i, j: (0, 0))],
        out_specs=[pl.BlockSpec((tm, tn), lambda i, j: (i, j)),
                   pl.BlockSpec((tm, LANES), lambda i, j: (i, 0))],
        out_shape=[jax.ShapeDtypeStruct((t, Z_COLS), BF16), jax.ShapeDtypeStruct((t, LANES), F32)],
        scratch_shapes=[pltpu.VMEM((tm, d), BF16)],
        compiler_params=_cparams(2),
        name="inproj",
    )(x2, g, sc, sh, w, b, ws, bs)


def _prep_kernel(zs_ref, ltri_ref, lblk_ref, pq_ref, pk_ref, cq_ref, ck_ref, wa_ref, ba_ref,
                 aq_ref, ak_ref, bc_ref, carry_ref):
    @pl.when(pl.program_id(1) == 0)
    def _():
        carry_ref[...] = jnp.zeros_like(carry_ref)

    zs = zs_ref[...]
    cum = _lmul_exact(ltri_ref[...], _log_sigmoid(zs)) + carry_ref[...]
    carry_ref[...] = cum[cum.shape[0] - 1:, :]
    hi, mid, lo = _split3(cum * LOG2E)
    aq = cq_ref[...] + _dot(hi, pq_ref[0]) + _dot(mid, pq_ref[1]) + _dot(lo, pq_ref[2])
    ak = ck_ref[...] + _dot(hi, pk_ref[0]) + _dot(mid, pk_ref[1]) + _dot(lo, pk_ref[2])
    aq_ref[...] = aq.astype(BF16)
    ak_ref[...] = ak.astype(BF16)
    la = _log_sigmoid(_dot_f32(zs, wa_ref[...]) + ba_ref[...]) * (1.0 / GLA_TAU)
    bc_ref[...] = _lmul_exact(lblk_ref[...], la)


def _prep_consts(ts):
    r = np.arange(ts)
    ltri = (r[:, None] >= r[None, :]).astype(np.float32)
    lblk = ltri * (r[:, None] // GLA_CHUNK == r[None, :] // GLA_CHUNK)
    pq = np.zeros((3, LANES, FOX_DIM), np.float32)
    pk = np.zeros((3, LANES, FOX_DIM), np.float32)
    cq = np.zeros((1, FOX_DIM), np.float32)
    ck = np.zeros((1, FOX_DIM), np.float32)
    for h in range(FOX_HEADS):
        for p in range(3):
            pq[p, ZS_F + h, h * FOX_HEAD_DIM + p] = 1.0
            pk[p, ZS_F + h, h * FOX_HEAD_DIM + 3 + p] = -1.0
            cq[0, h * FOX_HEAD_DIM + 3 + p] = 1.0
            ck[0, h * FOX_HEAD_DIM + p] = 1.0
    return (jnp.asarray(ltri, BF16), jnp.asarray(lblk, BF16), jnp.asarray(pq, BF16), jnp.asarray(pk, BF16),
            jnp.asarray(cq), jnp.asarray(ck))


def _prep(zs, wa_pad, ba, seq):
    t = zs.shape[0]
    ts = 256
    per_b = seq // ts
    ltri, lblk, pq, pk, cq, ck = _prep_consts(ts)
    full = lambda a: pl.BlockSpec(a.shape, lambda b, s: (0,) * a.ndim)
    row = lambda n: pl.BlockSpec((ts, n), lambda b, s: (b * per_b + s, 0))
    return pl.pallas_call(
        _prep_kernel,
        grid=(t // seq, per_b),
        in_specs=[row(LANES), full(ltri), full(lblk), full(pq), full(pk), full(cq), full(ck),
                  full(wa_pad), full(ba)],
        out_specs=[row(FOX_DIM), row(FOX_DIM), row(GLA_KDIM)],
        out_shape=[jax.ShapeDtypeStruct((t, FOX_DIM), BF16), jax.ShapeDtypeStruct((t, FOX_DIM), BF16),
                   jax.ShapeDtypeStruct((t, GLA_KDIM), F32)],
        scratch_shapes=[pltpu.VMEM((1, LANES), F32)],
        compiler_params=_cparams(2),
        name="prep",
    )(zs, ltri, lblk, pq, pk, cq, ck, wa_pad, ba)


CONV_HIST = 32


def _conv_kernel(u_ref, w_ref, cb_ref, lg_ref, lb_ref, o_ref, ypad_ref):
    ts = u_ref.shape[0]

    @pl.when(pl.program_id(1) == 0)
    def _():
        ypad_ref[0:CONV_HIST, :] = jnp.zeros((CONV_HIST, CONV_DIM), F32)

    u = u_ref[...].astype(F32)
    ypad_ref[CONV_HIST:CONV_HIST + ts, :] = u[:, :CONV_DIM] * jax.nn.sigmoid(u[:, CONV_DIM:])
    acc = jnp.zeros((ts, CONV_DIM), F32) + cb_ref[...]
    first = CONV_HIST - (CONV_WIDTH - 1)
    for j in range(CONV_WIDTH):
        acc = acc + w_ref[j:j + 1, :] * ypad_ref[first + j:first + j + ts, :]
    ypad_ref[0:CONV_HIST, :] = ypad_ref[ts:ts + CONV_HIST, :]
    mu = jnp.mean(acc, axis=-1, keepdims=True)
    cen = acc - mu
    var = jnp.mean(cen * cen, axis=-1, keepdims=True)
    y = cen * lax.rsqrt(var + EPS) * lg_ref[...] + lb_ref[...]
    o_ref[...] = _silu(y).astype(BF16)


def _conv(z, w, cb, lg, lb, seq):
    t = z.shape[0]
    ts = 512
    per_b = seq // ts
    full = lambda a: pl.BlockSpec(a.shape, lambda b, s: (0,) * a.ndim)
    return pl.pallas_call(
        _conv_kernel,
        grid=(t // seq, per_b),
        in_specs=[pl.BlockSpec((ts, 2 * CONV_DIM), lambda b, s: (b * per_b + s, Z_CONV * LANES // (2 * CONV_DIM))),
                  full(w), full(cb), full(lg), full(lb)],
        out_specs=pl.BlockSpec((ts, CONV_DIM), lambda b, s: (b * per_b + s, 0)),
        out_shape=jax.ShapeDtypeStruct((t, CONV_DIM), BF16),
        scratch_shapes=[pltpu.VMEM((ts + CONV_HIST, CONV_DIM), F32)],
        compiler_params=_cparams(2),
        name="conv",
    )(z, w, cb, lg, lb)


FOX_HEADS_PER_STEP = 2


def _fox_kernel(q_ref, aq_ref, k_ref, ak_ref, v_ref, o_ref, kk_ref, vt_ref, m_ref, acc_ref, *, blk):
    i = pl.program_id(2)
    dh = FOX_HEAD_DIM
    heads = range(FOX_HEADS_PER_STEP)
    col = lambda h: slice(h * dh, (h + 1) * dh)

    @pl.when(i == 0)
    def _():
        def fill(cb, carry):
            rows = pl.ds(pl.multiple_of(cb * blk, blk), blk)
            for h in heads:
                kk_ref[h, rows, 0:dh] = k_ref[rows, col(h)]
                kk_ref[h, rows, dh:2 * dh] = ak_ref[rows, col(h)]
                vt_ref[h, cb, 0:dh, :] = v_ref[rows, col(h)].astype(F32).T.astype(BF16)
                vt_ref[h, cb, dh:2 * dh, :] = jnp.ones((dh, blk), BF16)
            return carry
        lax.fori_loop(0, k_ref.shape[0] // blk, fill, 0)

    qq = [jnp.concatenate([q_ref[:, col(h)], aq_ref[:, col(h)]], axis=1) for h in heads]
    m_ref[...] = jnp.full(m_ref.shape, NEG, F32)
    acc_ref[...] = jnp.zeros(acc_ref.shape, F32)

    def step(j, masked):
        rows = pl.ds(pl.multiple_of(j * blk, blk), blk)
        for h in heads:
            st = _dot_nt(kk_ref[h, rows, :], qq[h])
            if masked:
                kv = lax.broadcasted_iota(I32, (blk, blk), 0)
                qi = lax.broadcasted_iota(I32, (blk, blk), 1)
                st = jnp.where(kv <= qi, st, NEG)
            m_old = m_ref[h]
            m_new = jnp.maximum(m_old, jnp.max(st, axis=0, keepdims=True))
            p = jnp.exp2(st - m_new).astype(BF16)
            acc_ref[h] = acc_ref[h] * jnp.exp2(m_old - m_new) + _dot(vt_ref[h, j], p)
            m_ref[h] = m_new

    def body(j, carry):
        step(j, False)
        return carry

    lax.fori_loop(0, i, body, 0)
    step(i, True)
    for h in heads:
        acc = acc_ref[h]
        o_ref[:, col(h)] = (acc[:dh, :] / acc[dh:, :]).T.astype(BF16)


def _fox(z, aq, ak, seq):
    t = z.shape[0]
    blk = min(512, seq)
    nq = seq // blk
    hps = FOX_HEADS_PER_STEP
    width = hps * FOX_HEAD_DIM
    unit = lambda base: base * LANES // width
    qspec = lambda base: pl.BlockSpec((blk, width), lambda b, g, i: (b * nq + i, unit(base) + g))
    kspec = lambda base: pl.BlockSpec((seq, width), lambda b, g, i: (b, unit(base) + g))
    return pl.pallas_call(
        functools.partial(_fox_kernel, blk=blk),
        grid=(t // seq, FOX_HEADS // hps, nq),
        in_specs=[qspec(Z_QF), qspec(0), kspec(Z_KF), kspec(0), kspec(Z_VF)],
        out_specs=qspec(0),
        out_shape=jax.ShapeDtypeStruct((t, FOX_DIM), BF16),
        scratch_shapes=[pltpu.VMEM((hps, seq, 2 * FOX_HEAD_DIM), BF16),
                        pltpu.VMEM((hps, nq, 2 * FOX_HEAD_DIM, blk), BF16),
                        pltpu.VMEM((hps, 1, blk), F32), pltpu.VMEM((hps, 2 * FOX_HEAD_DIM, blk), F32)],
        compiler_params=_cparams(3),
        name="fox",
    )(z, aq, z, ak, z)


def _gla_chunk(q, k, v, b, st, e3):
    c, nsub = GLA_CHUNK, GLA_CHUNK // GLA_SUB
    row = lax.broadcasted_iota(I32, (c, GLA_KDIM), 0)
    rblk, rloc = row // GLA_SUB, row % GLA_SUB
    lane_k = lax.broadcasted_iota(I32, (c, GLA_KDIM), 1)
    khead, kpos = lane_k // GLA_DK, lane_k % GLA_DK
    b3 = b.reshape(nsub, GLA_SUB, GLA_KDIM)
    k3 = k.reshape(nsub, GLA_SUB, GLA_KDIM)
    last = b3[:, GLA_SUB - 1:GLA_SUB, :]
    bcast = lambda a3, i: jnp.broadcast_to(a3[:, i:i + 1, :], (nsub, GLA_SUB, GLA_KDIM)).reshape(c, GLA_KDIM)

    prev = jnp.concatenate([last[:1], last[:nsub - 1]], axis=0)
    refq = jnp.broadcast_to(prev, (nsub, GLA_SUB, GLA_KDIM)).reshape(c, GLA_KDIM)
    qt = q * jnp.exp(jnp.minimum(b - refq, 0.0))
    qcat = jnp.concatenate([jnp.where(rblk == i, qt, 0.0) for i in range(1, nsub)], axis=1).astype(BF16)
    kts = []
    for i in range(1, nsub):
        kt = k * jnp.exp(jnp.minimum(last[i - 1] - b, 0.0))
        kts.append(jnp.where(row < GLA_SUB * i, kt, 0.0))
    kcat = jnp.concatenate(
        [jnp.concatenate([jnp.where(khead == h, kt, 0.0) for kt in kts], axis=1) for h in range(GLA_HEADS)],
        axis=0).astype(BF16)
    a_all = _dot_nt(qcat, kcat)

    for sl in range(GLA_SUB):
        p = q * bcast(k3, sl) * jnp.exp(jnp.minimum(b - bcast(b3, sl), 0.0))
        w = _dot(p.astype(BF16), e3)
        hit = (kpos == rblk * GLA_SUB + sl) & (rloc >= sl)
        a_all = a_all + jnp.where(hit, w, 0.0)

    lane_v = lax.broadcasted_iota(I32, (c, GLA_VDIM), 1) // GLA_DV
    vbd = jnp.concatenate([jnp.where(lane_v == h, v, jnp.zeros_like(v)) for h in range(GLA_HEADS)], axis=0)
    qe = (q * jnp.exp(b)).astype(BF16)
    o = _dot(a_all.astype(BF16), vbd) + _dot_nt(qe, st.astype(BF16))

    blast = b[c - 1:c, :]
    kd = (k * jnp.exp(blast - b)).astype(BF16)
    shead_v = lax.broadcasted_iota(I32, (GLA_VDIM, GLA_KDIM), 0) // GLA_DV
    shead_k = lax.broadcasted_iota(I32, (GLA_VDIM, GLA_KDIM), 1) // GLA_DK
    st_new = jnp.where(shead_v == shead_k, st * jnp.exp(blast) + _dot_tn(v, kd), 0.0)
    return o, st_new


def _gla_kernel(q_ref, k_ref, v_ref, r_ref, b_ref, ng_ref, e3_ref, o_ref, st_ref):
    @pl.when(pl.program_id(1) == 0)
    def _():
        st_ref[...] = jnp.zeros_like(st_ref)

    e3 = e3_ref[...]
    ng = ng_ref[...]

    def body(ci, carry):
        rows = pl.ds(pl.multiple_of(ci * GLA_CHUNK, GLA_CHUNK), GLA_CHUNK)
        o, st_new = _gla_chunk(q_ref[rows, :].astype(F32), k_ref[rows, :].astype(F32), v_ref[rows, :],
                               b_ref[rows, :], st_ref[...], e3)
        st_ref[...] = st_new
        outs = []
        for h in range(GLA_HEADS):
            oh = o[:, h * GLA_DV:(h + 1) * GLA_DV]
            outs.append(oh * lax.rsqrt(jnp.mean(oh * oh, axis=-1, keepdims=True) + EPS))
        y = jnp.concatenate(outs, axis=1) * ng * _silu(r_ref[rows, :].astype(F32))
        o_ref[rows, :] = y.astype(BF16)
        return carry

    lax.fori_loop(0, q_ref.shape[0] // GLA_CHUNK, body, 0)


def _gla(z, bc, ng, seq):
    t = z.shape[0]
    ts = min(512, seq)
    per_b = seq // ts
    lane = np.arange(GLA_KDIM)
    e3 = jnp.asarray((lane[:, None] // GLA_DK == lane[None, :] // GLA_DK).astype(np.float32), BF16)
    spec = lambda n, unit: pl.BlockSpec((ts, n), lambda b, s: (b * per_b + s, unit))
    full = lambda a: pl.BlockSpec(a.shape, lambda b, s: (0,) * a.ndim)
    return pl.pallas_call(
        _gla_kernel,
        grid=(t // seq, per_b),
        in_specs=[spec(GLA_KDIM, Z_QG * LANES // GLA_KDIM), spec(GLA_KDIM, Z_KG * LANES // GLA_KDIM),
                  spec(GLA_VDIM, Z_VG * LANES // GLA_VDIM), spec(GLA_VDIM, Z_RG * LANES // GLA_VDIM),
                  spec(GLA_KDIM, 0), full(ng), full(e3)],
        out_specs=spec(GLA_VDIM, 0),
        out_shape=jax.ShapeDtypeStruct((t, GLA_VDIM), BF16),
        scratch_shapes=[pltpu.VMEM((GLA_VDIM, GLA_KDIM), F32)],
        compiler_params=_cparams(2),
        name="gla",
    )(z, z, z, z, bc, ng, e3)


def _merge_kernel(x_ref, g1_ref, ya_ref, yb_ref, yc_ref, ga_ref, gb_ref, gc_ref,
                  wa_ref, wb_ref, wc_ref, wo_ref, o_ref):
    gate = lambda r: jax.nn.sigmoid(r[...].astype(F32))
    merged = (gate(ga_ref) * _dot(ya_ref[...], wa_ref[...]) + gate(gb_ref) * _dot(yb_ref[...], wb_ref[...])
              + gate(gc_ref) * _dot(yc_ref[...], wc_ref[...]))
    o_ref[...] = x_ref[...] + g1_ref[...] * _dot(merged.astype(BF16), wo_ref[...])


def _merge(x2, g1, ya, yb, yc, z, wa, wb, wc, wo, seq):
    t, d = x2.shape
    tm = 256
    per_b = seq // tm
    full = lambda a: pl.BlockSpec(a.shape, lambda i: (0,) * a.ndim)
    row = lambda n, unit=0: pl.BlockSpec((tm, n), lambda i: (i, unit))
    return pl.pallas_call(
        _merge_kernel,
        grid=(t // tm,),
        in_specs=[row(d), pl.BlockSpec((None, 1, d), lambda i: (i // per_b, 0, 0)),
                  row(CONV_DIM), row(FOX_DIM), row(GLA_VDIM), row(d, 0), row(d, 1), row(d, 2),
                  full(wa), full(wb), full(wc), full(wo)],
        out_specs=row(d),
        out_shape=jax.ShapeDtypeStruct((t, d), F32),
        compiler_params=_cparams(1),
        name="merge",
    )(x2, g1, ya, yb, yc, z, z, z, wa, wb, wc, wo)


def _pack_halves(y):
    n = y.shape[1] // 2
    lo = pltpu.bitcast(y[:, :n].astype(BF16).astype(F32), U32)
    hi = pltpu.bitcast(y[:, n:].astype(BF16).astype(F32), U32)
    return (hi & jnp.uint32(0xFFFF0000)) | (lo >> 16)


def _unpack_halves(p):
    lo = pltpu.bitcast(p << 16, F32)
    hi = pltpu.bitcast(p & jnp.uint32(0xFFFF0000), F32)
    return lo, hi


def _first_index(hit, idx, n):
    return jnp.min(jnp.where(hit, idx, n), axis=0, keepdims=True)


def _route_kernel(x_ref, g_ref, sc_ref, sh_ref, wrh_ref, wrl_ref, eb_ref, u_ref,
                  te_ref, tw_ref, rk_ref, cnt_ref, carry_ref):
    @pl.when(pl.program_id(0) == 0)
    def _():
        carry_ref[...] = jnp.zeros_like(carry_ref)

    x = x_ref[...]
    h = _modulated_norm(x, g_ref[...], sc_ref[...], sh_ref[...])
    hb = h.astype(BF16)

    hlo = (h - hb.astype(F32)).astype(BF16)
    logits = _dot_nt(wrh_ref[...], hb) + _dot_nt(wrh_ref[...], hlo) + _dot_nt(wrl_ref[...], hb)
    scores = jax.nn.sigmoid(logits)
    biased = scores + eb_ref[...]
    tm = x.shape[0]
    per_group = N_EXPERTS // N_GROUPS
    sub = lax.broadcasted_iota(I32, (per_group, tm), 0).astype(F32)
    gscore = []
    for g in range(N_GROUPS):
        blk = biased[g * per_group:(g + 1) * per_group, :]
        m1 = jnp.max(blk, axis=0, keepdims=True)
        rest = jnp.where(sub == _first_index(blk == m1, sub, float(per_group)), -jnp.inf, blk)
        gscore.append(m1 + jnp.max(rest, axis=0, keepdims=True))
    gs = jnp.concatenate(gscore, axis=0)
    gidx = lax.broadcasted_iota(I32, (N_GROUPS, tm), 0).astype(F32)
    gsel = jnp.zeros((N_GROUPS, tm), F32)
    for _ in range(TOPK_GROUPS):
        m = jnp.max(gs, axis=0, keepdims=True)
        hit = gidx == _first_index(gs == m, gidx, float(N_GROUPS))
        gsel = jnp.where(hit, 1.0, gsel)
        gs = jnp.where(hit, -jnp.inf, gs)
    cand = jnp.concatenate(
        [jnp.where(gsel[g:g + 1, :] > 0.5, biased[g * per_group:(g + 1) * per_group, :], -jnp.inf)
         for g in range(N_GROUPS)], axis=0)
    eidx = lax.broadcasted_iota(I32, (N_EXPERTS, tm), 0).astype(F32)
    hits, tops, tws = [], [], []
    sel = jnp.zeros((N_EXPERTS, tm), F32)
    for _ in range(TOP_K):
        m = jnp.max(cand, axis=0, keepdims=True)
        e = _first_index(cand == m, eidx, float(N_EXPERTS))
        hit = eidx == e
        hits.append(hit)
        tops.append(e)
        tws.append(jnp.sum(jnp.where(hit, scores, 0.0), axis=0, keepdims=True))
        cand = jnp.where(hit, -jnp.inf, cand)
        sel = jnp.where(hit, 1.0, sel)
    te_ref[...] = jnp.concatenate(tops, axis=0).astype(I32)
    tw = jnp.concatenate(tws, axis=0)
    tw_ref[...] = tw / jnp.sum(tw, axis=0, keepdims=True) * ROUTED_SCALE

    before = _dot(sel.astype(BF16), u_ref[...]) + carry_ref[...]
    rk_ref[...] = jnp.concatenate(
        [jnp.sum(jnp.where(hit, before, 0.0), axis=0, keepdims=True) for hit in hits], axis=0).astype(I32)
    carry = carry_ref[...] + jnp.sum(sel, axis=1, keepdims=True)
    carry_ref[...] = carry
    cnt_ref[...] = jnp.broadcast_to(carry, cnt_ref.shape).astype(I32)


MOE_TM = 256


def _route(x2, g, sc, sh, wrh, wrl, eb, seq):
    t, d = x2.shape
    tm = MOE_TM
    per_b = seq // tm
    r = np.arange(tm)
    u = jnp.asarray((r[:, None] < r[None, :]).astype(np.float32), BF16)
    full = lambda a: pl.BlockSpec(a.shape, lambda i: (0,) * a.ndim)
    vec = lambda: pl.BlockSpec((None, 1, d), lambda i: (i // per_b, 0, 0))
    colk = lambda: pl.BlockSpec((TOP_K, tm), lambda i: (0, i))
    return pl.pallas_call(
        _route_kernel,
        grid=(t // tm,),
        in_specs=[pl.BlockSpec((tm, d), lambda i: (i, 0)), full(g), vec(), vec(),
                  full(wrh), full(wrl), full(eb), full(u)],
        out_specs=[colk(), colk(), colk(), pl.BlockSpec((N_EXPERTS, LANES), lambda i: (0, 0))],
        out_shape=[jax.ShapeDtypeStruct((TOP_K, t), I32), jax.ShapeDtypeStruct((TOP_K, t), F32),
                   jax.ShapeDtypeStruct((TOP_K, t), I32), jax.ShapeDtypeStruct((N_EXPERTS, LANES), I32)],
        scratch_shapes=[pltpu.VMEM((N_EXPERTS, 1), F32)],
        compiler_params=_cparams(1),
        name="route",
    )(x2, g, sc, sh, wrh, wrl, eb, u)


ROW_TILE = 8


def _tiles_to_rows(ref, lead, first, m):
    return jnp.concatenate(
        [ref[lead + (pl.ds(first * ROW_TILE + j, m, stride=ROW_TILE), slice(None))] for j in range(ROW_TILE)], axis=1)


def _rows_to_tiles(ref, lead, val):
    m = val.shape[0]
    for j in range(ROW_TILE):
        ref[lead + (pl.ds(j, m, stride=ROW_TILE), slice(None))] = val[:, j * LANES:(j + 1) * LANES]


def _tile_rows(ref, lead, row):
    return ref.at[lead + (pl.ds(pl.multiple_of(row, ROW_TILE) if not isinstance(row, int) else row, ROW_TILE),)]


def _dispatch_kernel(dst_ref, x_ref, g_ref, sc_ref, sh_ref, g2_ref, ws1_ref, ws3_ref, ws2_ref, xs_in,
                     base_ref, xs_hbm, hbuf, sem, *, n_tiles):
    del xs_in
    i = pl.program_id(0)
    odd = i % 2 == 1
    even = jnp.logical_not(odd)
    tm = x_ref.shape[0]

    def copies(sl, start):
        for k in range(TOP_K):
            for t in range(tm):
                cp = pltpu.make_async_copy(_tile_rows(hbuf, (sl,), ROW_TILE * t),
                                           _tile_rows(xs_hbm, (), dst_ref[k, t] if start else 0), sem.at[sl])
                cp.start(priority=(k * tm + t) % 2) if start else cp.wait()

    pl.when((i >= 2) & even)(functools.partial(copies, 0, False))
    pl.when((i >= 2) & odd)(functools.partial(copies, 1, False))

    def main(sl):
        x = x_ref[...]
        h = _modulated_norm(x, g_ref[...], sc_ref[...], sh_ref[...])
        _rows_to_tiles(hbuf, (sl,), _pack_halves(h))
        hb = h.astype(BF16)
        hs = (_silu(_dot(hb, ws1_ref[...])) * _dot(hb, ws3_ref[...])).astype(BF16)
        base_ref[...] = x + g2_ref[...] * _dot(hs, ws2_ref[...])
        copies(sl, True)

    pl.when(even)(functools.partial(main, 0))
    pl.when(odd)(functools.partial(main, 1))

    @pl.when(i == n_tiles - 1)
    def _():
        if n_tiles >= 2:
            copies(n_tiles % 2, False)
        copies((n_tiles - 1) % 2, False)


def _dispatch(x2, g, sc, sh, g2, ws1, ws3, ws2, dest, n_rows, seq):
    t, d = x2.shape
    tm = MOE_TM
    per_b = seq // tm
    full = lambda a: pl.BlockSpec(a.shape, lambda i: (0,) * a.ndim)
    vec = lambda: pl.BlockSpec((None, 1, d), lambda i: (i // per_b, 0, 0))
    xs0 = jnp.zeros((n_rows * ROW_TILE, LANES), U32)
    return pl.pallas_call(
        functools.partial(_dispatch_kernel, n_tiles=t // tm),
        grid=(t // tm,),
        in_specs=[pl.BlockSpec((TOP_K, tm), lambda i: (0, i), memory_space=pltpu.SMEM),
                  pl.BlockSpec((tm, d), lambda i: (i, 0)), full(g), vec(), vec(), vec(),
                  full(ws1), full(ws3), full(ws2), pl.BlockSpec(memory_space=pl.ANY)],
        out_specs=[pl.BlockSpec((tm, d), lambda i: (i, 0)), pl.BlockSpec(memory_space=pl.ANY)],
        out_shape=[jax.ShapeDtypeStruct((t, d), F32), jax.ShapeDtypeStruct(xs0.shape, U32)],
        scratch_shapes=[pltpu.VMEM((2, tm * ROW_TILE, LANES), U32), pltpu.SemaphoreType.DMA((2,))],
        input_output_aliases={9: 1},
        compiler_params=_cparams(1),
        name="dispatch",
    )(dest, x2, g, sc, sh, g2, ws1, ws3, ws2, xs0)


def _experts_kernel(be_ref, xs_ref, w1_ref, w3_ref, w2_ref, ys_ref, w1b, w3b, w2b):
    s = pl.program_id(0)

    @pl.when((s == 0) | (be_ref[s] != be_ref[jnp.maximum(s - 1, 0)]))
    def _():
        w1b[...] = w1_ref[...].astype(BF16)
        w3b[...] = w3_ref[...].astype(BF16)
        w2b[...] = w2_ref[...].astype(BF16)

    lo, hi = _unpack_halves(_tiles_to_rows(xs_ref, (), 0, MOE_BLOCK))
    xb = jnp.concatenate([lo.astype(BF16), hi.astype(BF16)], axis=1)
    hid = (_silu(_dot(xb, w1b[...])) * _dot(xb, w3b[...])).astype(BF16)
    _rows_to_tiles(ys_ref, (), _pack_halves(_dot(hid, w2b[...])))


def _experts(layer, block_e, xs, w1, w3, w2):
    nb = block_e.shape[0]
    rows = pl.BlockSpec((MOE_BLOCK * ROW_TILE, LANES), lambda s, be: (s, 0))
    wspec = lambda a: pl.BlockSpec((None, None) + a.shape[2:], lambda s, be: (layer, be[s], 0, 0))
    return pl.pallas_call(
        _experts_kernel,
        grid_spec=pltpu.PrefetchScalarGridSpec(
            num_scalar_prefetch=1,
            grid=(nb,),
            in_specs=[rows, wspec(w1), wspec(w3), wspec(w2)],
            out_specs=rows,
            scratch_shapes=[pltpu.VMEM(w1.shape[2:], BF16), pltpu.VMEM(w3.shape[2:], BF16),
                            pltpu.VMEM(w2.shape[2:], BF16)]),
        out_shape=jax.ShapeDtypeStruct(xs.shape, U32),
        compiler_params=_cparams(1),
        name="experts",
    )(block_e, xs, w1, w3, w2)


def _combine_kernel(dst_ref, dstn_ref, base_ref, g2_ref, w_ref, fg_ref, ys_hbm, o_ref, gbuf, sem, *, n_tiles, final):
    i = pl.program_id(0)
    odd = i % 2 == 1
    even = jnp.logical_not(odd)
    tm = base_ref.shape[0]

    def fetch(idx_ref, sl, start):
        for k in range(TOP_K):
            for t in range(tm):
                cp = pltpu.make_async_copy(_tile_rows(ys_hbm, (), idx_ref[k, t] if start else 0),
                                           _tile_rows(gbuf, (sl,), ROW_TILE * (k * tm + t)), sem.at[sl])
                cp.start(priority=(k * tm + t) % 2) if start else cp.wait()

    @pl.when(i == 0)
    def _():
        fetch(dst_ref, 0, True)

    def main(sl):
        fetch(dst_ref, sl, False)
        fetch(dstn_ref, 1 - sl, True)
        w = w_ref[...]
        wb = [jnp.broadcast_to(w[:, k:k + 1], (tm, LANES)) for k in range(TOP_K)]
        half = o_ref.shape[1] // 2
        ssq = jnp.zeros((tm, 1), F32)
        for j in range(ROW_TILE):
            acc_lo = acc_hi = None
            for k in range(TOP_K):
                lo, hi = _unpack_halves(gbuf[sl, pl.ds((k * tm) * ROW_TILE + j, tm, stride=ROW_TILE), :])
                acc_lo = wb[k] * lo if acc_lo is None else acc_lo + wb[k] * lo
                acc_hi = wb[k] * hi if acc_hi is None else acc_hi + wb[k] * hi
            for acc, c0 in ((acc_lo, j * LANES), (acc_hi, half + j * LANES)):
                x = base_ref[:, c0:c0 + LANES] + g2_ref[:, c0:c0 + LANES] * acc
                o_ref[:, c0:c0 + LANES] = x
                ssq = ssq + jnp.sum(x * x, axis=-1, keepdims=True)
        if final:
            scale = lax.rsqrt(ssq * (1.0 / o_ref.shape[1]) + EPS)
            o_ref[...] = o_ref[...] * scale * fg_ref[...]

    pl.when(even)(functools.partial(main, 0))
    pl.when(odd)(functools.partial(main, 1))

    @pl.when(i == n_tiles - 1)
    def _():
        fetch(dst_ref, n_tiles % 2, False)


def _combine(base, g2, w, ys, dest, fg, seq, final):
    t, d = base.shape
    tm = MOE_TM
    per_b = seq // tm
    n_tiles = t // tm
    idx = lambda f: pl.BlockSpec((TOP_K, tm), f, memory_space=pltpu.SMEM)
    return pl.pallas_call(
        functools.partial(_combine_kernel, n_tiles=n_tiles, final=final),
        grid=(n_tiles,),
        in_specs=[idx(lambda i: (0, i)), idx(lambda i: (0, jnp.minimum(i + 1, n_tiles - 1))),
                  pl.BlockSpec((tm, d), lambda i: (i, 0)),
                  pl.BlockSpec((None, 1, d), lambda i: (i // per_b, 0, 0)),
                  pl.BlockSpec((tm, TOP_K), lambda i: (i, 0)),
                  pl.BlockSpec((1, d), lambda i: (0, 0)),
                  pl.BlockSpec(memory_space=pl.ANY)],
        out_specs=pl.BlockSpec((tm, d), lambda i: (i, 0)),
        out_shape=jax.ShapeDtypeStruct((t, d), F32),
        scratch_shapes=[pltpu.VMEM((2, TOP_K * tm * ROW_TILE, LANES), U32), pltpu.SemaphoreType.DMA((2,))],
        compiler_params=_cparams(1),
        name="combine",
    )(dest, dest, base, g2, w, fg, ys)


def _dispatch_tables(top_e, rank, counts, t):
    n_blocks = TOP_K * t // MOE_BLOCK + N_EXPERTS
    padded = (counts + MOE_BLOCK - 1) // MOE_BLOCK * MOE_BLOCK
    pends = jnp.cumsum(padded)
    pstarts = pends - padded
    experts = jnp.arange(N_EXPERTS, dtype=I32)
    pstart_of = jnp.sum(jnp.where(top_e[..., None] == experts, pstarts, 0), axis=-1)
    dest = (pstart_of + rank) * ROW_TILE
    starts = jnp.arange(n_blocks, dtype=I32) * MOE_BLOCK
    block_e = jnp.sum((pends[None, :] <= starts[:, None]).astype(I32), axis=1)
    block_e = jnp.minimum(block_e, jnp.max(jnp.where(starts < pends[-1], block_e, 0)))
    return block_e, dest, n_blocks * MOE_BLOCK


def _layout_inproj(w_in, b_in):
    o = np.cumsum([0, 2 * CONV_DIM, FOX_DIM, FOX_DIM, FOX_DIM, FOX_HEADS, GLA_KDIM, GLA_KDIM, GLA_VDIM, GLA_VDIM,
                   GLA_RANK, 3 * D_MODEL])
    seg = lambda a, i: a[..., int(o[i]):int(o[i + 1])]

    def main(a):
        return jnp.concatenate([seg(a, 10), seg(a, 0), seg(a, 1) * (FOX_HEAD_DIM ** -0.5 * LOG2E), seg(a, 2), seg(a, 3),
                                seg(a, 5) * GLA_DK ** -0.5, seg(a, 6), seg(a, 7), seg(a, 8)], axis=-1)

    def small(a):
        pad = jnp.zeros(a.shape[:-1] + (LANES - FOX_HEADS - GLA_RANK,), a.dtype)
        return jnp.concatenate([seg(a, 4), seg(a, 9), pad], axis=-1)

    return main(w_in).astype(BF16), main(b_in)[None, :], small(w_in).astype(BF16), small(b_in)[None, :]


def kernel(x, c, norm1_g, ada_w, ada_b, w_in, b_in, conv_w, conv_b, conv_ln_g, conv_ln_b, gla_wa, gla_ba,
           gla_norm_g, w_branch_a, w_branch_b, w_branch_c, w_out, norm2_g, w_router, e_bias, w1, w3, w2,
           ws1, ws3, ws2, final_g):
    bsz, seq, d = x.shape
    t = bsz * seq
    depth = ada_w.shape[0]
    mod = _ada_mod(c, ada_w, ada_b).reshape(depth, bsz, 6, 1, d)
    x2 = x.reshape(t, d)
    row = lambda a: a.reshape(1, -1)
    for l in range(depth):
        sh1, sc1, g1, sh2, sc2, g2 = (mod[l, :, i] for i in range(6))
        w, b, ws, bs = _layout_inproj(w_in[l], b_in[l])
        z, zs = _inproj(x2, row(norm1_g[l]), sc1, sh1, w, b, ws, bs, seq)
        wa_pad = jnp.zeros((LANES, GLA_KDIM), F32).at[ZS_A:ZS_A + GLA_RANK].set(gla_wa[l])
        aq, ak, bc = _prep(zs, wa_pad, row(gla_ba[l]), seq)
        ya = _conv(z, conv_w[l].reshape(CONV_WIDTH, CONV_DIM), row(conv_b[l]), row(conv_ln_g[l]),
                   row(conv_ln_b[l]), seq)
        yb = _fox(z, aq, ak, seq)
        yc = _gla(z, bc, row(gla_norm_g[l]), seq)
        x2 = _merge(x2, g1, ya, yb, yc, z, w_branch_a[l].astype(BF16), w_branch_b[l].astype(BF16),
                    w_branch_c[l].astype(BF16), w_out[l].astype(BF16), seq)

        wr_t = w_router[l].T
        wrh = wr_t.astype(BF16)
        wrl = (wr_t - wrh.astype(F32)).astype(BF16)
        top_e, top_w, rank, counts = _route(x2, row(norm2_g[l]), sc2, sh2, wrh, wrl,
                                            e_bias[l].reshape(N_EXPERTS, 1), seq)
        block_e, dest, n_rows = _dispatch_tables(top_e, rank, counts[:, 0], t)
        base, xs = _dispatch(x2, row(norm2_g[l]), sc2, sh2, g2, ws1[l].astype(BF16), ws3[l].astype(BF16),
                             ws2[l].astype(BF16), dest, n_rows, seq)
        ys = _experts(l, block_e, xs, w1, w3, w2)
        x2 = _combine(base, g2, top_w.T, ys, dest, row(final_g), seq, final=(l == depth - 1))
    return x2.reshape(bsz, seq, d)
```

```python
import functools

import numpy as np
import jax
import jax.numpy as jnp
from jax import lax
from jax.experimental import pallas as pl
from jax.experimental.pallas import tpu as pltpu

F32 = jnp.float32
BF16 = jnp.bfloat16
U32 = jnp.uint32
I32 = jnp.int32

D_MODEL = 2048
CONV_DIM = 512
CONV_WIDTH = 31
FOX_HEADS = 8
FOX_HEAD_DIM = 128
FOX_DIM = FOX_HEADS * FOX_HEAD_DIM
GLA_HEADS = 4
GLA_DK = 64
GLA_DV = 128
GLA_KDIM = GLA_HEADS * GLA_DK
GLA_VDIM = GLA_HEADS * GLA_DV
GLA_RANK = 16
GLA_TAU = 16.0
GLA_CHUNK = 64
GLA_SUB = 8
N_EXPERTS = 64
N_GROUPS = 8
TOPK_GROUPS = 4
TOP_K = 8
EXPERT_DIM = 512
SHARED_DIM = 512
ROUTED_SCALE = 2.5
MOE_BLOCK = 256
EPS = 1e-6

LANES = 128
V7X_VMEM_LIMIT = 56 * 1024 * 1024

Z_GATE = 0
Z_CONV = 48
Z_QF = 56
Z_KF = 64
Z_VF = 72
Z_QG = 80
Z_KG = 82
Z_VG = 84
Z_RG = 88
Z_COLS = 92 * LANES
ZS_F = 0
ZS_A = 8
NEG = -1e30
LOG2E = 1.4426950408889634


def _cparams(n_axes, vmem=V7X_VMEM_LIMIT, flags=None):
    return pltpu.CompilerParams(dimension_semantics=("arbitrary",) * n_axes, vmem_limit_bytes=vmem, flags=flags)


def _log_sigmoid(x):
    return jnp.minimum(x, 0.0) - jnp.log(1.0 + jnp.exp(-jnp.abs(x)))


def _silu(x):
    return x * jax.nn.sigmoid(x)


def _split3(x):
    hi = x.astype(BF16)
    r1 = x - hi.astype(F32)
    mid = r1.astype(BF16)
    lo = (r1 - mid.astype(F32)).astype(BF16)
    return hi, mid, lo


def _dot(a, b):
    return jnp.dot(a, b, preferred_element_type=F32)


def _dot_nt(a, b):
    return lax.dot_general(a, b, (((1,), (1,)), ((), ())), preferred_element_type=F32)


def _dot_tn(a, b):
    return lax.dot_general(a, b, (((0,), (0,)), ((), ())), preferred_element_type=F32)


def _lmul_exact(l01, x):
    hi, mid, lo = _split3(x)
    return _dot(l01, hi) + _dot(l01, mid) + _dot(l01, lo)


def _dot_f32(a, b):
    a0, a1, a2 = _split3(a)
    b0, b1, b2 = _split3(b)
    return (_dot(a0, b0) + _dot(a0, b1) + _dot(a1, b0)) + (_dot(a0, b2) + _dot(a1, b1) + _dot(a2, b0))


def _ada_kernel(ct_ref, w_ref, b_ref, o_ref):
    ct = ct_ref[...]
    act = _silu(ct)
    w = w_ref[...]
    rows = [jnp.sum(w * act[:, b:b + 1], axis=0, keepdims=True) for b in range(ct.shape[1])]
    o_ref[...] = jnp.concatenate(rows, axis=0) + b_ref[...]


def _ada_mod(c, ada_w, ada_b):
    n_l, d, n = ada_w.shape
    bsz = c.shape[0]
    tn = 512
    return pl.pallas_call(
        _ada_kernel,
        grid=(n_l, n // tn),
        in_specs=[pl.BlockSpec((d, bsz), lambda l, j: (0, 0)),
                  pl.BlockSpec((None, d, tn), lambda l, j: (l, 0, j)),
                  pl.BlockSpec((None, 1, tn), lambda l, j: (l, 0, j))],
        out_specs=pl.BlockSpec((None, bsz, tn), lambda l, j: (l, 0, j)),
        out_shape=jax.ShapeDtypeStruct((n_l, bsz, n), F32),
        compiler_params=_cparams(2),
        name="ada_mod",
    )(c.T, ada_w, ada_b.reshape(n_l, 1, n))


def _modulated_norm(x, g, sc, sh):
    ms = jnp.mean(x * x, axis=-1, keepdims=True)
    return (x * lax.rsqrt(ms + EPS) * g) * (1.0 + sc) + sh


def _inproj_kernel(x_ref, g_ref, sc_ref, sh_ref, w_ref, b_ref, ws_ref, bs_ref, z_ref, zs_ref, h_scr):
    @pl.when(pl.program_id(1) == 0)
    def _():
        h = _modulated_norm(x_ref[...], g_ref[...], sc_ref[...], sh_ref[...]).astype(BF16)
        h_scr[...] = h
        zs_ref[...] = _dot(h, ws_ref[...]) + bs_ref[...]

    z_ref[...] = (_dot(h_scr[...], w_ref[...]) + b_ref[...]).astype(BF16)


def _inproj(x2, g, sc, sh, w, b, ws, bs, seq):
    t, d = x2.shape
    tm, tn = 1024, 512
    per_b = seq // tm
    vec = lambda: pl.BlockSpec((None, 1, d), lambda i, j: (i // per_b, 0, 0))
    return pl.pallas_call(
        _inproj_kernel,
        grid=(t // tm, Z_COLS // tn),
        in_specs=[pl.BlockSpec((tm, d), lambda i, j: (i, 0)),
                  pl.BlockSpec((1, d), lambda i, j: (0, 0)),
                  vec(), vec(),
                  pl.BlockSpec((d, tn), lambda i, j: (0, j)),
                  pl.BlockSpec((1, tn), lambda i, j: (0, j)),
                  pl.BlockSpec((d, LANES), lambda i, j: (0, 0)),
                  pl.BlockSpec((1, LANES), lambda i, j: (0, 0))],
        out_specs=[pl.BlockSpec((tm, tn), lambda i, j: (i, j)),
                   pl.BlockSpec((tm, LANES), lambda i, j: (i, 0))],
        out_shape=[jax.ShapeDtypeStruct((t, Z_COLS), BF16), jax.ShapeDtypeStruct((t, LANES), F32)],
        scratch_shapes=[pltpu.VMEM((tm, d), BF16)],
        compiler_params=_cparams(2),
        name="inproj",
    )(x2, g, sc, sh, w, b, ws, bs)


def _prep_kernel(zs_ref, ltri_ref, lblk_ref, pq_ref, pk_ref, cq_ref, ck_ref, wa_ref, ba_ref,
                 aq_ref, ak_ref, bc_ref, carry_ref):
    @pl.when(pl.program_id(1) == 0)
    def _():
        carry_ref[...] = jnp.zeros_like(carry_ref)

    zs = zs_ref[...]
    cum = _lmul_exact(ltri_ref[...], _log_sigmoid(zs)) + carry_ref[...]
    carry_ref[...] = cum[cum.shape[0] - 1:, :]
    hi, mid, lo = _split3(cum * LOG2E)
    aq = cq_ref[...] + _dot(hi, pq_ref[0]) + _dot(mid, pq_ref[1]) + _dot(lo, pq_ref[2])
    ak = ck_ref[...] + _dot(hi, pk_ref[0]) + _dot(mid, pk_ref[1]) + _dot(lo, pk_ref[2])
    aq_ref[...] = aq.astype(BF16)
    ak_ref[...] = ak.astype(BF16)
    la = _log_sigmoid(_dot_f32(zs, wa_ref[...]) + ba_ref[...]) * (1.0 / GLA_TAU)
    bc_ref[...] = _lmul_exact(lblk_ref[...], la)


def _prep_consts(ts):
    r = np.arange(ts)
    ltri = (r[:, None] >= r[None, :]).astype(np.float32)
    lblk = ltri * (r[:, None] // GLA_CHUNK == r[None, :] // GLA_CHUNK)
    pq = np.zeros((3, LANES, FOX_DIM), np.float32)
    pk = np.zeros((3, LANES, FOX_DIM), np.float32)
    cq = np.zeros((1, FOX_DIM), np.float32)
    ck = np.zeros((1, FOX_DIM), np.float32)
    for h in range(FOX_HEADS):
        for p in range(3):
            pq[p, ZS_F + h, h * FOX_HEAD_DIM + p] = 1.0
            pk[p, ZS_F + h, h * FOX_HEAD_DIM + 3 + p] = -1.0
            cq[0, h * FOX_HEAD_DIM + 3 + p] = 1.0
            ck[0, h * FOX_HEAD_DIM + p] = 1.0
    return (jnp.asarray(ltri, BF16), jnp.asarray(lblk, BF16), jnp.asarray(pq, BF16), jnp.asarray(pk, BF16),
            jnp.asarray(cq), jnp.asarray(ck))


def _prep(zs, wa_pad, ba, seq):
    t = zs.shape[0]
    ts = 256
    per_b = seq // ts
    ltri, lblk, pq, pk, cq, ck = _prep_consts(ts)
    full = lambda a: pl.BlockSpec(a.shape, lambda b, s: (0,) * a.ndim)
    row = lambda n: pl.BlockSpec((ts, n), lambda b, s: (b * per_b + s, 0))
    return pl.pallas_call(
        _prep_kernel,
        grid=(t // seq, per_b),
        in_specs=[row(LANES), full(ltri), full(lblk), full(pq), full(pk), full(cq), full(ck),
                  full(wa_pad), full(ba)],
        out_specs=[row(FOX_DIM), row(FOX_DIM), row(GLA_KDIM)],
        out_shape=[jax.ShapeDtypeStruct((t, FOX_DIM), BF16), jax.ShapeDtypeStruct((t, FOX_DIM), BF16),
                   jax.ShapeDtypeStruct((t, GLA_KDIM), F32)],
        scratch_shapes=[pltpu.VMEM((1, LANES), F32)],
        compiler_params=_cparams(2),
        name="prep",
    )(zs, ltri, lblk, pq, pk, cq, ck, wa_pad, ba)


CONV_HIST = 32


def _conv_kernel(u_ref, w_ref, cb_ref, lg_ref, lb_ref, o_ref, ypad_ref):
    ts = u_ref.shape[0]

    @pl.when(pl.program_id(1) == 0)
    def _():
        ypad_ref[0:CONV_HIST, :] = jnp.zeros((CONV_HIST, CONV_DIM), F32)

    u = u_ref[...].astype(F32)
    ypad_ref[CONV_HIST:CONV_HIST + ts, :] = u[:, :CONV_DIM] * jax.nn.sigmoid(u[:, CONV_DIM:])
    acc = jnp.zeros((ts, CONV_DIM), F32) + cb_ref[...]
    first = CONV_HIST - (CONV_WIDTH - 1)
    for j in range(CONV_WIDTH):
        acc = acc + w_ref[j:j + 1, :] * ypad_ref[first + j:first + j + ts, :]
    ypad_ref[0:CONV_HIST, :] = ypad_ref[ts:ts + CONV_HIST, :]
    mu = jnp.mean(acc, axis=-1, keepdims=True)
    cen = acc - mu
    var = jnp.mean(cen * cen, axis=-1, keepdims=True)
    y = cen * lax.rsqrt(var + EPS) * lg_ref[...] + lb_ref[...]
    o_ref[...] = _silu(y).astype(BF16)


def _conv(z, w, cb, lg, lb, seq):
    t = z.shape[0]
    ts = 512
    per_b = seq // ts
    full = lambda a: pl.BlockSpec(a.shape, lambda b, s: (0,) * a.ndim)
    return pl.pallas_call(
        _conv_kernel,
        grid=(t // seq, per_b),
        in_specs=[pl.BlockSpec((ts, 2 * CONV_DIM), lambda b, s: (b * per_b + s, Z_CONV * LANES // (2 * CONV_DIM))),
                  full(w), full(cb), full(lg), full(lb)],
        out_specs=pl.BlockSpec((ts, CONV_DIM), lambda b, s: (b * per_b + s, 0)),
        out_shape=jax.ShapeDtypeStruct((t, CONV_DIM), BF16),
        scratch_shapes=[pltpu.VMEM((ts + CONV_HIST, CONV_DIM), F32)],
        compiler_params=_cparams(2),
        name="conv",
    )(z, w, cb, lg, lb)


FOX_HEADS_PER_STEP = 2


def _fox_kernel(q_ref, aq_ref, k_ref, ak_ref, v_ref, o_ref, kk_ref, vt_ref, m_ref, acc_ref, *, blk):
    i = pl.program_id(2)
    dh = FOX_HEAD_DIM
    heads = range(FOX_HEADS_PER_STEP)
    col = lambda h: slice(h * dh, (h + 1) * dh)

    @pl.when(i == 0)
    def _():
        def fill(cb, carry):
            rows = pl.ds(pl.multiple_of(cb * blk, blk), blk)
            for h in heads:
                kk_ref[h, rows, 0:dh] = k_ref[rows, col(h)]
                kk_ref[h, rows, dh:2 * dh] = ak_ref[rows, col(h)]
                vt_ref[h, cb, 0:dh, :] = v_ref[rows, col(h)].astype(F32).T.astype(BF16)
                vt_ref[h, cb, dh:2 * dh, :] = jnp.ones((dh, blk), BF16)
            return carry
        lax.fori_loop(0, k_ref.shape[0] // blk, fill, 0)

    qq = [jnp.concatenate([q_ref[:, col(h)], aq_ref[:, col(h)]], axis=1) for h in heads]
    m_ref[...] = jnp.full(m_ref.shape, NEG, F32)
    acc_ref[...] = jnp.zeros(acc_ref.shape, F32)

    def step(j, masked):
        rows = pl.ds(pl.multiple_of(j * blk, blk), blk)
        for h in heads:
            st = _dot_nt(kk_ref[h, rows, :], qq[h])
            if masked:
                kv = lax.broadcasted_iota(I32, (blk, blk), 0)
                qi = lax.broadcasted_iota(I32, (blk, blk), 1)
                st = jnp.where(kv <= qi, st, NEG)
            m_old = m_ref[h]
            m_new = jnp.maximum(m_old, jnp.max(st, axis=0, keepdims=True))
            p = jnp.exp2(st - m_new).astype(BF16)
            acc_ref[h] = acc_ref[h] * jnp.exp2(m_old - m_new) + _dot(vt_ref[h, j], p)
            m_ref[h] = m_new

    def body(j, carry):
        step(j, False)
        return carry

    lax.fori_loop(0, i, body, 0)
    step(i, True)
    for h in heads:
        acc = acc_ref[h]
        o_ref[:, col(h)] = (acc[:dh, :] / acc[dh:, :]).T.astype(BF16)


def _fox(z, aq, ak, seq):
    t = z.shape[0]
    blk = min(1024, seq)
    nq = seq // blk
    hps = FOX_HEADS_PER_STEP
    width = hps * FOX_HEAD_DIM
    unit = lambda base: base * LANES // width
    qspec = lambda base: pl.BlockSpec((blk, width), lambda b, g, i: (b * nq + i, unit(base) + g))
    kspec = lambda base: pl.BlockSpec((seq, width), lambda b, g, i: (b, unit(base) + g))
    return pl.pallas_call(
        functools.partial(_fox_kernel, blk=blk),
        grid=(t // seq, FOX_HEADS // hps, nq),
        in_specs=[qspec(Z_QF), qspec(0), kspec(Z_KF), kspec(0), kspec(Z_VF)],
        out_specs=qspec(0),
        out_shape=jax.ShapeDtypeStruct((t, FOX_DIM), BF16),
        scratch_shapes=[pltpu.VMEM((hps, seq, 2 * FOX_HEAD_DIM), BF16),
                        pltpu.VMEM((hps, nq, 2 * FOX_HEAD_DIM, blk), BF16),
                        pltpu.VMEM((hps, 1, blk), F32), pltpu.VMEM((hps, 2 * FOX_HEAD_DIM, blk), F32)],
        compiler_params=_cparams(3),
        name="fox",
    )(z, aq, z, ak, z)


def _gla_chunk(q, k, v, b, st, e3):
    c, nsub = GLA_CHUNK, GLA_CHUNK // GLA_SUB
    row = lax.broadcasted_iota(I32, (c, GLA_KDIM), 0)
    rblk, rloc = row // GLA_SUB, row % GLA_SUB
    lane_k = lax.broadcasted_iota(I32, (c, GLA_KDIM), 1)
    khead, kpos = lane_k // GLA_DK, lane_k % GLA_DK
    b3 = b.reshape(nsub, GLA_SUB, GLA_KDIM)
    k3 = k.reshape(nsub, GLA_SUB, GLA_KDIM)
    last = b3[:, GLA_SUB - 1:GLA_SUB, :]
    bcast = lambda a3, i: jnp.broadcast_to(a3[:, i:i + 1, :], (nsub, GLA_SUB, GLA_KDIM)).reshape(c, GLA_KDIM)

    prev = jnp.concatenate([last[:1], last[:nsub - 1]], axis=0)
    refq = jnp.broadcast_to(prev, (nsub, GLA_SUB, GLA_KDIM)).reshape(c, GLA_KDIM)
    qt = q * jnp.exp(jnp.minimum(b - refq, 0.0))
    qcat = jnp.concatenate([jnp.where(rblk == i, qt, 0.0) for i in range(1, nsub)], axis=1).astype(BF16)
    kts = []
    for i in range(1, nsub):
        kt = k * jnp.exp(jnp.minimum(last[i - 1] - b, 0.0))
        kts.append(jnp.where(row < GLA_SUB * i, kt, 0.0))
    kcat = jnp.concatenate(
        [jnp.concatenate([jnp.where(khead == h, kt, 0.0) for kt in kts], axis=1) for h in range(GLA_HEADS)],
        axis=0).astype(BF16)
    a_all = _dot_nt(qcat, kcat)

    for sl in range(GLA_SUB):
        p = q * bcast(k3, sl) * jnp.exp(jnp.minimum(b - bcast(b3, sl), 0.0))
        w = _dot(p.astype(BF16), e3)
        hit = (kpos == rblk * GLA_SUB + sl) & (rloc >= sl)
        a_all = a_all + jnp.where(hit, w, 0.0)

    lane_v = lax.broadcasted_iota(I32, (c, GLA_VDIM), 1) // GLA_DV
    vbd = jnp.concatenate([jnp.where(lane_v == h, v, jnp.zeros_like(v)) for h in range(GLA_HEADS)], axis=0)
    qe = (q * jnp.exp(b)).astype(BF16)
    o = _dot(a_all.astype(BF16), vbd) + _dot_nt(qe, st.astype(BF16))

    blast = b[c - 1:c, :]
    kd = (k * jnp.exp(blast - b)).astype(BF16)
    shead_v = lax.broadcasted_iota(I32, (GLA_VDIM, GLA_KDIM), 0) // GLA_DV
    shead_k = lax.broadcasted_iota(I32, (GLA_VDIM, GLA_KDIM), 1) // GLA_DK
    st_new = jnp.where(shead_v == shead_k, st * jnp.exp(blast) + _dot_tn(v, kd), 0.0)
    return o, st_new


def _gla_kernel(q_ref, k_ref, v_ref, r_ref, b_ref, ng_ref, e3_ref, o_ref, st_ref):
    @pl.when(pl.program_id(1) == 0)
    def _():
        st_ref[...] = jnp.zeros_like(st_ref)

    e3 = e3_ref[...]
    ng = ng_ref[...]

    def body(ci, carry):
        rows = pl.ds(pl.multiple_of(ci * GLA_CHUNK, GLA_CHUNK), GLA_CHUNK)
        o, st_new = _gla_chunk(q_ref[rows, :].astype(F32), k_ref[rows, :].astype(F32), v_ref[rows, :],
                               b_ref[rows, :], st_ref[...], e3)
        st_ref[...] = st_new
        outs = []
        for h in range(GLA_HEADS):
            oh = o[:, h * GLA_DV:(h + 1) * GLA_DV]
            outs.append(oh * lax.rsqrt(jnp.mean(oh * oh, axis=-1, keepdims=True) + EPS))
        y = jnp.concatenate(outs, axis=1) * ng * _silu(r_ref[rows, :].astype(F32))
        o_ref[rows, :] = y.astype(BF16)
        return carry

    lax.fori_loop(0, q_ref.shape[0] // GLA_CHUNK, body, 0)


def _gla(z, bc, ng, seq):
    t = z.shape[0]
    ts = min(512, seq)
    per_b = seq // ts
    lane = np.arange(GLA_KDIM)
    e3 = jnp.asarray((lane[:, None] // GLA_DK == lane[None, :] // GLA_DK).astype(np.float32), BF16)
    spec = lambda n, unit: pl.BlockSpec((ts, n), lambda b, s: (b * per_b + s, unit))
    full = lambda a: pl.BlockSpec(a.shape, lambda b, s: (0,) * a.ndim)
    return pl.pallas_call(
        _gla_kernel,
        grid=(t // seq, per_b),
        in_specs=[spec(GLA_KDIM, Z_QG * LANES // GLA_KDIM), spec(GLA_KDIM, Z_KG * LANES // GLA_KDIM),
                  spec(GLA_VDIM, Z_VG * LANES // GLA_VDIM), spec(GLA_VDIM, Z_RG * LANES // GLA_VDIM),
                  spec(GLA_KDIM, 0), full(ng), full(e3)],
        out_specs=spec(GLA_VDIM, 0),
        out_shape=jax.ShapeDtypeStruct((t, GLA_VDIM), BF16),
        scratch_shapes=[pltpu.VMEM((GLA_VDIM, GLA_KDIM), F32)],
        compiler_params=_cparams(2),
        name="gla",
    )(z, z, z, z, bc, ng, e3)


def _merge_kernel(x_ref, g1_ref, ya_ref, yb_ref, yc_ref, ga_ref, gb_ref, gc_ref,
                  wa_ref, wb_ref, wc_ref, wo_ref, o_ref):
    gate = lambda r: jax.nn.sigmoid(r[...].astype(F32))
    merged = (gate(ga_ref) * _dot(ya_ref[...], wa_ref[...]) + gate(gb_ref) * _dot(yb_ref[...], wb_ref[...])
              + gate(gc_ref) * _dot(yc_ref[...], wc_ref[...]))
    o_ref[...] = x_ref[...] + g1_ref[...] * _dot(merged.astype(BF16), wo_ref[...])


def _merge(x2, g1, ya, yb, yc, z, wa, wb, wc, wo, seq):
    t, d = x2.shape
    tm = 256
    per_b = seq // tm
    full = lambda a: pl.BlockSpec(a.shape, lambda i: (0,) * a.ndim)
    row = lambda n, unit=0: pl.BlockSpec((tm, n), lambda i: (i, unit))
    return pl.pallas_call(
        _merge_kernel,
        grid=(t // tm,),
        in_specs=[row(d), pl.BlockSpec((None, 1, d), lambda i: (i // per_b, 0, 0)),
                  row(CONV_DIM), row(FOX_DIM), row(GLA_VDIM), row(d, 0), row(d, 1), row(d, 2),
                  full(wa), full(wb), full(wc), full(wo)],
        out_specs=row(d),
        out_shape=jax.ShapeDtypeStruct((t, d), F32),
        compiler_params=_cparams(1),
        name="merge",
    )(x2, g1, ya, yb, yc, z, z, z, wa, wb, wc, wo)


def _pack_halves(y):
    n = y.shape[1] // 2
    lo = pltpu.bitcast(y[:, :n].astype(BF16).astype(F32), U32)
    hi = pltpu.bitcast(y[:, n:].astype(BF16).astype(F32), U32)
    return (hi & jnp.uint32(0xFFFF0000)) | (lo >> 16)


def _unpack_halves(p):
    lo = pltpu.bitcast(p << 16, F32)
    hi = pltpu.bitcast(p & jnp.uint32(0xFFFF0000), F32)
    return lo, hi


def _first_index(hit, idx, n):
    return jnp.min(jnp.where(hit, idx, n), axis=0, keepdims=True)


def _route_kernel(x_ref, g_ref, sc_ref, sh_ref, wrh_ref, wrl_ref, eb_ref, u_ref,
                  te_ref, tw_ref, rk_ref, cnt_ref, carry_ref):
    @pl.when(pl.program_id(0) == 0)
    def _():
        carry_ref[...] = jnp.zeros_like(carry_ref)

    x = x_ref[...]
    h = _modulated_norm(x, g_ref[...], sc_ref[...], sh_ref[...])
    hb = h.astype(BF16)

    hlo = (h - hb.astype(F32)).astype(BF16)
    logits = _dot_nt(wrh_ref[...], hb) + _dot_nt(wrh_ref[...], hlo) + _dot_nt(wrl_ref[...], hb)
    scores = jax.nn.sigmoid(logits)
    biased = scores + eb_ref[...]
    tm = x.shape[0]
    per_group = N_EXPERTS // N_GROUPS
    sub = lax.broadcasted_iota(I32, (per_group, tm), 0).astype(F32)
    gscore = []
    for g in range(N_GROUPS):
        blk = biased[g * per_group:(g + 1) * per_group, :]
        m1 = jnp.max(blk, axis=0, keepdims=True)
        rest = jnp.where(sub == _first_index(blk == m1, sub, float(per_group)), -jnp.inf, blk)
        gscore.append(m1 + jnp.max(rest, axis=0, keepdims=True))
    gs = jnp.concatenate(gscore, axis=0)
    gidx = lax.broadcasted_iota(I32, (N_GROUPS, tm), 0).astype(F32)
    gsel = jnp.zeros((N_GROUPS, tm), F32)
    for _ in range(TOPK_GROUPS):
        m = jnp.max(gs, axis=0, keepdims=True)
        hit = gidx == _first_index(gs == m, gidx, float(N_GROUPS))
        gsel = jnp.where(hit, 1.0, gsel)
        gs = jnp.where(hit, -jnp.inf, gs)
    cand = jnp.concatenate(
        [jnp.where(gsel[g:g + 1, :] > 0.5, biased[g * per_group:(g + 1) * per_group, :], -jnp.inf)
         for g in range(N_GROUPS)], axis=0)
    eidx = lax.broadcasted_iota(I32, (N_EXPERTS, tm), 0).astype(F32)
    hits, tops, tws = [], [], []
    sel = jnp.zeros((N_EXPERTS, tm), F32)
    for _ in range(TOP_K):
        m = jnp.max(cand, axis=0, keepdims=True)
        e = _first_index(cand == m, eidx, float(N_EXPERTS))
        hit = eidx == e
        hits.append(hit)
        tops.append(e)
        tws.append(jnp.sum(jnp.where(hit, scores, 0.0), axis=0, keepdims=True))
        cand = jnp.where(hit, -jnp.inf, cand)
        sel = jnp.where(hit, 1.0, sel)
    te_ref[...] = jnp.concatenate(tops, axis=0).astype(I32)
    tw = jnp.concatenate(tws, axis=0)
    tw_ref[...] = tw / jnp.sum(tw, axis=0, keepdims=True) * ROUTED_SCALE

    before = _dot(sel.astype(BF16), u_ref[...]) + carry_ref[...]
    rk_ref[...] = jnp.concatenate(
        [jnp.sum(jnp.where(hit, before, 0.0), axis=0, keepdims=True) for hit in hits], axis=0).astype(I32)
    carry = carry_ref[...] + jnp.sum(sel, axis=1, keepdims=True)
    carry_ref[...] = carry
    cnt_ref[...] = jnp.broadcast_to(carry, cnt_ref.shape).astype(I32)


MOE_TM = 256


def _route(x2, g, sc, sh, wrh, wrl, eb, seq):
    t, d = x2.shape
    tm = MOE_TM
    per_b = seq // tm
    r = np.arange(tm)
    u = jnp.asarray((r[:, None] < r[None, :]).astype(np.float32), BF16)
    full = lambda a: pl.BlockSpec(a.shape, lambda i: (0,) * a.ndim)
    vec = lambda: pl.BlockSpec((None, 1, d), lambda i: (i // per_b, 0, 0))
    colk = lambda: pl.BlockSpec((TOP_K, tm), lambda i: (0, i))
    return pl.pallas_call(
        _route_kernel,
        grid=(t // tm,),
        in_specs=[pl.BlockSpec((tm, d), lambda i: (i, 0)), full(g), vec(), vec(),
                  full(wrh), full(wrl), full(eb), full(u)],
        out_specs=[colk(), colk(), colk(), pl.BlockSpec((N_EXPERTS, LANES), lambda i: (0, 0))],
        out_shape=[jax.ShapeDtypeStruct((TOP_K, t), I32), jax.ShapeDtypeStruct((TOP_K, t), F32),
                   jax.ShapeDtypeStruct((TOP_K, t), I32), jax.ShapeDtypeStruct((N_EXPERTS, LANES), I32)],
        scratch_shapes=[pltpu.VMEM((N_EXPERTS, 1), F32)],
        compiler_params=_cparams(1),
        name="route",
    )(x2, g, sc, sh, wrh, wrl, eb, u)


ROW_TILE = 8


def _tiles_to_rows(ref, lead, first, m):
    return jnp.concatenate(
        [ref[lead + (pl.ds(first * ROW_TILE + j, m, stride=ROW_TILE), slice(None))] for j in range(ROW_TILE)], axis=1)


def _rows_to_tiles(ref, lead, val):
    m = val.shape[0]
    for j in range(ROW_TILE):
        ref[lead + (pl.ds(j, m, stride=ROW_TILE), slice(None))] = val[:, j * LANES:(j + 1) * LANES]


def _tile_rows(ref, lead, row):
    return ref.at[lead + (pl.ds(pl.multiple_of(row, ROW_TILE) if not isinstance(row, int) else row, ROW_TILE),)]


def _dispatch_kernel(pend_ref, dst_ref, x_ref, g_ref, sc_ref, sh_ref, g2_ref, ws1_ref, ws3_ref, ws2_ref,
                     base_ref, xs_hbm, hbuf, zbuf, sem, zsem, *, n_tiles):
    i = pl.program_id(0)
    odd = i % 2 == 1
    even = jnp.logical_not(odd)
    tm = x_ref.shape[0]

    @pl.when(i == 0)
    def _():
        zbuf[...] = jnp.zeros_like(zbuf)
        def clear(e, start):
            first = jnp.maximum(pend_ref[e] - MOE_BLOCK, 0) * ROW_TILE
            cp = pltpu.make_async_copy(zbuf, xs_hbm.at[pl.ds(pl.multiple_of(first, ROW_TILE), zbuf.shape[0])], zsem)
            cp.start() if start else cp.wait()
        for e in range(N_EXPERTS):
            clear(e, True)
        for e in range(N_EXPERTS):
            clear(e, False)

    def copies(sl, start):
        for k in range(TOP_K):
            for t in range(tm):
                cp = pltpu.make_async_copy(_tile_rows(hbuf, (sl,), ROW_TILE * t),
                                           _tile_rows(xs_hbm, (), dst_ref[k, t] if start else 0), sem.at[sl])
                cp.start(priority=(k * tm + t) % 2) if start else cp.wait()

    pl.when((i >= 2) & even)(functools.partial(copies, 0, False))
    pl.when((i >= 2) & odd)(functools.partial(copies, 1, False))

    def main(sl):
        x = x_ref[...]
        h = _modulated_norm(x, g_ref[...], sc_ref[...], sh_ref[...])
        _rows_to_tiles(hbuf, (sl,), _pack_halves(h))
        hb = h.astype(BF16)
        hs = (_silu(_dot(hb, ws1_ref[...])) * _dot(hb, ws3_ref[...])).astype(BF16)
        base_ref[...] = x + g2_ref[...] * _dot(hs, ws2_ref[...])
        copies(sl, True)

    pl.when(even)(functools.partial(main, 0))
    pl.when(odd)(functools.partial(main, 1))

    @pl.when(i == n_tiles - 1)
    def _():
        if n_tiles >= 2:
            copies(n_tiles % 2, False)
        copies((n_tiles - 1) % 2, False)


def _dispatch(x2, g, sc, sh, g2, ws1, ws3, ws2, pends, dest, n_rows, seq):
    t, d = x2.shape
    tm = MOE_TM
    per_b = seq // tm
    full = lambda a: pl.BlockSpec(a.shape, lambda i, pe: (0,) * a.ndim)
    vec = lambda: pl.BlockSpec((None, 1, d), lambda i, pe: (i // per_b, 0, 0))
    return pl.pallas_call(
        functools.partial(_dispatch_kernel, n_tiles=t // tm),
        grid_spec=pltpu.PrefetchScalarGridSpec(
            num_scalar_prefetch=1,
            grid=(t // tm,),
            in_specs=[pl.BlockSpec((TOP_K, tm), lambda i, pe: (0, i), memory_space=pltpu.SMEM),
                      pl.BlockSpec((tm, d), lambda i, pe: (i, 0)), full(g), vec(), vec(), vec(),
                      full(ws1), full(ws3), full(ws2)],
            out_specs=[pl.BlockSpec((tm, d), lambda i, pe: (i, 0)), pl.BlockSpec(memory_space=pl.ANY)],
            scratch_shapes=[pltpu.VMEM((2, tm * ROW_TILE, LANES), U32),
                            pltpu.VMEM((MOE_BLOCK * ROW_TILE, LANES), U32),
                            pltpu.SemaphoreType.DMA((2,)), pltpu.SemaphoreType.DMA(())]),
        out_shape=[jax.ShapeDtypeStruct((t, d), F32), jax.ShapeDtypeStruct((n_rows * ROW_TILE, LANES), U32)],
        compiler_params=_cparams(1),
        name="dispatch",
    )(pends, dest, x2, g, sc, sh, g2, ws1, ws3, ws2)


def _experts_kernel(be_ref, nu_ref, xs_ref, w1_ref, w3_ref, w2_ref, ys_ref, w1b, w3b, w2b):
    s = pl.program_id(0)

    @pl.when((s == 0) | (be_ref[s] != be_ref[jnp.maximum(s - 1, 0)]))
    def _():
        w1b[...] = w1_ref[...].astype(BF16)
        w3b[...] = w3_ref[...].astype(BF16)
        w2b[...] = w2_ref[...].astype(BF16)

    @pl.when(s < nu_ref[0])
    def _():
        lo, hi = _unpack_halves(_tiles_to_rows(xs_ref, (), 0, MOE_BLOCK))
        xb = jnp.concatenate([lo.astype(BF16), hi.astype(BF16)], axis=1)
        hid = (_silu(_dot(xb, w1b[...])) * _dot(xb, w3b[...])).astype(BF16)
        _rows_to_tiles(ys_ref, (), _pack_halves(_dot(hid, w2b[...])))


def _experts(layer, block_e, n_used, xs, w1, w3, w2):
    nb = block_e.shape[0]
    rows = pl.BlockSpec((MOE_BLOCK * ROW_TILE, LANES), lambda s, be, nu: (jnp.minimum(s, nu[0] - 1), 0))
    wspec = lambda a: pl.BlockSpec((None, None) + a.shape[2:], lambda s, be, nu: (layer, be[s], 0, 0))
    return pl.pallas_call(
        _experts_kernel,
        grid_spec=pltpu.PrefetchScalarGridSpec(
            num_scalar_prefetch=2,
            grid=(nb,),
            in_specs=[rows, wspec(w1), wspec(w3), wspec(w2)],
            out_specs=rows,
            scratch_shapes=[pltpu.VMEM(w1.shape[2:], BF16), pltpu.VMEM(w3.shape[2:], BF16),
                            pltpu.VMEM(w2.shape[2:], BF16)]),
        out_shape=jax.ShapeDtypeStruct(xs.shape, U32),
        compiler_params=_cparams(1),
        name="experts",
    )(block_e, n_used, xs, w1, w3, w2)


def _combine_kernel(dst_ref, dstn_ref, base_ref, g2_ref, w_ref, fg_ref, ys_hbm, o_ref, gbuf, sem, *, n_tiles, final):
    i = pl.program_id(0)
    odd = i % 2 == 1
    even = jnp.logical_not(odd)
    tm = base_ref.shape[0]

    def fetch(idx_ref, sl, start):
        for k in range(TOP_K):
            for t in range(tm):
                cp = pltpu.make_async_copy(_tile_rows(ys_hbm, (), idx_ref[k, t] if start else 0),
                                           _tile_rows(gbuf, (sl,), ROW_TILE * (k * tm + t)), sem.at[sl])
                cp.start(priority=(k * tm + t) % 2) if start else cp.wait()

    @pl.when(i == 0)
    def _():
        fetch(dst_ref, 0, True)

    def main(sl):
        fetch(dst_ref, sl, False)
        fetch(dstn_ref, 1 - sl, True)
        half = o_ref.shape[1] // 2
        rb = 32

        def rows_step(ib, carry):
            r0 = pl.multiple_of(ib * rb, rb)
            rows = pl.ds(r0, rb)
            w = w_ref[rows, :]
            wb = [jnp.broadcast_to(w[:, k:k + 1], (rb, LANES)) for k in range(TOP_K)]
            ssq = jnp.zeros((rb, 1), F32)
            for j in range(ROW_TILE):
                acc_lo = acc_hi = None
                for k in range(TOP_K):
                    lo, hi = _unpack_halves(gbuf[sl, pl.ds((k * tm + r0) * ROW_TILE + j, rb, stride=ROW_TILE), :])
                    acc_lo = wb[k] * lo if acc_lo is None else acc_lo + wb[k] * lo
                    acc_hi = wb[k] * hi if acc_hi is None else acc_hi + wb[k] * hi
                for acc, c0 in ((acc_lo, j * LANES), (acc_hi, half + j * LANES)):
                    x = base_ref[rows, c0:c0 + LANES] + g2_ref[:, c0:c0 + LANES] * acc
                    o_ref[rows, c0:c0 + LANES] = x
                    ssq = ssq + jnp.sum(x * x, axis=-1, keepdims=True)
            if final:
                scale = lax.rsqrt(ssq * (1.0 / o_ref.shape[1]) + EPS)
                o_ref[rows, :] = o_ref[rows, :] * scale * fg_ref[...]
            return carry

        lax.fori_loop(0, tm // rb, rows_step, 0)

    pl.when(even)(functools.partial(main, 0))
    pl.when(odd)(functools.partial(main, 1))

    @pl.when(i == n_tiles - 1)
    def _():
        fetch(dst_ref, n_tiles % 2, False)


def _combine(base, g2, w, ys, dest, fg, seq, final):
    t, d = base.shape
    tm = MOE_TM
    per_b = seq // tm
    n_tiles = t // tm
    idx = lambda f: pl.BlockSpec((TOP_K, tm), f, memory_space=pltpu.SMEM)
    return pl.pallas_call(
        functools.partial(_combine_kernel, n_tiles=n_tiles, final=final),
        grid=(n_tiles,),
        in_specs=[idx(lambda i: (0, i)), idx(lambda i: (0, jnp.minimum(i + 1, n_tiles - 1))),
                  pl.BlockSpec((tm, d), lambda i: (i, 0)),
                  pl.BlockSpec((None, 1, d), lambda i: (i // per_b, 0, 0)),
                  pl.BlockSpec((tm, TOP_K), lambda i: (i, 0)),
                  pl.BlockSpec((1, d), lambda i: (0, 0)),
                  pl.BlockSpec(memory_space=pl.ANY)],
        out_specs=pl.BlockSpec((tm, d), lambda i: (i, 0)),
        out_shape=jax.ShapeDtypeStruct((t, d), F32),
        scratch_shapes=[pltpu.VMEM((2, TOP_K * tm * ROW_TILE, LANES), U32), pltpu.SemaphoreType.DMA((2,))],
        compiler_params=_cparams(1),
        name="combine",
    )(dest, dest, base, g2, w, fg, ys)


def _dispatch_tables(top_e, rank, counts, t):
    n_blocks = TOP_K * t // MOE_BLOCK + N_EXPERTS
    padded = (counts + MOE_BLOCK - 1) // MOE_BLOCK * MOE_BLOCK
    pends = jnp.cumsum(padded)
    pstarts = pends - padded
    experts = jnp.arange(N_EXPERTS, dtype=I32)
    pstart_of = jnp.sum(jnp.where(top_e[..., None] == experts, pstarts, 0), axis=-1)
    dest = (pstart_of + rank) * ROW_TILE
    starts = jnp.arange(n_blocks, dtype=I32) * MOE_BLOCK
    block_e = jnp.sum((pends[None, :] <= starts[:, None]).astype(I32), axis=1)
    block_e = jnp.minimum(block_e, jnp.max(jnp.where(starts < pends[-1], block_e, 0)))
    n_used = (pends[-1:] // MOE_BLOCK).astype(I32)
    return block_e, n_used, pends.astype(I32), dest, n_blocks * MOE_BLOCK


def _layout_inproj(w_in, b_in):
    o = np.cumsum([0, 2 * CONV_DIM, FOX_DIM, FOX_DIM, FOX_DIM, FOX_HEADS, GLA_KDIM, GLA_KDIM, GLA_VDIM, GLA_VDIM,
                   GLA_RANK, 3 * D_MODEL])
    seg = lambda a, i: a[..., int(o[i]):int(o[i + 1])]

    def main(a):
        return jnp.concatenate([seg(a, 10), seg(a, 0), seg(a, 1) * (FOX_HEAD_DIM ** -0.5 * LOG2E), seg(a, 2), seg(a, 3),
                                seg(a, 5) * GLA_DK ** -0.5, seg(a, 6), seg(a, 7), seg(a, 8)], axis=-1)

    def small(a):
        pad = jnp.zeros(a.shape[:-1] + (LANES - FOX_HEADS - GLA_RANK,), a.dtype)
        return jnp.concatenate([seg(a, 4), seg(a, 9), pad], axis=-1)

    return main(w_in).astype(BF16), main(b_in)[None, :], small(w_in).astype(BF16), small(b_in)[None, :]


def kernel(x, c, norm1_g, ada_w, ada_b, w_in, b_in, conv_w, conv_b, conv_ln_g, conv_ln_b, gla_wa, gla_ba,
           gla_norm_g, w_branch_a, w_branch_b, w_branch_c, w_out, norm2_g, w_router, e_bias, w1, w3, w2,
           ws1, ws3, ws2, final_g):
    bsz, seq, d = x.shape
    t = bsz * seq
    depth = ada_w.shape[0]
    mod = _ada_mod(c, ada_w, ada_b).reshape(depth, bsz, 6, 1, d)
    x2 = x.reshape(t, d)
    row = lambda a: a.reshape(1, -1)
    for l in range(depth):
        sh1, sc1, g1, sh2, sc2, g2 = (mod[l, :, i] for i in range(6))
        w, b, ws, bs = _layout_inproj(w_in[l], b_in[l])
        z, zs = _inproj(x2, row(norm1_g[l]), sc1, sh1, w, b, ws, bs, seq)
        wa_pad = jnp.zeros((LANES, GLA_KDIM), F32).at[ZS_A:ZS_A + GLA_RANK].set(gla_wa[l])
        aq, ak, bc = _prep(zs, wa_pad, row(gla_ba[l]), seq)
        ya = _conv(z, conv_w[l].reshape(CONV_WIDTH, CONV_DIM), row(conv_b[l]), row(conv_ln_g[l]),
                   row(conv_ln_b[l]), seq)
        yb = _fox(z, aq, ak, seq)
        yc = _gla(z, bc, row(gla_norm_g[l]), seq)
        x2 = _merge(x2, g1, ya, yb, yc, z, w_branch_a[l].astype(BF16), w_branch_b[l].astype(BF16),
                    w_branch_c[l].astype(BF16), w_out[l].astype(BF16), seq)

        wr_t = w_router[l].T
        wrh = wr_t.astype(BF16)
        wrl = (wr_t - wrh.astype(F32)).astype(BF16)
        top_e, top_w, rank, counts = _route(x2, row(norm2_g[l]), sc2, sh2, wrh, wrl,
                                            e_bias[l].reshape(N_EXPERTS, 1), seq)
        block_e, n_used, pends, dest, n_rows = _dispatch_tables(top_e, rank, counts[:, 0], t)
        base, xs = _dispatch(x2, row(norm2_g[l]), sc2, sh2, g2, ws1[l].astype(BF16), ws3[l].astype(BF16),
                             ws2[l].astype(BF16), pends, dest, n_rows, seq)
        ys = _experts(l, block_e, n_used, xs, w1, w3, w2)
        x2 = _combine(base, g2, top_w.T, ys, dest, row(final_g), seq, final=(l == depth - 1))
    return x2.reshape(bsz, seq, d)
```

```python
import functools

import numpy as np
import jax
import jax.numpy as jnp
from jax import lax
from jax.experimental import pallas as pl
from jax.experimental.pallas import tpu as pltpu

F32 = jnp.float32
BF16 = jnp.bfloat16
U32 = jnp.uint32
I32 = jnp.int32

D_MODEL = 2048
CONV_DIM = 512
CONV_WIDTH = 31
FOX_HEADS = 8
FOX_HEAD_DIM = 128
FOX_DIM = FOX_HEADS * FOX_HEAD_DIM
GLA_HEADS = 4
GLA_DK = 64
GLA_DV = 128
GLA_KDIM = GLA_HEADS * GLA_DK
GLA_VDIM = GLA_HEADS * GLA_DV
GLA_RANK = 16
GLA_TAU = 16.0
GLA_CHUNK = 64
GLA_SUB = 8
N_EXPERTS = 64
N_GROUPS = 8
TOPK_GROUPS = 4
TOP_K = 8
EXPERT_DIM = 512
SHARED_DIM = 512
ROUTED_SCALE = 2.5
MOE_BLOCK = 512
EPS = 1e-6

LANES = 128
V7X_VMEM_LIMIT = 56 * 1024 * 1024

Z_GATE = 0
Z_CONV = 48
Z_QF = 56
Z_KF = 64
Z_VF = 72
Z_QG = 80
Z_KG = 82
Z_VG = 84
Z_RG = 88
Z_COLS = 92 * LANES
ZS_F = 0
ZS_A = 8
NEG = -1e30
LOG2E = 1.4426950408889634


def _cparams(n_axes, vmem=V7X_VMEM_LIMIT, flags=None):
    return pltpu.CompilerParams(dimension_semantics=("arbitrary",) * n_axes, vmem_limit_bytes=vmem, flags=flags)


def _log_sigmoid(x):
    return jnp.minimum(x, 0.0) - jnp.log(1.0 + jnp.exp(-jnp.abs(x)))


def _silu(x):
    return x * jax.nn.sigmoid(x)


def _split3(x):
    hi = x.astype(BF16)
    r1 = x - hi.astype(F32)
    mid = r1.astype(BF16)
    lo = (r1 - mid.astype(F32)).astype(BF16)
    return hi, mid, lo


def _dot(a, b):
    return jnp.dot(a, b, preferred_element_type=F32)


def _dot_nt(a, b):
    return lax.dot_general(a, b, (((1,), (1,)), ((), ())), preferred_element_type=F32)


def _dot_tn(a, b):
    return lax.dot_general(a, b, (((0,), (0,)), ((), ())), preferred_element_type=F32)


def _lmul_exact(l01, x):
    hi, mid, lo = _split3(x)
    return _dot(l01, hi) + _dot(l01, mid) + _dot(l01, lo)


def _dot_f32(a, b):
    a0, a1, a2 = _split3(a)
    b0, b1, b2 = _split3(b)
    return (_dot(a0, b0) + _dot(a0, b1) + _dot(a1, b0)) + (_dot(a0, b2) + _dot(a1, b1) + _dot(a2, b0))


def _ada_kernel(ct_ref, w_ref, b_ref, o_ref):
    ct = ct_ref[...]
    act = _silu(ct)
    w = w_ref[...]
    rows = [jnp.sum(w * act[:, b:b + 1], axis=0, keepdims=True) for b in range(ct.shape[1])]
    o_ref[...] = jnp.concatenate(rows, axis=0) + b_ref[...]


def _ada_mod(c, ada_w, ada_b):
    n_l, d, n = ada_w.shape
    bsz = c.shape[0]
    tn = 1024
    return pl.pallas_call(
        _ada_kernel,
        grid=(n_l, n // tn),
        in_specs=[pl.BlockSpec((d, bsz), lambda l, j: (0, 0)),
                  pl.BlockSpec((None, d, tn), lambda l, j: (l, 0, j)),
                  pl.BlockSpec((None, 1, tn), lambda l, j: (l, 0, j))],
        out_specs=pl.BlockSpec((None, bsz, tn), lambda l, j: (l, 0, j)),
        out_shape=jax.ShapeDtypeStruct((n_l, bsz, n), F32),
        compiler_params=_cparams(2),
        name="ada_mod",
    )(c.T, ada_w, ada_b.reshape(n_l, 1, n))


def _modulated_norm(x, g, sc, sh):
    ms = jnp.mean(x * x, axis=-1, keepdims=True)
    return (x * lax.rsqrt(ms + EPS) * g) * (1.0 + sc) + sh


def _inproj_kernel(x_ref, g_ref, sc_ref, sh_ref, w_ref, b_ref, ws_ref, bs_ref, z_ref, zs_ref, h_scr):
    @pl.when(pl.program_id(1) == 0)
    def _():
        h = _modulated_norm(x_ref[...], g_ref[...], sc_ref[...], sh_ref[...]).astype(BF16)
        h_scr[...] = h
        zs_ref[...] = _dot(h, ws_ref[...]) + bs_ref[...]

    z_ref[...] = (_dot(h_scr[...], w_ref[...]) + b_ref[...]).astype(BF16)


def _inproj(x2, g, sc, sh, w, b, ws, bs, seq):
    t, d = x2.shape
    tm, tn = 1024, 512
    per_b = seq // tm
    vec = lambda: pl.BlockSpec((None, 1, d), lambda i, j: (i // per_b, 0, 0))
    return pl.pallas_call(
        _inproj_kernel,
        grid=(t // tm, Z_COLS // tn),
        in_specs=[pl.BlockSpec((tm, d), lambda i, j: (i, 0)),
                  pl.BlockSpec((1, d), lambda i, j: (0, 0)),
                  vec(), vec(),
                  pl.BlockSpec((d, tn), lambda i, j: (0, j)),
                  pl.BlockSpec((1, tn), lambda i, j: (0, j)),
                  pl.BlockSpec((d, LANES), lambda i, j: (0, 0)),
                  pl.BlockSpec((1, LANES), lambda i, j: (0, 0))],
        out_specs=[pl.BlockSpec((tm, tn), lambda i, j: (i, j)),
                   pl.BlockSpec((tm, LANES), lambda i, j: (i, 0))],
        out_shape=[jax.ShapeDtypeStruct((t, Z_COLS), BF16), jax.ShapeDtypeStruct((t, LANES), F32)],
        scratch_shapes=[pltpu.VMEM((tm, d), BF16)],
        compiler_params=_cparams(2),
        name="inproj",
    )(x2, g, sc, sh, w, b, ws, bs)


def _prep_kernel(zs_ref, ltri_ref, lblk_ref, pq_ref, pk_ref, cq_ref, ck_ref, wa_ref, ba_ref,
                 aq_ref, ak_ref, bc_ref, carry_ref):
    @pl.when(pl.program_id(1) == 0)
    def _():
        carry_ref[...] = jnp.zeros_like(carry_ref)

    zs = zs_ref[...]
    cum = _lmul_exact(ltri_ref[...], _log_sigmoid(zs)) + carry_ref[...]
    carry_ref[...] = cum[cum.shape[0] - 1:, :]
    hi, mid, lo = _split3(cum * LOG2E)
    aq = cq_ref[...] + _dot(hi, pq_ref[0]) + _dot(mid, pq_ref[1]) + _dot(lo, pq_ref[2])
    ak = ck_ref[...] + _dot(hi, pk_ref[0]) + _dot(mid, pk_ref[1]) + _dot(lo, pk_ref[2])
    aq_ref[...] = aq.astype(BF16)
    ak_ref[...] = ak.astype(BF16)
    la = _log_sigmoid(_dot_f32(zs, wa_ref[...]) + ba_ref[...]) * (1.0 / GLA_TAU)
    bc_ref[...] = _lmul_exact(lblk_ref[...], la)


def _prep_consts(ts):
    r = np.arange(ts)
    ltri = (r[:, None] >= r[None, :]).astype(np.float32)
    lblk = ltri * (r[:, None] // GLA_CHUNK == r[None, :] // GLA_CHUNK)
    pq = np.zeros((3, LANES, FOX_DIM), np.float32)
    pk = np.zeros((3, LANES, FOX_DIM), np.float32)
    cq = np.zeros((1, FOX_DIM), np.float32)
    ck = np.zeros((1, FOX_DIM), np.float32)
    for h in range(FOX_HEADS):
        for p in range(3):
            pq[p, ZS_F + h, h * FOX_HEAD_DIM + p] = 1.0
            pk[p, ZS_F + h, h * FOX_HEAD_DIM + 3 + p] = -1.0
            cq[0, h * FOX_HEAD_DIM + 3 + p] = 1.0
            ck[0, h * FOX_HEAD_DIM + p] = 1.0
    return (jnp.asarray(ltri, BF16), jnp.asarray(lblk, BF16), jnp.asarray(pq, BF16), jnp.asarray(pk, BF16),
            jnp.asarray(cq), jnp.asarray(ck))


def _prep(zs, wa_pad, ba, seq):
    t = zs.shape[0]
    ts = 256
    per_b = seq // ts
    ltri, lblk, pq, pk, cq, ck = _prep_consts(ts)
    full = lambda a: pl.BlockSpec(a.shape, lambda b, s: (0,) * a.ndim)
    row = lambda n: pl.BlockSpec((ts, n), lambda b, s: (b * per_b + s, 0))
    return pl.pallas_call(
        _prep_kernel,
        grid=(t // seq, per_b),
        in_specs=[row(LANES), full(ltri), full(lblk), full(pq), full(pk), full(cq), full(ck),
                  full(wa_pad), full(ba)],
        out_specs=[row(FOX_DIM), row(FOX_DIM), row(GLA_KDIM)],
        out_shape=[jax.ShapeDtypeStruct((t, FOX_DIM), BF16), jax.ShapeDtypeStruct((t, FOX_DIM), BF16),
                   jax.ShapeDtypeStruct((t, GLA_KDIM), F32)],
        scratch_shapes=[pltpu.VMEM((1, LANES), F32)],
        compiler_params=_cparams(2),
        name="prep",
    )(zs, ltri, lblk, pq, pk, cq, ck, wa_pad, ba)


CONV_HIST = 32


def _conv_kernel(u_ref, w_ref, cb_ref, lg_ref, lb_ref, o_ref, ypad_ref, shift_ref):
    ts = u_ref.shape[0]
    sub = 8

    @pl.when(pl.program_id(1) == 0)
    def _():
        ypad_ref[0:CONV_HIST, :] = jnp.zeros((CONV_HIST, CONV_DIM), F32)

    u = u_ref[...].astype(F32)
    ypad_ref[CONV_HIST:CONV_HIST + ts, :] = u[:, :CONV_DIM] * jax.nn.sigmoid(u[:, CONV_DIM:])
    span = ts + CONV_HIST - sub
    for r in range(1, sub):
        shift_ref[r - 1] = ypad_ref[r:r + span, :]
    acc = jnp.zeros((ts, CONV_DIM), F32) + cb_ref[...]
    first = CONV_HIST - (CONV_WIDTH - 1)
    for j in range(CONV_WIDTH):
        r = (first + j) % sub
        base = first + j - r
        rows = ypad_ref[base:base + ts, :] if r == 0 else shift_ref[r - 1, base:base + ts, :]
        acc = acc + w_ref[j:j + 1, :] * rows
    ypad_ref[0:CONV_HIST, :] = ypad_ref[ts:ts + CONV_HIST, :]
    mu = jnp.mean(acc, axis=-1, keepdims=True)
    cen = acc - mu
    var = jnp.mean(cen * cen, axis=-1, keepdims=True)
    y = cen * lax.rsqrt(var + EPS) * lg_ref[...] + lb_ref[...]
    o_ref[...] = _silu(y).astype(BF16)


def _conv(z, w, cb, lg, lb, seq):
    t = z.shape[0]
    ts = 512
    per_b = seq // ts
    full = lambda a: pl.BlockSpec(a.shape, lambda b, s: (0,) * a.ndim)
    return pl.pallas_call(
        _conv_kernel,
        grid=(t // seq, per_b),
        in_specs=[pl.BlockSpec((ts, 2 * CONV_DIM), lambda b, s: (b * per_b + s, Z_CONV * LANES // (2 * CONV_DIM))),
                  full(w), full(cb), full(lg), full(lb)],
        out_specs=pl.BlockSpec((ts, CONV_DIM), lambda b, s: (b * per_b + s, 0)),
        out_shape=jax.ShapeDtypeStruct((t, CONV_DIM), BF16),
        scratch_shapes=[pltpu.VMEM((ts + CONV_HIST, CONV_DIM), F32),
                        pltpu.VMEM((7, ts + CONV_HIST - 8, CONV_DIM), F32)],
        compiler_params=_cparams(2),
        name="conv",
    )(z, w, cb, lg, lb)


FOX_HEADS_PER_STEP = 2


def _fox_kernel(q_ref, aq_ref, k_ref, ak_ref, v_ref, o_ref, kk_ref, vt_ref, m_ref, acc_ref, *, blk):
    i = pl.program_id(2)
    dh = FOX_HEAD_DIM
    heads = range(FOX_HEADS_PER_STEP)
    col = lambda h: slice(h * dh, (h + 1) * dh)

    @pl.when(i == 0)
    def _():
        def fill(cb, carry):
            rows = pl.ds(pl.multiple_of(cb * blk, blk), blk)
            for h in heads:
                kk_ref[h, rows, 0:dh] = k_ref[rows, col(h)]
                kk_ref[h, rows, dh:2 * dh] = ak_ref[rows, col(h)]
                vt_ref[h, cb, 0:dh, :] = v_ref[rows, col(h)].astype(F32).T.astype(BF16)
                vt_ref[h, cb, dh:2 * dh, :] = jnp.ones((dh, blk), BF16)
            return carry
        lax.fori_loop(0, k_ref.shape[0] // blk, fill, 0)

    qq = [jnp.concatenate([q_ref[:, col(h)], aq_ref[:, col(h)]], axis=1) for h in heads]
    m_ref[...] = jnp.full(m_ref.shape, NEG, F32)
    acc_ref[...] = jnp.zeros(acc_ref.shape, F32)

    def step(j, masked):
        rows = pl.ds(pl.multiple_of(j * blk, blk), blk)
        for h in heads:
            st = _dot_nt(kk_ref[h, rows, :], qq[h])
            if masked:
                kv = lax.broadcasted_iota(I32, (blk, blk), 0)
                qi = lax.broadcasted_iota(I32, (blk, blk), 1)
                st = jnp.where(kv <= qi, st, NEG)
            m_old = m_ref[h]
            m_new = jnp.maximum(m_old, jnp.max(st, axis=0, keepdims=True))
            p = jnp.exp2(st - m_new).astype(BF16)
            acc_ref[h] = acc_ref[h] * jnp.exp2(m_old - m_new) + _dot(vt_ref[h, j], p)
            m_ref[h] = m_new

    def body(j, carry):
        step(j, False)
        return carry

    lax.fori_loop(0, i, body, 0)
    step(i, True)
    for h in heads:
        acc = acc_ref[h]
        o_ref[:, col(h)] = (acc[:dh, :] / acc[dh:, :]).T.astype(BF16)


def _fox(z, aq, ak, seq):
    t = z.shape[0]
    blk = min(1024, seq)
    nq = seq // blk
    hps = FOX_HEADS_PER_STEP
    width = hps * FOX_HEAD_DIM
    unit = lambda base: base * LANES // width
    qspec = lambda base: pl.BlockSpec((blk, width), lambda b, g, i: (b * nq + i, unit(base) + g))
    kspec = lambda base: pl.BlockSpec((seq, width), lambda b, g, i: (b, unit(base) + g))
    return pl.pallas_call(
        functools.partial(_fox_kernel, blk=blk),
        grid=(t // seq, FOX_HEADS // hps, nq),
        in_specs=[qspec(Z_QF), qspec(0), kspec(Z_KF), kspec(0), kspec(Z_VF)],
        out_specs=qspec(0),
        out_shape=jax.ShapeDtypeStruct((t, FOX_DIM), BF16),
        scratch_shapes=[pltpu.VMEM((hps, seq, 2 * FOX_HEAD_DIM), BF16),
                        pltpu.VMEM((hps, nq, 2 * FOX_HEAD_DIM, blk), BF16),
                        pltpu.VMEM((hps, 1, blk), F32), pltpu.VMEM((hps, 2 * FOX_HEAD_DIM, blk), F32)],
        compiler_params=_cparams(3),
        name="fox",
    )(z, aq, z, ak, z)


def _gla_chunk(q, k, v, b, st, e3):
    c, nsub = GLA_CHUNK, GLA_CHUNK // GLA_SUB
    row = lax.broadcasted_iota(I32, (c, GLA_KDIM), 0)
    rblk, rloc = row // GLA_SUB, row % GLA_SUB
    lane_k = lax.broadcasted_iota(I32, (c, GLA_KDIM), 1)
    khead, kpos = lane_k // GLA_DK, lane_k % GLA_DK
    b3 = b.reshape(nsub, GLA_SUB, GLA_KDIM)
    k3 = k.reshape(nsub, GLA_SUB, GLA_KDIM)
    last = b3[:, GLA_SUB - 1:GLA_SUB, :]
    bcast = lambda a3, i: jnp.broadcast_to(a3[:, i:i + 1, :], (nsub, GLA_SUB, GLA_KDIM)).reshape(c, GLA_KDIM)

    prev = jnp.concatenate([last[:1], last[:nsub - 1]], axis=0)
    refq = jnp.broadcast_to(prev, (nsub, GLA_SUB, GLA_KDIM)).reshape(c, GLA_KDIM)
    qt = q * jnp.exp(jnp.minimum(b - refq, 0.0))
    qcat = jnp.concatenate([jnp.where(rblk == i, qt, 0.0) for i in range(1, nsub)], axis=1).astype(BF16)
    kts = []
    for i in range(1, nsub):
        kt = k * jnp.exp(jnp.minimum(last[i - 1] - b, 0.0))
        kts.append(jnp.where(row < GLA_SUB * i, kt, 0.0))
    kcat = jnp.concatenate(
        [jnp.concatenate([jnp.where(khead == h, kt, 0.0) for kt in kts], axis=1) for h in range(GLA_HEADS)],
        axis=0).astype(BF16)
    a_all = _dot_nt(qcat, kcat)

    for sl in range(GLA_SUB):
        p = q * bcast(k3, sl) * jnp.exp(jnp.minimum(b - bcast(b3, sl), 0.0))
        w = _dot(p.astype(BF16), e3)
        hit = (kpos == rblk * GLA_SUB + sl) & (rloc >= sl)
        a_all = a_all + jnp.where(hit, w, 0.0)

    lane_v = lax.broadcasted_iota(I32, (c, GLA_VDIM), 1) // GLA_DV
    vbd = jnp.concatenate([jnp.where(lane_v == h, v, jnp.zeros_like(v)) for h in range(GLA_HEADS)], axis=0)
    qe = (q * jnp.exp(b)).astype(BF16)
    o = _dot(a_all.astype(BF16), vbd) + _dot_nt(qe, st.astype(BF16))

    blast = b[c - 1:c, :]
    kd = (k * jnp.exp(blast - b)).astype(BF16)
    shead_v = lax.broadcasted_iota(I32, (GLA_VDIM, GLA_KDIM), 0) // GLA_DV
    shead_k = lax.broadcasted_iota(I32, (GLA_VDIM, GLA_KDIM), 1) // GLA_DK
    st_new = jnp.where(shead_v == shead_k, st * jnp.exp(blast) + _dot_tn(v, kd), 0.0)
    return o, st_new


def _gla_kernel(q_ref, k_ref, v_ref, r_ref, b_ref, ng_ref, e3_ref, o_ref, st_ref):
    @pl.when(pl.program_id(1) == 0)
    def _():
        st_ref[...] = jnp.zeros_like(st_ref)

    e3 = e3_ref[...]
    ng = ng_ref[...]

    def body(ci, carry):
        rows = pl.ds(pl.multiple_of(ci * GLA_CHUNK, GLA_CHUNK), GLA_CHUNK)
        o, st_new = _gla_chunk(q_ref[rows, :].astype(F32), k_ref[rows, :].astype(F32), v_ref[rows, :],
                               b_ref[rows, :], st_ref[...], e3)
        st_ref[...] = st_new
        outs = []
        for h in range(GLA_HEADS):
            oh = o[:, h * GLA_DV:(h + 1) * GLA_DV]
            outs.append(oh * lax.rsqrt(jnp.mean(oh * oh, axis=-1, keepdims=True) + EPS))
        y = jnp.concatenate(outs, axis=1) * ng * _silu(r_ref[rows, :].astype(F32))
        o_ref[rows, :] = y.astype(BF16)
        return carry

    lax.fori_loop(0, q_ref.shape[0] // GLA_CHUNK, body, 0)


def _gla(z, bc, ng, seq):
    t = z.shape[0]
    ts = min(512, seq)
    per_b = seq // ts
    lane = np.arange(GLA_KDIM)
    e3 = jnp.asarray((lane[:, None] // GLA_DK == lane[None, :] // GLA_DK).astype(np.float32), BF16)
    spec = lambda n, unit: pl.BlockSpec((ts, n), lambda b, s: (b * per_b + s, unit))
    full = lambda a: pl.BlockSpec(a.shape, lambda b, s: (0,) * a.ndim)
    return pl.pallas_call(
        _gla_kernel,
        grid=(t // seq, per_b),
        in_specs=[spec(GLA_KDIM, Z_QG * LANES // GLA_KDIM), spec(GLA_KDIM, Z_KG * LANES // GLA_KDIM),
                  spec(GLA_VDIM, Z_VG * LANES // GLA_VDIM), spec(GLA_VDIM, Z_RG * LANES // GLA_VDIM),
                  spec(GLA_KDIM, 0), full(ng), full(e3)],
        out_specs=spec(GLA_VDIM, 0),
        out_shape=jax.ShapeDtypeStruct((t, GLA_VDIM), BF16),
        scratch_shapes=[pltpu.VMEM((GLA_VDIM, GLA_KDIM), F32)],
        compiler_params=_cparams(2),
        name="gla",
    )(z, z, z, z, bc, ng, e3)


def _merge_kernel(x_ref, g1_ref, ya_ref, yb_ref, yc_ref, ga_ref, gb_ref, gc_ref,
                  wa_ref, wb_ref, wc_ref, wo_ref, o_ref):
    gate = lambda r: jax.nn.sigmoid(r[...].astype(F32))
    merged = (gate(ga_ref) * _dot(ya_ref[...], wa_ref[...]) + gate(gb_ref) * _dot(yb_ref[...], wb_ref[...])
              + gate(gc_ref) * _dot(yc_ref[...], wc_ref[...]))
    o_ref[...] = x_ref[...] + g1_ref[...] * _dot(merged.astype(BF16), wo_ref[...])


def _merge(x2, g1, ya, yb, yc, z, wa, wb, wc, wo, seq):
    t, d = x2.shape
    tm = 256
    per_b = seq // tm
    full = lambda a: pl.BlockSpec(a.shape, lambda i: (0,) * a.ndim)
    row = lambda n, unit=0: pl.BlockSpec((tm, n), lambda i: (i, unit))
    return pl.pallas_call(
        _merge_kernel,
        grid=(t // tm,),
        in_specs=[row(d), pl.BlockSpec((None, 1, d), lambda i: (i // per_b, 0, 0)),
                  row(CONV_DIM), row(FOX_DIM), row(GLA_VDIM), row(d, 0), row(d, 1), row(d, 2),
                  full(wa), full(wb), full(wc), full(wo)],
        out_specs=row(d),
        out_shape=jax.ShapeDtypeStruct((t, d), F32),
        compiler_params=_cparams(1),
        name="merge",
    )(x2, g1, ya, yb, yc, z, z, z, wa, wb, wc, wo)


def _pack_halves(y):
    n = y.shape[1] // 2
    lo = pltpu.bitcast(y[:, :n].astype(BF16).astype(F32), U32)
    hi = pltpu.bitcast(y[:, n:].astype(BF16).astype(F32), U32)
    return (hi & jnp.uint32(0xFFFF0000)) | (lo >> 16)


def _unpack_halves(p):
    lo = pltpu.bitcast(p << 16, F32)
    hi = pltpu.bitcast(p & jnp.uint32(0xFFFF0000), F32)
    return lo, hi


def _first_index(hit, idx, n):
    return jnp.min(jnp.where(hit, idx, n), axis=0, keepdims=True)


def _route_kernel(x_ref, g_ref, sc_ref, sh_ref, wrh_ref, wrl_ref, eb_ref, u_ref,
                  te_ref, tw_ref, rk_ref, cnt_ref, carry_ref):
    @pl.when(pl.program_id(0) == 0)
    def _():
        carry_ref[...] = jnp.zeros_like(carry_ref)

    x = x_ref[...]
    h = _modulated_norm(x, g_ref[...], sc_ref[...], sh_ref[...])
    hb = h.astype(BF16)

    hlo = (h - hb.astype(F32)).astype(BF16)
    logits = _dot_nt(wrh_ref[...], hb) + _dot_nt(wrh_ref[...], hlo) + _dot_nt(wrl_ref[...], hb)
    scores = jax.nn.sigmoid(logits)
    biased = scores + eb_ref[...]
    tm = x.shape[0]
    per_group = N_EXPERTS // N_GROUPS
    sub = lax.broadcasted_iota(I32, (per_group, tm), 0).astype(F32)
    gscore = []
    for g in range(N_GROUPS):
        blk = biased[g * per_group:(g + 1) * per_group, :]
        m1 = jnp.max(blk, axis=0, keepdims=True)
        rest = jnp.where(sub == _first_index(blk == m1, sub, float(per_group)), -jnp.inf, blk)
        gscore.append(m1 + jnp.max(rest, axis=0, keepdims=True))
    gs = jnp.concatenate(gscore, axis=0)
    gidx = lax.broadcasted_iota(I32, (N_GROUPS, tm), 0).astype(F32)
    gsel = jnp.zeros((N_GROUPS, tm), F32)
    for _ in range(TOPK_GROUPS):
        m = jnp.max(gs, axis=0, keepdims=True)
        hit = gidx == _first_index(gs == m, gidx, float(N_GROUPS))
        gsel = jnp.where(hit, 1.0, gsel)
        gs = jnp.where(hit, -jnp.inf, gs)
    cand = jnp.concatenate(
        [jnp.where(gsel[g:g + 1, :] > 0.5, biased[g * per_group:(g + 1) * per_group, :], -jnp.inf)
         for g in range(N_GROUPS)], axis=0)
    eidx = lax.broadcasted_iota(I32, (N_EXPERTS, tm), 0).astype(F32)
    hits, tops, tws = [], [], []
    sel = jnp.zeros((N_EXPERTS, tm), F32)
    for _ in range(TOP_K):
        m = jnp.max(cand, axis=0, keepdims=True)
        e = _first_index(cand == m, eidx, float(N_EXPERTS))
        hit = eidx == e
        hits.append(hit)
        tops.append(e)
        tws.append(jnp.sum(jnp.where(hit, scores, 0.0), axis=0, keepdims=True))
        cand = jnp.where(hit, -jnp.inf, cand)
        sel = jnp.where(hit, 1.0, sel)
    te_ref[...] = jnp.concatenate(tops, axis=0).astype(I32)
    tw = jnp.concatenate(tws, axis=0)
    tw_ref[...] = tw / jnp.sum(tw, axis=0, keepdims=True) * ROUTED_SCALE

    before = _dot(sel.astype(BF16), u_ref[...]) + carry_ref[...]
    rk_ref[...] = jnp.concatenate(
        [jnp.sum(jnp.where(hit, before, 0.0), axis=0, keepdims=True) for hit in hits], axis=0).astype(I32)
    carry = carry_ref[...] + jnp.sum(sel, axis=1, keepdims=True)
    carry_ref[...] = carry
    cnt_ref[...] = jnp.broadcast_to(carry, cnt_ref.shape).astype(I32)


MOE_TM = 256


def _route(x2, g, sc, sh, wrh, wrl, eb, seq):
    t, d = x2.shape
    tm = MOE_TM
    per_b = seq // tm
    r = np.arange(tm)
    u = jnp.asarray((r[:, None] < r[None, :]).astype(np.float32), BF16)
    full = lambda a: pl.BlockSpec(a.shape, lambda i: (0,) * a.ndim)
    vec = lambda: pl.BlockSpec((None, 1, d), lambda i: (i // per_b, 0, 0))
    colk = lambda: pl.BlockSpec((TOP_K, tm), lambda i: (0, i))
    return pl.pallas_call(
        _route_kernel,
        grid=(t // tm,),
        in_specs=[pl.BlockSpec((tm, d), lambda i: (i, 0)), full(g), vec(), vec(),
                  full(wrh), full(wrl), full(eb), full(u)],
        out_specs=[colk(), colk(), colk(), pl.BlockSpec((N_EXPERTS, LANES), lambda i: (0, 0))],
        out_shape=[jax.ShapeDtypeStruct((TOP_K, t), I32), jax.ShapeDtypeStruct((TOP_K, t), F32),
                   jax.ShapeDtypeStruct((TOP_K, t), I32), jax.ShapeDtypeStruct((N_EXPERTS, LANES), I32)],
        scratch_shapes=[pltpu.VMEM((N_EXPERTS, 1), F32)],
        compiler_params=_cparams(1),
        name="route",
    )(x2, g, sc, sh, wrh, wrl, eb, u)


ROW_TILE = 8


def _tiles_to_rows(ref, lead, first, m):
    return jnp.concatenate(
        [ref[lead + (pl.ds(first * ROW_TILE + j, m, stride=ROW_TILE), slice(None))] for j in range(ROW_TILE)], axis=1)


def _rows_to_tiles(ref, lead, val):
    m = val.shape[0]
    for j in range(ROW_TILE):
        ref[lead + (pl.ds(j, m, stride=ROW_TILE), slice(None))] = val[:, j * LANES:(j + 1) * LANES]


def _tile_rows(ref, lead, row):
    return ref.at[lead + (pl.ds(pl.multiple_of(row, ROW_TILE) if not isinstance(row, int) else row, ROW_TILE),)]


def _dispatch_kernel(pend_ref, dst_ref, x_ref, g_ref, sc_ref, sh_ref, g2_ref, ws1_ref, ws3_ref, ws2_ref,
                     base_ref, xs_hbm, hbuf, zbuf, sem, zsem, *, n_tiles):
    i = pl.program_id(0)
    odd = i % 2 == 1
    even = jnp.logical_not(odd)
    tm = x_ref.shape[0]

    @pl.when(i == 0)
    def _():
        zbuf[...] = jnp.zeros_like(zbuf)
        def clear(e, start):
            first = jnp.maximum(pend_ref[e] - MOE_BLOCK, 0) * ROW_TILE
            cp = pltpu.make_async_copy(zbuf, xs_hbm.at[pl.ds(pl.multiple_of(first, ROW_TILE), zbuf.shape[0])], zsem)
            cp.start() if start else cp.wait()
        for e in range(N_EXPERTS):
            clear(e, True)
        for e in range(N_EXPERTS):
            clear(e, False)

    def copies(sl, start):
        for k in range(TOP_K):
            for t in range(tm):
                cp = pltpu.make_async_copy(_tile_rows(hbuf, (sl,), ROW_TILE * t),
                                           _tile_rows(xs_hbm, (), dst_ref[k, t] if start else 0), sem.at[sl])
                cp.start(priority=(k * tm + t) % 2) if start else cp.wait()

    pl.when((i >= 2) & even)(functools.partial(copies, 0, False))
    pl.when((i >= 2) & odd)(functools.partial(copies, 1, False))

    def main(sl):
        x = x_ref[...]
        h = _modulated_norm(x, g_ref[...], sc_ref[...], sh_ref[...])
        _rows_to_tiles(hbuf, (sl,), _pack_halves(h))
        copies(sl, True)
        hb = h.astype(BF16)
        hs = (_silu(_dot(hb, ws1_ref[...])) * _dot(hb, ws3_ref[...])).astype(BF16)
        base_ref[...] = x + g2_ref[...] * _dot(hs, ws2_ref[...])

    pl.when(even)(functools.partial(main, 0))
    pl.when(odd)(functools.partial(main, 1))

    @pl.when(i == n_tiles - 1)
    def _():
        if n_tiles >= 2:
            copies(n_tiles % 2, False)
        copies((n_tiles - 1) % 2, False)


def _dispatch(x2, g, sc, sh, g2, ws1, ws3, ws2, pends, dest, n_rows, seq):
    t, d = x2.shape
    tm = MOE_TM
    per_b = seq // tm
    full = lambda a: pl.BlockSpec(a.shape, lambda i, pe: (0,) * a.ndim)
    vec = lambda: pl.BlockSpec((None, 1, d), lambda i, pe: (i // per_b, 0, 0))
    return pl.pallas_call(
        functools.partial(_dispatch_kernel, n_tiles=t // tm),
        grid_spec=pltpu.PrefetchScalarGridSpec(
            num_scalar_prefetch=1,
            grid=(t // tm,),
            in_specs=[pl.BlockSpec((TOP_K, tm), lambda i, pe: (0, i), memory_space=pltpu.SMEM),
                      pl.BlockSpec((tm, d), lambda i, pe: (i, 0)), full(g), vec(), vec(), vec(),
                      full(ws1), full(ws3), full(ws2)],
            out_specs=[pl.BlockSpec((tm, d), lambda i, pe: (i, 0)), pl.BlockSpec(memory_space=pl.ANY)],
            scratch_shapes=[pltpu.VMEM((2, tm * ROW_TILE, LANES), U32),
                            pltpu.VMEM((MOE_BLOCK * ROW_TILE, LANES), U32),
                            pltpu.SemaphoreType.DMA((2,)), pltpu.SemaphoreType.DMA(())]),
        out_shape=[jax.ShapeDtypeStruct((t, d), F32), jax.ShapeDtypeStruct((n_rows * ROW_TILE, LANES), U32)],
        compiler_params=_cparams(1),
        name="dispatch",
    )(pends, dest, x2, g, sc, sh, g2, ws1, ws3, ws2)


def _experts_kernel(be_ref, nu_ref, xs_ref, w1_ref, w3_ref, w2_ref, ys_ref, w1b, w3b, w2b):
    s = pl.program_id(0)

    @pl.when((s == 0) | (be_ref[s] != be_ref[jnp.maximum(s - 1, 0)]))
    def _():
        w1b[...] = w1_ref[...].astype(BF16)
        w3b[...] = w3_ref[...].astype(BF16)
        w2b[...] = w2_ref[...].astype(BF16)

    @pl.when(s < nu_ref[0])
    def _():
        lo, hi = _unpack_halves(_tiles_to_rows(xs_ref, (), 0, MOE_BLOCK))
        xb = jnp.concatenate([lo.astype(BF16), hi.astype(BF16)], axis=1)
        hid = (_silu(_dot(xb, w1b[...])) * _dot(xb, w3b[...])).astype(BF16)
        _rows_to_tiles(ys_ref, (), _pack_halves(_dot(hid, w2b[...])))


def _experts(layer, block_e, n_used, xs, w1, w3, w2):
    nb = block_e.shape[0]
    rows = pl.BlockSpec((MOE_BLOCK * ROW_TILE, LANES), lambda s, be, nu: (jnp.minimum(s, nu[0] - 1), 0))
    wspec = lambda a: pl.BlockSpec((None, None) + a.shape[2:], lambda s, be, nu: (layer, be[s], 0, 0))
    return pl.pallas_call(
        _experts_kernel,
        grid_spec=pltpu.PrefetchScalarGridSpec(
            num_scalar_prefetch=2,
            grid=(nb,),
            in_specs=[rows, wspec(w1), wspec(w3), wspec(w2)],
            out_specs=rows,
            scratch_shapes=[pltpu.VMEM(w1.shape[2:], BF16), pltpu.VMEM(w3.shape[2:], BF16),
                            pltpu.VMEM(w2.shape[2:], BF16)]),
        out_shape=jax.ShapeDtypeStruct(xs.shape, U32),
        compiler_params=_cparams(1),
        name="experts",
    )(block_e, n_used, xs, w1, w3, w2)


def _combine_kernel(dst_ref, dstn_ref, base_ref, g2_ref, w_ref, fg_ref, ys_hbm, o_ref, gbuf, sem, *, n_tiles, final):
    i = pl.program_id(0)
    odd = i % 2 == 1
    even = jnp.logical_not(odd)
    tm = base_ref.shape[0]

    def fetch(idx_ref, sl, start):
        for k in range(TOP_K):
            for t in range(tm):
                cp = pltpu.make_async_copy(_tile_rows(ys_hbm, (), idx_ref[k, t] if start else 0),
                                           _tile_rows(gbuf, (sl,), ROW_TILE * (k * tm + t)), sem.at[sl])
                cp.start(priority=(k * tm + t) % 2) if start else cp.wait()

    @pl.when(i == 0)
    def _():
        fetch(dst_ref, 0, True)

    def main(sl):
        fetch(dst_ref, sl, False)
        fetch(dstn_ref, 1 - sl, True)
        half = o_ref.shape[1] // 2
        rb = 32

        def rows_step(ib, carry):
            r0 = pl.multiple_of(ib * rb, rb)
            rows = pl.ds(r0, rb)
            w = w_ref[rows, :]
            wb = [jnp.broadcast_to(w[:, k:k + 1], (rb, LANES)) for k in range(TOP_K)]
            ssq = jnp.zeros((rb, 1), F32)
            for j in range(ROW_TILE):
                acc_lo = acc_hi = None
                for k in range(TOP_K):
                    lo, hi = _unpack_halves(gbuf[sl, pl.ds((k * tm + r0) * ROW_TILE + j, rb, stride=ROW_TILE), :])
                    acc_lo = wb[k] * lo if acc_lo is None else acc_lo + wb[k] * lo
                    acc_hi = wb[k] * hi if acc_hi is None else acc_hi + wb[k] * hi
                for acc, c0 in ((acc_lo, j * LANES), (acc_hi, half + j * LANES)):
                    x = base_ref[rows, c0:c0 + LANES] + g2_ref[:, c0:c0 + LANES] * acc
                    o_ref[rows, c0:c0 + LANES] = x
                    ssq = ssq + jnp.sum(x * x, axis=-1, keepdims=True)
            if final:
                scale = lax.rsqrt(ssq * (1.0 / o_ref.shape[1]) + EPS)
                o_ref[rows, :] = o_ref[rows, :] * scale * fg_ref[...]
            return carry

        lax.fori_loop(0, tm // rb, rows_step, 0)

    pl.when(even)(functools.partial(main, 0))
    pl.when(odd)(functools.partial(main, 1))

    @pl.when(i == n_tiles - 1)
    def _():
        fetch(dst_ref, n_tiles % 2, False)


def _combine(base, g2, w, ys, dest, fg, seq, final):
    t, d = base.shape
    tm = MOE_TM
    per_b = seq // tm
    n_tiles = t // tm
    idx = lambda f: pl.BlockSpec((TOP_K, tm), f, memory_space=pltpu.SMEM)
    return pl.pallas_call(
        functools.partial(_combine_kernel, n_tiles=n_tiles, final=final),
        grid=(n_tiles,),
        in_specs=[idx(lambda i: (0, i)), idx(lambda i: (0, jnp.minimum(i + 1, n_tiles - 1))),
                  pl.BlockSpec((tm, d), lambda i: (i, 0)),
                  pl.BlockSpec((None, 1, d), lambda i: (i // per_b, 0, 0)),
                  pl.BlockSpec((tm, TOP_K), lambda i: (i, 0)),
                  pl.BlockSpec((1, d), lambda i: (0, 0)),
                  pl.BlockSpec(memory_space=pl.ANY)],
        out_specs=pl.BlockSpec((tm, d), lambda i: (i, 0)),
        out_shape=jax.ShapeDtypeStruct((t, d), F32),
        scratch_shapes=[pltpu.VMEM((2, TOP_K * tm * ROW_TILE, LANES), U32), pltpu.SemaphoreType.DMA((2,))],
        compiler_params=_cparams(1),
        name="combine",
    )(dest, dest, base, g2, w, fg, ys)


def _dispatch_tables(top_e, rank, counts, t):
    n_blocks = TOP_K * t // MOE_BLOCK + N_EXPERTS
    padded = (counts + MOE_BLOCK - 1) // MOE_BLOCK * MOE_BLOCK
    pends = jnp.cumsum(padded)
    pstarts = pends - padded
    experts = jnp.arange(N_EXPERTS, dtype=I32)
    pstart_of = jnp.sum(jnp.where(top_e[..., None] == experts, pstarts, 0), axis=-1)
    dest = (pstart_of + rank) * ROW_TILE
    starts = jnp.arange(n_blocks, dtype=I32) * MOE_BLOCK
    block_e = jnp.sum((pends[None, :] <= starts[:, None]).astype(I32), axis=1)
    block_e = jnp.minimum(block_e, jnp.max(jnp.where(starts < pends[-1], block_e, 0)))
    n_used = (pends[-1:] // MOE_BLOCK).astype(I32)
    return block_e, n_used, pends.astype(I32), dest, n_blocks * MOE_BLOCK


def _layout_inproj(w_in, b_in):
    o = np.cumsum([0, 2 * CONV_DIM, FOX_DIM, FOX_DIM, FOX_DIM, FOX_HEADS, GLA_KDIM, GLA_KDIM, GLA_VDIM, GLA_VDIM,
                   GLA_RANK, 3 * D_MODEL])
    seg = lambda a, i: a[..., int(o[i]):int(o[i + 1])]

    def main(a):
        return jnp.concatenate([seg(a, 10), seg(a, 0), seg(a, 1) * (FOX_HEAD_DIM ** -0.5 * LOG2E), seg(a, 2), seg(a, 3),
                                seg(a, 5) * GLA_DK ** -0.5, seg(a, 6), seg(a, 7), seg(a, 8)], axis=-1)

    def small(a):
        pad = jnp.zeros(a.shape[:-1] + (LANES - FOX_HEADS - GLA_RANK,), a.dtype)
        return jnp.concatenate([seg(a, 4), seg(a, 9), pad], axis=-1)

    return main(w_in).astype(BF16), main(b_in)[None, :], small(w_in).astype(BF16), small(b_in)[None, :]


def kernel(x, c, norm1_g, ada_w, ada_b, w_in, b_in, conv_w, conv_b, conv_ln_g, conv_ln_b, gla_wa, gla_ba,
           gla_norm_g, w_branch_a, w_branch_b, w_branch_c, w_out, norm2_g, w_router, e_bias, w1, w3, w2,
           ws1, ws3, ws2, final_g):
    bsz, seq, d = x.shape
    t = bsz * seq
    depth = ada_w.shape[0]
    mod = _ada_mod(c, ada_w, ada_b).reshape(depth, bsz, 6, 1, d)
    x2 = x.reshape(t, d)
    row = lambda a: a.reshape(1, -1)
    for l in range(depth):
        sh1, sc1, g1, sh2, sc2, g2 = (mod[l, :, i] for i in range(6))
        w, b, ws, bs = _layout_inproj(w_in[l], b_in[l])
        z, zs = _inproj(x2, row(norm1_g[l]), sc1, sh1, w, b, ws, bs, seq)
        wa_pad = jnp.zeros((LANES, GLA_KDIM), F32).at[ZS_A:ZS_A + GLA_RANK].set(gla_wa[l])
        aq, ak, bc = _prep(zs, wa_pad, row(gla_ba[l]), seq)
        ya = _conv(z, conv_w[l].reshape(CONV_WIDTH, CONV_DIM), row(conv_b[l]), row(conv_ln_g[l]),
                   row(conv_ln_b[l]), seq)
        yb = _fox(z, aq, ak, seq)
        yc = _gla(z, bc, row(gla_norm_g[l]), seq)
        x2 = _merge(x2, g1, ya, yb, yc, z, w_branch_a[l].astype(BF16), w_branch_b[l].astype(BF16),
                    w_branch_c[l].astype(BF16), w_out[l].astype(BF16), seq)

        wr_t = w_router[l].T
        wrh = wr_t.astype(BF16)
        wrl = (wr_t - wrh.astype(F32)).astype(BF16)
        top_e, top_w, rank, counts = _route(x2, row(norm2_g[l]), sc2, sh2, wrh, wrl,
                                            e_bias[l].reshape(N_EXPERTS, 1), seq)
        block_e, n_used, pends, dest, n_rows = _dispatch_tables(top_e, rank, counts[:, 0], t)
        base, xs = _dispatch(x2, row(norm2_g[l]), sc2, sh2, g2, ws1[l].astype(BF16), ws3[l].astype(BF16),
                             ws2[l].astype(BF16), pends, dest, n_rows, seq)
        ys = _experts(l, block_e, n_used, xs, w1, w3, w2)
        x2 = _combine(base, g2, top_w.T, ys, dest, row(final_g), seq, final=(l == depth - 1))
    return x2.reshape(bsz, seq, d)
```

```python
import functools

import numpy as np
import jax
import jax.numpy as jnp
from jax import lax
from jax.experimental import pallas as pl
from jax.experimental.pallas import tpu as pltpu

F32 = jnp.float32
BF16 = jnp.bfloat16
U32 = jnp.uint32
I32 = jnp.int32

D_MODEL = 2048
CONV_DIM = 512
CONV_WIDTH = 31
FOX_HEADS = 8
FOX_HEAD_DIM = 128
FOX_DIM = FOX_HEADS * FOX_HEAD_DIM
GLA_HEADS = 4
GLA_DK = 64
GLA_DV = 128
GLA_KDIM = GLA_HEADS * GLA_DK
GLA_VDIM = GLA_HEADS * GLA_DV
GLA_RANK = 16
GLA_TAU = 16.0
GLA_CHUNK = 64
GLA_SUB = 8
N_EXPERTS = 64
N_GROUPS = 8
TOPK_GROUPS = 4
TOP_K = 8
EXPERT_DIM = 512
SHARED_DIM = 512
ROUTED_SCALE = 2.5
MOE_BLOCK = 512
EPS = 1e-6

LANES = 128
V7X_VMEM_LIMIT = 56 * 1024 * 1024

Z_GATE = 0
Z_CONV = 48
Z_QF = 56
Z_KF = 64
Z_VF = 72
Z_QG = 80
Z_KG = 82
Z_VG = 84
Z_RG = 88
Z_COLS = 92 * LANES
ZS_F = 0
ZS_A = 8
NEG = -1e30
LOG2E = 1.4426950408889634


def _cparams(n_axes, vmem=V7X_VMEM_LIMIT, flags=None):
    return pltpu.CompilerParams(dimension_semantics=("arbitrary",) * n_axes, vmem_limit_bytes=vmem, flags=flags)


def _log_sigmoid(x):
    return jnp.minimum(x, 0.0) - jnp.log(1.0 + jnp.exp(-jnp.abs(x)))


def _silu(x):
    return x * jax.nn.sigmoid(x)


def _split3(x):
    hi = x.astype(BF16)
    r1 = x - hi.astype(F32)
    mid = r1.astype(BF16)
    lo = (r1 - mid.astype(F32)).astype(BF16)
    return hi, mid, lo


def _dot(a, b):
    return jnp.dot(a, b, preferred_element_type=F32)


def _dot_nt(a, b):
    return lax.dot_general(a, b, (((1,), (1,)), ((), ())), preferred_element_type=F32)


def _dot_tn(a, b):
    return lax.dot_general(a, b, (((0,), (0,)), ((), ())), preferred_element_type=F32)


def _lmul_exact(l01, x):
    hi, mid, lo = _split3(x)
    return _dot(l01, hi) + _dot(l01, mid) + _dot(l01, lo)


def _dot_f32(a, b):
    a0, a1, a2 = _split3(a)
    b0, b1, b2 = _split3(b)
    return (_dot(a0, b0) + _dot(a0, b1) + _dot(a1, b0)) + (_dot(a0, b2) + _dot(a1, b1) + _dot(a2, b0))


def _ada_kernel(ct_ref, w_ref, b_ref, o_ref):
    ct = ct_ref[...]
    act = _silu(ct)
    w = w_ref[...]
    rows = [jnp.sum(w * act[:, b:b + 1], axis=0, keepdims=True) for b in range(ct.shape[1])]
    o_ref[...] = jnp.concatenate(rows, axis=0) + b_ref[...]


def _ada_mod(c, ada_w, ada_b):
    n_l, d, n = ada_w.shape
    bsz = c.shape[0]
    tn = 1024
    return pl.pallas_call(
        _ada_kernel,
        grid=(n_l, n // tn),
        in_specs=[pl.BlockSpec((d, bsz), lambda l, j: (0, 0)),
                  pl.BlockSpec((None, d, tn), lambda l, j: (l, 0, j)),
                  pl.BlockSpec((None, 1, tn), lambda l, j: (l, 0, j))],
        out_specs=pl.BlockSpec((None, bsz, tn), lambda l, j: (l, 0, j)),
        out_shape=jax.ShapeDtypeStruct((n_l, bsz, n), F32),
        compiler_params=_cparams(2),
        name="ada_mod",
    )(c.T, ada_w, ada_b.reshape(n_l, 1, n))


def _modulated_norm(x, g, sc, sh):
    ms = jnp.mean(x * x, axis=-1, keepdims=True)
    return (x * lax.rsqrt(ms + EPS) * g) * (1.0 + sc) + sh


def _inproj_kernel(x_ref, g_ref, sc_ref, sh_ref, w_ref, b_ref, ws_ref, bs_ref, z_ref, zs_ref, h_scr):
    @pl.when(pl.program_id(1) == 0)
    def _():
        h = _modulated_norm(x_ref[...], g_ref[...], sc_ref[...], sh_ref[...]).astype(BF16)
        h_scr[...] = h
        zs_ref[...] = _dot(h, ws_ref[...]) + bs_ref[...]

    z_ref[...] = (_dot(h_scr[...], w_ref[...]) + b_ref[...]).astype(BF16)


def _inproj(x2, g, sc, sh, w, b, ws, bs, seq):
    t, d = x2.shape
    tm, tn = 1024, 512
    per_b = seq // tm
    vec = lambda: pl.BlockSpec((None, 1, d), lambda i, j: (i // per_b, 0, 0))
    return pl.pallas_call(
        _inproj_kernel,
        grid=(t // tm, Z_COLS // tn),
        in_specs=[pl.BlockSpec((tm, d), lambda i, j: (i, 0)),
                  pl.BlockSpec((1, d), lambda i, j: (0, 0)),
                  vec(), vec(),
                  pl.BlockSpec((d, tn), lambda i, j: (0, j)),
                  pl.BlockSpec((1, tn), lambda i, j: (0, j)),
                  pl.BlockSpec((d, LANES), lambda i, j: (0, 0)),
                  pl.BlockSpec((1, LANES), lambda i, j: (0, 0))],
        out_specs=[pl.BlockSpec((tm, tn), lambda i, j: (i, j)),
                   pl.BlockSpec((tm, LANES), lambda i, j: (i, 0))],
        out_shape=[jax.ShapeDtypeStruct((t, Z_COLS), BF16), jax.ShapeDtypeStruct((t, LANES), F32)],
        scratch_shapes=[pltpu.VMEM((tm, d), BF16)],
        compiler_params=_cparams(2),
        name="inproj",
    )(x2, g, sc, sh, w, b, ws, bs)


def _prep_kernel(zs_ref, ltri_ref, lblk_ref, pq_ref, pk_ref, cq_ref, ck_ref, wa_ref, ba_ref,
                 aq_ref, ak_ref, bc_ref, carry_ref):
    @pl.when(pl.program_id(1) == 0)
    def _():
        carry_ref[...] = jnp.zeros_like(carry_ref)

    zs = zs_ref[...]
    cum = _lmul_exact(ltri_ref[...], _log_sigmoid(zs)) + carry_ref[...]
    carry_ref[...] = cum[cum.shape[0] - 1:, :]
    hi, mid, lo = _split3(cum * LOG2E)
    aq = cq_ref[...] + _dot(hi, pq_ref[0]) + _dot(mid, pq_ref[1]) + _dot(lo, pq_ref[2])
    ak = ck_ref[...] + _dot(hi, pk_ref[0]) + _dot(mid, pk_ref[1]) + _dot(lo, pk_ref[2])
    aq_ref[...] = aq.astype(BF16)
    ak_ref[...] = ak.astype(BF16)
    la = _log_sigmoid(_dot_f32(zs, wa_ref[...]) + ba_ref[...]) * (1.0 / GLA_TAU)
    bc_ref[...] = _lmul_exact(lblk_ref[...], la)


def _prep_consts(ts):
    r = np.arange(ts)
    ltri = (r[:, None] >= r[None, :]).astype(np.float32)
    lblk = ltri * (r[:, None] // GLA_CHUNK == r[None, :] // GLA_CHUNK)
    pq = np.zeros((3, LANES, FOX_DIM), np.float32)
    pk = np.zeros((3, LANES, FOX_DIM), np.float32)
    cq = np.zeros((1, FOX_DIM), np.float32)
    ck = np.zeros((1, FOX_DIM), np.float32)
    for h in range(FOX_HEADS):
        for p in range(3):
            pq[p, ZS_F + h, h * FOX_HEAD_DIM + p] = 1.0
            pk[p, ZS_F + h, h * FOX_HEAD_DIM + 3 + p] = -1.0
            cq[0, h * FOX_HEAD_DIM + 3 + p] = 1.0
            ck[0, h * FOX_HEAD_DIM + p] = 1.0
    return (jnp.asarray(ltri, BF16), jnp.asarray(lblk, BF16), jnp.asarray(pq, BF16), jnp.asarray(pk, BF16),
            jnp.asarray(cq), jnp.asarray(ck))


def _prep(zs, wa_pad, ba, seq):
    t = zs.shape[0]
    ts = 256
    per_b = seq // ts
    ltri, lblk, pq, pk, cq, ck = _prep_consts(ts)
    full = lambda a: pl.BlockSpec(a.shape, lambda b, s: (0,) * a.ndim)
    row = lambda n: pl.BlockSpec((ts, n), lambda b, s: (b * per_b + s, 0))
    return pl.pallas_call(
        _prep_kernel,
        grid=(t // seq, per_b),
        in_specs=[row(LANES), full(ltri), full(lblk), full(pq), full(pk), full(cq), full(ck),
                  full(wa_pad), full(ba)],
        out_specs=[row(FOX_DIM), row(FOX_DIM), row(GLA_KDIM)],
        out_shape=[jax.ShapeDtypeStruct((t, FOX_DIM), BF16), jax.ShapeDtypeStruct((t, FOX_DIM), BF16),
                   jax.ShapeDtypeStruct((t, GLA_KDIM), F32)],
        scratch_shapes=[pltpu.VMEM((1, LANES), F32)],
        compiler_params=_cparams(2),
        name="prep",
    )(zs, ltri, lblk, pq, pk, cq, ck, wa_pad, ba)


CONV_HIST = 32


def _conv_kernel(u_ref, w_ref, cb_ref, lg_ref, lb_ref, o_ref, ypad_ref, shift_ref):
    ts = u_ref.shape[0]
    sub = 8

    @pl.when(pl.program_id(1) == 0)
    def _():
        ypad_ref[0:CONV_HIST, :] = jnp.zeros((CONV_HIST, CONV_DIM), F32)

    u = u_ref[...].astype(F32)
    ypad_ref[CONV_HIST:CONV_HIST + ts, :] = u[:, :CONV_DIM] * jax.nn.sigmoid(u[:, CONV_DIM:])
    span = ts + CONV_HIST - sub
    for r in range(1, sub):
        shift_ref[r - 1] = ypad_ref[r:r + span, :]
    acc = jnp.zeros((ts, CONV_DIM), F32) + cb_ref[...]
    first = CONV_HIST - (CONV_WIDTH - 1)
    for j in range(CONV_WIDTH):
        r = (first + j) % sub
        base = first + j - r
        rows = ypad_ref[base:base + ts, :] if r == 0 else shift_ref[r - 1, base:base + ts, :]
        acc = acc + w_ref[j:j + 1, :] * rows
    ypad_ref[0:CONV_HIST, :] = ypad_ref[ts:ts + CONV_HIST, :]
    mu = jnp.mean(acc, axis=-1, keepdims=True)
    cen = acc - mu
    var = jnp.mean(cen * cen, axis=-1, keepdims=True)
    y = cen * lax.rsqrt(var + EPS) * lg_ref[...] + lb_ref[...]
    o_ref[...] = _silu(y).astype(BF16)


def _conv(z, w, cb, lg, lb, seq):
    t = z.shape[0]
    ts = 512
    per_b = seq // ts
    full = lambda a: pl.BlockSpec(a.shape, lambda b, s: (0,) * a.ndim)
    return pl.pallas_call(
        _conv_kernel,
        grid=(t // seq, per_b),
        in_specs=[pl.BlockSpec((ts, 2 * CONV_DIM), lambda b, s: (b * per_b + s, Z_CONV * LANES // (2 * CONV_DIM))),
                  full(w), full(cb), full(lg), full(lb)],
        out_specs=pl.BlockSpec((ts, CONV_DIM), lambda b, s: (b * per_b + s, 0)),
        out_shape=jax.ShapeDtypeStruct((t, CONV_DIM), BF16),
        scratch_shapes=[pltpu.VMEM((ts + CONV_HIST, CONV_DIM), F32),
                        pltpu.VMEM((7, ts + CONV_HIST - 8, CONV_DIM), F32)],
        compiler_params=_cparams(2),
        name="conv",
    )(z, w, cb, lg, lb)


FOX_HEADS_PER_STEP = 2


def _fox_kernel(q_ref, aq_ref, k_ref, ak_ref, v_ref, o_ref, kk_ref, vt_ref, m_ref, acc_ref, *, blk):
    i = pl.program_id(2)
    dh = FOX_HEAD_DIM
    heads = range(FOX_HEADS_PER_STEP)
    col = lambda h: slice(h * dh, (h + 1) * dh)

    @pl.when(i == 0)
    def _():
        def fill(cb, carry):
            rows = pl.ds(pl.multiple_of(cb * blk, blk), blk)
            for h in heads:
                kk_ref[h, rows, 0:dh] = k_ref[rows, col(h)]
                kk_ref[h, rows, dh:2 * dh] = ak_ref[rows, col(h)]
                vt_ref[h, cb, 0:dh, :] = v_ref[rows, col(h)].astype(F32).T.astype(BF16)
                vt_ref[h, cb, dh:2 * dh, :] = jnp.ones((dh, blk), BF16)
            return carry
        lax.fori_loop(0, k_ref.shape[0] // blk, fill, 0)

    qq = [jnp.concatenate([q_ref[:, col(h)], aq_ref[:, col(h)]], axis=1) for h in heads]
    m_ref[...] = jnp.full(m_ref.shape, NEG, F32)
    acc_ref[...] = jnp.zeros(acc_ref.shape, F32)

    def step(j, masked):
        rows = pl.ds(pl.multiple_of(j * blk, blk), blk)
        sts = [_dot_nt(kk_ref[h, rows, :], qq[h]) for h in heads]
        for h in heads:
            st = sts[h]
            if masked:
                kv = lax.broadcasted_iota(I32, (blk, blk), 0)
                qi = lax.broadcasted_iota(I32, (blk, blk), 1)
                st = jnp.where(kv <= qi, st, NEG)
            m_old = m_ref[h]
            m_new = jnp.maximum(m_old, jnp.max(st, axis=0, keepdims=True))
            p = jnp.exp2(st - m_new).astype(BF16)
            acc_ref[h] = acc_ref[h] * jnp.exp2(m_old - m_new) + _dot(vt_ref[h, j], p)
            m_ref[h] = m_new

    def body(j, carry):
        step(j, False)
        return carry

    lax.fori_loop(0, i, body, 0)
    step(i, True)
    for h in heads:
        acc = acc_ref[h]
        o_ref[:, col(h)] = (acc[:dh, :] / acc[dh:, :]).T.astype(BF16)


def _fox(z, aq, ak, seq):
    t = z.shape[0]
    blk = min(1024, seq)
    nq = seq // blk
    hps = FOX_HEADS_PER_STEP
    width = hps * FOX_HEAD_DIM
    unit = lambda base: base * LANES // width
    qspec = lambda base: pl.BlockSpec((blk, width), lambda b, g, i: (b * nq + i, unit(base) + g))
    kspec = lambda base: pl.BlockSpec((seq, width), lambda b, g, i: (b, unit(base) + g),
                                      pipeline_mode=pl.Buffered(1))
    return pl.pallas_call(
        functools.partial(_fox_kernel, blk=blk),
        grid=(t // seq, FOX_HEADS // hps, nq),
        in_specs=[qspec(Z_QF), qspec(0), kspec(Z_KF), kspec(0), kspec(Z_VF)],
        out_specs=qspec(0),
        out_shape=jax.ShapeDtypeStruct((t, FOX_DIM), BF16),
        scratch_shapes=[pltpu.VMEM((hps, seq, 2 * FOX_HEAD_DIM), BF16),
                        pltpu.VMEM((hps, nq, 2 * FOX_HEAD_DIM, blk), BF16),
                        pltpu.VMEM((hps, 1, blk), F32), pltpu.VMEM((hps, 2 * FOX_HEAD_DIM, blk), F32)],
        compiler_params=_cparams(3),
        name="fox",
    )(z, aq, z, ak, z)


def _gla_chunk(q, k, v, b, st, e3):
    c, nsub = GLA_CHUNK, GLA_CHUNK // GLA_SUB
    row = lax.broadcasted_iota(I32, (c, GLA_KDIM), 0)
    rblk, rloc = row // GLA_SUB, row % GLA_SUB
    lane_k = lax.broadcasted_iota(I32, (c, GLA_KDIM), 1)
    khead, kpos = lane_k // GLA_DK, lane_k % GLA_DK
    b3 = b.reshape(nsub, GLA_SUB, GLA_KDIM)
    k3 = k.reshape(nsub, GLA_SUB, GLA_KDIM)
    last = b3[:, GLA_SUB - 1:GLA_SUB, :]
    bcast = lambda a3, i: jnp.broadcast_to(a3[:, i:i + 1, :], (nsub, GLA_SUB, GLA_KDIM)).reshape(c, GLA_KDIM)

    prev = jnp.concatenate([last[:1], last[:nsub - 1]], axis=0)
    refq = jnp.broadcast_to(prev, (nsub, GLA_SUB, GLA_KDIM)).reshape(c, GLA_KDIM)
    qt = q * jnp.exp(jnp.minimum(b - refq, 0.0))
    qcat = jnp.concatenate([jnp.where(rblk == i, qt, 0.0) for i in range(1, nsub)], axis=1).astype(BF16)
    kts = []
    for i in range(1, nsub):
        kt = k * jnp.exp(jnp.minimum(last[i - 1] - b, 0.0))
        kts.append(jnp.where(row < GLA_SUB * i, kt, 0.0))
    kcat = jnp.concatenate(
        [jnp.concatenate([jnp.where(khead == h, kt, 0.0) for kt in kts], axis=1) for h in range(GLA_HEADS)],
        axis=0).astype(BF16)
    a_all = _dot_nt(qcat, kcat)

    for sl in range(GLA_SUB):
        p = q * bcast(k3, sl) * jnp.exp(jnp.minimum(b - bcast(b3, sl), 0.0))
        w = _dot(p.astype(BF16), e3)
        hit = (kpos == rblk * GLA_SUB + sl) & (rloc >= sl)
        a_all = a_all + jnp.where(hit, w, 0.0)

    lane_v = lax.broadcasted_iota(I32, (c, GLA_VDIM), 1) // GLA_DV
    vbd = jnp.concatenate([jnp.where(lane_v == h, v, jnp.zeros_like(v)) for h in range(GLA_HEADS)], axis=0)
    qe = (q * jnp.exp(b)).astype(BF16)
    o = _dot(a_all.astype(BF16), vbd) + _dot_nt(qe, st.astype(BF16))

    blast = b[c - 1:c, :]
    kd = (k * jnp.exp(blast - b)).astype(BF16)
    shead_v = lax.broadcasted_iota(I32, (GLA_VDIM, GLA_KDIM), 0) // GLA_DV
    shead_k = lax.broadcasted_iota(I32, (GLA_VDIM, GLA_KDIM), 1) // GLA_DK
    st_new = jnp.where(shead_v == shead_k, st * jnp.exp(blast) + _dot_tn(v, kd), 0.0)
    return o, st_new


def _gla_kernel(q_ref, k_ref, v_ref, r_ref, b_ref, ng_ref, e3_ref, o_ref, st_ref):
    @pl.when(pl.program_id(1) == 0)
    def _():
        st_ref[...] = jnp.zeros_like(st_ref)

    e3 = e3_ref[...]
    ng = ng_ref[...]

    def body(ci, carry):
        rows = pl.ds(pl.multiple_of(ci * GLA_CHUNK, GLA_CHUNK), GLA_CHUNK)
        o, st_new = _gla_chunk(q_ref[rows, :].astype(F32), k_ref[rows, :].astype(F32), v_ref[rows, :],
                               b_ref[rows, :], st_ref[...], e3)
        st_ref[...] = st_new
        outs = []
        for h in range(GLA_HEADS):
            oh = o[:, h * GLA_DV:(h + 1) * GLA_DV]
            outs.append(oh * lax.rsqrt(jnp.mean(oh * oh, axis=-1, keepdims=True) + EPS))
        y = jnp.concatenate(outs, axis=1) * ng * _silu(r_ref[rows, :].astype(F32))
        o_ref[rows, :] = y.astype(BF16)
        return carry

    lax.fori_loop(0, q_ref.shape[0] // GLA_CHUNK, body, 0)


def _gla(z, bc, ng, seq):
    t = z.shape[0]
    ts = min(512, seq)
    per_b = seq // ts
    lane = np.arange(GLA_KDIM)
    e3 = jnp.asarray((lane[:, None] // GLA_DK == lane[None, :] // GLA_DK).astype(np.float32), BF16)
    spec = lambda n, unit: pl.BlockSpec((ts, n), lambda b, s: (b * per_b + s, unit))
    full = lambda a: pl.BlockSpec(a.shape, lambda b, s: (0,) * a.ndim)
    return pl.pallas_call(
        _gla_kernel,
        grid=(t // seq, per_b),
        in_specs=[spec(GLA_KDIM, Z_QG * LANES // GLA_KDIM), spec(GLA_KDIM, Z_KG * LANES // GLA_KDIM),
                  spec(GLA_VDIM, Z_VG * LANES // GLA_VDIM), spec(GLA_VDIM, Z_RG * LANES // GLA_VDIM),
                  spec(GLA_KDIM, 0), full(ng), full(e3)],
        out_specs=spec(GLA_VDIM, 0),
        out_shape=jax.ShapeDtypeStruct((t, GLA_VDIM), BF16),
        scratch_shapes=[pltpu.VMEM((GLA_VDIM, GLA_KDIM), F32)],
        compiler_params=_cparams(2),
        name="gla",
    )(z, z, z, z, bc, ng, e3)


def _merge_kernel(x_ref, g1_ref, ya_ref, yb_ref, yc_ref, ga_ref, gb_ref, gc_ref,
                  wa_ref, wb_ref, wc_ref, wo_ref, o_ref):
    gate = lambda r: jax.nn.sigmoid(r[...].astype(F32))
    merged = (gate(ga_ref) * _dot(ya_ref[...], wa_ref[...]) + gate(gb_ref) * _dot(yb_ref[...], wb_ref[...])
              + gate(gc_ref) * _dot(yc_ref[...], wc_ref[...]))
    o_ref[...] = x_ref[...] + g1_ref[...] * _dot(merged.astype(BF16), wo_ref[...])


def _merge(x2, g1, ya, yb, yc, z, wa, wb, wc, wo, seq):
    t, d = x2.shape
    tm = 256
    per_b = seq // tm
    full = lambda a: pl.BlockSpec(a.shape, lambda i: (0,) * a.ndim)
    row = lambda n, unit=0: pl.BlockSpec((tm, n), lambda i: (i, unit))
    return pl.pallas_call(
        _merge_kernel,
        grid=(t // tm,),
        in_specs=[row(d), pl.BlockSpec((None, 1, d), lambda i: (i // per_b, 0, 0)),
                  row(CONV_DIM), row(FOX_DIM), row(GLA_VDIM), row(d, 0), row(d, 1), row(d, 2),
                  full(wa), full(wb), full(wc), full(wo)],
        out_specs=row(d),
        out_shape=jax.ShapeDtypeStruct((t, d), F32),
        compiler_params=_cparams(1),
        name="merge",
    )(x2, g1, ya, yb, yc, z, z, z, wa, wb, wc, wo)


def _pack_halves(y):
    n = y.shape[1] // 2
    lo = pltpu.bitcast(y[:, :n].astype(BF16).astype(F32), U32)
    hi = pltpu.bitcast(y[:, n:].astype(BF16).astype(F32), U32)
    return (hi & jnp.uint32(0xFFFF0000)) | (lo >> 16)


def _unpack_halves(p):
    lo = pltpu.bitcast(p << 16, F32)
    hi = pltpu.bitcast(p & jnp.uint32(0xFFFF0000), F32)
    return lo, hi


def _first_index(hit, idx, n):
    return jnp.min(jnp.where(hit, idx, n), axis=0, keepdims=True)


def _route_kernel(x_ref, g_ref, sc_ref, sh_ref, wrh_ref, wrl_ref, eb_ref, u_ref,
                  te_ref, tw_ref, rk_ref, cnt_ref, carry_ref):
    @pl.when(pl.program_id(0) == 0)
    def _():
        carry_ref[...] = jnp.zeros_like(carry_ref)

    x = x_ref[...]
    h = _modulated_norm(x, g_ref[...], sc_ref[...], sh_ref[...])
    hb = h.astype(BF16)

    hlo = (h - hb.astype(F32)).astype(BF16)
    logits = _dot_nt(wrh_ref[...], hb) + _dot_nt(wrh_ref[...], hlo) + _dot_nt(wrl_ref[...], hb)
    scores = jax.nn.sigmoid(logits)
    biased = scores + eb_ref[...]
    tm = x.shape[0]
    per_group = N_EXPERTS // N_GROUPS
    sub = lax.broadcasted_iota(I32, (per_group, tm), 0).astype(F32)
    gscore = []
    for g in range(N_GROUPS):
        blk = biased[g * per_group:(g + 1) * per_group, :]
        m1 = jnp.max(blk, axis=0, keepdims=True)
        rest = jnp.where(sub == _first_index(blk == m1, sub, float(per_group)), -jnp.inf, blk)
        gscore.append(m1 + jnp.max(rest, axis=0, keepdims=True))
    gs = jnp.concatenate(gscore, axis=0)
    gidx = lax.broadcasted_iota(I32, (N_GROUPS, tm), 0).astype(F32)
    gsel = jnp.zeros((N_GROUPS, tm), F32)
    for _ in range(TOPK_GROUPS):
        m = jnp.max(gs, axis=0, keepdims=True)
        hit = gidx == _first_index(gs == m, gidx, float(N_GROUPS))
        gsel = jnp.where(hit, 1.0, gsel)
        gs = jnp.where(hit, -jnp.inf, gs)
    cand = jnp.concatenate(
        [jnp.where(gsel[g:g + 1, :] > 0.5, biased[g * per_group:(g + 1) * per_group, :], -jnp.inf)
         for g in range(N_GROUPS)], axis=0)
    eidx = lax.broadcasted_iota(I32, (N_EXPERTS, tm), 0).astype(F32)
    hits, tops, tws = [], [], []
    sel = jnp.zeros((N_EXPERTS, tm), F32)
    for _ in range(TOP_K):
        m = jnp.max(cand, axis=0, keepdims=True)
        e = _first_index(cand == m, eidx, float(N_EXPERTS))
        hit = eidx == e
        hits.append(hit)
        tops.append(e)
        tws.append(jnp.sum(jnp.where(hit, scores, 0.0), axis=0, keepdims=True))
        cand = jnp.where(hit, -jnp.inf, cand)
        sel = jnp.where(hit, 1.0, sel)
    te_ref[...] = jnp.concatenate(tops, axis=0).astype(I32)
    tw = jnp.concatenate(tws, axis=0)
    tw_ref[...] = tw / jnp.sum(tw, axis=0, keepdims=True) * ROUTED_SCALE

    before = _dot(sel.astype(BF16), u_ref[...]) + carry_ref[...]
    rk_ref[...] = jnp.concatenate(
        [jnp.sum(jnp.where(hit, before, 0.0), axis=0, keepdims=True) for hit in hits], axis=0).astype(I32)
    carry = carry_ref[...] + jnp.sum(sel, axis=1, keepdims=True)
    carry_ref[...] = carry
    cnt_ref[...] = jnp.broadcast_to(carry, cnt_ref.shape).astype(I32)


MOE_TM = 256


def _route(x2, g, sc, sh, wrh, wrl, eb, seq):
    t, d = x2.shape
    tm = MOE_TM
    per_b = seq // tm
    r = np.arange(tm)
    u = jnp.asarray((r[:, None] < r[None, :]).astype(np.float32), BF16)
    full = lambda a: pl.BlockSpec(a.shape, lambda i: (0,) * a.ndim)
    vec = lambda: pl.BlockSpec((None, 1, d), lambda i: (i // per_b, 0, 0))
    colk = lambda: pl.BlockSpec((TOP_K, tm), lambda i: (0, i))
    return pl.pallas_call(
        _route_kernel,
        grid=(t // tm,),
        in_specs=[pl.BlockSpec((tm, d), lambda i: (i, 0)), full(g), vec(), vec(),
                  full(wrh), full(wrl), full(eb), full(u)],
        out_specs=[colk(), colk(), colk(), pl.BlockSpec((N_EXPERTS, LANES), lambda i: (0, 0))],
        out_shape=[jax.ShapeDtypeStruct((TOP_K, t), I32), jax.ShapeDtypeStruct((TOP_K, t), F32),
                   jax.ShapeDtypeStruct((TOP_K, t), I32), jax.ShapeDtypeStruct((N_EXPERTS, LANES), I32)],
        scratch_shapes=[pltpu.VMEM((N_EXPERTS, 1), F32)],
        compiler_params=_cparams(1),
        name="route",
    )(x2, g, sc, sh, wrh, wrl, eb, u)


ROW_TILE = 8


def _tiles_to_rows(ref, lead, first, m):
    return jnp.concatenate(
        [ref[lead + (pl.ds(first * ROW_TILE + j, m, stride=ROW_TILE), slice(None))] for j in range(ROW_TILE)], axis=1)


def _rows_to_tiles(ref, lead, val):
    m = val.shape[0]
    for j in range(ROW_TILE):
        ref[lead + (pl.ds(j, m, stride=ROW_TILE), slice(None))] = val[:, j * LANES:(j + 1) * LANES]


def _tile_rows(ref, lead, row):
    return ref.at[lead + (pl.ds(pl.multiple_of(row, ROW_TILE) if not isinstance(row, int) else row, ROW_TILE),)]


def _dispatch_kernel(pend_ref, dst_ref, x_ref, g_ref, sc_ref, sh_ref, g2_ref, ws1_ref, ws3_ref, ws2_ref,
                     base_ref, xs_hbm, hbuf, zbuf, sem, zsem, *, n_tiles):
    i = pl.program_id(0)
    odd = i % 2 == 1
    even = jnp.logical_not(odd)
    tm = x_ref.shape[0]

    @pl.when(i == 0)
    def _():
        zbuf[...] = jnp.zeros_like(zbuf)
        def clear(e, start):
            first = jnp.maximum(pend_ref[e] - MOE_BLOCK, 0) * ROW_TILE
            cp = pltpu.make_async_copy(zbuf, xs_hbm.at[pl.ds(pl.multiple_of(first, ROW_TILE), zbuf.shape[0])], zsem)
            cp.start() if start else cp.wait()
        for e in range(N_EXPERTS):
            clear(e, True)
        for e in range(N_EXPERTS):
            clear(e, False)

    def copies(sl, start):
        for k in range(TOP_K):
            for t in range(tm):
                cp = pltpu.make_async_copy(_tile_rows(hbuf, (sl,), ROW_TILE * t),
                                           _tile_rows(xs_hbm, (), dst_ref[k, t] if start else 0), sem.at[sl])
                cp.start(priority=(k * tm + t) % 2) if start else cp.wait()

    pl.when((i >= 2) & even)(functools.partial(copies, 0, False))
    pl.when((i >= 2) & odd)(functools.partial(copies, 1, False))

    def main(sl):
        x = x_ref[...]
        h = _modulated_norm(x, g_ref[...], sc_ref[...], sh_ref[...])
        _rows_to_tiles(hbuf, (sl,), _pack_halves(h))
        copies(sl, True)
        hb = h.astype(BF16)
        hs = (_silu(_dot(hb, ws1_ref[...])) * _dot(hb, ws3_ref[...])).astype(BF16)
        base_ref[...] = x + g2_ref[...] * _dot(hs, ws2_ref[...])

    pl.when(even)(functools.partial(main, 0))
    pl.when(odd)(functools.partial(main, 1))

    @pl.when(i == n_tiles - 1)
    def _():
        if n_tiles >= 2:
            copies(n_tiles % 2, False)
        copies((n_tiles - 1) % 2, False)


def _dispatch(x2, g, sc, sh, g2, ws1, ws3, ws2, pends, dest, n_rows, seq):
    t, d = x2.shape
    tm = MOE_TM
    per_b = seq // tm
    full = lambda a: pl.BlockSpec(a.shape, lambda i, pe: (0,) * a.ndim)
    vec = lambda: pl.BlockSpec((None, 1, d), lambda i, pe: (i // per_b, 0, 0))
    return pl.pallas_call(
        functools.partial(_dispatch_kernel, n_tiles=t // tm),
        grid_spec=pltpu.PrefetchScalarGridSpec(
            num_scalar_prefetch=1,
            grid=(t // tm,),
            in_specs=[pl.BlockSpec((TOP_K, tm), lambda i, pe: (0, i), memory_space=pltpu.SMEM),
                      pl.BlockSpec((tm, d), lambda i, pe: (i, 0)), full(g), vec(), vec(), vec(),
                      full(ws1), full(ws3), full(ws2)],
            out_specs=[pl.BlockSpec((tm, d), lambda i, pe: (i, 0)), pl.BlockSpec(memory_space=pl.ANY)],
            scratch_shapes=[pltpu.VMEM((2, tm * ROW_TILE, LANES), U32),
                            pltpu.VMEM((MOE_BLOCK * ROW_TILE, LANES), U32),
                            pltpu.SemaphoreType.DMA((2,)), pltpu.SemaphoreType.DMA(())]),
        out_shape=[jax.ShapeDtypeStruct((t, d), F32), jax.ShapeDtypeStruct((n_rows * ROW_TILE, LANES), U32)],
        compiler_params=_cparams(1),
        name="dispatch",
    )(pends, dest, x2, g, sc, sh, g2, ws1, ws3, ws2)


def _experts_kernel(be_ref, nu_ref, xs_ref, w1_ref, w3_ref, w2_ref, ys_ref, w1b, w3b, w2b):
    s = pl.program_id(0)

    @pl.when((s == 0) | (be_ref[s] != be_ref[jnp.maximum(s - 1, 0)]))
    def _():
        w1b[...] = w1_ref[...].astype(BF16)
        w3b[...] = w3_ref[...].astype(BF16)
        w2b[...] = w2_ref[...].astype(BF16)

    @pl.when(s < nu_ref[0])
    def _():
        lo, hi = _unpack_halves(_tiles_to_rows(xs_ref, (), 0, MOE_BLOCK))
        xb = jnp.concatenate([lo.astype(BF16), hi.astype(BF16)], axis=1)
        hid = (_silu(_dot(xb, w1b[...])) * _dot(xb, w3b[...])).astype(BF16)
        _rows_to_tiles(ys_ref, (), _pack_halves(_dot(hid, w2b[...])))


def _experts(layer, block_e, n_used, xs, w1, w3, w2):
    nb = block_e.shape[0]
    rows = pl.BlockSpec((MOE_BLOCK * ROW_TILE, LANES), lambda s, be, nu: (jnp.minimum(s, nu[0] - 1), 0))
    wspec = lambda a: pl.BlockSpec((None, None) + a.shape[2:], lambda s, be, nu: (layer, be[s], 0, 0))
    return pl.pallas_call(
        _experts_kernel,
        grid_spec=pltpu.PrefetchScalarGridSpec(
            num_scalar_prefetch=2,
            grid=(nb,),
            in_specs=[rows, wspec(w1), wspec(w3), wspec(w2)],
            out_specs=rows,
            scratch_shapes=[pltpu.VMEM(w1.shape[2:], BF16), pltpu.VMEM(w3.shape[2:], BF16),
                            pltpu.VMEM(w2.shape[2:], BF16)]),
        out_shape=jax.ShapeDtypeStruct(xs.shape, U32),
        compiler_params=_cparams(1),
        name="experts",
    )(block_e, n_used, xs, w1, w3, w2)


def _combine_kernel(dst_ref, dstn_ref, base_ref, g2_ref, w_ref, fg_ref, ys_hbm, o_ref, gbuf, sem, *, n_tiles, final):
    i = pl.program_id(0)
    odd = i % 2 == 1
    even = jnp.logical_not(odd)
    tm = base_ref.shape[0]

    def fetch(idx_ref, sl, start):
        for k in range(TOP_K):
            for t in range(tm):
                cp = pltpu.make_async_copy(_tile_rows(ys_hbm, (), idx_ref[k, t] if start else 0),
                                           _tile_rows(gbuf, (sl,), ROW_TILE * (k * tm + t)), sem.at[sl])
                cp.start(priority=(k * tm + t) % 2) if start else cp.wait()

    @pl.when(i == 0)
    def _():
        fetch(dst_ref, 0, True)

    def main(sl):
        fetch(dst_ref, sl, False)
        fetch(dstn_ref, 1 - sl, True)
        half = o_ref.shape[1] // 2
        rb = 32

        def rows_step(ib, carry):
            r0 = pl.multiple_of(ib * rb, rb)
            rows = pl.ds(r0, rb)
            w = w_ref[rows, :]
            wb = [jnp.broadcast_to(w[:, k:k + 1], (rb, LANES)) for k in range(TOP_K)]
            ssq = jnp.zeros((rb, 1), F32)
            for j in range(ROW_TILE):
                acc_lo = acc_hi = None
                for k in range(TOP_K):
                    lo, hi = _unpack_halves(gbuf[sl, pl.ds((k * tm + r0) * ROW_TILE + j, rb, stride=ROW_TILE), :])
                    acc_lo = wb[k] * lo if acc_lo is None else acc_lo + wb[k] * lo
                    acc_hi = wb[k] * hi if acc_hi is None else acc_hi + wb[k] * hi
                for acc, c0 in ((acc_lo, j * LANES), (acc_hi, half + j * LANES)):
                    x = base_ref[rows, c0:c0 + LANES] + g2_ref[:, c0:c0 + LANES] * acc
                    o_ref[rows, c0:c0 + LANES] = x
                    ssq = ssq + jnp.sum(x * x, axis=-1, keepdims=True)
            if final:
                scale = lax.rsqrt(ssq * (1.0 / o_ref.shape[1]) + EPS)
                o_ref[rows, :] = o_ref[rows, :] * scale * fg_ref[...]
            return carry

        lax.fori_loop(0, tm // rb, rows_step, 0)

    pl.when(even)(functools.partial(main, 0))
    pl.when(odd)(functools.partial(main, 1))

    @pl.when(i == n_tiles - 1)
    def _():
        fetch(dst_ref, n_tiles % 2, False)


def _combine(base, g2, w, ys, dest, fg, seq, final):
    t, d = base.shape
    tm = MOE_TM
    per_b = seq // tm
    n_tiles = t // tm
    idx = lambda f: pl.BlockSpec((TOP_K, tm), f, memory_space=pltpu.SMEM)
    return pl.pallas_call(
        functools.partial(_combine_kernel, n_tiles=n_tiles, final=final),
        grid=(n_tiles,),
        in_specs=[idx(lambda i: (0, i)), idx(lambda i: (0, jnp.minimum(i + 1, n_tiles - 1))),
                  pl.BlockSpec((tm, d), lambda i: (i, 0)),
                  pl.BlockSpec((None, 1, d), lambda i: (i // per_b, 0, 0)),
                  pl.BlockSpec((tm, TOP_K), lambda i: (i, 0)),
                  pl.BlockSpec((1, d), lambda i: (0, 0)),
                  pl.BlockSpec(memory_space=pl.ANY)],
        out_specs=pl.BlockSpec((tm, d), lambda i: (i, 0)),
        out_shape=jax.ShapeDtypeStruct((t, d), F32),
        scratch_shapes=[pltpu.VMEM((2, TOP_K * tm * ROW_TILE, LANES), U32), pltpu.SemaphoreType.DMA((2,))],
        compiler_params=_cparams(1),
        name="combine",
    )(dest, dest, base, g2, w, fg, ys)


def _dispatch_tables(top_e, rank, counts, t):
    n_blocks = TOP_K * t // MOE_BLOCK + N_EXPERTS
    padded = (counts + MOE_BLOCK - 1) // MOE_BLOCK * MOE_BLOCK
    pends = jnp.cumsum(padded)
    pstarts = pends - padded
    experts = jnp.arange(N_EXPERTS, dtype=I32)
    pstart_of = jnp.sum(jnp.where(top_e[..., None] == experts, pstarts, 0), axis=-1)
    dest = (pstart_of + rank) * ROW_TILE
    starts = jnp.arange(n_blocks, dtype=I32) * MOE_BLOCK
    block_e = jnp.sum((pends[None, :] <= starts[:, None]).astype(I32), axis=1)
    block_e = jnp.minimum(block_e, jnp.max(jnp.where(starts < pends[-1], block_e, 0)))
    n_used = (pends[-1:] // MOE_BLOCK).astype(I32)
    return block_e, n_used, pends.astype(I32), dest, n_blocks * MOE_BLOCK


def _layout_inproj(w_in, b_in):
    o = np.cumsum([0, 2 * CONV_DIM, FOX_DIM, FOX_DIM, FOX_DIM, FOX_HEADS, GLA_KDIM, GLA_KDIM, GLA_VDIM, GLA_VDIM,
                   GLA_RANK, 3 * D_MODEL])
    seg = lambda a, i: a[..., int(o[i]):int(o[i + 1])]

    def main(a):
        return jnp.concatenate([seg(a, 10), seg(a, 0), seg(a, 1) * (FOX_HEAD_DIM ** -0.5 * LOG2E), seg(a, 2), seg(a, 3),
                                seg(a, 5) * GLA_DK ** -0.5, seg(a, 6), seg(a, 7), seg(a, 8)], axis=-1)

    def small(a):
        pad = jnp.zeros(a.shape[:-1] + (LANES - FOX_HEADS - GLA_RANK,), a.dtype)
        return jnp.concatenate([seg(a, 4), seg(a, 9), pad], axis=-1)

    return main(w_in).astype(BF16), main(b_in)[None, :], small(w_in).astype(BF16), small(b_in)[None, :]


def kernel(x, c, norm1_g, ada_w, ada_b, w_in, b_in, conv_w, conv_b, conv_ln_g, conv_ln_b, gla_wa, gla_ba,
           gla_norm_g, w_branch_a, w_branch_b, w_branch_c, w_out, norm2_g, w_router, e_bias, w1, w3, w2,
           ws1, ws3, ws2, final_g):
    bsz, seq, d = x.shape
    t = bsz * seq
    depth = ada_w.shape[0]
    mod = _ada_mod(c, ada_w, ada_b).reshape(depth, bsz, 6, 1, d)
    x2 = x.reshape(t, d)
    row = lambda a: a.reshape(1, -1)
    for l in range(depth):
        sh1, sc1, g1, sh2, sc2, g2 = (mod[l, :, i] for i in range(6))
        w, b, ws, bs = _layout_inproj(w_in[l], b_in[l])
        z, zs = _inproj(x2, row(norm1_g[l]), sc1, sh1, w, b, ws, bs, seq)
        wa_pad = jnp.zeros((LANES, GLA_KDIM), F32).at[ZS_A:ZS_A + GLA_RANK].set(gla_wa[l])
        aq, ak, bc = _prep(zs, wa_pad, row(gla_ba[l]), seq)
        ya = _conv(z, conv_w[l].reshape(CONV_WIDTH, CONV_DIM), row(conv_b[l]), row(conv_ln_g[l]),
                   row(conv_ln_b[l]), seq)
        yb = _fox(z, aq, ak, seq)
        yc = _gla(z, bc, row(gla_norm_g[l]), seq)
        x2 = _merge(x2, g1, ya, yb, yc, z, w_branch_a[l].astype(BF16), w_branch_b[l].astype(BF16),
                    w_branch_c[l].astype(BF16), w_out[l].astype(BF16), seq)

        wr_t = w_router[l].T
        wrh = wr_t.astype(BF16)
        wrl = (wr_t - wrh.astype(F32)).astype(BF16)
        top_e, top_w, rank, counts = _route(x2, row(norm2_g[l]), sc2, sh2, wrh, wrl,
                                            e_bias[l].reshape(N_EXPERTS, 1), seq)
        block_e, n_used, pends, dest, n_rows = _dispatch_tables(top_e, rank, counts[:, 0], t)
        base, xs = _dispatch(x2, row(norm2_g[l]), sc2, sh2, g2, ws1[l].astype(BF16), ws3[l].astype(BF16),
                             ws2[l].astype(BF16), pends, dest, n_rows, seq)
        ys = _experts(l, block_e, n_used, xs, w1, w3, w2)
        x2 = _combine(base, g2, top_w.T, ys, dest, row(final_g), seq, final=(l == depth - 1))
    return x2.reshape(bsz, seq, d)
```

```python
import functools

import numpy as np
import jax
import jax.numpy as jnp
from jax import lax
from jax.experimental import pallas as pl
from jax.experimental.pallas import tpu as pltpu

F32 = jnp.float32
BF16 = jnp.bfloat16
U32 = jnp.uint32
I32 = jnp.int32

D_MODEL = 2048
CONV_DIM = 512
CONV_WIDTH = 31
FOX_HEADS = 8
FOX_HEAD_DIM = 128
FOX_DIM = FOX_HEADS * FOX_HEAD_DIM
GLA_HEADS = 4
GLA_DK = 64
GLA_DV = 128
GLA_KDIM = GLA_HEADS * GLA_DK
GLA_VDIM = GLA_HEADS * GLA_DV
GLA_RANK = 16
GLA_TAU = 16.0
GLA_CHUNK = 64
GLA_SUB = 8
N_EXPERTS = 64
N_GROUPS = 8
TOPK_GROUPS = 4
TOP_K = 8
EXPERT_DIM = 512
SHARED_DIM = 512
ROUTED_SCALE = 2.5
MOE_BLOCK = 512
EPS = 1e-6

LANES = 128
V7X_VMEM_LIMIT = 56 * 1024 * 1024

Z_GATE = 0
Z_CONV = 48
Z_QF = 56
Z_KF = 64
Z_VF = 72
Z_QG = 80
Z_KG = 82
Z_VG = 84
Z_RG = 88
Z_COLS = 92 * LANES
ZS_F = 0
ZS_A = 8
NEG = -1e30
LOG2E = 1.4426950408889634


def _cparams(n_axes, vmem=V7X_VMEM_LIMIT, flags=None):
    return pltpu.CompilerParams(dimension_semantics=("arbitrary",) * n_axes, vmem_limit_bytes=vmem, flags=flags)


def _log_sigmoid(x):
    return jnp.minimum(x, 0.0) - jnp.log(1.0 + jnp.exp(-jnp.abs(x)))


def _silu(x):
    return x * jax.nn.sigmoid(x)


def _split3(x):
    hi = x.astype(BF16)
    r1 = x - hi.astype(F32)
    mid = r1.astype(BF16)
    lo = (r1 - mid.astype(F32)).astype(BF16)
    return hi, mid, lo


def _dot(a, b):
    return jnp.dot(a, b, preferred_element_type=F32)


def _dot_nt(a, b):
    return lax.dot_general(a, b, (((1,), (1,)), ((), ())), preferred_element_type=F32)


def _dot_tn(a, b):
    return lax.dot_general(a, b, (((0,), (0,)), ((), ())), preferred_element_type=F32)


def _lmul_exact(l01, x):
    hi, mid, lo = _split3(x)
    return _dot(l01, hi) + _dot(l01, mid) + _dot(l01, lo)


def _dot_f32(a, b):
    a0, a1, a2 = _split3(a)
    b0, b1, b2 = _split3(b)
    return (_dot(a0, b0) + _dot(a0, b1) + _dot(a1, b0)) + (_dot(a0, b2) + _dot(a1, b1) + _dot(a2, b0))


def _ada_kernel(ct_ref, w_ref, b_ref, o_ref):
    ct = ct_ref[...]
    act = _silu(ct)
    w = w_ref[...]
    rows = [jnp.sum(w * act[:, b:b + 1], axis=0, keepdims=True) for b in range(ct.shape[1])]
    o_ref[...] = jnp.concatenate(rows, axis=0) + b_ref[...]


def _ada_mod(c, ada_w, ada_b):
    n_l, d, n = ada_w.shape
    bsz = c.shape[0]
    tn = 1024
    return pl.pallas_call(
        _ada_kernel,
        grid=(n_l, n // tn),
        in_specs=[pl.BlockSpec((d, bsz), lambda l, j: (0, 0)),
                  pl.BlockSpec((None, d, tn), lambda l, j: (l, 0, j)),
                  pl.BlockSpec((None, 1, tn), lambda l, j: (l, 0, j))],
        out_specs=pl.BlockSpec((None, bsz, tn), lambda l, j: (l, 0, j)),
        out_shape=jax.ShapeDtypeStruct((n_l, bsz, n), F32),
        compiler_params=_cparams(2),
        name="ada_mod",
    )(c.T, ada_w, ada_b.reshape(n_l, 1, n))


def _modulated_norm(x, g, sc, sh):
    ms = jnp.mean(x * x, axis=-1, keepdims=True)
    return (x * lax.rsqrt(ms + EPS) * g) * (1.0 + sc) + sh


def _inproj_kernel(x_ref, g_ref, sc_ref, sh_ref, w_ref, b_ref, ws_ref, bs_ref, z_ref, zs_ref, h_scr):
    @pl.when(pl.program_id(1) == 0)
    def _():
        h = _modulated_norm(x_ref[...], g_ref[...], sc_ref[...], sh_ref[...]).astype(BF16)
        h_scr[...] = h
        zs_ref[...] = _dot(h, ws_ref[...]) + bs_ref[...]

    z_ref[...] = (_dot(h_scr[...], w_ref[...]) + b_ref[...]).astype(BF16)


def _inproj(x2, g, sc, sh, w, b, ws, bs, seq):
    t, d = x2.shape
    tm, tn = 1024, 512
    per_b = seq // tm
    vec = lambda: pl.BlockSpec((None, 1, d), lambda i, j: (i // per_b, 0, 0))
    return pl.pallas_call(
        _inproj_kernel,
        grid=(t // tm, Z_COLS // tn),
        in_specs=[pl.BlockSpec((tm, d), lambda i, j: (i, 0)),
                  pl.BlockSpec((1, d), lambda i, j: (0, 0)),
                  vec(), vec(),
                  pl.BlockSpec((d, tn), lambda i, j: (0, j)),
                  pl.BlockSpec((1, tn), lambda i, j: (0, j)),
                  pl.BlockSpec((d, LANES), lambda i, j: (0, 0)),
                  pl.BlockSpec((1, LANES), lambda i, j: (0, 0))],
        out_specs=[pl.BlockSpec((tm, tn), lambda i, j: (i, j)),
                   pl.BlockSpec((tm, LANES), lambda i, j: (i, 0))],
        out_shape=[jax.ShapeDtypeStruct((t, Z_COLS), BF16), jax.ShapeDtypeStruct((t, LANES), F32)],
        scratch_shapes=[pltpu.VMEM((tm, d), BF16)],
        compiler_params=_cparams(2),
        name="inproj",
    )(x2, g, sc, sh, w, b, ws, bs)


def _prep_kernel(zs_ref, ltri_ref, lblk_ref, pq_ref, pk_ref, cq_ref, ck_ref, wa_ref, ba_ref,
                 aq_ref, ak_ref, bc_ref, carry_ref):
    @pl.when(pl.program_id(1) == 0)
    def _():
        carry_ref[...] = jnp.zeros_like(carry_ref)

    zs = zs_ref[...]
    cum = _lmul_exact(ltri_ref[...], _log_sigmoid(zs)) + carry_ref[...]
    carry_ref[...] = cum[cum.shape[0] - 1:, :]
    hi, mid, lo = _split3(cum * LOG2E)
    aq = cq_ref[...] + _dot(hi, pq_ref[0]) + _dot(mid, pq_ref[1]) + _dot(lo, pq_ref[2])
    ak = ck_ref[...] + _dot(hi, pk_ref[0]) + _dot(mid, pk_ref[1]) + _dot(lo, pk_ref[2])
    aq_ref[...] = aq.astype(BF16)
    ak_ref[...] = ak.astype(BF16)
    la = _log_sigmoid(_dot_f32(zs, wa_ref[...]) + ba_ref[...]) * (1.0 / GLA_TAU)
    bc_ref[...] = _lmul_exact(lblk_ref[...], la)


def _prep_consts(ts):
    r = np.arange(ts)
    ltri = (r[:, None] >= r[None, :]).astype(np.float32)
    lblk = ltri * (r[:, None] // GLA_CHUNK == r[None, :] // GLA_CHUNK)
    pq = np.zeros((3, LANES, FOX_DIM), np.float32)
    pk = np.zeros((3, LANES, FOX_DIM), np.float32)
    cq = np.zeros((1, FOX_DIM), np.float32)
    ck = np.zeros((1, FOX_DIM), np.float32)
    for h in range(FOX_HEADS):
        for p in range(3):
            pq[p, ZS_F + h, h * FOX_HEAD_DIM + p] = 1.0
            pk[p, ZS_F + h, h * FOX_HEAD_DIM + 3 + p] = -1.0
            cq[0, h * FOX_HEAD_DIM + 3 + p] = 1.0
            ck[0, h * FOX_HEAD_DIM + p] = 1.0
    return (jnp.asarray(ltri, BF16), jnp.asarray(lblk, BF16), jnp.asarray(pq, BF16), jnp.asarray(pk, BF16),
            jnp.asarray(cq), jnp.asarray(ck))


def _prep(zs, wa_pad, ba, seq):
    t = zs.shape[0]
    ts = 256
    per_b = seq // ts
    ltri, lblk, pq, pk, cq, ck = _prep_consts(ts)
    full = lambda a: pl.BlockSpec(a.shape, lambda b, s: (0,) * a.ndim)
    row = lambda n: pl.BlockSpec((ts, n), lambda b, s: (b * per_b + s, 0))
    return pl.pallas_call(
        _prep_kernel,
        grid=(t // seq, per_b),
        in_specs=[row(LANES), full(ltri), full(lblk), full(pq), full(pk), full(cq), full(ck),
                  full(wa_pad), full(ba)],
        out_specs=[row(FOX_DIM), row(FOX_DIM), row(GLA_KDIM)],
        out_shape=[jax.ShapeDtypeStruct((t, FOX_DIM), BF16), jax.ShapeDtypeStruct((t, FOX_DIM), BF16),
                   jax.ShapeDtypeStruct((t, GLA_KDIM), F32)],
        scratch_shapes=[pltpu.VMEM((1, LANES), F32)],
        compiler_params=_cparams(2),
        name="prep",
    )(zs, ltri, lblk, pq, pk, cq, ck, wa_pad, ba)


CONV_HIST = 32


def _conv_kernel(u_ref, w_ref, cb_ref, lg_ref, lb_ref, o_ref, ypad_ref, shift_ref):
    ts = u_ref.shape[0]
    sub = 8

    @pl.when(pl.program_id(1) == 0)
    def _():
        ypad_ref[0:CONV_HIST, :] = jnp.zeros((CONV_HIST, CONV_DIM), F32)

    u = u_ref[...].astype(F32)
    ypad_ref[CONV_HIST:CONV_HIST + ts, :] = u[:, :CONV_DIM] * jax.nn.sigmoid(u[:, CONV_DIM:])
    span = ts + CONV_HIST - sub
    for r in range(1, sub):
        shift_ref[r - 1] = ypad_ref[r:r + span, :]
    acc = jnp.zeros((ts, CONV_DIM), F32) + cb_ref[...]
    first = CONV_HIST - (CONV_WIDTH - 1)
    for j in range(CONV_WIDTH):
        r = (first + j) % sub
        base = first + j - r
        rows = ypad_ref[base:base + ts, :] if r == 0 else shift_ref[r - 1, base:base + ts, :]
        acc = acc + w_ref[j:j + 1, :] * rows
    ypad_ref[0:CONV_HIST, :] = ypad_ref[ts:ts + CONV_HIST, :]
    mu = jnp.mean(acc, axis=-1, keepdims=True)
    cen = acc - mu
    var = jnp.mean(cen * cen, axis=-1, keepdims=True)
    y = cen * lax.rsqrt(var + EPS) * lg_ref[...] + lb_ref[...]
    o_ref[...] = _silu(y).astype(BF16)


def _conv(z, w, cb, lg, lb, seq):
    t = z.shape[0]
    ts = 512
    per_b = seq // ts
    full = lambda a: pl.BlockSpec(a.shape, lambda b, s: (0,) * a.ndim)
    return pl.pallas_call(
        _conv_kernel,
        grid=(t // seq, per_b),
        in_specs=[pl.BlockSpec((ts, 2 * CONV_DIM), lambda b, s: (b * per_b + s, Z_CONV * LANES // (2 * CONV_DIM))),
                  full(w), full(cb), full(lg), full(lb)],
        out_specs=pl.BlockSpec((ts, CONV_DIM), lambda b, s: (b * per_b + s, 0)),
        out_shape=jax.ShapeDtypeStruct((t, CONV_DIM), BF16),
        scratch_shapes=[pltpu.VMEM((ts + CONV_HIST, CONV_DIM), F32),
                        pltpu.VMEM((7, ts + CONV_HIST - 8, CONV_DIM), F32)],
        compiler_params=_cparams(2),
        name="conv",
    )(z, w, cb, lg, lb)


FOX_HEADS_PER_STEP = 2


def _fox_kernel(q_ref, aq_ref, k_ref, ak_ref, v_ref, o_ref, kk_ref, vt_ref, m_ref, acc_ref, *, blk):
    i = pl.program_id(2)
    dh = FOX_HEAD_DIM
    heads = range(FOX_HEADS_PER_STEP)
    col = lambda h: slice(h * dh, (h + 1) * dh)

    @pl.when(i == 0)
    def _():
        def fill(cb, carry):
            rows = pl.ds(pl.multiple_of(cb * blk, blk), blk)
            for h in heads:
                kk_ref[h, rows, 0:dh] = k_ref[rows, col(h)]
                kk_ref[h, rows, dh:2 * dh] = ak_ref[rows, col(h)]
                vt_ref[h, cb, 0:dh, :] = v_ref[rows, col(h)].astype(F32).T.astype(BF16)
                vt_ref[h, cb, dh:2 * dh, :] = jnp.ones((dh, blk), BF16)
            return carry
        lax.fori_loop(0, k_ref.shape[0] // blk, fill, 0)

    qq = [jnp.concatenate([q_ref[:, col(h)], aq_ref[:, col(h)]], axis=1) for h in heads]
    m_ref[...] = jnp.full(m_ref.shape, NEG, F32)
    acc_ref[...] = jnp.zeros(acc_ref.shape, F32)

    def step(j, masked):
        rows = pl.ds(pl.multiple_of(j * blk, blk), blk)
        sts = [_dot_nt(kk_ref[h, rows, :], qq[h]) for h in heads]
        for h in heads:
            st = sts[h]
            if masked:
                kv = lax.broadcasted_iota(I32, (blk, blk), 0)
                qi = lax.broadcasted_iota(I32, (blk, blk), 1)
                st = jnp.where(kv <= qi, st, NEG)
            m_old = m_ref[h]
            m_new = jnp.maximum(m_old, jnp.max(st, axis=0, keepdims=True))
            p = jnp.exp2(st - m_new).astype(BF16)
            acc_ref[h] = acc_ref[h] * jnp.exp2(m_old - m_new) + _dot(vt_ref[h, j], p)
            m_ref[h] = m_new

    def body(j, carry):
        step(j, False)
        return carry

    lax.fori_loop(0, i, body, 0)
    step(i, True)
    for h in heads:
        acc = acc_ref[h]
        o_ref[:, col(h)] = (acc[:dh, :] / acc[dh:, :]).T.astype(BF16)


def _fox(z, aq, ak, seq):
    t = z.shape[0]
    blk = min(1024, seq)
    nq = seq // blk
    hps = FOX_HEADS_PER_STEP
    width = hps * FOX_HEAD_DIM
    unit = lambda base: base * LANES // width
    qspec = lambda base: pl.BlockSpec((blk, width), lambda b, g, i: (b * nq + i, unit(base) + g))
    kspec = lambda base: pl.BlockSpec((seq, width), lambda b, g, i: (b, unit(base) + g),
                                      pipeline_mode=pl.Buffered(1))
    return pl.pallas_call(
        functools.partial(_fox_kernel, blk=blk),
        grid=(t // seq, FOX_HEADS // hps, nq),
        in_specs=[qspec(Z_QF), qspec(0), kspec(Z_KF), kspec(0), kspec(Z_VF)],
        out_specs=qspec(0),
        out_shape=jax.ShapeDtypeStruct((t, FOX_DIM), BF16),
        scratch_shapes=[pltpu.VMEM((hps, seq, 2 * FOX_HEAD_DIM), BF16),
                        pltpu.VMEM((hps, nq, 2 * FOX_HEAD_DIM, blk), BF16),
                        pltpu.VMEM((hps, 1, blk), F32), pltpu.VMEM((hps, 2 * FOX_HEAD_DIM, blk), F32)],
        compiler_params=_cparams(3),
        name="fox",
    )(z, aq, z, ak, z)


def _gla_chunk(q, k, v, b, st, e3):
    c, nsub = GLA_CHUNK, GLA_CHUNK // GLA_SUB
    row = lax.broadcasted_iota(I32, (c, GLA_KDIM), 0)
    rblk, rloc = row // GLA_SUB, row % GLA_SUB
    lane_k = lax.broadcasted_iota(I32, (c, GLA_KDIM), 1)
    khead, kpos = lane_k // GLA_DK, lane_k % GLA_DK
    b3 = b.reshape(nsub, GLA_SUB, GLA_KDIM)
    k3 = k.reshape(nsub, GLA_SUB, GLA_KDIM)
    last = b3[:, GLA_SUB - 1:GLA_SUB, :]
    bcast = lambda a3, i: jnp.broadcast_to(a3[:, i:i + 1, :], (nsub, GLA_SUB, GLA_KDIM)).reshape(c, GLA_KDIM)

    prev = jnp.concatenate([last[:1], last[:nsub - 1]], axis=0)
    refq = jnp.broadcast_to(prev, (nsub, GLA_SUB, GLA_KDIM)).reshape(c, GLA_KDIM)
    qt = q * jnp.exp(jnp.minimum(b - refq, 0.0))
    qcat = jnp.concatenate([jnp.where(rblk == i, qt, 0.0) for i in range(1, nsub)], axis=1).astype(BF16)
    kts = []
    for i in range(1, nsub):
        kt = k * jnp.exp(jnp.minimum(last[i - 1] - b, 0.0))
        kts.append(jnp.where(row < GLA_SUB * i, kt, 0.0))
    kcat = jnp.concatenate(
        [jnp.concatenate([jnp.where(khead == h, kt, 0.0) for kt in kts], axis=1) for h in range(GLA_HEADS)],
        axis=0).astype(BF16)
    a_all = _dot_nt(qcat, kcat)

    for sl in range(GLA_SUB):
        p = q * bcast(k3, sl) * jnp.exp(jnp.minimum(b - bcast(b3, sl), 0.0))
        w = _dot(p.astype(BF16), e3)
        hit = (kpos == rblk * GLA_SUB + sl) & (rloc >= sl)
        a_all = a_all + jnp.where(hit, w, 0.0)

    lane_v = lax.broadcasted_iota(I32, (c, GLA_VDIM), 1) // GLA_DV
    vbd = jnp.concatenate([jnp.where(lane_v == h, v, jnp.zeros_like(v)) for h in range(GLA_HEADS)], axis=0)
    qe = (q * jnp.exp(b)).astype(BF16)
    o = _dot(a_all.astype(BF16), vbd) + _dot_nt(qe, st.astype(BF16))

    blast = b[c - 1:c, :]
    kd = (k * jnp.exp(blast - b)).astype(BF16)
    shead_v = lax.broadcasted_iota(I32, (GLA_VDIM, GLA_KDIM), 0) // GLA_DV
    shead_k = lax.broadcasted_iota(I32, (GLA_VDIM, GLA_KDIM), 1) // GLA_DK
    st_new = jnp.where(shead_v == shead_k, st * jnp.exp(blast) + _dot_tn(v, kd), 0.0)
    return o, st_new


def _gla_kernel(q_ref, k_ref, v_ref, r_ref, b_ref, ng_ref, e3_ref, o_ref, st_ref):
    @pl.when(pl.program_id(1) == 0)
    def _():
        st_ref[...] = jnp.zeros_like(st_ref)

    e3 = e3_ref[...]
    ng = ng_ref[...]

    def body(ci, carry):
        rows = pl.ds(pl.multiple_of(ci * GLA_CHUNK, GLA_CHUNK), GLA_CHUNK)
        o, st_new = _gla_chunk(q_ref[rows, :].astype(F32), k_ref[rows, :].astype(F32), v_ref[rows, :],
                               b_ref[rows, :], st_ref[...], e3)
        st_ref[...] = st_new
        outs = []
        for h in range(GLA_HEADS):
            oh = o[:, h * GLA_DV:(h + 1) * GLA_DV]
            outs.append(oh * lax.rsqrt(jnp.mean(oh * oh, axis=-1, keepdims=True) + EPS))
        y = jnp.concatenate(outs, axis=1) * ng * _silu(r_ref[rows, :].astype(F32))
        o_ref[rows, :] = y.astype(BF16)
        return carry

    lax.fori_loop(0, q_ref.shape[0] // GLA_CHUNK, body, 0)


def _gla(z, bc, ng, seq):
    t = z.shape[0]
    ts = min(512, seq)
    per_b = seq // ts
    lane = np.arange(GLA_KDIM)
    e3 = jnp.asarray((lane[:, None] // GLA_DK == lane[None, :] // GLA_DK).astype(np.float32), BF16)
    spec = lambda n, unit: pl.BlockSpec((ts, n), lambda b, s: (b * per_b + s, unit))
    full = lambda a: pl.BlockSpec(a.shape, lambda b, s: (0,) * a.ndim)
    return pl.pallas_call(
        _gla_kernel,
        grid=(t // seq, per_b),
        in_specs=[spec(GLA_KDIM, Z_QG * LANES // GLA_KDIM), spec(GLA_KDIM, Z_KG * LANES // GLA_KDIM),
                  spec(GLA_VDIM, Z_VG * LANES // GLA_VDIM), spec(GLA_VDIM, Z_RG * LANES // GLA_VDIM),
                  spec(GLA_KDIM, 0), full(ng), full(e3)],
        out_specs=spec(GLA_VDIM, 0),
        out_shape=jax.ShapeDtypeStruct((t, GLA_VDIM), BF16),
        scratch_shapes=[pltpu.VMEM((GLA_VDIM, GLA_KDIM), F32)],
        compiler_params=_cparams(2),
        name="gla",
    )(z, z, z, z, bc, ng, e3)


def _merge_kernel(x_ref, g1_ref, ya_ref, yb_ref, yc_ref, ga_ref, gb_ref, gc_ref,
                  wa_ref, wb_ref, wc_ref, wo_ref, o_ref):
    gate = lambda r: jax.nn.sigmoid(r[...].astype(F32))
    merged = (gate(ga_ref) * _dot(ya_ref[...], wa_ref[...]) + gate(gb_ref) * _dot(yb_ref[...], wb_ref[...])
              + gate(gc_ref) * _dot(yc_ref[...], wc_ref[...]))
    o_ref[...] = x_ref[...] + g1_ref[...] * _dot(merged.astype(BF16), wo_ref[...])


def _merge(x2, g1, ya, yb, yc, z, wa, wb, wc, wo, seq):
    t, d = x2.shape
    tm = 256
    per_b = seq // tm
    full = lambda a: pl.BlockSpec(a.shape, lambda i: (0,) * a.ndim)
    row = lambda n, unit=0: pl.BlockSpec((tm, n), lambda i: (i, unit))
    return pl.pallas_call(
        _merge_kernel,
        grid=(t // tm,),
        in_specs=[row(d), pl.BlockSpec((None, 1, d), lambda i: (i // per_b, 0, 0)),
                  row(CONV_DIM), row(FOX_DIM), row(GLA_VDIM), row(d, 0), row(d, 1), row(d, 2),
                  full(wa), full(wb), full(wc), full(wo)],
        out_specs=row(d),
        out_shape=jax.ShapeDtypeStruct((t, d), F32),
        compiler_params=_cparams(1),
        name="merge",
    )(x2, g1, ya, yb, yc, z, z, z, wa, wb, wc, wo)


def _pack_halves(y):
    n = y.shape[1] // 2
    lo = pltpu.bitcast(y[:, :n].astype(BF16).astype(F32), U32)
    hi = pltpu.bitcast(y[:, n:].astype(BF16).astype(F32), U32)
    return (hi & jnp.uint32(0xFFFF0000)) | (lo >> 16)


def _unpack_halves(p):
    lo = pltpu.bitcast(p << 16, F32)
    hi = pltpu.bitcast(p & jnp.uint32(0xFFFF0000), F32)
    return lo, hi


def _first_index(hit, idx, n):
    return jnp.min(jnp.where(hit, idx, n), axis=0, keepdims=True)


def _route_kernel(x_ref, g_ref, sc_ref, sh_ref, wrh_ref, wrl_ref, eb_ref, u_ref,
                  te_ref, tw_ref, rk_ref, cnt_ref, carry_ref):
    @pl.when(pl.program_id(0) == 0)
    def _():
        carry_ref[...] = jnp.zeros_like(carry_ref)

    x = x_ref[...]
    h = _modulated_norm(x, g_ref[...], sc_ref[...], sh_ref[...])
    hb = h.astype(BF16)

    hlo = (h - hb.astype(F32)).astype(BF16)
    logits = _dot_nt(wrh_ref[...], hb) + _dot_nt(wrh_ref[...], hlo) + _dot_nt(wrl_ref[...], hb)
    scores = jax.nn.sigmoid(logits)
    biased = scores + eb_ref[...]
    tm = x.shape[0]
    per_group = N_EXPERTS // N_GROUPS
    sub = lax.broadcasted_iota(I32, (per_group, tm), 0).astype(F32)
    gscore = []
    for g in range(N_GROUPS):
        blk = biased[g * per_group:(g + 1) * per_group, :]
        m1 = jnp.max(blk, axis=0, keepdims=True)
        rest = jnp.where(sub == _first_index(blk == m1, sub, float(per_group)), -jnp.inf, blk)
        gscore.append(m1 + jnp.max(rest, axis=0, keepdims=True))
    gs = jnp.concatenate(gscore, axis=0)
    gidx = lax.broadcasted_iota(I32, (N_GROUPS, tm), 0).astype(F32)
    gsel = jnp.zeros((N_GROUPS, tm), F32)
    for _ in range(TOPK_GROUPS):
        m = jnp.max(gs, axis=0, keepdims=True)
        hit = gidx == _first_index(gs == m, gidx, float(N_GROUPS))
        gsel = jnp.where(hit, 1.0, gsel)
        gs = jnp.where(hit, -jnp.inf, gs)
    cand = jnp.concatenate(
        [jnp.where(gsel[g:g + 1, :] > 0.5, biased[g * per_group:(g + 1) * per_group, :], -jnp.inf)
         for g in range(N_GROUPS)], axis=0)
    eidx = lax.broadcasted_iota(I32, (N_EXPERTS, tm), 0).astype(F32)
    hits, tops, tws = [], [], []
    sel = jnp.zeros((N_EXPERTS, tm), F32)
    for _ in range(TOP_K):
        m = jnp.max(cand, axis=0, keepdims=True)
        e = _first_index(cand == m, eidx, float(N_EXPERTS))
        hit = eidx == e
        hits.append(hit)
        tops.append(e)
        tws.append(jnp.sum(jnp.where(hit, scores, 0.0), axis=0, keepdims=True))
        cand = jnp.where(hit, -jnp.inf, cand)
        sel = jnp.where(hit, 1.0, sel)
    te_ref[...] = jnp.concatenate(tops, axis=0).astype(I32)
    tw = jnp.concatenate(tws, axis=0)
    tw_ref[...] = tw / jnp.sum(tw, axis=0, keepdims=True) * ROUTED_SCALE

    before = _dot(sel.astype(BF16), u_ref[...]) + carry_ref[...]
    rk_ref[...] = jnp.concatenate(
        [jnp.sum(jnp.where(hit, before, 0.0), axis=0, keepdims=True) for hit in hits], axis=0).astype(I32)
    carry = carry_ref[...] + jnp.sum(sel, axis=1, keepdims=True)
    carry_ref[...] = carry
    cnt_ref[...] = jnp.broadcast_to(carry, cnt_ref.shape).astype(I32)


MOE_TM = 256


def _route(x2, g, sc, sh, wrh, wrl, eb, seq):
    t, d = x2.shape
    tm = MOE_TM
    per_b = seq // tm
    r = np.arange(tm)
    u = jnp.asarray((r[:, None] < r[None, :]).astype(np.float32), BF16)
    full = lambda a: pl.BlockSpec(a.shape, lambda i: (0,) * a.ndim)
    vec = lambda: pl.BlockSpec((None, 1, d), lambda i: (i // per_b, 0, 0))
    colk = lambda: pl.BlockSpec((TOP_K, tm), lambda i: (0, i))
    return pl.pallas_call(
        _route_kernel,
        grid=(t // tm,),
        in_specs=[pl.BlockSpec((tm, d), lambda i: (i, 0)), full(g), vec(), vec(),
                  full(wrh), full(wrl), full(eb), full(u)],
        out_specs=[colk(), colk(), colk(), pl.BlockSpec((N_EXPERTS, LANES), lambda i: (0, 0))],
        out_shape=[jax.ShapeDtypeStruct((TOP_K, t), I32), jax.ShapeDtypeStruct((TOP_K, t), F32),
                   jax.ShapeDtypeStruct((TOP_K, t), I32), jax.ShapeDtypeStruct((N_EXPERTS, LANES), I32)],
        scratch_shapes=[pltpu.VMEM((N_EXPERTS, 1), F32)],
        compiler_params=_cparams(1),
        name="route",
    )(x2, g, sc, sh, wrh, wrl, eb, u)


ROW_TILE = 8


def _tiles_to_rows(ref, lead, first, m):
    return jnp.concatenate(
        [ref[lead + (pl.ds(first * ROW_TILE + j, m, stride=ROW_TILE), slice(None))] for j in range(ROW_TILE)], axis=1)


def _rows_to_tiles(ref, lead, val):
    m = val.shape[0]
    for j in range(ROW_TILE):
        ref[lead + (pl.ds(j, m, stride=ROW_TILE), slice(None))] = val[:, j * LANES:(j + 1) * LANES]


def _tile_rows(ref, lead, row):
    return ref.at[lead + (pl.ds(pl.multiple_of(row, ROW_TILE) if not isinstance(row, int) else row, ROW_TILE),)]


def _dispatch_kernel(pend_ref, dst_ref, x_ref, g_ref, sc_ref, sh_ref, g2_ref, ws1_ref, ws3_ref, ws2_ref,
                     base_ref, xs_hbm, hbuf, zbuf, sem, zsem, *, n_tiles):
    i = pl.program_id(0)
    odd = i % 2 == 1
    even = jnp.logical_not(odd)
    tm = x_ref.shape[0]

    @pl.when(i == 0)
    def _():
        zbuf[...] = jnp.zeros_like(zbuf)
        def clear(e, start):
            first = jnp.maximum(pend_ref[e] - MOE_BLOCK, 0) * ROW_TILE
            cp = pltpu.make_async_copy(zbuf, xs_hbm.at[pl.ds(pl.multiple_of(first, ROW_TILE), zbuf.shape[0])], zsem)
            cp.start() if start else cp.wait()
        for e in range(N_EXPERTS):
            clear(e, True)
        for e in range(N_EXPERTS):
            clear(e, False)

    n_parts = 4

    def copies(sl, start, part=None):
        for k in range(TOP_K):
            for t in range(tm):
                n = k * tm + t
                if part is not None and n * n_parts // (TOP_K * tm) != part:
                    continue
                cp = pltpu.make_async_copy(_tile_rows(hbuf, (sl,), ROW_TILE * t),
                                           _tile_rows(xs_hbm, (), dst_ref[k, t] if start else 0), sem.at[sl])
                cp.start(priority=n % 2) if start else cp.wait()

    pl.when((i >= 2) & even)(functools.partial(copies, 0, False))
    pl.when((i >= 2) & odd)(functools.partial(copies, 1, False))

    def main(sl):
        x = x_ref[...]
        h = _modulated_norm(x, g_ref[...], sc_ref[...], sh_ref[...])
        _rows_to_tiles(hbuf, (sl,), _pack_halves(h))
        copies(sl, True, part=0)
        hb = h.astype(BF16)
        up = _dot(hb, ws1_ref[...])
        copies(sl, True, part=1)
        gate = _dot(hb, ws3_ref[...])
        copies(sl, True, part=2)
        hs = (_silu(up) * gate).astype(BF16)
        base_ref[...] = x + g2_ref[...] * _dot(hs, ws2_ref[...])
        copies(sl, True, part=3)

    pl.when(even)(functools.partial(main, 0))
    pl.when(odd)(functools.partial(main, 1))

    @pl.when(i == n_tiles - 1)
    def _():
        if n_tiles >= 2:
            copies(n_tiles % 2, False)
        copies((n_tiles - 1) % 2, False)


def _dispatch(x2, g, sc, sh, g2, ws1, ws3, ws2, pends, dest, n_rows, seq):
    t, d = x2.shape
    tm = MOE_TM
    per_b = seq // tm
    full = lambda a: pl.BlockSpec(a.shape, lambda i, pe: (0,) * a.ndim)
    vec = lambda: pl.BlockSpec((None, 1, d), lambda i, pe: (i // per_b, 0, 0))
    return pl.pallas_call(
        functools.partial(_dispatch_kernel, n_tiles=t // tm),
        grid_spec=pltpu.PrefetchScalarGridSpec(
            num_scalar_prefetch=1,
            grid=(t // tm,),
            in_specs=[pl.BlockSpec((TOP_K, tm), lambda i, pe: (0, i), memory_space=pltpu.SMEM),
                      pl.BlockSpec((tm, d), lambda i, pe: (i, 0)), full(g), vec(), vec(), vec(),
                      full(ws1), full(ws3), full(ws2)],
            out_specs=[pl.BlockSpec((tm, d), lambda i, pe: (i, 0)), pl.BlockSpec(memory_space=pl.ANY)],
            scratch_shapes=[pltpu.VMEM((2, tm * ROW_TILE, LANES), U32),
                            pltpu.VMEM((MOE_BLOCK * ROW_TILE, LANES), U32),
                            pltpu.SemaphoreType.DMA((2,)), pltpu.SemaphoreType.DMA(())]),
        out_shape=[jax.ShapeDtypeStruct((t, d), F32), jax.ShapeDtypeStruct((n_rows * ROW_TILE, LANES), U32)],
        compiler_params=_cparams(1),
        name="dispatch",
    )(pends, dest, x2, g, sc, sh, g2, ws1, ws3, ws2)


def _experts_kernel(be_ref, nu_ref, xs_ref, w1_ref, w3_ref, w2_ref, ys_ref, w1b, w3b, w2b):
    s = pl.program_id(0)

    @pl.when((s == 0) | (be_ref[s] != be_ref[jnp.maximum(s - 1, 0)]))
    def _():
        w1b[...] = w1_ref[...].astype(BF16)
        w3b[...] = w3_ref[...].astype(BF16)
        w2b[...] = w2_ref[...].astype(BF16)

    @pl.when(s < nu_ref[0])
    def _():
        lo, hi = _unpack_halves(_tiles_to_rows(xs_ref, (), 0, MOE_BLOCK))
        xb = jnp.concatenate([lo.astype(BF16), hi.astype(BF16)], axis=1)
        hid = (_silu(_dot(xb, w1b[...])) * _dot(xb, w3b[...])).astype(BF16)
        _rows_to_tiles(ys_ref, (), _pack_halves(_dot(hid, w2b[...])))


def _experts(layer, block_e, n_used, xs, w1, w3, w2):
    nb = block_e.shape[0]
    rows = pl.BlockSpec((MOE_BLOCK * ROW_TILE, LANES), lambda s, be, nu: (jnp.minimum(s, nu[0] - 1), 0))
    wspec = lambda a: pl.BlockSpec((None, None) + a.shape[2:], lambda s, be, nu: (layer, be[s], 0, 0))
    return pl.pallas_call(
        _experts_kernel,
        grid_spec=pltpu.PrefetchScalarGridSpec(
            num_scalar_prefetch=2,
            grid=(nb,),
            in_specs=[rows, wspec(w1), wspec(w3), wspec(w2)],
            out_specs=rows,
            scratch_shapes=[pltpu.VMEM(w1.shape[2:], BF16), pltpu.VMEM(w3.shape[2:], BF16),
                            pltpu.VMEM(w2.shape[2:], BF16)]),
        out_shape=jax.ShapeDtypeStruct(xs.shape, U32),
        compiler_params=_cparams(1),
        name="experts",
    )(block_e, n_used, xs, w1, w3, w2)


def _combine_kernel(dst_ref, dstn_ref, base_ref, g2_ref, w_ref, fg_ref, ys_hbm, o_ref, gbuf, sem, *, n_tiles, final):
    i = pl.program_id(0)
    odd = i % 2 == 1
    even = jnp.logical_not(odd)
    tm = base_ref.shape[0]

    rb = 32
    n_rb = tm // rb

    def fetch(idx_ref, sl, start, part=None):
        for k in range(TOP_K):
            for t in range(tm):
                n = k * tm + t
                if part is not None and n * n_rb // (TOP_K * tm) != part:
                    continue
                cp = pltpu.make_async_copy(_tile_rows(ys_hbm, (), idx_ref[k, t] if start else 0),
                                           _tile_rows(gbuf, (sl,), ROW_TILE * n), sem.at[sl])
                cp.start(priority=n % 2) if start else cp.wait()

    @pl.when(i == 0)
    def _():
        fetch(dst_ref, 0, True)

    def main(sl):
        fetch(dst_ref, sl, False)
        half = o_ref.shape[1] // 2

        def rows_step(ib):
            r0 = ib * rb
            rows = pl.ds(r0, rb)
            w = w_ref[rows, :]
            wb = [jnp.broadcast_to(w[:, k:k + 1], (rb, LANES)) for k in range(TOP_K)]
            ssq = jnp.zeros((rb, 1), F32)
            for j in range(ROW_TILE):
                acc_lo = acc_hi = None
                for k in range(TOP_K):
                    lo, hi = _unpack_halves(gbuf[sl, pl.ds((k * tm + r0) * ROW_TILE + j, rb, stride=ROW_TILE), :])
                    acc_lo = wb[k] * lo if acc_lo is None else acc_lo + wb[k] * lo
                    acc_hi = wb[k] * hi if acc_hi is None else acc_hi + wb[k] * hi
                for acc, c0 in ((acc_lo, j * LANES), (acc_hi, half + j * LANES)):
                    x = base_ref[rows, c0:c0 + LANES] + g2_ref[:, c0:c0 + LANES] * acc
                    o_ref[rows, c0:c0 + LANES] = x
                    ssq = ssq + jnp.sum(x * x, axis=-1, keepdims=True)
            if final:
                scale = lax.rsqrt(ssq * (1.0 / o_ref.shape[1]) + EPS)
                o_ref[rows, :] = o_ref[rows, :] * scale * fg_ref[...]

        for ib in range(n_rb):
            rows_step(ib)
            fetch(dstn_ref, 1 - sl, True, part=ib)

    pl.when(even)(functools.partial(main, 0))
    pl.when(odd)(functools.partial(main, 1))

    @pl.when(i == n_tiles - 1)
    def _():
        fetch(dst_ref, n_tiles % 2, False)


def _combine(base, g2, w, ys, dest, fg, seq, final):
    t, d = base.shape
    tm = MOE_TM
    per_b = seq // tm
    n_tiles = t // tm
    idx = lambda f: pl.BlockSpec((TOP_K, tm), f, memory_space=pltpu.SMEM)
    return pl.pallas_call(
        functools.partial(_combine_kernel, n_tiles=n_tiles, final=final),
        grid=(n_tiles,),
        in_specs=[idx(lambda i: (0, i)), idx(lambda i: (0, jnp.minimum(i + 1, n_tiles - 1))),
                  pl.BlockSpec((tm, d), lambda i: (i, 0)),
                  pl.BlockSpec((None, 1, d), lambda i: (i // per_b, 0, 0)),
                  pl.BlockSpec((tm, TOP_K), lambda i: (i, 0)),
                  pl.BlockSpec((1, d), lambda i: (0, 0)),
                  pl.BlockSpec(memory_space=pl.ANY)],
        out_specs=pl.BlockSpec((tm, d), lambda i: (i, 0)),
        out_shape=jax.ShapeDtypeStruct((t, d), F32),
        scratch_shapes=[pltpu.VMEM((2, TOP_K * tm * ROW_TILE, LANES), U32), pltpu.SemaphoreType.DMA((2,))],
        compiler_params=_cparams(1),
        name="combine",
    )(dest, dest, base, g2, w, fg, ys)


def _dispatch_tables(top_e, rank, counts, t):
    n_blocks = TOP_K * t // MOE_BLOCK + N_EXPERTS
    padded = (counts + MOE_BLOCK - 1) // MOE_BLOCK * MOE_BLOCK
    pends = jnp.cumsum(padded)
    pstarts = pends - padded
    experts = jnp.arange(N_EXPERTS, dtype=I32)
    pstart_of = jnp.sum(jnp.where(top_e[..., None] == experts, pstarts, 0), axis=-1)
    dest = (pstart_of + rank) * ROW_TILE
    starts = jnp.arange(n_blocks, dtype=I32) * MOE_BLOCK
    block_e = jnp.sum((pends[None, :] <= starts[:, None]).astype(I32), axis=1)
    block_e = jnp.minimum(block_e, jnp.max(jnp.where(starts < pends[-1], block_e, 0)))
    n_used = (pends[-1:] // MOE_BLOCK).astype(I32)
    return block_e, n_used, pends.astype(I32), dest, n_blocks * MOE_BLOCK


def _layout_inproj(w_in, b_in):
    o = np.cumsum([0, 2 * CONV_DIM, FOX_DIM, FOX_DIM, FOX_DIM, FOX_HEADS, GLA_KDIM, GLA_KDIM, GLA_VDIM, GLA_VDIM,
                   GLA_RANK, 3 * D_MODEL])
    seg = lambda a, i: a[..., int(o[i]):int(o[i + 1])]

    def main(a):
        return jnp.concatenate([seg(a, 10), seg(a, 0), seg(a, 1) * (FOX_HEAD_DIM ** -0.5 * LOG2E), seg(a, 2), seg(a, 3),
                                seg(a, 5) * GLA_DK ** -0.5, seg(a, 6), seg(a, 7), seg(a, 8)], axis=-1)

    def small(a):
        pad = jnp.zeros(a.shape[:-1] + (LANES - FOX_HEADS - GLA_RANK,), a.dtype)
        return jnp.concatenate([seg(a, 4), seg(a, 9), pad], axis=-1)

    return main(w_in).astype(BF16), main(b_in)[None, :], small(w_in).astype(BF16), small(b_in)[None, :]


def kernel(x, c, norm1_g, ada_w, ada_b, w_in, b_in, conv_w, conv_b, conv_ln_g, conv_ln_b, gla_wa, gla_ba,
           gla_norm_g, w_branch_a, w_branch_b, w_branch_c, w_out, norm2_g, w_router, e_bias, w1, w3, w2,
           ws1, ws3, ws2, final_g):
    bsz, seq, d = x.shape
    t = bsz * seq
    depth = ada_w.shape[0]
    mod = _ada_mod(c, ada_w, ada_b).reshape(depth, bsz, 6, 1, d)
    x2 = x.reshape(t, d)
    row = lambda a: a.reshape(1, -1)
    for l in range(depth):
        sh1, sc1, g1, sh2, sc2, g2 = (mod[l, :, i] for i in range(6))
        w, b, ws, bs = _layout_inproj(w_in[l], b_in[l])
        z, zs = _inproj(x2, row(norm1_g[l]), sc1, sh1, w, b, ws, bs, seq)
        wa_pad = jnp.zeros((LANES, GLA_KDIM), F32).at[ZS_A:ZS_A + GLA_RANK].set(gla_wa[l])
        aq, ak, bc = _prep(zs, wa_pad, row(gla_ba[l]), seq)
        ya = _conv(z, conv_w[l].reshape(CONV_WIDTH, CONV_DIM), row(conv_b[l]), row(conv_ln_g[l]),
                   row(conv_ln_b[l]), seq)
        yb = _fox(z, aq, ak, seq)
        yc = _gla(z, bc, row(gla_norm_g[l]), seq)
        x2 = _merge(x2, g1, ya, yb, yc, z, w_branch_a[l].astype(BF16), w_branch_b[l].astype(BF16),
                    w_branch_c[l].astype(BF16), w_out[l].astype(BF16), seq)

        wr_t = w_router[l].T
        wrh = wr_t.astype(BF16)
        wrl = (wr_t - wrh.astype(F32)).astype(BF16)
        top_e, top_w, rank, counts = _route(x2, row(norm2_g[l]), sc2, sh2, wrh, wrl,
                                            e_bias[l].reshape(N_EXPERTS, 1), seq)
        block_e, n_used, pends, dest, n_rows = _dispatch_tables(top_e, rank, counts[:, 0], t)
        base, xs = _dispatch(x2, row(norm2_g[l]), sc2, sh2, g2, ws1[l].astype(BF16), ws3[l].astype(BF16),
                             ws2[l].astype(BF16), pends, dest, n_rows, seq)
        ys = _experts(l, block_e, n_used, xs, w1, w3, w2)
        x2 = _combine(base, g2, top_w.T, ys, dest, row(final_g), seq, final=(l == depth - 1))
    return x2.reshape(bsz, seq, d)
```

```python
import functools

import numpy as np
import jax
import jax.numpy as jnp
from jax import lax
from jax.experimental import pallas as pl
from jax.experimental.pallas import tpu as pltpu

F32 = jnp.float32
BF16 = jnp.bfloat16
U32 = jnp.uint32
I32 = jnp.int32

D_MODEL = 2048
CONV_DIM = 512
CONV_WIDTH = 31
FOX_HEADS = 8
FOX_HEAD_DIM = 128
FOX_DIM = FOX_HEADS * FOX_HEAD_DIM
GLA_HEADS = 4
GLA_DK = 64
GLA_DV = 128
GLA_KDIM = GLA_HEADS * GLA_DK
GLA_VDIM = GLA_HEADS * GLA_DV
GLA_RANK = 16
GLA_TAU = 16.0
GLA_CHUNK = 64
GLA_SUB = 8
N_EXPERTS = 64
N_GROUPS = 8
TOPK_GROUPS = 4
TOP_K = 8
EXPERT_DIM = 512
SHARED_DIM = 512
ROUTED_SCALE = 2.5
MOE_BLOCK = 512
EPS = 1e-6

LANES = 128
V7X_VMEM_LIMIT = 56 * 1024 * 1024

Z_GATE = 0
Z_CONV = 48
Z_QF = 56
Z_KF = 64
Z_VF = 72
Z_QG = 80
Z_KG = 82
Z_VG = 84
Z_RG = 88
Z_COLS = 92 * LANES
ZS_F = 0
ZS_A = 8
NEG = -1e30
LOG2E = 1.4426950408889634


def _cparams(n_axes, vmem=V7X_VMEM_LIMIT, flags=None):
    return pltpu.CompilerParams(dimension_semantics=("arbitrary",) * n_axes, vmem_limit_bytes=vmem, flags=flags)


def _log_sigmoid(x):
    return jnp.minimum(x, 0.0) - jnp.log(1.0 + jnp.exp(-jnp.abs(x)))


def _silu(x):
    return x * jax.nn.sigmoid(x)


def _split3(x):
    hi = x.astype(BF16)
    r1 = x - hi.astype(F32)
    mid = r1.astype(BF16)
    lo = (r1 - mid.astype(F32)).astype(BF16)
    return hi, mid, lo


def _dot(a, b):
    return jnp.dot(a, b, preferred_element_type=F32)


def _dot_nt(a, b):
    return lax.dot_general(a, b, (((1,), (1,)), ((), ())), preferred_element_type=F32)


def _dot_tn(a, b):
    return lax.dot_general(a, b, (((0,), (0,)), ((), ())), preferred_element_type=F32)


def _lmul_exact(l01, x):
    hi, mid, lo = _split3(x)
    return _dot(l01, hi) + _dot(l01, mid) + _dot(l01, lo)


def _dot_f32(a, b):
    a0, a1, a2 = _split3(a)
    b0, b1, b2 = _split3(b)
    return (_dot(a0, b0) + _dot(a0, b1) + _dot(a1, b0)) + (_dot(a0, b2) + _dot(a1, b1) + _dot(a2, b0))


def _ada_kernel(ct_ref, w_ref, b_ref, o_ref):
    ct = ct_ref[...]
    act = _silu(ct)
    w = w_ref[...]
    rows = [jnp.sum(w * act[:, b:b + 1], axis=0, keepdims=True) for b in range(ct.shape[1])]
    o_ref[...] = jnp.concatenate(rows, axis=0) + b_ref[...]


def _ada_mod(c, ada_w, ada_b):
    n_l, d, n = ada_w.shape
    bsz = c.shape[0]
    tn = 1024
    return pl.pallas_call(
        _ada_kernel,
        grid=(n_l, n // tn),
        in_specs=[pl.BlockSpec((d, bsz), lambda l, j: (0, 0)),
                  pl.BlockSpec((None, d, tn), lambda l, j: (l, 0, j)),
                  pl.BlockSpec((None, 1, tn), lambda l, j: (l, 0, j))],
        out_specs=pl.BlockSpec((None, bsz, tn), lambda l, j: (l, 0, j)),
        out_shape=jax.ShapeDtypeStruct((n_l, bsz, n), F32),
        compiler_params=_cparams(2),
        name="ada_mod",
    )(c.T, ada_w, ada_b.reshape(n_l, 1, n))


def _modulated_norm(x, g, sc, sh):
    ms = jnp.mean(x * x, axis=-1, keepdims=True)
    return (x * lax.rsqrt(ms + EPS) * g) * (1.0 + sc) + sh


def _inproj_kernel(x_ref, g_ref, sc_ref, sh_ref, w_ref, b_ref, ws_ref, bs_ref, z_ref, zs_ref, h_scr):
    @pl.when(pl.program_id(1) == 0)
    def _():
        h = _modulated_norm(x_ref[...], g_ref[...], sc_ref[...], sh_ref[...]).astype(BF16)
        h_scr[...] = h
        zs_ref[...] = _dot(h, ws_ref[...]) + bs_ref[...]

    z_ref[...] = (_dot(h_scr[...], w_ref[...]) + b_ref[...]).astype(BF16)


def _inproj(x2, g, sc, sh, w, b, ws, bs, seq):
    t, d = x2.shape
    tm, tn = 512, Z_COLS // 4
    per_b = seq // tm
    vec = lambda: pl.BlockSpec((None, 1, d), lambda i, j: (i // per_b, 0, 0))
    return pl.pallas_call(
        _inproj_kernel,
        grid=(t // tm, Z_COLS // tn),
        in_specs=[pl.BlockSpec((tm, d), lambda i, j: (i, 0)),
                  pl.BlockSpec((1, d), lambda i, j: (0, 0)),
                  vec(), vec(),
                  pl.BlockSpec((d, tn), lambda i, j: (0, j)),
                  pl.BlockSpec((1, tn), lambda i, j: (0, j)),
                  pl.BlockSpec((d, LANES), lambda i, j: (0, 0)),
                  pl.BlockSpec((1, LANES), lambda i, j: (0, 0))],
        out_specs=[pl.BlockSpec((tm, tn), lambda i, j: (i, j)),
                   pl.BlockSpec((tm, LANES), lambda i, j: (i, 0))],
        out_shape=[jax.ShapeDtypeStruct((t, Z_COLS), BF16), jax.ShapeDtypeStruct((t, LANES), F32)],
        scratch_shapes=[pltpu.VMEM((tm, d), BF16)],
        compiler_params=_cparams(2),
        name="inproj",
    )(x2, g, sc, sh, w, b, ws, bs)


def _prep_kernel(zs_ref, ltri_ref, lblk_ref, pq_ref, pk_ref, cq_ref, ck_ref, wa_ref, ba_ref,
                 aq_ref, ak_ref, bc_ref, carry_ref):
    @pl.when(pl.program_id(1) == 0)
    def _():
        carry_ref[...] = jnp.zeros_like(carry_ref)

    zs = zs_ref[...]
    cum = _lmul_exact(ltri_ref[...], _log_sigmoid(zs)) + carry_ref[...]
    carry_ref[...] = cum[cum.shape[0] - 1:, :]
    hi, mid, lo = _split3(cum * LOG2E)
    aq = cq_ref[...] + _dot(hi, pq_ref[0]) + _dot(mid, pq_ref[1]) + _dot(lo, pq_ref[2])
    ak = ck_ref[...] + _dot(hi, pk_ref[0]) + _dot(mid, pk_ref[1]) + _dot(lo, pk_ref[2])
    aq_ref[...] = aq.astype(BF16)
    ak_ref[...] = ak.astype(BF16)
    la = _log_sigmoid(_dot_f32(zs, wa_ref[...]) + ba_ref[...]) * (1.0 / GLA_TAU)
    bc_ref[...] = _lmul_exact(lblk_ref[...], la)


def _prep_consts(ts):
    r = np.arange(ts)
    ltri = (r[:, None] >= r[None, :]).astype(np.float32)
    lblk = ltri * (r[:, None] // GLA_CHUNK == r[None, :] // GLA_CHUNK)
    pq = np.zeros((3, LANES, FOX_DIM), np.float32)
    pk = np.zeros((3, LANES, FOX_DIM), np.float32)
    cq = np.zeros((1, FOX_DIM), np.float32)
    ck = np.zeros((1, FOX_DIM), np.float32)
    for h in range(FOX_HEADS):
        for p in range(3):
            pq[p, ZS_F + h, h * FOX_HEAD_DIM + p] = 1.0
            pk[p, ZS_F + h, h * FOX_HEAD_DIM + 3 + p] = -1.0
            cq[0, h * FOX_HEAD_DIM + 3 + p] = 1.0
            ck[0, h * FOX_HEAD_DIM + p] = 1.0
    return (jnp.asarray(ltri, BF16), jnp.asarray(lblk, BF16), jnp.asarray(pq, BF16), jnp.asarray(pk, BF16),
            jnp.asarray(cq), jnp.asarray(ck))


def _prep(zs, wa_pad, ba, seq):
    t = zs.shape[0]
    ts = 256
    per_b = seq // ts
    ltri, lblk, pq, pk, cq, ck = _prep_consts(ts)
    full = lambda a: pl.BlockSpec(a.shape, lambda b, s: (0,) * a.ndim)
    row = lambda n: pl.BlockSpec((ts, n), lambda b, s: (b * per_b + s, 0))
    return pl.pallas_call(
        _prep_kernel,
        grid=(t // seq, per_b),
        in_specs=[row(LANES), full(ltri), full(lblk), full(pq), full(pk), full(cq), full(ck),
                  full(wa_pad), full(ba)],
        out_specs=[row(FOX_DIM), row(FOX_DIM), row(GLA_KDIM)],
        out_shape=[jax.ShapeDtypeStruct((t, FOX_DIM), BF16), jax.ShapeDtypeStruct((t, FOX_DIM), BF16),
                   jax.ShapeDtypeStruct((t, GLA_KDIM), F32)],
        scratch_shapes=[pltpu.VMEM((1, LANES), F32)],
        compiler_params=_cparams(2),
        name="prep",
    )(zs, ltri, lblk, pq, pk, cq, ck, wa_pad, ba)


CONV_HIST = 32


def _conv_kernel(u_ref, w_ref, cb_ref, lg_ref, lb_ref, o_ref, ypad_ref, shift_ref):
    ts = u_ref.shape[0]
    sub = 8

    @pl.when(pl.program_id(1) == 0)
    def _():
        ypad_ref[0:CONV_HIST, :] = jnp.zeros((CONV_HIST, CONV_DIM), F32)

    u = u_ref[...].astype(F32)
    ypad_ref[CONV_HIST:CONV_HIST + ts, :] = u[:, :CONV_DIM] * jax.nn.sigmoid(u[:, CONV_DIM:])
    span = ts + CONV_HIST - sub
    for r in range(1, sub):
        shift_ref[r - 1] = ypad_ref[r:r + span, :]
    acc = jnp.zeros((ts, CONV_DIM), F32) + cb_ref[...]
    first = CONV_HIST - (CONV_WIDTH - 1)
    for j in range(CONV_WIDTH):
        r = (first + j) % sub
        base = first + j - r
        rows = ypad_ref[base:base + ts, :] if r == 0 else shift_ref[r - 1, base:base + ts, :]
        acc = acc + w_ref[j:j + 1, :] * rows
    ypad_ref[0:CONV_HIST, :] = ypad_ref[ts:ts + CONV_HIST, :]
    mu = jnp.mean(acc, axis=-1, keepdims=True)
    cen = acc - mu
    var = jnp.mean(cen * cen, axis=-1, keepdims=True)
    y = cen * lax.rsqrt(var + EPS) * lg_ref[...] + lb_ref[...]
    o_ref[...] = _silu(y).astype(BF16)


def _conv(z, w, cb, lg, lb, seq):
    t = z.shape[0]
    ts = 512
    per_b = seq // ts
    full = lambda a: pl.BlockSpec(a.shape, lambda b, s: (0,) * a.ndim)
    return pl.pallas_call(
        _conv_kernel,
        grid=(t // seq, per_b),
        in_specs=[pl.BlockSpec((ts, 2 * CONV_DIM), lambda b, s: (b * per_b + s, Z_CONV * LANES // (2 * CONV_DIM))),
                  full(w), full(cb), full(lg), full(lb)],
        out_specs=pl.BlockSpec((ts, CONV_DIM), lambda b, s: (b * per_b + s, 0)),
        out_shape=jax.ShapeDtypeStruct((t, CONV_DIM), BF16),
        scratch_shapes=[pltpu.VMEM((ts + CONV_HIST, CONV_DIM), F32),
                        pltpu.VMEM((7, ts + CONV_HIST - 8, CONV_DIM), F32)],
        compiler_params=_cparams(2),
        name="conv",
    )(z, w, cb, lg, lb)


FOX_HEADS_PER_STEP = 2


def _fox_kernel(q_ref, aq_ref, k_ref, ak_ref, v_ref, o_ref, kk_ref, vt_ref, m_ref, acc_ref, *, blk):
    i = pl.program_id(2)
    dh = FOX_HEAD_DIM
    heads = range(FOX_HEADS_PER_STEP)
    col = lambda h: slice(h * dh, (h + 1) * dh)

    @pl.when(i == 0)
    def _():
        def fill(cb, carry):
            rows = pl.ds(pl.multiple_of(cb * blk, blk), blk)
            for h in heads:
                kk_ref[h, rows, 0:dh] = k_ref[rows, col(h)]
                kk_ref[h, rows, dh:2 * dh] = ak_ref[rows, col(h)]
                vt_ref[h, cb, 0:dh, :] = v_ref[rows, col(h)].astype(F32).T.astype(BF16)
                vt_ref[h, cb, dh:2 * dh, :] = jnp.ones((dh, blk), BF16)
            return carry
        lax.fori_loop(0, k_ref.shape[0] // blk, fill, 0)

    qq = [jnp.concatenate([q_ref[:, col(h)], aq_ref[:, col(h)]], axis=1) for h in heads]
    m_ref[...] = jnp.full(m_ref.shape, NEG, F32)
    acc_ref[...] = jnp.zeros(acc_ref.shape, F32)

    def step(j, masked):
        rows = pl.ds(pl.multiple_of(j * blk, blk), blk)
        sts = [_dot_nt(kk_ref[h, rows, :], qq[h]) for h in heads]
        for h in heads:
            st = sts[h]
            if masked:
                kv = lax.broadcasted_iota(I32, (blk, blk), 0)
                qi = lax.broadcasted_iota(I32, (blk, blk), 1)
                st = jnp.where(kv <= qi, st, NEG)
            m_old = m_ref[h]
            m_new = jnp.maximum(m_old, jnp.max(st, axis=0, keepdims=True))
            p = jnp.exp2(st - m_new).astype(BF16)
            acc_ref[h] = acc_ref[h] * jnp.exp2(m_old - m_new) + _dot(vt_ref[h, j], p)
            m_ref[h] = m_new

    def body(j, carry):
        step(j, False)
        return carry

    lax.fori_loop(0, i, body, 0)
    step(i, True)
    for h in heads:
        acc = acc_ref[h]
        o_ref[:, col(h)] = (acc[:dh, :] / acc[dh:, :]).T.astype(BF16)


def _fox(z, aq, ak, seq):
    t = z.shape[0]
    blk = min(1024, seq)
    nq = seq // blk
    hps = FOX_HEADS_PER_STEP
    width = hps * FOX_HEAD_DIM
    unit = lambda base: base * LANES // width
    qspec = lambda base: pl.BlockSpec((blk, width), lambda b, g, i: (b * nq + i, unit(base) + g))
    kspec = lambda base: pl.BlockSpec((seq, width), lambda b, g, i: (b, unit(base) + g),
                                      pipeline_mode=pl.Buffered(1))
    return pl.pallas_call(
        functools.partial(_fox_kernel, blk=blk),
        grid=(t // seq, FOX_HEADS // hps, nq),
        in_specs=[qspec(Z_QF), qspec(0), kspec(Z_KF), kspec(0), kspec(Z_VF)],
        out_specs=qspec(0),
        out_shape=jax.ShapeDtypeStruct((t, FOX_DIM), BF16),
        scratch_shapes=[pltpu.VMEM((hps, seq, 2 * FOX_HEAD_DIM), BF16),
                        pltpu.VMEM((hps, nq, 2 * FOX_HEAD_DIM, blk), BF16),
                        pltpu.VMEM((hps, 1, blk), F32), pltpu.VMEM((hps, 2 * FOX_HEAD_DIM, blk), F32)],
        compiler_params=_cparams(3),
        name="fox",
    )(z, aq, z, ak, z)


def _gla_chunk(q, k, v, b, st, e3):
    c, nsub = GLA_CHUNK, GLA_CHUNK // GLA_SUB
    row = lax.broadcasted_iota(I32, (c, GLA_KDIM), 0)
    rblk, rloc = row // GLA_SUB, row % GLA_SUB
    lane_k = lax.broadcasted_iota(I32, (c, GLA_KDIM), 1)
    khead, kpos = lane_k // GLA_DK, lane_k % GLA_DK
    b3 = b.reshape(nsub, GLA_SUB, GLA_KDIM)
    k3 = k.reshape(nsub, GLA_SUB, GLA_KDIM)
    last = b3[:, GLA_SUB - 1:GLA_SUB, :]
    bcast = lambda a3, i: jnp.broadcast_to(a3[:, i:i + 1, :], (nsub, GLA_SUB, GLA_KDIM)).reshape(c, GLA_KDIM)

    prev = jnp.concatenate([last[:1], last[:nsub - 1]], axis=0)
    refq = jnp.broadcast_to(prev, (nsub, GLA_SUB, GLA_KDIM)).reshape(c, GLA_KDIM)
    qt = q * jnp.exp(jnp.minimum(b - refq, 0.0))
    qcat = jnp.concatenate([jnp.where(rblk == i, qt, 0.0) for i in range(1, nsub)], axis=1).astype(BF16)
    kts = []
    for i in range(1, nsub):
        kt = k * jnp.exp(jnp.minimum(last[i - 1] - b, 0.0))
        kts.append(jnp.where(row < GLA_SUB * i, kt, 0.0))
    kcat = jnp.concatenate(
        [jnp.concatenate([jnp.where(khead == h, kt, 0.0) for kt in kts], axis=1) for h in range(GLA_HEADS)],
        axis=0).astype(BF16)
    a_all = _dot_nt(qcat, kcat)

    for sl in range(GLA_SUB):
        p = q * bcast(k3, sl) * jnp.exp(jnp.minimum(b - bcast(b3, sl), 0.0))
        w = _dot(p.astype(BF16), e3)
        hit = (kpos == rblk * GLA_SUB + sl) & (rloc >= sl)
        a_all = a_all + jnp.where(hit, w, 0.0)

    lane_v = lax.broadcasted_iota(I32, (c, GLA_VDIM), 1) // GLA_DV
    vbd = jnp.concatenate([jnp.where(lane_v == h, v, jnp.zeros_like(v)) for h in range(GLA_HEADS)], axis=0)
    qe = (q * jnp.exp(b)).astype(BF16)
    o = _dot(a_all.astype(BF16), vbd) + _dot_nt(qe, st.astype(BF16))

    blast = b[c - 1:c, :]
    kd = (k * jnp.exp(blast - b)).astype(BF16)
    shead_v = lax.broadcasted_iota(I32, (GLA_VDIM, GLA_KDIM), 0) // GLA_DV
    shead_k = lax.broadcasted_iota(I32, (GLA_VDIM, GLA_KDIM), 1) // GLA_DK
    st_new = jnp.where(shead_v == shead_k, st * jnp.exp(blast) + _dot_tn(v, kd), 0.0)
    return o, st_new


def _gla_kernel(q_ref, k_ref, v_ref, r_ref, b_ref, ng_ref, e3_ref, o_ref, st_ref):
    @pl.when(pl.program_id(1) == 0)
    def _():
        st_ref[...] = jnp.zeros_like(st_ref)

    e3 = e3_ref[...]
    ng = ng_ref[...]

    def body(ci, carry):
        rows = pl.ds(pl.multiple_of(ci * GLA_CHUNK, GLA_CHUNK), GLA_CHUNK)
        o, st_new = _gla_chunk(q_ref[rows, :].astype(F32), k_ref[rows, :].astype(F32), v_ref[rows, :],
                               b_ref[rows, :], st_ref[...], e3)
        st_ref[...] = st_new
        outs = []
        for h in range(GLA_HEADS):
            oh = o[:, h * GLA_DV:(h + 1) * GLA_DV]
            outs.append(oh * lax.rsqrt(jnp.mean(oh * oh, axis=-1, keepdims=True) + EPS))
        y = jnp.concatenate(outs, axis=1) * ng * _silu(r_ref[rows, :].astype(F32))
        o_ref[rows, :] = y.astype(BF16)
        return carry

    lax.fori_loop(0, q_ref.shape[0] // GLA_CHUNK, body, 0)


def _gla(z, bc, ng, seq):
    t = z.shape[0]
    ts = min(512, seq)
    per_b = seq // ts
    lane = np.arange(GLA_KDIM)
    e3 = jnp.asarray((lane[:, None] // GLA_DK == lane[None, :] // GLA_DK).astype(np.float32), BF16)
    spec = lambda n, unit: pl.BlockSpec((ts, n), lambda b, s: (b * per_b + s, unit))
    full = lambda a: pl.BlockSpec(a.shape, lambda b, s: (0,) * a.ndim)
    return pl.pallas_call(
        _gla_kernel,
        grid=(t // seq, per_b),
        in_specs=[spec(GLA_KDIM, Z_QG * LANES // GLA_KDIM), spec(GLA_KDIM, Z_KG * LANES // GLA_KDIM),
                  spec(GLA_VDIM, Z_VG * LANES // GLA_VDIM), spec(GLA_VDIM, Z_RG * LANES // GLA_VDIM),
                  spec(GLA_KDIM, 0), full(ng), full(e3)],
        out_specs=spec(GLA_VDIM, 0),
        out_shape=jax.ShapeDtypeStruct((t, GLA_VDIM), BF16),
        scratch_shapes=[pltpu.VMEM((GLA_VDIM, GLA_KDIM), F32)],
        compiler_params=_cparams(2),
        name="gla",
    )(z, z, z, z, bc, ng, e3)


def _merge_kernel(x_ref, g1_ref, ya_ref, yb_ref, yc_ref, ga_ref, gb_ref, gc_ref,
                  wa_ref, wb_ref, wc_ref, wo_ref, o_ref):
    gate = lambda r: jax.nn.sigmoid(r[...].astype(F32))
    merged = (gate(ga_ref) * _dot(ya_ref[...], wa_ref[...]) + gate(gb_ref) * _dot(yb_ref[...], wb_ref[...])
              + gate(gc_ref) * _dot(yc_ref[...], wc_ref[...]))
    o_ref[...] = x_ref[...] + g1_ref[...] * _dot(merged.astype(BF16), wo_ref[...])


def _merge(x2, g1, ya, yb, yc, z, wa, wb, wc, wo, seq):
    t, d = x2.shape
    tm = 256
    per_b = seq // tm
    full = lambda a: pl.BlockSpec(a.shape, lambda i: (0,) * a.ndim)
    row = lambda n, unit=0: pl.BlockSpec((tm, n), lambda i: (i, unit))
    return pl.pallas_call(
        _merge_kernel,
        grid=(t // tm,),
        in_specs=[row(d), pl.BlockSpec((None, 1, d), lambda i: (i // per_b, 0, 0)),
                  row(CONV_DIM), row(FOX_DIM), row(GLA_VDIM), row(d, 0), row(d, 1), row(d, 2),
                  full(wa), full(wb), full(wc), full(wo)],
        out_specs=row(d),
        out_shape=jax.ShapeDtypeStruct((t, d), F32),
        compiler_params=_cparams(1),
        name="merge",
    )(x2, g1, ya, yb, yc, z, z, z, wa, wb, wc, wo)


def _pack_halves(y):
    n = y.shape[1] // 2
    lo = pltpu.bitcast(y[:, :n].astype(BF16).astype(F32), U32)
    hi = pltpu.bitcast(y[:, n:].astype(BF16).astype(F32), U32)
    return (hi & jnp.uint32(0xFFFF0000)) | (lo >> 16)


def _unpack_halves(p):
    lo = pltpu.bitcast(p << 16, F32)
    hi = pltpu.bitcast(p & jnp.uint32(0xFFFF0000), F32)
    return lo, hi


def _first_index(hit, idx, n):
    return jnp.min(jnp.where(hit, idx, n), axis=0, keepdims=True)


def _route_kernel(x_ref, g_ref, sc_ref, sh_ref, wrh_ref, wrl_ref, eb_ref, u_ref,
                  te_ref, tw_ref, rk_ref, cnt_ref, carry_ref):
    @pl.when(pl.program_id(0) == 0)
    def _():
        carry_ref[...] = jnp.zeros_like(carry_ref)

    x = x_ref[...]
    h = _modulated_norm(x, g_ref[...], sc_ref[...], sh_ref[...])
    hb = h.astype(BF16)

    hlo = (h - hb.astype(F32)).astype(BF16)
    logits = _dot_nt(wrh_ref[...], hb) + _dot_nt(wrh_ref[...], hlo) + _dot_nt(wrl_ref[...], hb)
    scores = jax.nn.sigmoid(logits)
    biased = scores + eb_ref[...]
    tm = x.shape[0]
    per_group = N_EXPERTS // N_GROUPS
    sub = lax.broadcasted_iota(I32, (per_group, tm), 0).astype(F32)
    gscore = []
    for g in range(N_GROUPS):
        blk = biased[g * per_group:(g + 1) * per_group, :]
        m1 = jnp.max(blk, axis=0, keepdims=True)
        rest = jnp.where(sub == _first_index(blk == m1, sub, float(per_group)), -jnp.inf, blk)
        gscore.append(m1 + jnp.max(rest, axis=0, keepdims=True))
    gs = jnp.concatenate(gscore, axis=0)
    gidx = lax.broadcasted_iota(I32, (N_GROUPS, tm), 0).astype(F32)
    gsel = jnp.zeros((N_GROUPS, tm), F32)
    for _ in range(TOPK_GROUPS):
        m = jnp.max(gs, axis=0, keepdims=True)
        hit = gidx == _first_index(gs == m, gidx, float(N_GROUPS))
        gsel = jnp.where(hit, 1.0, gsel)
        gs = jnp.where(hit, -jnp.inf, gs)
    cand = jnp.concatenate(
        [jnp.where(gsel[g:g + 1, :] > 0.5, biased[g * per_group:(g + 1) * per_group, :], -jnp.inf)
         for g in range(N_GROUPS)], axis=0)
    eidx = lax.broadcasted_iota(I32, (N_EXPERTS, tm), 0).astype(F32)
    hits, tops, tws = [], [], []
    sel = jnp.zeros((N_EXPERTS, tm), F32)
    for _ in range(TOP_K):
        m = jnp.max(cand, axis=0, keepdims=True)
        e = _first_index(cand == m, eidx, float(N_EXPERTS))
        hit = eidx == e
        hits.append(hit)
        tops.append(e)
        tws.append(jnp.sum(jnp.where(hit, scores, 0.0), axis=0, keepdims=True))
        cand = jnp.where(hit, -jnp.inf, cand)
        sel = jnp.where(hit, 1.0, sel)
    te_ref[...] = jnp.concatenate(tops, axis=0).astype(I32)
    tw = jnp.concatenate(tws, axis=0)
    tw_ref[...] = tw / jnp.sum(tw, axis=0, keepdims=True) * ROUTED_SCALE

    before = _dot(sel.astype(BF16), u_ref[...]) + carry_ref[...]
    rk_ref[...] = jnp.concatenate(
        [jnp.sum(jnp.where(hit, before, 0.0), axis=0, keepdims=True) for hit in hits], axis=0).astype(I32)
    carry = carry_ref[...] + jnp.sum(sel, axis=1, keepdims=True)
    carry_ref[...] = carry
    cnt_ref[...] = jnp.broadcast_to(carry, cnt_ref.shape).astype(I32)


MOE_TM = 256


def _route(x2, g, sc, sh, wrh, wrl, eb, seq):
    t, d = x2.shape
    tm = MOE_TM
    per_b = seq // tm
    r = np.arange(tm)
    u = jnp.asarray((r[:, None] < r[None, :]).astype(np.float32), BF16)
    full = lambda a: pl.BlockSpec(a.shape, lambda i: (0,) * a.ndim)
    vec = lambda: pl.BlockSpec((None, 1, d), lambda i: (i // per_b, 0, 0))
    colk = lambda: pl.BlockSpec((TOP_K, tm), lambda i: (0, i))
    return pl.pallas_call(
        _route_kernel,
        grid=(t // tm,),
        in_specs=[pl.BlockSpec((tm, d), lambda i: (i, 0)), full(g), vec(), vec(),
                  full(wrh), full(wrl), full(eb), full(u)],
        out_specs=[colk(), colk(), colk(), pl.BlockSpec((N_EXPERTS, LANES), lambda i: (0, 0))],
        out_shape=[jax.ShapeDtypeStruct((TOP_K, t), I32), jax.ShapeDtypeStruct((TOP_K, t), F32),
                   jax.ShapeDtypeStruct((TOP_K, t), I32), jax.ShapeDtypeStruct((N_EXPERTS, LANES), I32)],
        scratch_shapes=[pltpu.VMEM((N_EXPERTS, 1), F32)],
        compiler_params=_cparams(1),
        name="route",
    )(x2, g, sc, sh, wrh, wrl, eb, u)


ROW_TILE = 8


def _tiles_to_rows(ref, lead, first, m):
    return jnp.concatenate(
        [ref[lead + (pl.ds(first * ROW_TILE + j, m, stride=ROW_TILE), slice(None))] for j in range(ROW_TILE)], axis=1)


def _rows_to_tiles(ref, lead, val):
    m = val.shape[0]
    for j in range(ROW_TILE):
        ref[lead + (pl.ds(j, m, stride=ROW_TILE), slice(None))] = val[:, j * LANES:(j + 1) * LANES]


def _tile_rows(ref, lead, row):
    return ref.at[lead + (pl.ds(pl.multiple_of(row, ROW_TILE) if not isinstance(row, int) else row, ROW_TILE),)]


def _dispatch_kernel(pend_ref, dst_ref, x_ref, g_ref, sc_ref, sh_ref, g2_ref, ws1_ref, ws3_ref, ws2_ref,
                     base_ref, xs_hbm, hbuf, zbuf, sem, zsem, *, n_tiles):
    i = pl.program_id(0)
    odd = i % 2 == 1
    even = jnp.logical_not(odd)
    tm = x_ref.shape[0]

    @pl.when(i == 0)
    def _():
        zbuf[...] = jnp.zeros_like(zbuf)
        def clear(e, start):
            first = jnp.maximum(pend_ref[e] - MOE_BLOCK, 0) * ROW_TILE
            cp = pltpu.make_async_copy(zbuf, xs_hbm.at[pl.ds(pl.multiple_of(first, ROW_TILE), zbuf.shape[0])], zsem)
            cp.start() if start else cp.wait()
        for e in range(N_EXPERTS):
            clear(e, True)
        for e in range(N_EXPERTS):
            clear(e, False)

    n_parts = 4

    def copies(sl, start, part=None):
        for k in range(TOP_K):
            for t in range(tm):
                n = k * tm + t
                if part is not None and n * n_parts // (TOP_K * tm) != part:
                    continue
                cp = pltpu.make_async_copy(_tile_rows(hbuf, (sl,), ROW_TILE * t),
                                           _tile_rows(xs_hbm, (), dst_ref[k, t] if start else 0), sem.at[sl])
                cp.start(priority=n % 2) if start else cp.wait()

    pl.when((i >= 2) & even)(functools.partial(copies, 0, False))
    pl.when((i >= 2) & odd)(functools.partial(copies, 1, False))

    def main(sl):
        x = x_ref[...]
        h = _modulated_norm(x, g_ref[...], sc_ref[...], sh_ref[...])
        _rows_to_tiles(hbuf, (sl,), _pack_halves(h))
        copies(sl, True, part=0)
        hb = h.astype(BF16)
        up = _dot(hb, ws1_ref[...])
        copies(sl, True, part=1)
        gate = _dot(hb, ws3_ref[...])
        copies(sl, True, part=2)
        hs = (_silu(up) * gate).astype(BF16)
        base_ref[...] = x + g2_ref[...] * _dot(hs, ws2_ref[...])
        copies(sl, True, part=3)

    pl.when(even)(functools.partial(main, 0))
    pl.when(odd)(functools.partial(main, 1))

    @pl.when(i == n_tiles - 1)
    def _():
        if n_tiles >= 2:
            copies(n_tiles % 2, False)
        copies((n_tiles - 1) % 2, False)


def _dispatch(x2, g, sc, sh, g2, ws1, ws3, ws2, pends, dest, n_rows, seq):
    t, d = x2.shape
    tm = MOE_TM
    per_b = seq // tm
    full = lambda a: pl.BlockSpec(a.shape, lambda i, pe: (0,) * a.ndim)
    vec = lambda: pl.BlockSpec((None, 1, d), lambda i, pe: (i // per_b, 0, 0))
    return pl.pallas_call(
        functools.partial(_dispatch_kernel, n_tiles=t // tm),
        grid_spec=pltpu.PrefetchScalarGridSpec(
            num_scalar_prefetch=1,
            grid=(t // tm,),
            in_specs=[pl.BlockSpec((TOP_K, tm), lambda i, pe: (0, i), memory_space=pltpu.SMEM),
                      pl.BlockSpec((tm, d), lambda i, pe: (i, 0)), full(g), vec(), vec(), vec(),
                      full(ws1), full(ws3), full(ws2)],
            out_specs=[pl.BlockSpec((tm, d), lambda i, pe: (i, 0)), pl.BlockSpec(memory_space=pl.ANY)],
            scratch_shapes=[pltpu.VMEM((2, tm * ROW_TILE, LANES), U32),
                            pltpu.VMEM((MOE_BLOCK * ROW_TILE, LANES), U32),
                            pltpu.SemaphoreType.DMA((2,)), pltpu.SemaphoreType.DMA(())]),
        out_shape=[jax.ShapeDtypeStruct((t, d), F32), jax.ShapeDtypeStruct((n_rows * ROW_TILE, LANES), U32)],
        compiler_params=_cparams(1),
        name="dispatch",
    )(pends, dest, x2, g, sc, sh, g2, ws1, ws3, ws2)


def _experts_kernel(be_ref, nu_ref, xs_ref, w1_ref, w3_ref, w2_ref, ys_ref, w1b, w3b, w2b):
    s = pl.program_id(0)

    @pl.when((s == 0) | (be_ref[s] != be_ref[jnp.maximum(s - 1, 0)]))
    def _():
        w1b[...] = w1_ref[...].astype(BF16)
        w3b[...] = w3_ref[...].astype(BF16)
        w2b[...] = w2_ref[...].astype(BF16)

    @pl.when(s < nu_ref[0])
    def _():
        lo, hi = _unpack_halves(_tiles_to_rows(xs_ref, (), 0, MOE_BLOCK))
        xb = jnp.concatenate([lo.astype(BF16), hi.astype(BF16)], axis=1)
        hid = (_silu(_dot(xb, w1b[...])) * _dot(xb, w3b[...])).astype(BF16)
        _rows_to_tiles(ys_ref, (), _pack_halves(_dot(hid, w2b[...])))


def _experts(layer, block_e, n_used, xs, w1, w3, w2):
    nb = block_e.shape[0]
    rows = pl.BlockSpec((MOE_BLOCK * ROW_TILE, LANES), lambda s, be, nu: (jnp.minimum(s, nu[0] - 1), 0))
    wspec = lambda a: pl.BlockSpec((None, None) + a.shape[2:], lambda s, be, nu: (layer, be[s], 0, 0))
    return pl.pallas_call(
        _experts_kernel,
        grid_spec=pltpu.PrefetchScalarGridSpec(
            num_scalar_prefetch=2,
            grid=(nb,),
            in_specs=[rows, wspec(w1), wspec(w3), wspec(w2)],
            out_specs=rows,
            scratch_shapes=[pltpu.VMEM(w1.shape[2:], BF16), pltpu.VMEM(w3.shape[2:], BF16),
                            pltpu.VMEM(w2.shape[2:], BF16)]),
        out_shape=jax.ShapeDtypeStruct(xs.shape, U32),
        compiler_params=_cparams(1),
        name="experts",
    )(block_e, n_used, xs, w1, w3, w2)


def _combine_kernel(dst_ref, dstn_ref, base_ref, g2_ref, w_ref, fg_ref, ys_hbm, o_ref, gbuf, sem, *, n_tiles, final):
    i = pl.program_id(0)
    odd = i % 2 == 1
    even = jnp.logical_not(odd)
    tm = base_ref.shape[0]

    rb = 32
    n_rb = tm // rb

    def fetch(idx_ref, sl, start, part=None):
        for k in range(TOP_K):
            for t in range(tm):
                n = k * tm + t
                if part is not None and n * n_rb // (TOP_K * tm) != part:
                    continue
                cp = pltpu.make_async_copy(_tile_rows(ys_hbm, (), idx_ref[k, t] if start else 0),
                                           _tile_rows(gbuf, (sl,), ROW_TILE * n), sem.at[sl])
                cp.start(priority=n % 2) if start else cp.wait()

    @pl.when(i == 0)
    def _():
        fetch(dst_ref, 0, True)

    def main(sl):
        fetch(dst_ref, sl, False)
        half = o_ref.shape[1] // 2

        def rows_step(ib):
            r0 = ib * rb
            rows = pl.ds(r0, rb)
            w = w_ref[rows, :]
            wb = [jnp.broadcast_to(w[:, k:k + 1], (rb, LANES)) for k in range(TOP_K)]
            ssq = jnp.zeros((rb, 1), F32)
            for j in range(ROW_TILE):
                acc_lo = acc_hi = None
                for k in range(TOP_K):
                    lo, hi = _unpack_halves(gbuf[sl, pl.ds((k * tm + r0) * ROW_TILE + j, rb, stride=ROW_TILE), :])
                    acc_lo = wb[k] * lo if acc_lo is None else acc_lo + wb[k] * lo
                    acc_hi = wb[k] * hi if acc_hi is None else acc_hi + wb[k] * hi
                for acc, c0 in ((acc_lo, j * LANES), (acc_hi, half + j * LANES)):
                    x = base_ref[rows, c0:c0 + LANES] + g2_ref[:, c0:c0 + LANES] * acc
                    o_ref[rows, c0:c0 + LANES] = x
                    ssq = ssq + jnp.sum(x * x, axis=-1, keepdims=True)
            if final:
                scale = lax.rsqrt(ssq * (1.0 / o_ref.shape[1]) + EPS)
                o_ref[rows, :] = o_ref[rows, :] * scale * fg_ref[...]

        for ib in range(n_rb):
            rows_step(ib)
            fetch(dstn_ref, 1 - sl, True, part=ib)

    pl.when(even)(functools.partial(main, 0))
    pl.when(odd)(functools.partial(main, 1))

    @pl.when(i == n_tiles - 1)
    def _():
        fetch(dst_ref, n_tiles % 2, False)


def _combine(base, g2, w, ys, dest, fg, seq, final):
    t, d = base.shape
    tm = MOE_TM
    per_b = seq // tm
    n_tiles = t // tm
    idx = lambda f: pl.BlockSpec((TOP_K, tm), f, memory_space=pltpu.SMEM)
    return pl.pallas_call(
        functools.partial(_combine_kernel, n_tiles=n_tiles, final=final),
        grid=(n_tiles,),
        in_specs=[idx(lambda i: (0, i)), idx(lambda i: (0, jnp.minimum(i + 1, n_tiles - 1))),
                  pl.BlockSpec((tm, d), lambda i: (i, 0)),
                  pl.BlockSpec((None, 1, d), lambda i: (i // per_b, 0, 0)),
                  pl.BlockSpec((tm, TOP_K), lambda i: (i, 0)),
                  pl.BlockSpec((1, d), lambda i: (0, 0)),
                  pl.BlockSpec(memory_space=pl.ANY)],
        out_specs=pl.BlockSpec((tm, d), lambda i: (i, 0)),
        out_shape=jax.ShapeDtypeStruct((t, d), F32),
        scratch_shapes=[pltpu.VMEM((2, TOP_K * tm * ROW_TILE, LANES), U32), pltpu.SemaphoreType.DMA((2,))],
        compiler_params=_cparams(1),
        name="combine",
    )(dest, dest, base, g2, w, fg, ys)


def _dispatch_tables(top_e, rank, counts, t):
    n_blocks = TOP_K * t // MOE_BLOCK + N_EXPERTS
    padded = (counts + MOE_BLOCK - 1) // MOE_BLOCK * MOE_BLOCK
    pends = jnp.cumsum(padded)
    pstarts = pends - padded
    experts = jnp.arange(N_EXPERTS, dtype=I32)
    pstart_of = jnp.sum(jnp.where(top_e[..., None] == experts, pstarts, 0), axis=-1)
    dest = (pstart_of + rank) * ROW_TILE
    starts = jnp.arange(n_blocks, dtype=I32) * MOE_BLOCK
    block_e = jnp.sum((pends[None, :] <= starts[:, None]).astype(I32), axis=1)
    block_e = jnp.minimum(block_e, jnp.max(jnp.where(starts < pends[-1], block_e, 0)))
    n_used = (pends[-1:] // MOE_BLOCK).astype(I32)
    return block_e, n_used, pends.astype(I32), dest, n_blocks * MOE_BLOCK


def _layout_inproj(w_in, b_in):
    o = np.cumsum([0, 2 * CONV_DIM, FOX_DIM, FOX_DIM, FOX_DIM, FOX_HEADS, GLA_KDIM, GLA_KDIM, GLA_VDIM, GLA_VDIM,
                   GLA_RANK, 3 * D_MODEL])
    seg = lambda a, i: a[..., int(o[i]):int(o[i + 1])]

    def main(a):
        return jnp.concatenate([seg(a, 10), seg(a, 0), seg(a, 1) * (FOX_HEAD_DIM ** -0.5 * LOG2E), seg(a, 2), seg(a, 3),
                                seg(a, 5) * GLA_DK ** -0.5, seg(a, 6), seg(a, 7), seg(a, 8)], axis=-1)

    def small(a):
        pad = jnp.zeros(a.shape[:-1] + (LANES - FOX_HEADS - GLA_RANK,), a.dtype)
        return jnp.concatenate([seg(a, 4), seg(a, 9), pad], axis=-1)

    return main(w_in).astype(BF16), main(b_in)[None, :], small(w_in).astype(BF16), small(b_in)[None, :]


def kernel(x, c, norm1_g, ada_w, ada_b, w_in, b_in, conv_w, conv_b, conv_ln_g, conv_ln_b, gla_wa, gla_ba,
           gla_norm_g, w_branch_a, w_branch_b, w_branch_c, w_out, norm2_g, w_router, e_bias, w1, w3, w2,
           ws1, ws3, ws2, final_g):
    bsz, seq, d = x.shape
    t = bsz * seq
    depth = ada_w.shape[0]
    mod = _ada_mod(c, ada_w, ada_b).reshape(depth, bsz, 6, 1, d)
    x2 = x.reshape(t, d)
    row = lambda a: a.reshape(1, -1)
    for l in range(depth):
        sh1, sc1, g1, sh2, sc2, g2 = (mod[l, :, i] for i in range(6))
        w, b, ws, bs = _layout_inproj(w_in[l], b_in[l])
        z, zs = _inproj(x2, row(norm1_g[l]), sc1, sh1, w, b, ws, bs, seq)
        wa_pad = jnp.zeros((LANES, GLA_KDIM), F32).at[ZS_A:ZS_A + GLA_RANK].set(gla_wa[l])
        aq, ak, bc = _prep(zs, wa_pad, row(gla_ba[l]), seq)
        ya = _conv(z, conv_w[l].reshape(CONV_WIDTH, CONV_DIM), row(conv_b[l]), row(conv_ln_g[l]),
                   row(conv_ln_b[l]), seq)
        yb = _fox(z, aq, ak, seq)
        yc = _gla(z, bc, row(gla_norm_g[l]), seq)
        x2 = _merge(x2, g1, ya, yb, yc, z, w_branch_a[l].astype(BF16), w_branch_b[l].astype(BF16),
                    w_branch_c[l].astype(BF16), w_out[l].astype(BF16), seq)

        wr_t = w_router[l].T
        wrh = wr_t.astype(BF16)
        wrl = (wr_t - wrh.astype(F32)).astype(BF16)
        top_e, top_w, rank, counts = _route(x2, row(norm2_g[l]), sc2, sh2, wrh, wrl,
                                            e_bias[l].reshape(N_EXPERTS, 1), seq)
        block_e, n_used, pends, dest, n_rows = _dispatch_tables(top_e, rank, counts[:, 0], t)
        base, xs = _dispatch(x2, row(norm2_g[l]), sc2, sh2, g2, ws1[l].astype(BF16), ws3[l].astype(BF16),
                             ws2[l].astype(BF16), pends, dest, n_rows, seq)
        ys = _experts(l, block_e, n_used, xs, w1, w3, w2)
        x2 = _combine(base, g2, top_w.T, ys, dest, row(final_g), seq, final=(l == depth - 1))
    return x2.reshape(bsz, seq, d)
```

```python
import functools

import numpy as np
import jax
import jax.numpy as jnp
from jax import lax
from jax.experimental import pallas as pl
from jax.experimental.pallas import tpu as pltpu

F32 = jnp.float32
BF16 = jnp.bfloat16
U32 = jnp.uint32
I32 = jnp.int32

D_MODEL = 2048
CONV_DIM = 512
CONV_WIDTH = 31
FOX_HEADS = 8
FOX_HEAD_DIM = 128
FOX_DIM = FOX_HEADS * FOX_HEAD_DIM
GLA_HEADS = 4
GLA_DK = 64
GLA_DV = 128
GLA_KDIM = GLA_HEADS * GLA_DK
GLA_VDIM = GLA_HEADS * GLA_DV
GLA_RANK = 16
GLA_TAU = 16.0
GLA_CHUNK = 64
GLA_SUB = 8
N_EXPERTS = 64
N_GROUPS = 8
TOPK_GROUPS = 4
TOP_K = 8
EXPERT_DIM = 512
SHARED_DIM = 512
ROUTED_SCALE = 2.5
MOE_BLOCK = 512
EPS = 1e-6

LANES = 128
V7X_VMEM_LIMIT = 56 * 1024 * 1024

Z_GATE = 0
Z_CONV = 48
Z_QF = 56
Z_KF = 64
Z_VF = 72
Z_QG = 80
Z_KG = 82
Z_VG = 84
Z_RG = 88
Z_COLS = 92 * LANES
ZS_F = 0
ZS_A = 8
NEG = -1e30
LOG2E = 1.4426950408889634


def _cparams(n_axes, vmem=V7X_VMEM_LIMIT, flags=None):
    return pltpu.CompilerParams(dimension_semantics=("arbitrary",) * n_axes, vmem_limit_bytes=vmem, flags=flags)


def _log_sigmoid(x):
    return jnp.minimum(x, 0.0) - jnp.log(1.0 + jnp.exp(-jnp.abs(x)))


def _silu(x):
    return x * jax.nn.sigmoid(x)


def _split3(x):
    hi = x.astype(BF16)
    r1 = x - hi.astype(F32)
    mid = r1.astype(BF16)
    lo = (r1 - mid.astype(F32)).astype(BF16)
    return hi, mid, lo


def _dot(a, b):
    return jnp.dot(a, b, preferred_element_type=F32)


def _dot_nt(a, b):
    return lax.dot_general(a, b, (((1,), (1,)), ((), ())), preferred_element_type=F32)


def _dot_tn(a, b):
    return lax.dot_general(a, b, (((0,), (0,)), ((), ())), preferred_element_type=F32)


def _lmul_exact(l01, x):
    hi, mid, lo = _split3(x)
    return _dot(l01, hi) + _dot(l01, mid) + _dot(l01, lo)


def _dot_f32(a, b):
    a0, a1, a2 = _split3(a)
    b0, b1, b2 = _split3(b)
    return (_dot(a0, b0) + _dot(a0, b1) + _dot(a1, b0)) + (_dot(a0, b2) + _dot(a1, b1) + _dot(a2, b0))


def _ada_kernel(ct_ref, w_ref, b_ref, o_ref):
    ct = ct_ref[...]
    act = _silu(ct)
    w = w_ref[...]
    rows = [jnp.sum(w * act[:, b:b + 1], axis=0, keepdims=True) for b in range(ct.shape[1])]
    o_ref[...] = jnp.concatenate(rows, axis=0) + b_ref[...]


def _ada_mod(c, ada_w, ada_b):
    n_l, d, n = ada_w.shape
    bsz = c.shape[0]
    tn = 1024
    return pl.pallas_call(
        _ada_kernel,
        grid=(n_l, n // tn),
        in_specs=[pl.BlockSpec((d, bsz), lambda l, j: (0, 0)),
                  pl.BlockSpec((None, d, tn), lambda l, j: (l, 0, j)),
                  pl.BlockSpec((None, 1, tn), lambda l, j: (l, 0, j))],
        out_specs=pl.BlockSpec((None, bsz, tn), lambda l, j: (l, 0, j)),
        out_shape=jax.ShapeDtypeStruct((n_l, bsz, n), F32),
        compiler_params=_cparams(2),
        name="ada_mod",
    )(c.T, ada_w, ada_b.reshape(n_l, 1, n))


def _modulated_norm(x, g, sc, sh):
    ms = jnp.mean(x * x, axis=-1, keepdims=True)
    return (x * lax.rsqrt(ms + EPS) * g) * (1.0 + sc) + sh


def _inproj_kernel(x_ref, g_ref, sc_ref, sh_ref, w_ref, b_ref, ws_ref, bs_ref, z_ref, zs_ref, h_scr):
    @pl.when(pl.program_id(1) == 0)
    def _():
        h = _modulated_norm(x_ref[...], g_ref[...], sc_ref[...], sh_ref[...]).astype(BF16)
        h_scr[...] = h
        zs_ref[...] = _dot(h, ws_ref[...]) + bs_ref[...]

    z_ref[...] = (_dot(h_scr[...], w_ref[...]) + b_ref[...]).astype(BF16)


def _inproj(layer, x2, g, sc, sh, w, b, ws, bs, seq):
    t, d = x2.shape
    tm, tn = 512, Z_COLS // 4
    per_b = seq // tm
    vec = lambda: pl.BlockSpec((None, 1, d), lambda i, j: (i // per_b, 0, 0))
    return pl.pallas_call(
        _inproj_kernel,
        grid=(t // tm, Z_COLS // tn),
        in_specs=[pl.BlockSpec((tm, d), lambda i, j: (i, 0)),
                  pl.BlockSpec((1, d), lambda i, j: (0, 0)),
                  vec(), vec(),
                  pl.BlockSpec((None, d, tn), lambda i, j: (layer, 0, j)),
                  pl.BlockSpec((None, 1, tn), lambda i, j: (layer, 0, j)),
                  pl.BlockSpec((None, d, LANES), lambda i, j: (layer, 0, 0)),
                  pl.BlockSpec((None, 1, LANES), lambda i, j: (layer, 0, 0))],
        out_specs=[pl.BlockSpec((tm, tn), lambda i, j: (i, j)),
                   pl.BlockSpec((tm, LANES), lambda i, j: (i, 0))],
        out_shape=[jax.ShapeDtypeStruct((t, Z_COLS), BF16), jax.ShapeDtypeStruct((t, LANES), F32)],
        scratch_shapes=[pltpu.VMEM((tm, d), BF16)],
        compiler_params=_cparams(2),
        name="inproj",
    )(x2, g, sc, sh, w, b, ws, bs)


def _prep_kernel(zs_ref, ltri_ref, lblk_ref, pq_ref, pk_ref, cq_ref, ck_ref, wa_ref, ba_ref,
                 aq_ref, ak_ref, bc_ref, carry_ref):
    @pl.when(pl.program_id(1) == 0)
    def _():
        carry_ref[...] = jnp.zeros_like(carry_ref)

    zs = zs_ref[...]
    cum = _lmul_exact(ltri_ref[...], _log_sigmoid(zs)) + carry_ref[...]
    carry_ref[...] = cum[cum.shape[0] - 1:, :]
    hi, mid, lo = _split3(cum * LOG2E)
    aq = cq_ref[...] + _dot(hi, pq_ref[0]) + _dot(mid, pq_ref[1]) + _dot(lo, pq_ref[2])
    ak = ck_ref[...] + _dot(hi, pk_ref[0]) + _dot(mid, pk_ref[1]) + _dot(lo, pk_ref[2])
    aq_ref[...] = aq.astype(BF16)
    ak_ref[...] = ak.astype(BF16)
    la = _log_sigmoid(_dot_f32(zs, wa_ref[...]) + ba_ref[...]) * (1.0 / GLA_TAU)
    bc_ref[...] = _lmul_exact(lblk_ref[...], la)


def _prep_consts(ts):
    r = np.arange(ts)
    ltri = (r[:, None] >= r[None, :]).astype(np.float32)
    lblk = ltri * (r[:, None] // GLA_CHUNK == r[None, :] // GLA_CHUNK)
    pq = np.zeros((3, LANES, FOX_DIM), np.float32)
    pk = np.zeros((3, LANES, FOX_DIM), np.float32)
    cq = np.zeros((1, FOX_DIM), np.float32)
    ck = np.zeros((1, FOX_DIM), np.float32)
    for h in range(FOX_HEADS):
        for p in range(3):
            pq[p, ZS_F + h, h * FOX_HEAD_DIM + p] = 1.0
            pk[p, ZS_F + h, h * FOX_HEAD_DIM + 3 + p] = -1.0
            cq[0, h * FOX_HEAD_DIM + 3 + p] = 1.0
            ck[0, h * FOX_HEAD_DIM + p] = 1.0
    return (jnp.asarray(ltri, BF16), jnp.asarray(lblk, BF16), jnp.asarray(pq, BF16), jnp.asarray(pk, BF16),
            jnp.asarray(cq), jnp.asarray(ck))


def _prep(zs, wa_pad, ba, seq):
    t = zs.shape[0]
    ts = 256
    per_b = seq // ts
    ltri, lblk, pq, pk, cq, ck = _prep_consts(ts)
    full = lambda a: pl.BlockSpec(a.shape, lambda b, s: (0,) * a.ndim)
    row = lambda n: pl.BlockSpec((ts, n), lambda b, s: (b * per_b + s, 0))
    return pl.pallas_call(
        _prep_kernel,
        grid=(t // seq, per_b),
        in_specs=[row(LANES), full(ltri), full(lblk), full(pq), full(pk), full(cq), full(ck),
                  full(wa_pad), full(ba)],
        out_specs=[row(FOX_DIM), row(FOX_DIM), row(GLA_KDIM)],
        out_shape=[jax.ShapeDtypeStruct((t, FOX_DIM), BF16), jax.ShapeDtypeStruct((t, FOX_DIM), BF16),
                   jax.ShapeDtypeStruct((t, GLA_KDIM), F32)],
        scratch_shapes=[pltpu.VMEM((1, LANES), F32)],
        compiler_params=_cparams(2),
        name="prep",
    )(zs, ltri, lblk, pq, pk, cq, ck, wa_pad, ba)


CONV_HIST = 32


def _conv_kernel(u_ref, w_ref, cb_ref, lg_ref, lb_ref, o_ref, ypad_ref, shift_ref):
    ts = u_ref.shape[0]
    sub = 8

    @pl.when(pl.program_id(1) == 0)
    def _():
        ypad_ref[0:CONV_HIST, :] = jnp.zeros((CONV_HIST, CONV_DIM), F32)

    u = u_ref[...].astype(F32)
    ypad_ref[CONV_HIST:CONV_HIST + ts, :] = u[:, :CONV_DIM] * jax.nn.sigmoid(u[:, CONV_DIM:])
    span = ts + CONV_HIST - sub
    for r in range(1, sub):
        shift_ref[r - 1] = ypad_ref[r:r + span, :]
    acc = jnp.zeros((ts, CONV_DIM), F32) + cb_ref[...]
    first = CONV_HIST - (CONV_WIDTH - 1)
    for j in range(CONV_WIDTH):
        r = (first + j) % sub
        base = first + j - r
        rows = ypad_ref[base:base + ts, :] if r == 0 else shift_ref[r - 1, base:base + ts, :]
        acc = acc + w_ref[j:j + 1, :] * rows
    ypad_ref[0:CONV_HIST, :] = ypad_ref[ts:ts + CONV_HIST, :]
    mu = jnp.mean(acc, axis=-1, keepdims=True)
    cen = acc - mu
    var = jnp.mean(cen * cen, axis=-1, keepdims=True)
    y = cen * lax.rsqrt(var + EPS) * lg_ref[...] + lb_ref[...]
    o_ref[...] = _silu(y).astype(BF16)


def _conv(z, w, cb, lg, lb, seq):
    t = z.shape[0]
    ts = 512
    per_b = seq // ts
    full = lambda a: pl.BlockSpec(a.shape, lambda b, s: (0,) * a.ndim)
    return pl.pallas_call(
        _conv_kernel,
        grid=(t // seq, per_b),
        in_specs=[pl.BlockSpec((ts, 2 * CONV_DIM), lambda b, s: (b * per_b + s, Z_CONV * LANES // (2 * CONV_DIM))),
                  full(w), full(cb), full(lg), full(lb)],
        out_specs=pl.BlockSpec((ts, CONV_DIM), lambda b, s: (b * per_b + s, 0)),
        out_shape=jax.ShapeDtypeStruct((t, CONV_DIM), BF16),
        scratch_shapes=[pltpu.VMEM((ts + CONV_HIST, CONV_DIM), F32),
                        pltpu.VMEM((7, ts + CONV_HIST - 8, CONV_DIM), F32)],
        compiler_params=_cparams(2),
        name="conv",
    )(z, w, cb, lg, lb)


FOX_HEADS_PER_STEP = 2


def _fox_kernel(q_ref, aq_ref, k_ref, ak_ref, v_ref, o_ref, kk_ref, vt_ref, m_ref, acc_ref, *, blk):
    i = pl.program_id(2)
    dh = FOX_HEAD_DIM
    heads = range(FOX_HEADS_PER_STEP)
    col = lambda h: slice(h * dh, (h + 1) * dh)

    @pl.when(i == 0)
    def _():
        def fill(cb, carry):
            rows = pl.ds(pl.multiple_of(cb * blk, blk), blk)
            for h in heads:
                kk_ref[h, rows, 0:dh] = k_ref[rows, col(h)]
                kk_ref[h, rows, dh:2 * dh] = ak_ref[rows, col(h)]
                vt_ref[h, cb, 0:dh, :] = v_ref[rows, col(h)].astype(F32).T.astype(BF16)
                vt_ref[h, cb, dh:2 * dh, :] = jnp.ones((dh, blk), BF16)
            return carry
        lax.fori_loop(0, k_ref.shape[0] // blk, fill, 0)

    qq = [jnp.concatenate([q_ref[:, col(h)], aq_ref[:, col(h)]], axis=1) for h in heads]
    m_ref[...] = jnp.full(m_ref.shape, NEG, F32)
    acc_ref[...] = jnp.zeros(acc_ref.shape, F32)

    def step(j, masked):
        rows = pl.ds(pl.multiple_of(j * blk, blk), blk)
        sts = [_dot_nt(kk_ref[h, rows, :], qq[h]) for h in heads]
        for h in heads:
            st = sts[h]
            if masked:
                kv = lax.broadcasted_iota(I32, (blk, blk), 0)
                qi = lax.broadcasted_iota(I32, (blk, blk), 1)
                st = jnp.where(kv <= qi, st, NEG)
            m_old = m_ref[h]
            m_new = jnp.maximum(m_old, jnp.max(st, axis=0, keepdims=True))
            p = jnp.exp2(st - m_new).astype(BF16)
            acc_ref[h] = acc_ref[h] * jnp.exp2(m_old - m_new) + _dot(vt_ref[h, j], p)
            m_ref[h] = m_new

    def body(j, carry):
        step(j, False)
        return carry

    lax.fori_loop(0, i, body, 0)
    step(i, True)
    for h in heads:
        acc = acc_ref[h]
        o_ref[:, col(h)] = (acc[:dh, :] / acc[dh:, :]).T.astype(BF16)


def _fox(z, aq, ak, seq):
    t = z.shape[0]
    blk = min(1024, seq)
    nq = seq // blk
    hps = FOX_HEADS_PER_STEP
    width = hps * FOX_HEAD_DIM
    unit = lambda base: base * LANES // width
    qspec = lambda base: pl.BlockSpec((blk, width), lambda b, g, i: (b * nq + i, unit(base) + g))
    kspec = lambda base: pl.BlockSpec((seq, width), lambda b, g, i: (b, unit(base) + g),
                                      pipeline_mode=pl.Buffered(1))
    return pl.pallas_call(
        functools.partial(_fox_kernel, blk=blk),
        grid=(t // seq, FOX_HEADS // hps, nq),
        in_specs=[qspec(Z_QF), qspec(0), kspec(Z_KF), kspec(0), kspec(Z_VF)],
        out_specs=qspec(0),
        out_shape=jax.ShapeDtypeStruct((t, FOX_DIM), BF16),
        scratch_shapes=[pltpu.VMEM((hps, seq, 2 * FOX_HEAD_DIM), BF16),
                        pltpu.VMEM((hps, nq, 2 * FOX_HEAD_DIM, blk), BF16),
                        pltpu.VMEM((hps, 1, blk), F32), pltpu.VMEM((hps, 2 * FOX_HEAD_DIM, blk), F32)],
        compiler_params=_cparams(3),
        name="fox",
    )(z, aq, z, ak, z)


def _gla_chunk(q, k, v, b, st, e3):
    c, nsub = GLA_CHUNK, GLA_CHUNK // GLA_SUB
    row = lax.broadcasted_iota(I32, (c, GLA_KDIM), 0)
    rblk, rloc = row // GLA_SUB, row % GLA_SUB
    lane_k = lax.broadcasted_iota(I32, (c, GLA_KDIM), 1)
    khead, kpos = lane_k // GLA_DK, lane_k % GLA_DK
    b3 = b.reshape(nsub, GLA_SUB, GLA_KDIM)
    k3 = k.reshape(nsub, GLA_SUB, GLA_KDIM)
    last = b3[:, GLA_SUB - 1:GLA_SUB, :]
    bcast = lambda a3, i: jnp.broadcast_to(a3[:, i:i + 1, :], (nsub, GLA_SUB, GLA_KDIM)).reshape(c, GLA_KDIM)

    prev = jnp.concatenate([last[:1], last[:nsub - 1]], axis=0)
    refq = jnp.broadcast_to(prev, (nsub, GLA_SUB, GLA_KDIM)).reshape(c, GLA_KDIM)
    qt = q * jnp.exp(jnp.minimum(b - refq, 0.0))
    qcat = jnp.concatenate([jnp.where(rblk == i, qt, 0.0) for i in range(1, nsub)], axis=1).astype(BF16)
    kts = []
    for i in range(1, nsub):
        kt = k * jnp.exp(jnp.minimum(last[i - 1] - b, 0.0))
        kts.append(jnp.where(row < GLA_SUB * i, kt, 0.0))
    kcat = jnp.concatenate(
        [jnp.concatenate([jnp.where(khead == h, kt, 0.0) for kt in kts], axis=1) for h in range(GLA_HEADS)],
        axis=0).astype(BF16)
    a_all = _dot_nt(qcat, kcat)

    for sl in range(GLA_SUB):
        p = q * bcast(k3, sl) * jnp.exp(jnp.minimum(b - bcast(b3, sl), 0.0))
        w = _dot(p.astype(BF16), e3)
        hit = (kpos == rblk * GLA_SUB + sl) & (rloc >= sl)
        a_all = a_all + jnp.where(hit, w, 0.0)

    lane_v = lax.broadcasted_iota(I32, (c, GLA_VDIM), 1) // GLA_DV
    vbd = jnp.concatenate([jnp.where(lane_v == h, v, jnp.zeros_like(v)) for h in range(GLA_HEADS)], axis=0)
    qe = (q * jnp.exp(b)).astype(BF16)
    o = _dot(a_all.astype(BF16), vbd) + _dot_nt(qe, st.astype(BF16))

    blast = b[c - 1:c, :]
    kd = (k * jnp.exp(blast - b)).astype(BF16)
    shead_v = lax.broadcasted_iota(I32, (GLA_VDIM, GLA_KDIM), 0) // GLA_DV
    shead_k = lax.broadcasted_iota(I32, (GLA_VDIM, GLA_KDIM), 1) // GLA_DK
    st_new = jnp.where(shead_v == shead_k, st * jnp.exp(blast) + _dot_tn(v, kd), 0.0)
    return o, st_new


def _gla_kernel(q_ref, k_ref, v_ref, r_ref, b_ref, ng_ref, e3_ref, o_ref, st_ref):
    @pl.when(pl.program_id(1) == 0)
    def _():
        st_ref[...] = jnp.zeros_like(st_ref)

    e3 = e3_ref[...]
    ng = ng_ref[...]

    def body(ci, carry):
        rows = pl.ds(pl.multiple_of(ci * GLA_CHUNK, GLA_CHUNK), GLA_CHUNK)
        o, st_new = _gla_chunk(q_ref[rows, :].astype(F32), k_ref[rows, :].astype(F32), v_ref[rows, :],
                               b_ref[rows, :], st_ref[...], e3)
        st_ref[...] = st_new
        outs = []
        for h in range(GLA_HEADS):
            oh = o[:, h * GLA_DV:(h + 1) * GLA_DV]
            outs.append(oh * lax.rsqrt(jnp.mean(oh * oh, axis=-1, keepdims=True) + EPS))
        y = jnp.concatenate(outs, axis=1) * ng * _silu(r_ref[rows, :].astype(F32))
        o_ref[rows, :] = y.astype(BF16)
        return carry

    lax.fori_loop(0, q_ref.shape[0] // GLA_CHUNK, body, 0)


def _gla(z, bc, ng, seq):
    t = z.shape[0]
    ts = min(512, seq)
    per_b = seq // ts
    lane = np.arange(GLA_KDIM)
    e3 = jnp.asarray((lane[:, None] // GLA_DK == lane[None, :] // GLA_DK).astype(np.float32), BF16)
    spec = lambda n, unit: pl.BlockSpec((ts, n), lambda b, s: (b * per_b + s, unit))
    full = lambda a: pl.BlockSpec(a.shape, lambda b, s: (0,) * a.ndim)
    return pl.pallas_call(
        _gla_kernel,
        grid=(t // seq, per_b),
        in_specs=[spec(GLA_KDIM, Z_QG * LANES // GLA_KDIM), spec(GLA_KDIM, Z_KG * LANES // GLA_KDIM),
                  spec(GLA_VDIM, Z_VG * LANES // GLA_VDIM), spec(GLA_VDIM, Z_RG * LANES // GLA_VDIM),
                  spec(GLA_KDIM, 0), full(ng), full(e3)],
        out_specs=spec(GLA_VDIM, 0),
        out_shape=jax.ShapeDtypeStruct((t, GLA_VDIM), BF16),
        scratch_shapes=[pltpu.VMEM((GLA_VDIM, GLA_KDIM), F32)],
        compiler_params=_cparams(2),
        name="gla",
    )(z, z, z, z, bc, ng, e3)


def _merge_kernel(x_ref, g1_ref, ya_ref, yb_ref, yc_ref, ga_ref, gb_ref, gc_ref,
                  wa_ref, wb_ref, wc_ref, wo_ref, o_ref):
    gate = lambda r: jax.nn.sigmoid(r[...].astype(F32))
    merged = (gate(ga_ref) * _dot(ya_ref[...], wa_ref[...]) + gate(gb_ref) * _dot(yb_ref[...], wb_ref[...])
              + gate(gc_ref) * _dot(yc_ref[...], wc_ref[...]))
    o_ref[...] = x_ref[...] + g1_ref[...] * _dot(merged.astype(BF16), wo_ref[...])


def _merge(x2, g1, ya, yb, yc, z, wa, wb, wc, wo, seq):
    t, d = x2.shape
    tm = 256
    per_b = seq // tm
    full = lambda a: pl.BlockSpec(a.shape, lambda i: (0,) * a.ndim)
    row = lambda n, unit=0: pl.BlockSpec((tm, n), lambda i: (i, unit))
    return pl.pallas_call(
        _merge_kernel,
        grid=(t // tm,),
        in_specs=[row(d), pl.BlockSpec((None, 1, d), lambda i: (i // per_b, 0, 0)),
                  row(CONV_DIM), row(FOX_DIM), row(GLA_VDIM), row(d, 0), row(d, 1), row(d, 2),
                  full(wa), full(wb), full(wc), full(wo)],
        out_specs=row(d),
        out_shape=jax.ShapeDtypeStruct((t, d), F32),
        compiler_params=_cparams(1),
        name="merge",
    )(x2, g1, ya, yb, yc, z, z, z, wa, wb, wc, wo)


def _pack_halves(y):
    n = y.shape[1] // 2
    lo = pltpu.bitcast(y[:, :n].astype(BF16).astype(F32), U32)
    hi = pltpu.bitcast(y[:, n:].astype(BF16).astype(F32), U32)
    return (hi & jnp.uint32(0xFFFF0000)) | (lo >> 16)


def _unpack_halves(p):
    lo = pltpu.bitcast(p << 16, F32)
    hi = pltpu.bitcast(p & jnp.uint32(0xFFFF0000), F32)
    return lo, hi


def _first_index(hit, idx, n):
    return jnp.min(jnp.where(hit, idx, n), axis=0, keepdims=True)


def _route_kernel(x_ref, g_ref, sc_ref, sh_ref, wrh_ref, wrl_ref, eb_ref, u_ref,
                  te_ref, tw_ref, rk_ref, cnt_ref, carry_ref):
    @pl.when(pl.program_id(0) == 0)
    def _():
        carry_ref[...] = jnp.zeros_like(carry_ref)

    x = x_ref[...]
    h = _modulated_norm(x, g_ref[...], sc_ref[...], sh_ref[...])
    hb = h.astype(BF16)

    hlo = (h - hb.astype(F32)).astype(BF16)
    logits = _dot_nt(wrh_ref[...], hb) + _dot_nt(wrh_ref[...], hlo) + _dot_nt(wrl_ref[...], hb)
    scores = jax.nn.sigmoid(logits)
    biased = scores + eb_ref[...]
    tm = x.shape[0]
    per_group = N_EXPERTS // N_GROUPS
    sub = lax.broadcasted_iota(I32, (per_group, tm), 0).astype(F32)
    gscore = []
    for g in range(N_GROUPS):
        blk = biased[g * per_group:(g + 1) * per_group, :]
        m1 = jnp.max(blk, axis=0, keepdims=True)
        rest = jnp.where(sub == _first_index(blk == m1, sub, float(per_group)), -jnp.inf, blk)
        gscore.append(m1 + jnp.max(rest, axis=0, keepdims=True))
    gs = jnp.concatenate(gscore, axis=0)
    gidx = lax.broadcasted_iota(I32, (N_GROUPS, tm), 0).astype(F32)
    gsel = jnp.zeros((N_GROUPS, tm), F32)
    for _ in range(TOPK_GROUPS):
        m = jnp.max(gs, axis=0, keepdims=True)
        hit = gidx == _first_index(gs == m, gidx, float(N_GROUPS))
        gsel = jnp.where(hit, 1.0, gsel)
        gs = jnp.where(hit, -jnp.inf, gs)
    cand = jnp.concatenate(
        [jnp.where(gsel[g:g + 1, :] > 0.5, biased[g * per_group:(g + 1) * per_group, :], -jnp.inf)
         for g in range(N_GROUPS)], axis=0)
    eidx = lax.broadcasted_iota(I32, (N_EXPERTS, tm), 0).astype(F32)
    hits, tops, tws = [], [], []
    sel = jnp.zeros((N_EXPERTS, tm), F32)
    for _ in range(TOP_K):
        m = jnp.max(cand, axis=0, keepdims=True)
        e = _first_index(cand == m, eidx, float(N_EXPERTS))
        hit = eidx == e
        hits.append(hit)
        tops.append(e)
        tws.append(jnp.sum(jnp.where(hit, scores, 0.0), axis=0, keepdims=True))
        cand = jnp.where(hit, -jnp.inf, cand)
        sel = jnp.where(hit, 1.0, sel)
    te_ref[...] = jnp.concatenate(tops, axis=0).astype(I32)
    tw = jnp.concatenate(tws, axis=0)
    tw_ref[...] = tw / jnp.sum(tw, axis=0, keepdims=True) * ROUTED_SCALE

    before = _dot(sel.astype(BF16), u_ref[...]) + carry_ref[...]
    rk_ref[...] = jnp.concatenate(
        [jnp.sum(jnp.where(hit, before, 0.0), axis=0, keepdims=True) for hit in hits], axis=0).astype(I32)
    carry = carry_ref[...] + jnp.sum(sel, axis=1, keepdims=True)
    carry_ref[...] = carry
    cnt_ref[...] = jnp.broadcast_to(carry, cnt_ref.shape).astype(I32)


MOE_TM = 256


def _route(x2, g, sc, sh, wrh, wrl, eb, seq):
    t, d = x2.shape
    tm = MOE_TM
    per_b = seq // tm
    r = np.arange(tm)
    u = jnp.asarray((r[:, None] < r[None, :]).astype(np.float32), BF16)
    full = lambda a: pl.BlockSpec(a.shape, lambda i: (0,) * a.ndim)
    vec = lambda: pl.BlockSpec((None, 1, d), lambda i: (i // per_b, 0, 0))
    colk = lambda: pl.BlockSpec((TOP_K, tm), lambda i: (0, i))
    return pl.pallas_call(
        _route_kernel,
        grid=(t // tm,),
        in_specs=[pl.BlockSpec((tm, d), lambda i: (i, 0)), full(g), vec(), vec(),
                  full(wrh), full(wrl), full(eb), full(u)],
        out_specs=[colk(), colk(), colk(), pl.BlockSpec((N_EXPERTS, LANES), lambda i: (0, 0))],
        out_shape=[jax.ShapeDtypeStruct((TOP_K, t), I32), jax.ShapeDtypeStruct((TOP_K, t), F32),
                   jax.ShapeDtypeStruct((TOP_K, t), I32), jax.ShapeDtypeStruct((N_EXPERTS, LANES), I32)],
        scratch_shapes=[pltpu.VMEM((N_EXPERTS, 1), F32)],
        compiler_params=_cparams(1),
        name="route",
    )(x2, g, sc, sh, wrh, wrl, eb, u)


ROW_TILE = 8


def _tiles_to_rows(ref, lead, first, m):
    return jnp.concatenate(
        [ref[lead + (pl.ds(first * ROW_TILE + j, m, stride=ROW_TILE), slice(None))] for j in range(ROW_TILE)], axis=1)


def _rows_to_tiles(ref, lead, val):
    m = val.shape[0]
    for j in range(ROW_TILE):
        ref[lead + (pl.ds(j, m, stride=ROW_TILE), slice(None))] = val[:, j * LANES:(j + 1) * LANES]


def _tile_rows(ref, lead, row):
    return ref.at[lead + (pl.ds(pl.multiple_of(row, ROW_TILE) if not isinstance(row, int) else row, ROW_TILE),)]


def _dispatch_kernel(pend_ref, dst_ref, x_ref, g_ref, sc_ref, sh_ref, g2_ref, ws1_ref, ws3_ref, ws2_ref,
                     base_ref, xs_hbm, hbuf, zbuf, sem, zsem, *, n_tiles):
    i = pl.program_id(0)
    odd = i % 2 == 1
    even = jnp.logical_not(odd)
    tm = x_ref.shape[0]

    @pl.when(i == 0)
    def _():
        zbuf[...] = jnp.zeros_like(zbuf)
        def clear(e, start):
            first = jnp.maximum(pend_ref[e] - MOE_BLOCK, 0) * ROW_TILE
            cp = pltpu.make_async_copy(zbuf, xs_hbm.at[pl.ds(pl.multiple_of(first, ROW_TILE), zbuf.shape[0])], zsem)
            cp.start() if start else cp.wait()
        for e in range(N_EXPERTS):
            clear(e, True)
        for e in range(N_EXPERTS):
            clear(e, False)

    n_parts = 4

    def copies(sl, start, part=None):
        for k in range(TOP_K):
            for t in range(tm):
                n = k * tm + t
                if part is not None and n * n_parts // (TOP_K * tm) != part:
                    continue
                cp = pltpu.make_async_copy(_tile_rows(hbuf, (sl,), ROW_TILE * t),
                                           _tile_rows(xs_hbm, (), dst_ref[k, t] if start else 0), sem.at[sl])
                cp.start(priority=n % 2) if start else cp.wait()

    pl.when((i >= 2) & even)(functools.partial(copies, 0, False))
    pl.when((i >= 2) & odd)(functools.partial(copies, 1, False))

    def main(sl):
        x = x_ref[...]
        h = _modulated_norm(x, g_ref[...], sc_ref[...], sh_ref[...])
        _rows_to_tiles(hbuf, (sl,), _pack_halves(h))
        copies(sl, True, part=0)
        hb = h.astype(BF16)
        up = _dot(hb, ws1_ref[...])
        copies(sl, True, part=1)
        gate = _dot(hb, ws3_ref[...])
        copies(sl, True, part=2)
        hs = (_silu(up) * gate).astype(BF16)
        base_ref[...] = x + g2_ref[...] * _dot(hs, ws2_ref[...])
        copies(sl, True, part=3)

    pl.when(even)(functools.partial(main, 0))
    pl.when(odd)(functools.partial(main, 1))

    @pl.when(i == n_tiles - 1)
    def _():
        if n_tiles >= 2:
            copies(n_tiles % 2, False)
        copies((n_tiles - 1) % 2, False)


def _dispatch(x2, g, sc, sh, g2, ws1, ws3, ws2, pends, dest, n_rows, seq):
    t, d = x2.shape
    tm = MOE_TM
    per_b = seq // tm
    full = lambda a: pl.BlockSpec(a.shape, lambda i, pe: (0,) * a.ndim)
    vec = lambda: pl.BlockSpec((None, 1, d), lambda i, pe: (i // per_b, 0, 0))
    return pl.pallas_call(
        functools.partial(_dispatch_kernel, n_tiles=t // tm),
        grid_spec=pltpu.PrefetchScalarGridSpec(
            num_scalar_prefetch=1,
            grid=(t // tm,),
            in_specs=[pl.BlockSpec((TOP_K, tm), lambda i, pe: (0, i), memory_space=pltpu.SMEM),
                      pl.BlockSpec((tm, d), lambda i, pe: (i, 0)), full(g), vec(), vec(), vec(),
                      full(ws1), full(ws3), full(ws2)],
            out_specs=[pl.BlockSpec((tm, d), lambda i, pe: (i, 0)), pl.BlockSpec(memory_space=pl.ANY)],
            scratch_shapes=[pltpu.VMEM((2, tm * ROW_TILE, LANES), U32),
                            pltpu.VMEM((MOE_BLOCK * ROW_TILE, LANES), U32),
                            pltpu.SemaphoreType.DMA((2,)), pltpu.SemaphoreType.DMA(())]),
        out_shape=[jax.ShapeDtypeStruct((t, d), F32), jax.ShapeDtypeStruct((n_rows * ROW_TILE, LANES), U32)],
        compiler_params=_cparams(1),
        name="dispatch",
    )(pends, dest, x2, g, sc, sh, g2, ws1, ws3, ws2)


def _experts_kernel(be_ref, nu_ref, xs_ref, w1_ref, w3_ref, w2_ref, ys_ref, w1b, w3b, w2b):
    s = pl.program_id(0)

    @pl.when((s == 0) | (be_ref[s] != be_ref[jnp.maximum(s - 1, 0)]))
    def _():
        w1b[...] = w1_ref[...].astype(BF16)
        w3b[...] = w3_ref[...].astype(BF16)
        w2b[...] = w2_ref[...].astype(BF16)

    @pl.when(s < nu_ref[0])
    def _():
        lo, hi = _unpack_halves(_tiles_to_rows(xs_ref, (), 0, MOE_BLOCK))
        xb = jnp.concatenate([lo.astype(BF16), hi.astype(BF16)], axis=1)
        hid = (_silu(_dot(xb, w1b[...])) * _dot(xb, w3b[...])).astype(BF16)
        _rows_to_tiles(ys_ref, (), _pack_halves(_dot(hid, w2b[...])))


def _experts(layer, block_e, n_used, xs, w1, w3, w2):
    nb = block_e.shape[0]
    rows = pl.BlockSpec((MOE_BLOCK * ROW_TILE, LANES), lambda s, be, nu: (jnp.minimum(s, nu[0] - 1), 0))
    wspec = lambda a: pl.BlockSpec((None, None) + a.shape[2:], lambda s, be, nu: (layer, be[s], 0, 0))
    return pl.pallas_call(
        _experts_kernel,
        grid_spec=pltpu.PrefetchScalarGridSpec(
            num_scalar_prefetch=2,
            grid=(nb,),
            in_specs=[rows, wspec(w1), wspec(w3), wspec(w2)],
            out_specs=rows,
            scratch_shapes=[pltpu.VMEM(w1.shape[2:], BF16), pltpu.VMEM(w3.shape[2:], BF16),
                            pltpu.VMEM(w2.shape[2:], BF16)]),
        out_shape=jax.ShapeDtypeStruct(xs.shape, U32),
        compiler_params=_cparams(1),
        name="experts",
    )(block_e, n_used, xs, w1, w3, w2)


def _combine_kernel(dst_ref, dstn_ref, base_ref, g2_ref, w_ref, fg_ref, ys_hbm, o_ref, gbuf, sem, *, n_tiles, final):
    i = pl.program_id(0)
    odd = i % 2 == 1
    even = jnp.logical_not(odd)
    tm = base_ref.shape[0]

    rb = 32
    n_rb = tm // rb

    def fetch(idx_ref, sl, start, part=None):
        for k in range(TOP_K):
            for t in range(tm):
                n = k * tm + t
                if part is not None and n * n_rb // (TOP_K * tm) != part:
                    continue
                cp = pltpu.make_async_copy(_tile_rows(ys_hbm, (), idx_ref[k, t] if start else 0),
                                           _tile_rows(gbuf, (sl,), ROW_TILE * n), sem.at[sl])
                cp.start(priority=n % 2) if start else cp.wait()

    @pl.when(i == 0)
    def _():
        fetch(dst_ref, 0, True)

    def main(sl):
        fetch(dst_ref, sl, False)
        half = o_ref.shape[1] // 2

        def rows_step(ib):
            r0 = ib * rb
            rows = pl.ds(r0, rb)
            w = w_ref[rows, :]
            wb = [jnp.broadcast_to(w[:, k:k + 1], (rb, LANES)) for k in range(TOP_K)]
            ssq = jnp.zeros((rb, 1), F32)
            for j in range(ROW_TILE):
                acc_lo = acc_hi = None
                for k in range(TOP_K):
                    lo, hi = _unpack_halves(gbuf[sl, pl.ds((k * tm + r0) * ROW_TILE + j, rb, stride=ROW_TILE), :])
                    acc_lo = wb[k] * lo if acc_lo is None else acc_lo + wb[k] * lo
                    acc_hi = wb[k] * hi if acc_hi is None else acc_hi + wb[k] * hi
                for acc, c0 in ((acc_lo, j * LANES), (acc_hi, half + j * LANES)):
                    x = base_ref[rows, c0:c0 + LANES] + g2_ref[:, c0:c0 + LANES] * acc
                    o_ref[rows, c0:c0 + LANES] = x
                    ssq = ssq + jnp.sum(x * x, axis=-1, keepdims=True)
            if final:
                scale = lax.rsqrt(ssq * (1.0 / o_ref.shape[1]) + EPS)
                o_ref[rows, :] = o_ref[rows, :] * scale * fg_ref[...]

        for ib in range(n_rb):
            rows_step(ib)
            fetch(dstn_ref, 1 - sl, True, part=ib)

    pl.when(even)(functools.partial(main, 0))
    pl.when(odd)(functools.partial(main, 1))

    @pl.when(i == n_tiles - 1)
    def _():
        fetch(dst_ref, n_tiles % 2, False)


def _combine(base, g2, w, ys, dest, fg, seq, final):
    t, d = base.shape
    tm = MOE_TM
    per_b = seq // tm
    n_tiles = t // tm
    idx = lambda f: pl.BlockSpec((TOP_K, tm), f, memory_space=pltpu.SMEM)
    return pl.pallas_call(
        functools.partial(_combine_kernel, n_tiles=n_tiles, final=final),
        grid=(n_tiles,),
        in_specs=[idx(lambda i: (0, i)), idx(lambda i: (0, jnp.minimum(i + 1, n_tiles - 1))),
                  pl.BlockSpec((tm, d), lambda i: (i, 0)),
                  pl.BlockSpec((None, 1, d), lambda i: (i // per_b, 0, 0)),
                  pl.BlockSpec((tm, TOP_K), lambda i: (i, 0)),
                  pl.BlockSpec((1, d), lambda i: (0, 0)),
                  pl.BlockSpec(memory_space=pl.ANY)],
        out_specs=pl.BlockSpec((tm, d), lambda i: (i, 0)),
        out_shape=jax.ShapeDtypeStruct((t, d), F32),
        scratch_shapes=[pltpu.VMEM((2, TOP_K * tm * ROW_TILE, LANES), U32), pltpu.SemaphoreType.DMA((2,))],
        compiler_params=_cparams(1),
        name="combine",
    )(dest, dest, base, g2, w, fg, ys)


def _dispatch_tables(top_e, rank, counts, t):
    n_blocks = TOP_K * t // MOE_BLOCK + N_EXPERTS
    padded = (counts + MOE_BLOCK - 1) // MOE_BLOCK * MOE_BLOCK
    pends = jnp.cumsum(padded)
    pstarts = pends - padded
    experts = jnp.arange(N_EXPERTS, dtype=I32)
    pstart_of = jnp.sum(jnp.where(top_e[..., None] == experts, pstarts, 0), axis=-1)
    dest = (pstart_of + rank) * ROW_TILE
    starts = jnp.arange(n_blocks, dtype=I32) * MOE_BLOCK
    block_e = jnp.sum((pends[None, :] <= starts[:, None]).astype(I32), axis=1)
    block_e = jnp.minimum(block_e, jnp.max(jnp.where(starts < pends[-1], block_e, 0)))
    n_used = (pends[-1:] // MOE_BLOCK).astype(I32)
    return block_e, n_used, pends.astype(I32), dest, n_blocks * MOE_BLOCK


def _layout_inproj(w_in, b_in):
    o = np.cumsum([0, 2 * CONV_DIM, FOX_DIM, FOX_DIM, FOX_DIM, FOX_HEADS, GLA_KDIM, GLA_KDIM, GLA_VDIM, GLA_VDIM,
                   GLA_RANK, 3 * D_MODEL])
    seg = lambda a, i: a[..., int(o[i]):int(o[i + 1])]

    def main(a):
        return jnp.concatenate([seg(a, 10), seg(a, 0), seg(a, 1) * (FOX_HEAD_DIM ** -0.5 * LOG2E), seg(a, 2), seg(a, 3),
                                seg(a, 5) * GLA_DK ** -0.5, seg(a, 6), seg(a, 7), seg(a, 8)], axis=-1)

    def small(a):
        pad = jnp.zeros(a.shape[:-1] + (LANES - FOX_HEADS - GLA_RANK,), a.dtype)
        return jnp.concatenate([seg(a, 4), seg(a, 9), pad], axis=-1)

    return (main(w_in).astype(BF16), main(b_in)[..., None, :], small(w_in).astype(BF16),
            small(b_in)[..., None, :])


def kernel(x, c, norm1_g, ada_w, ada_b, w_in, b_in, conv_w, conv_b, conv_ln_g, conv_ln_b, gla_wa, gla_ba,
           gla_norm_g, w_branch_a, w_branch_b, w_branch_c, w_out, norm2_g, w_router, e_bias, w1, w3, w2,
           ws1, ws3, ws2, final_g):
    bsz, seq, d = x.shape
    t = bsz * seq
    depth = ada_w.shape[0]
    mod = _ada_mod(c, ada_w, ada_b).reshape(depth, bsz, 6, 1, d)
    x2 = x.reshape(t, d)
    row = lambda a: a.reshape(1, -1)
    w, b, ws, bs = _layout_inproj(w_in, b_in)
    for l in range(depth):
        sh1, sc1, g1, sh2, sc2, g2 = (mod[l, :, i] for i in range(6))
        z, zs = _inproj(l, x2, row(norm1_g[l]), sc1, sh1, w, b, ws, bs, seq)
        wa_pad = jnp.zeros((LANES, GLA_KDIM), F32).at[ZS_A:ZS_A + GLA_RANK].set(gla_wa[l])
        aq, ak, bc = _prep(zs, wa_pad, row(gla_ba[l]), seq)
        ya = _conv(z, conv_w[l].reshape(CONV_WIDTH, CONV_DIM), row(conv_b[l]), row(conv_ln_g[l]),
                   row(conv_ln_b[l]), seq)
        yb = _fox(z, aq, ak, seq)
        yc = _gla(z, bc, row(gla_norm_g[l]), seq)
        x2 = _merge(x2, g1, ya, yb, yc, z, w_branch_a[l].astype(BF16), w_branch_b[l].astype(BF16),
                    w_branch_c[l].astype(BF16), w_out[l].astype(BF16), seq)

        wr_t = w_router[l].T
        wrh = wr_t.astype(BF16)
        wrl = (wr_t - wrh.astype(F32)).astype(BF16)
        top_e, top_w, rank, counts = _route(x2, row(norm2_g[l]), sc2, sh2, wrh, wrl,
                                            e_bias[l].reshape(N_EXPERTS, 1), seq)
        block_e, n_used, pends, dest, n_rows = _dispatch_tables(top_e, rank, counts[:, 0], t)
        base, xs = _dispatch(x2, row(norm2_g[l]), sc2, sh2, g2, ws1[l].astype(BF16), ws3[l].astype(BF16),
                             ws2[l].astype(BF16), pends, dest, n_rows, seq)
        ys = _experts(l, block_e, n_used, xs, w1, w3, w2)
        x2 = _combine(base, g2, top_w.T, ys, dest, row(final_g), seq, final=(l == depth - 1))
    return x2.reshape(bsz, seq, d)
```

```python
import functools

import numpy as np
import jax
import jax.numpy as jnp
from jax import lax
from jax.experimental import pallas as pl
from jax.experimental.pallas import tpu as pltpu

F32 = jnp.float32
BF16 = jnp.bfloat16
U32 = jnp.uint32
I32 = jnp.int32

D_MODEL = 2048
CONV_DIM = 512
CONV_WIDTH = 31
FOX_HEADS = 8
FOX_HEAD_DIM = 128
FOX_DIM = FOX_HEADS * FOX_HEAD_DIM
GLA_HEADS = 4
GLA_DK = 64
GLA_DV = 128
GLA_KDIM = GLA_HEADS * GLA_DK
GLA_VDIM = GLA_HEADS * GLA_DV
GLA_RANK = 16
GLA_TAU = 16.0
GLA_CHUNK = 64
GLA_SUB = 8
N_EXPERTS = 64
N_GROUPS = 8
TOPK_GROUPS = 4
TOP_K = 8
EXPERT_DIM = 512
SHARED_DIM = 512
ROUTED_SCALE = 2.5
MOE_BLOCK = 512
EPS = 1e-6

LANES = 128
V7X_VMEM_LIMIT = 56 * 1024 * 1024

Z_GATE = 0
Z_CONV = 48
Z_QF = 56
Z_KF = 64
Z_VF = 72
Z_QG = 80
Z_KG = 82
Z_VG = 84
Z_RG = 88
Z_COLS = 92 * LANES
ZS_F = 0
ZS_A = 8
NEG = -1e30
LOG2E = 1.4426950408889634


def _cparams(n_axes, vmem=V7X_VMEM_LIMIT, flags=None):
    return pltpu.CompilerParams(dimension_semantics=("arbitrary",) * n_axes, vmem_limit_bytes=vmem, flags=flags)


def _log_sigmoid(x):
    return jnp.minimum(x, 0.0) - jnp.log(1.0 + jnp.exp(-jnp.abs(x)))


def _silu(x):
    return x * jax.nn.sigmoid(x)


def _split3(x):
    hi = x.astype(BF16)
    r1 = x - hi.astype(F32)
    mid = r1.astype(BF16)
    lo = (r1 - mid.astype(F32)).astype(BF16)
    return hi, mid, lo


def _dot(a, b):
    return jnp.dot(a, b, preferred_element_type=F32)


def _dot_nt(a, b):
    return lax.dot_general(a, b, (((1,), (1,)), ((), ())), preferred_element_type=F32)


def _dot_tn(a, b):
    return lax.dot_general(a, b, (((0,), (0,)), ((), ())), preferred_element_type=F32)


def _lmul_exact(l01, x):
    hi, mid, lo = _split3(x)
    return _dot(l01, hi) + _dot(l01, mid) + _dot(l01, lo)


def _dot_f32(a, b):
    a0, a1, a2 = _split3(a)
    b0, b1, b2 = _split3(b)
    return (_dot(a0, b0) + _dot(a0, b1) + _dot(a1, b0)) + (_dot(a0, b2) + _dot(a1, b1) + _dot(a2, b0))


def _ada_kernel(ct_ref, w_ref, b_ref, o_ref):
    ct = ct_ref[...]
    act = _silu(ct)
    w = w_ref[...]
    rows = [jnp.sum(w * act[:, b:b + 1], axis=0, keepdims=True) for b in range(ct.shape[1])]
    o_ref[...] = jnp.concatenate(rows, axis=0) + b_ref[...]


def _ada_mod(c, ada_w, ada_b):
    n_l, d, n = ada_w.shape
    bsz = c.shape[0]
    tn = 1024
    return pl.pallas_call(
        _ada_kernel,
        grid=(n_l, n // tn),
        in_specs=[pl.BlockSpec((d, bsz), lambda l, j: (0, 0)),
                  pl.BlockSpec((None, d, tn), lambda l, j: (l, 0, j)),
                  pl.BlockSpec((None, 1, tn), lambda l, j: (l, 0, j))],
        out_specs=pl.BlockSpec((None, bsz, tn), lambda l, j: (l, 0, j)),
        out_shape=jax.ShapeDtypeStruct((n_l, bsz, n), F32),
        compiler_params=_cparams(2),
        name="ada_mod",
    )(c.T, ada_w, ada_b.reshape(n_l, 1, n))


def _modulated_norm(x, g, sc, sh):
    ms = jnp.mean(x * x, axis=-1, keepdims=True)
    return (x * lax.rsqrt(ms + EPS) * g) * (1.0 + sc) + sh


def _inproj_kernel(x_ref, g_ref, sc_ref, sh_ref, w_ref, b_ref, ws_ref, bs_ref, z_ref, zs_ref, h_scr):
    @pl.when(pl.program_id(1) == 0)
    def _():
        h = _modulated_norm(x_ref[...], g_ref[...], sc_ref[...], sh_ref[...]).astype(BF16)
        h_scr[...] = h
        zs_ref[...] = _dot(h, ws_ref[...]) + bs_ref[...]

    z_ref[...] = (_dot(h_scr[...], w_ref[...]) + b_ref[...]).astype(BF16)


def _inproj(layer, x2, g, sc, sh, w, b, ws, bs, seq):
    t, d = x2.shape
    tm, tn = 512, Z_COLS // 4
    per_b = seq // tm
    vec = lambda: pl.BlockSpec((None, 1, d), lambda i, j: (i // per_b, 0, 0))
    return pl.pallas_call(
        _inproj_kernel,
        grid=(t // tm, Z_COLS // tn),
        in_specs=[pl.BlockSpec((tm, d), lambda i, j: (i, 0)),
                  pl.BlockSpec((1, d), lambda i, j: (0, 0)),
                  vec(), vec(),
                  pl.BlockSpec((None, d, tn), lambda i, j: (layer, 0, j)),
                  pl.BlockSpec((None, 1, tn), lambda i, j: (layer, 0, j)),
                  pl.BlockSpec((None, d, LANES), lambda i, j: (layer, 0, 0)),
                  pl.BlockSpec((None, 1, LANES), lambda i, j: (layer, 0, 0))],
        out_specs=[pl.BlockSpec((tm, tn), lambda i, j: (i, j)),
                   pl.BlockSpec((tm, LANES), lambda i, j: (i, 0))],
        out_shape=[jax.ShapeDtypeStruct((t, Z_COLS), BF16), jax.ShapeDtypeStruct((t, LANES), F32)],
        scratch_shapes=[pltpu.VMEM((tm, d), BF16)],
        compiler_params=_cparams(2),
        name="inproj",
    )(x2, g, sc, sh, w, b, ws, bs)


def _prep_kernel(zs_ref, ltri_ref, lblk_ref, pq_ref, pk_ref, cq_ref, ck_ref, wa_ref, ba_ref,
                 aq_ref, ak_ref, bc_ref, carry_ref):
    @pl.when(pl.program_id(1) == 0)
    def _():
        carry_ref[...] = jnp.zeros_like(carry_ref)

    zs = zs_ref[...]
    cum = _lmul_exact(ltri_ref[...], _log_sigmoid(zs)) + carry_ref[...]
    carry_ref[...] = cum[cum.shape[0] - 1:, :]
    hi, mid, lo = _split3(cum * LOG2E)
    aq = cq_ref[...] + _dot(hi, pq_ref[0]) + _dot(mid, pq_ref[1]) + _dot(lo, pq_ref[2])
    ak = ck_ref[...] + _dot(hi, pk_ref[0]) + _dot(mid, pk_ref[1]) + _dot(lo, pk_ref[2])
    aq_ref[...] = aq.astype(BF16)
    ak_ref[...] = ak.astype(BF16)
    la = _log_sigmoid(_dot_f32(zs, wa_ref[...]) + ba_ref[...]) * (1.0 / GLA_TAU)
    bc_ref[...] = _lmul_exact(lblk_ref[...], la)


def _prep_consts(ts):
    r = np.arange(ts)
    ltri = (r[:, None] >= r[None, :]).astype(np.float32)
    lblk = ltri * (r[:, None] // GLA_CHUNK == r[None, :] // GLA_CHUNK)
    pq = np.zeros((3, LANES, FOX_DIM), np.float32)
    pk = np.zeros((3, LANES, FOX_DIM), np.float32)
    cq = np.zeros((1, FOX_DIM), np.float32)
    ck = np.zeros((1, FOX_DIM), np.float32)
    for h in range(FOX_HEADS):
        for p in range(3):
            pq[p, ZS_F + h, h * FOX_HEAD_DIM + p] = 1.0
            pk[p, ZS_F + h, h * FOX_HEAD_DIM + 3 + p] = -1.0
            cq[0, h * FOX_HEAD_DIM + 3 + p] = 1.0
            ck[0, h * FOX_HEAD_DIM + p] = 1.0
    return (jnp.asarray(ltri, BF16), jnp.asarray(lblk, BF16), jnp.asarray(pq, BF16), jnp.asarray(pk, BF16),
            jnp.asarray(cq), jnp.asarray(ck))


def _prep(zs, wa_pad, ba, seq):
    t = zs.shape[0]
    ts = 256
    per_b = seq // ts
    ltri, lblk, pq, pk, cq, ck = _prep_consts(ts)
    full = lambda a: pl.BlockSpec(a.shape, lambda b, s: (0,) * a.ndim)
    row = lambda n: pl.BlockSpec((ts, n), lambda b, s: (b * per_b + s, 0))
    return pl.pallas_call(
        _prep_kernel,
        grid=(t // seq, per_b),
        in_specs=[row(LANES), full(ltri), full(lblk), full(pq), full(pk), full(cq), full(ck),
                  full(wa_pad), full(ba)],
        out_specs=[row(FOX_DIM), row(FOX_DIM), row(GLA_KDIM)],
        out_shape=[jax.ShapeDtypeStruct((t, FOX_DIM), BF16), jax.ShapeDtypeStruct((t, FOX_DIM), BF16),
                   jax.ShapeDtypeStruct((t, GLA_KDIM), F32)],
        scratch_shapes=[pltpu.VMEM((1, LANES), F32)],
        compiler_params=_cparams(2),
        name="prep",
    )(zs, ltri, lblk, pq, pk, cq, ck, wa_pad, ba)


CONV_HIST = 32


def _conv_kernel(u_ref, w_ref, cb_ref, lg_ref, lb_ref, o_ref, ypad_ref, shift_ref):
    ts = u_ref.shape[0]
    sub = 8

    @pl.when(pl.program_id(1) == 0)
    def _():
        ypad_ref[0:CONV_HIST, :] = jnp.zeros((CONV_HIST, CONV_DIM), F32)

    u = u_ref[...].astype(F32)
    ypad_ref[CONV_HIST:CONV_HIST + ts, :] = u[:, :CONV_DIM] * jax.nn.sigmoid(u[:, CONV_DIM:])
    span = ts + CONV_HIST - sub
    for r in range(1, sub):
        shift_ref[r - 1] = ypad_ref[r:r + span, :]
    acc = jnp.zeros((ts, CONV_DIM), F32) + cb_ref[...]
    first = CONV_HIST - (CONV_WIDTH - 1)
    for j in range(CONV_WIDTH):
        r = (first + j) % sub
        base = first + j - r
        rows = ypad_ref[base:base + ts, :] if r == 0 else shift_ref[r - 1, base:base + ts, :]
        acc = acc + w_ref[j:j + 1, :] * rows
    ypad_ref[0:CONV_HIST, :] = ypad_ref[ts:ts + CONV_HIST, :]
    mu = jnp.mean(acc, axis=-1, keepdims=True)
    cen = acc - mu
    var = jnp.mean(cen * cen, axis=-1, keepdims=True)
    y = cen * lax.rsqrt(var + EPS) * lg_ref[...] + lb_ref[...]
    o_ref[...] = _silu(y).astype(BF16)


def _conv(z, w, cb, lg, lb, seq):
    t = z.shape[0]
    ts = 512
    per_b = seq // ts
    full = lambda a: pl.BlockSpec(a.shape, lambda b, s: (0,) * a.ndim)
    return pl.pallas_call(
        _conv_kernel,
        grid=(t // seq, per_b),
        in_specs=[pl.BlockSpec((ts, 2 * CONV_DIM), lambda b, s: (b * per_b + s, Z_CONV * LANES // (2 * CONV_DIM))),
                  full(w), full(cb), full(lg), full(lb)],
        out_specs=pl.BlockSpec((ts, CONV_DIM), lambda b, s: (b * per_b + s, 0)),
        out_shape=jax.ShapeDtypeStruct((t, CONV_DIM), BF16),
        scratch_shapes=[pltpu.VMEM((ts + CONV_HIST, CONV_DIM), F32),
                        pltpu.VMEM((7, ts + CONV_HIST - 8, CONV_DIM), F32)],
        compiler_params=_cparams(2),
        name="conv",
    )(z, w, cb, lg, lb)


FOX_HEADS_PER_STEP = 2


def _fox_kernel(q_ref, aq_ref, k_ref, ak_ref, v_ref, o_ref, kk_ref, vt_ref, m_ref, acc_ref, *, blk):
    i = pl.program_id(2)
    dh = FOX_HEAD_DIM
    heads = range(FOX_HEADS_PER_STEP)
    col = lambda h: slice(h * dh, (h + 1) * dh)

    @pl.when(i == 0)
    def _():
        def fill(cb, carry):
            rows = pl.ds(pl.multiple_of(cb * blk, blk), blk)
            for h in heads:
                kk_ref[h, rows, 0:dh] = k_ref[rows, col(h)]
                kk_ref[h, rows, dh:2 * dh] = ak_ref[rows, col(h)]
                vt_ref[h, cb, 0:dh, :] = v_ref[rows, col(h)].astype(F32).T.astype(BF16)
                vt_ref[h, cb, dh:2 * dh, :] = jnp.ones((dh, blk), BF16)
            return carry
        lax.fori_loop(0, k_ref.shape[0] // blk, fill, 0)

    qq = [jnp.concatenate([q_ref[:, col(h)], aq_ref[:, col(h)]], axis=1) for h in heads]
    m_ref[...] = jnp.full(m_ref.shape, NEG, F32)
    acc_ref[...] = jnp.zeros(acc_ref.shape, F32)

    def step(j, masked):
        off = pl.multiple_of(j * blk, blk)
        parts = [(0, blk, blk)] if not masked else [(0, blk // 2, blk // 2), (blk // 2, blk // 2, blk)]
        chains = [(h, q0, nq, nk) for h in heads for q0, nq, nk in parts]
        sts = [_dot_nt(kk_ref[h, pl.ds(off, nk), :], qq[h][q0:q0 + nq, :]) for h, q0, nq, nk in chains]
        for (h, q0, nq, nk), st in zip(chains, sts):
            if masked:
                kv = lax.broadcasted_iota(I32, (nk, nq), 0)
                qi = lax.broadcasted_iota(I32, (nk, nq), 1) + q0
                st = jnp.where(kv <= qi, st, NEG)
            m_old = m_ref[h, :, q0:q0 + nq]
            m_new = jnp.maximum(m_old, jnp.max(st, axis=0, keepdims=True))
            p = jnp.exp2(st - m_new).astype(BF16)
            acc_ref[h, :, q0:q0 + nq] = (acc_ref[h, :, q0:q0 + nq] * jnp.exp2(m_old - m_new)
                                          + _dot(vt_ref[h, j, :, 0:nk], p))
            m_ref[h, :, q0:q0 + nq] = m_new

    def body(j, carry):
        step(j, False)
        return carry

    lax.fori_loop(0, i, body, 0)
    step(i, True)
    for h in heads:
        acc = acc_ref[h]
        o_ref[:, col(h)] = (acc[:dh, :] / acc[dh:, :]).T.astype(BF16)


def _fox(z, aq, ak, seq):
    t = z.shape[0]
    blk = min(1024, seq)
    nq = seq // blk
    hps = FOX_HEADS_PER_STEP
    width = hps * FOX_HEAD_DIM
    unit = lambda base: base * LANES // width
    qspec = lambda base: pl.BlockSpec((blk, width), lambda b, g, i: (b * nq + i, unit(base) + g))
    kspec = lambda base: pl.BlockSpec((seq, width), lambda b, g, i: (b, unit(base) + g),
                                      pipeline_mode=pl.Buffered(1))
    return pl.pallas_call(
        functools.partial(_fox_kernel, blk=blk),
        grid=(t // seq, FOX_HEADS // hps, nq),
        in_specs=[qspec(Z_QF), qspec(0), kspec(Z_KF), kspec(0), kspec(Z_VF)],
        out_specs=qspec(0),
        out_shape=jax.ShapeDtypeStruct((t, FOX_DIM), BF16),
        scratch_shapes=[pltpu.VMEM((hps, seq, 2 * FOX_HEAD_DIM), BF16),
                        pltpu.VMEM((hps, nq, 2 * FOX_HEAD_DIM, blk), BF16),
                        pltpu.VMEM((hps, 1, blk), F32), pltpu.VMEM((hps, 2 * FOX_HEAD_DIM, blk), F32)],
        compiler_params=_cparams(3),
        name="fox",
    )(z, aq, z, ak, z)


def _gla_chunk(q, k, v, b, st, e3):
    c, nsub = GLA_CHUNK, GLA_CHUNK // GLA_SUB
    row = lax.broadcasted_iota(I32, (c, GLA_KDIM), 0)
    rblk, rloc = row // GLA_SUB, row % GLA_SUB
    lane_k = lax.broadcasted_iota(I32, (c, GLA_KDIM), 1)
    khead, kpos = lane_k // GLA_DK, lane_k % GLA_DK
    b3 = b.reshape(nsub, GLA_SUB, GLA_KDIM)
    k3 = k.reshape(nsub, GLA_SUB, GLA_KDIM)
    last = b3[:, GLA_SUB - 1:GLA_SUB, :]
    bcast = lambda a3, i: jnp.broadcast_to(a3[:, i:i + 1, :], (nsub, GLA_SUB, GLA_KDIM)).reshape(c, GLA_KDIM)

    prev = jnp.concatenate([last[:1], last[:nsub - 1]], axis=0)
    refq = jnp.broadcast_to(prev, (nsub, GLA_SUB, GLA_KDIM)).reshape(c, GLA_KDIM)
    qt = q * jnp.exp(jnp.minimum(b - refq, 0.0))
    qcat = jnp.concatenate([jnp.where(rblk == i, qt, 0.0) for i in range(1, nsub)], axis=1).astype(BF16)
    kts = []
    for i in range(1, nsub):
        kt = k * jnp.exp(jnp.minimum(last[i - 1] - b, 0.0))
        kts.append(jnp.where(row < GLA_SUB * i, kt, 0.0))
    kcat = jnp.concatenate(
        [jnp.concatenate([jnp.where(khead == h, kt, 0.0) for kt in kts], axis=1) for h in range(GLA_HEADS)],
        axis=0).astype(BF16)
    a_all = _dot_nt(qcat, kcat)

    for sl in range(GLA_SUB):
        p = q * bcast(k3, sl) * jnp.exp(jnp.minimum(b - bcast(b3, sl), 0.0))
        w = _dot(p.astype(BF16), e3)
        hit = (kpos == rblk * GLA_SUB + sl) & (rloc >= sl)
        a_all = a_all + jnp.where(hit, w, 0.0)

    lane_v = lax.broadcasted_iota(I32, (c, GLA_VDIM), 1) // GLA_DV
    vbd = jnp.concatenate([jnp.where(lane_v == h, v, jnp.zeros_like(v)) for h in range(GLA_HEADS)], axis=0)
    qe = (q * jnp.exp(b)).astype(BF16)
    o = _dot(a_all.astype(BF16), vbd) + _dot_nt(qe, st.astype(BF16))

    blast = b[c - 1:c, :]
    kd = (k * jnp.exp(blast - b)).astype(BF16)
    shead_v = lax.broadcasted_iota(I32, (GLA_VDIM, GLA_KDIM), 0) // GLA_DV
    shead_k = lax.broadcasted_iota(I32, (GLA_VDIM, GLA_KDIM), 1) // GLA_DK
    st_new = jnp.where(shead_v == shead_k, st * jnp.exp(blast) + _dot_tn(v, kd), 0.0)
    return o, st_new


def _gla_kernel(q_ref, k_ref, v_ref, r_ref, b_ref, ng_ref, e3_ref, o_ref, st_ref):
    @pl.when(pl.program_id(1) == 0)
    def _():
        st_ref[...] = jnp.zeros_like(st_ref)

    e3 = e3_ref[...]
    ng = ng_ref[...]

    def body(ci, carry):
        rows = pl.ds(pl.multiple_of(ci * GLA_CHUNK, GLA_CHUNK), GLA_CHUNK)
        o, st_new = _gla_chunk(q_ref[rows, :].astype(F32), k_ref[rows, :].astype(F32), v_ref[rows, :],
                               b_ref[rows, :], st_ref[...], e3)
        st_ref[...] = st_new
        outs = []
        for h in range(GLA_HEADS):
            oh = o[:, h * GLA_DV:(h + 1) * GLA_DV]
            outs.append(oh * lax.rsqrt(jnp.mean(oh * oh, axis=-1, keepdims=True) + EPS))
        y = jnp.concatenate(outs, axis=1) * ng * _silu(r_ref[rows, :].astype(F32))
        o_ref[rows, :] = y.astype(BF16)
        return carry

    lax.fori_loop(0, q_ref.shape[0] // GLA_CHUNK, body, 0)


def _gla(z, bc, ng, seq):
    t = z.shape[0]
    ts = min(512, seq)
    per_b = seq // ts
    lane = np.arange(GLA_KDIM)
    e3 = jnp.asarray((lane[:, None] // GLA_DK == lane[None, :] // GLA_DK).astype(np.float32), BF16)
    spec = lambda n, unit: pl.BlockSpec((ts, n), lambda b, s: (b * per_b + s, unit))
    full = lambda a: pl.BlockSpec(a.shape, lambda b, s: (0,) * a.ndim)
    return pl.pallas_call(
        _gla_kernel,
        grid=(t // seq, per_b),
        in_specs=[spec(GLA_KDIM, Z_QG * LANES // GLA_KDIM), spec(GLA_KDIM, Z_KG * LANES // GLA_KDIM),
                  spec(GLA_VDIM, Z_VG * LANES // GLA_VDIM), spec(GLA_VDIM, Z_RG * LANES // GLA_VDIM),
                  spec(GLA_KDIM, 0), full(ng), full(e3)],
        out_specs=spec(GLA_VDIM, 0),
        out_shape=jax.ShapeDtypeStruct((t, GLA_VDIM), BF16),
        scratch_shapes=[pltpu.VMEM((GLA_VDIM, GLA_KDIM), F32)],
        compiler_params=_cparams(2),
        name="gla",
    )(z, z, z, z, bc, ng, e3)


def _merge_kernel(x_ref, g1_ref, ya_ref, yb_ref, yc_ref, ga_ref, gb_ref, gc_ref,
                  wa_ref, wb_ref, wc_ref, wo_ref, o_ref):
    gate = lambda r: jax.nn.sigmoid(r[...].astype(F32))
    merged = (gate(ga_ref) * _dot(ya_ref[...], wa_ref[...]) + gate(gb_ref) * _dot(yb_ref[...], wb_ref[...])
              + gate(gc_ref) * _dot(yc_ref[...], wc_ref[...]))
    o_ref[...] = x_ref[...] + g1_ref[...] * _dot(merged.astype(BF16), wo_ref[...])


def _merge(x2, g1, ya, yb, yc, z, wa, wb, wc, wo, seq):
    t, d = x2.shape
    tm = 256
    per_b = seq // tm
    full = lambda a: pl.BlockSpec(a.shape, lambda i: (0,) * a.ndim)
    row = lambda n, unit=0: pl.BlockSpec((tm, n), lambda i: (i, unit))
    return pl.pallas_call(
        _merge_kernel,
        grid=(t // tm,),
        in_specs=[row(d), pl.BlockSpec((None, 1, d), lambda i: (i // per_b, 0, 0)),
                  row(CONV_DIM), row(FOX_DIM), row(GLA_VDIM), row(d, 0), row(d, 1), row(d, 2),
                  full(wa), full(wb), full(wc), full(wo)],
        out_specs=row(d),
        out_shape=jax.ShapeDtypeStruct((t, d), F32),
        compiler_params=_cparams(1),
        name="merge",
    )(x2, g1, ya, yb, yc, z, z, z, wa, wb, wc, wo)


def _pack_halves(y):
    n = y.shape[1] // 2
    lo = pltpu.bitcast(y[:, :n].astype(BF16).astype(F32), U32)
    hi = pltpu.bitcast(y[:, n:].astype(BF16).astype(F32), U32)
    return (hi & jnp.uint32(0xFFFF0000)) | (lo >> 16)


def _unpack_halves(p):
    lo = pltpu.bitcast(p << 16, F32)
    hi = pltpu.bitcast(p & jnp.uint32(0xFFFF0000), F32)
    return lo, hi


def _first_index(hit, idx, n):
    return jnp.min(jnp.where(hit, idx, n), axis=0, keepdims=True)


def _route_kernel(x_ref, g_ref, sc_ref, sh_ref, wrh_ref, wrl_ref, eb_ref, u_ref,
                  te_ref, tw_ref, rk_ref, cnt_ref, carry_ref):
    @pl.when(pl.program_id(0) == 0)
    def _():
        carry_ref[...] = jnp.zeros_like(carry_ref)

    x = x_ref[...]
    h = _modulated_norm(x, g_ref[...], sc_ref[...], sh_ref[...])
    hb = h.astype(BF16)

    hlo = (h - hb.astype(F32)).astype(BF16)
    logits = _dot_nt(wrh_ref[...], hb) + _dot_nt(wrh_ref[...], hlo) + _dot_nt(wrl_ref[...], hb)
    scores = jax.nn.sigmoid(logits)
    biased = scores + eb_ref[...]
    tm = x.shape[0]
    per_group = N_EXPERTS // N_GROUPS
    sub = lax.broadcasted_iota(I32, (per_group, tm), 0).astype(F32)
    gscore = []
    for g in range(N_GROUPS):
        blk = biased[g * per_group:(g + 1) * per_group, :]
        m1 = jnp.max(blk, axis=0, keepdims=True)
        rest = jnp.where(sub == _first_index(blk == m1, sub, float(per_group)), -jnp.inf, blk)
        gscore.append(m1 + jnp.max(rest, axis=0, keepdims=True))
    gs = jnp.concatenate(gscore, axis=0)
    gidx = lax.broadcasted_iota(I32, (N_GROUPS, tm), 0).astype(F32)
    gsel = jnp.zeros((N_GROUPS, tm), F32)
    for _ in range(TOPK_GROUPS):
        m = jnp.max(gs, axis=0, keepdims=True)
        hit = gidx == _first_index(gs == m, gidx, float(N_GROUPS))
        gsel = jnp.where(hit, 1.0, gsel)
        gs = jnp.where(hit, -jnp.inf, gs)
    cand = jnp.concatenate(
        [jnp.where(gsel[g:g + 1, :] > 0.5, biased[g * per_group:(g + 1) * per_group, :], -jnp.inf)
         for g in range(N_GROUPS)], axis=0)
    eidx = lax.broadcasted_iota(I32, (N_EXPERTS, tm), 0).astype(F32)
    hits, tops, tws = [], [], []
    sel = jnp.zeros((N_EXPERTS, tm), F32)
    for _ in range(TOP_K):
        m = jnp.max(cand, axis=0, keepdims=True)
        e = _first_index(cand == m, eidx, float(N_EXPERTS))
        hit = eidx == e
        hits.append(hit)
        tops.append(e)
        tws.append(jnp.sum(jnp.where(hit, scores, 0.0), axis=0, keepdims=True))
        cand = jnp.where(hit, -jnp.inf, cand)
        sel = jnp.where(hit, 1.0, sel)
    te_ref[...] = jnp.concatenate(tops, axis=0).astype(I32)
    tw = jnp.concatenate(tws, axis=0)
    tw_ref[...] = tw / jnp.sum(tw, axis=0, keepdims=True) * ROUTED_SCALE

    before = _dot(sel.astype(BF16), u_ref[...]) + carry_ref[...]
    rk_ref[...] = jnp.concatenate(
        [jnp.sum(jnp.where(hit, before, 0.0), axis=0, keepdims=True) for hit in hits], axis=0).astype(I32)
    carry = carry_ref[...] + jnp.sum(sel, axis=1, keepdims=True)
    carry_ref[...] = carry
    cnt_ref[...] = jnp.broadcast_to(carry, cnt_ref.shape).astype(I32)


MOE_TM = 256


def _route(x2, g, sc, sh, wrh, wrl, eb, seq):
    t, d = x2.shape
    tm = MOE_TM
    per_b = seq // tm
    r = np.arange(tm)
    u = jnp.asarray((r[:, None] < r[None, :]).astype(np.float32), BF16)
    full = lambda a: pl.BlockSpec(a.shape, lambda i: (0,) * a.ndim)
    vec = lambda: pl.BlockSpec((None, 1, d), lambda i: (i // per_b, 0, 0))
    colk = lambda: pl.BlockSpec((TOP_K, tm), lambda i: (0, i))
    return pl.pallas_call(
        _route_kernel,
        grid=(t // tm,),
        in_specs=[pl.BlockSpec((tm, d), lambda i: (i, 0)), full(g), vec(), vec(),
                  full(wrh), full(wrl), full(eb), full(u)],
        out_specs=[colk(), colk(), colk(), pl.BlockSpec((N_EXPERTS, LANES), lambda i: (0, 0))],
        out_shape=[jax.ShapeDtypeStruct((TOP_K, t), I32), jax.ShapeDtypeStruct((TOP_K, t), F32),
                   jax.ShapeDtypeStruct((TOP_K, t), I32), jax.ShapeDtypeStruct((N_EXPERTS, LANES), I32)],
        scratch_shapes=[pltpu.VMEM((N_EXPERTS, 1), F32)],
        compiler_params=_cparams(1),
        name="route",
    )(x2, g, sc, sh, wrh, wrl, eb, u)


ROW_TILE = 8


def _tiles_to_rows(ref, lead, first, m):
    return jnp.concatenate(
        [ref[lead + (pl.ds(first * ROW_TILE + j, m, stride=ROW_TILE), slice(None))] for j in range(ROW_TILE)], axis=1)


def _rows_to_tiles(ref, lead, val):
    m = val.shape[0]
    for j in range(ROW_TILE):
        ref[lead + (pl.ds(j, m, stride=ROW_TILE), slice(None))] = val[:, j * LANES:(j + 1) * LANES]


def _tile_rows(ref, lead, row):
    return ref.at[lead + (pl.ds(pl.multiple_of(row, ROW_TILE) if not isinstance(row, int) else row, ROW_TILE),)]


def _dispatch_kernel(pend_ref, dst_ref, x_ref, g_ref, sc_ref, sh_ref, g2_ref, ws1_ref, ws3_ref, ws2_ref,
                     base_ref, xs_hbm, hbuf, zbuf, sem, zsem, *, n_tiles):
    i = pl.program_id(0)
    odd = i % 2 == 1
    even = jnp.logical_not(odd)
    tm = x_ref.shape[0]

    @pl.when(i == 0)
    def _():
        zbuf[...] = jnp.zeros_like(zbuf)
        def clear(e, start):
            first = jnp.maximum(pend_ref[e] - MOE_BLOCK, 0) * ROW_TILE
            cp = pltpu.make_async_copy(zbuf, xs_hbm.at[pl.ds(pl.multiple_of(first, ROW_TILE), zbuf.shape[0])], zsem)
            cp.start() if start else cp.wait()
        for e in range(N_EXPERTS):
            clear(e, True)
        for e in range(N_EXPERTS):
            clear(e, False)

    n_parts = 4

    def copies(sl, start, part=None):
        for k in range(TOP_K):
            for t in range(tm):
                n = k * tm + t
                if part is not None and n * n_parts // (TOP_K * tm) != part:
                    continue
                cp = pltpu.make_async_copy(_tile_rows(hbuf, (sl,), ROW_TILE * t),
                                           _tile_rows(xs_hbm, (), dst_ref[k, t] if start else 0), sem.at[sl])
                cp.start(priority=n % 2) if start else cp.wait()

    pl.when((i >= 2) & even)(functools.partial(copies, 0, False))
    pl.when((i >= 2) & odd)(functools.partial(copies, 1, False))

    def main(sl):
        x = x_ref[...]
        h = _modulated_norm(x, g_ref[...], sc_ref[...], sh_ref[...])
        _rows_to_tiles(hbuf, (sl,), _pack_halves(h))
        copies(sl, True, part=0)
        hb = h.astype(BF16)
        up = _dot(hb, ws1_ref[...])
        copies(sl, True, part=1)
        gate = _dot(hb, ws3_ref[...])
        copies(sl, True, part=2)
        hs = (_silu(up) * gate).astype(BF16)
        base_ref[...] = x + g2_ref[...] * _dot(hs, ws2_ref[...])
        copies(sl, True, part=3)

    pl.when(even)(functools.partial(main, 0))
    pl.when(odd)(functools.partial(main, 1))

    @pl.when(i == n_tiles - 1)
    def _():
        if n_tiles >= 2:
            copies(n_tiles % 2, False)
        copies((n_tiles - 1) % 2, False)


def _dispatch(x2, g, sc, sh, g2, ws1, ws3, ws2, pends, dest, n_rows, seq):
    t, d = x2.shape
    tm = MOE_TM
    per_b = seq // tm
    full = lambda a: pl.BlockSpec(a.shape, lambda i, pe: (0,) * a.ndim)
    vec = lambda: pl.BlockSpec((None, 1, d), lambda i, pe: (i // per_b, 0, 0))
    return pl.pallas_call(
        functools.partial(_dispatch_kernel, n_tiles=t // tm),
        grid_spec=pltpu.PrefetchScalarGridSpec(
            num_scalar_prefetch=1,
            grid=(t // tm,),
            in_specs=[pl.BlockSpec((TOP_K, tm), lambda i, pe: (0, i), memory_space=pltpu.SMEM),
                      pl.BlockSpec((tm, d), lambda i, pe: (i, 0)), full(g), vec(), vec(), vec(),
                      full(ws1), full(ws3), full(ws2)],
            out_specs=[pl.BlockSpec((tm, d), lambda i, pe: (i, 0)), pl.BlockSpec(memory_space=pl.ANY)],
            scratch_shapes=[pltpu.VMEM((2, tm * ROW_TILE, LANES), U32),
                            pltpu.VMEM((MOE_BLOCK * ROW_TILE, LANES), U32),
                            pltpu.SemaphoreType.DMA((2,)), pltpu.SemaphoreType.DMA(())]),
        out_shape=[jax.ShapeDtypeStruct((t, d), F32), jax.ShapeDtypeStruct((n_rows * ROW_TILE, LANES), U32)],
        compiler_params=_cparams(1),
        name="dispatch",
    )(pends, dest, x2, g, sc, sh, g2, ws1, ws3, ws2)


def _experts_kernel(be_ref, nu_ref, xs_ref, w1_ref, w3_ref, w2_ref, ys_ref, w1b, w3b, w2b):
    s = pl.program_id(0)

    @pl.when((s == 0) | (be_ref[s] != be_ref[jnp.maximum(s - 1, 0)]))
    def _():
        w1b[...] = w1_ref[...].astype(BF16)
        w3b[...] = w3_ref[...].astype(BF16)
        w2b[...] = w2_ref[...].astype(BF16)

    @pl.when(s < nu_ref[0])
    def _():
        lo, hi = _unpack_halves(_tiles_to_rows(xs_ref, (), 0, MOE_BLOCK))
        xb = jnp.concatenate([lo.astype(BF16), hi.astype(BF16)], axis=1)
        hid = (_silu(_dot(xb, w1b[...])) * _dot(xb, w3b[...])).astype(BF16)
        _rows_to_tiles(ys_ref, (), _pack_halves(_dot(hid, w2b[...])))


def _experts(layer, block_e, n_used, xs, w1, w3, w2):
    nb = block_e.shape[0]
    rows = pl.BlockSpec((MOE_BLOCK * ROW_TILE, LANES), lambda s, be, nu: (jnp.minimum(s, nu[0] - 1), 0))
    wspec = lambda a: pl.BlockSpec((None, None) + a.shape[2:], lambda s, be, nu: (layer, be[s], 0, 0))
    return pl.pallas_call(
        _experts_kernel,
        grid_spec=pltpu.PrefetchScalarGridSpec(
            num_scalar_prefetch=2,
            grid=(nb,),
            in_specs=[rows, wspec(w1), wspec(w3), wspec(w2)],
            out_specs=rows,
            scratch_shapes=[pltpu.VMEM(w1.shape[2:], BF16), pltpu.VMEM(w3.shape[2:], BF16),
                            pltpu.VMEM(w2.shape[2:], BF16)]),
        out_shape=jax.ShapeDtypeStruct(xs.shape, U32),
        compiler_params=_cparams(1),
        name="experts",
    )(block_e, n_used, xs, w1, w3, w2)


def _combine_kernel(dst_ref, dstn_ref, base_ref, g2_ref, w_ref, fg_ref, ys_hbm, o_ref, gbuf, sem, *, n_tiles, final):
    i = pl.program_id(0)
    odd = i % 2 == 1
    even = jnp.logical_not(odd)
    tm = base_ref.shape[0]

    rb = 32
    n_rb = tm // rb

    def fetch(idx_ref, sl, start, part=None):
        for k in range(TOP_K):
            for t in range(tm):
                n = k * tm + t
                if part is not None and n * n_rb // (TOP_K * tm) != part:
                    continue
                cp = pltpu.make_async_copy(_tile_rows(ys_hbm, (), idx_ref[k, t] if start else 0),
                                           _tile_rows(gbuf, (sl,), ROW_TILE * n), sem.at[sl])
                cp.start(priority=n % 2) if start else cp.wait()

    @pl.when(i == 0)
    def _():
        fetch(dst_ref, 0, True)

    def main(sl):
        fetch(dst_ref, sl, False)
        half = o_ref.shape[1] // 2

        def rows_step(ib):
            r0 = ib * rb
            rows = pl.ds(r0, rb)
            w = w_ref[rows, :]
            wb = [jnp.broadcast_to(w[:, k:k + 1], (rb, LANES)) for k in range(TOP_K)]
            ssq = jnp.zeros((rb, 1), F32)
            for j in range(ROW_TILE):
                acc_lo = acc_hi = None
                for k in range(TOP_K):
                    lo, hi = _unpack_halves(gbuf[sl, pl.ds((k * tm + r0) * ROW_TILE + j, rb, stride=ROW_TILE), :])
                    acc_lo = wb[k] * lo if acc_lo is None else acc_lo + wb[k] * lo
                    acc_hi = wb[k] * hi if acc_hi is None else acc_hi + wb[k] * hi
                for acc, c0 in ((acc_lo, j * LANES), (acc_hi, half + j * LANES)):
                    x = base_ref[rows, c0:c0 + LANES] + g2_ref[:, c0:c0 + LANES] * acc
                    o_ref[rows, c0:c0 + LANES] = x
                    ssq = ssq + jnp.sum(x * x, axis=-1, keepdims=True)
            if final:
                scale = lax.rsqrt(ssq * (1.0 / o_ref.shape[1]) + EPS)
                o_ref[rows, :] = o_ref[rows, :] * scale * fg_ref[...]

        for ib in range(n_rb):
            rows_step(ib)
            fetch(dstn_ref, 1 - sl, True, part=ib)

    pl.when(even)(functools.partial(main, 0))
    pl.when(odd)(functools.partial(main, 1))

    @pl.when(i == n_tiles - 1)
    def _():
        fetch(dst_ref, n_tiles % 2, False)


def _combine(base, g2, w, ys, dest, fg, seq, final):
    t, d = base.shape
    tm = MOE_TM
    per_b = seq // tm
    n_tiles = t // tm
    idx = lambda f: pl.BlockSpec((TOP_K, tm), f, memory_space=pltpu.SMEM)
    return pl.pallas_call(
        functools.partial(_combine_kernel, n_tiles=n_tiles, final=final),
        grid=(n_tiles,),
        in_specs=[idx(lambda i: (0, i)), idx(lambda i: (0, jnp.minimum(i + 1, n_tiles - 1))),
                  pl.BlockSpec((tm, d), lambda i: (i, 0)),
                  pl.BlockSpec((None, 1, d), lambda i: (i // per_b, 0, 0)),
                  pl.BlockSpec((tm, TOP_K), lambda i: (i, 0)),
                  pl.BlockSpec((1, d), lambda i: (0, 0)),
                  pl.BlockSpec(memory_space=pl.ANY)],
        out_specs=pl.BlockSpec((tm, d), lambda i: (i, 0)),
        out_shape=jax.ShapeDtypeStruct((t, d), F32),
        scratch_shapes=[pltpu.VMEM((2, TOP_K * tm * ROW_TILE, LANES), U32), pltpu.SemaphoreType.DMA((2,))],
        compiler_params=_cparams(1),
        name="combine",
    )(dest, dest, base, g2, w, fg, ys)


def _dispatch_tables(top_e, rank, counts, t):
    n_blocks = TOP_K * t // MOE_BLOCK + N_EXPERTS
    padded = (counts + MOE_BLOCK - 1) // MOE_BLOCK * MOE_BLOCK
    pends = jnp.cumsum(padded)
    pstarts = pends - padded
    experts = jnp.arange(N_EXPERTS, dtype=I32)
    pstart_of = jnp.sum(jnp.where(top_e[..., None] == experts, pstarts, 0), axis=-1)
    dest = (pstart_of + rank) * ROW_TILE
    starts = jnp.arange(n_blocks, dtype=I32) * MOE_BLOCK
    block_e = jnp.sum((pends[None, :] <= starts[:, None]).astype(I32), axis=1)
    block_e = jnp.minimum(block_e, jnp.max(jnp.where(starts < pends[-1], block_e, 0)))
    n_used = (pends[-1:] // MOE_BLOCK).astype(I32)
    return block_e, n_used, pends.astype(I32), dest, n_blocks * MOE_BLOCK


def _layout_inproj(w_in, b_in):
    o = np.cumsum([0, 2 * CONV_DIM, FOX_DIM, FOX_DIM, FOX_DIM, FOX_HEADS, GLA_KDIM, GLA_KDIM, GLA_VDIM, GLA_VDIM,
                   GLA_RANK, 3 * D_MODEL])
    seg = lambda a, i: a[..., int(o[i]):int(o[i + 1])]

    def main(a):
        return jnp.concatenate([seg(a, 10), seg(a, 0), seg(a, 1) * (FOX_HEAD_DIM ** -0.5 * LOG2E), seg(a, 2), seg(a, 3),
                                seg(a, 5) * GLA_DK ** -0.5, seg(a, 6), seg(a, 7), seg(a, 8)], axis=-1)

    def small(a):
        pad = jnp.zeros(a.shape[:-1] + (LANES - FOX_HEADS - GLA_RANK,), a.dtype)
        return jnp.concatenate([seg(a, 4), seg(a, 9), pad], axis=-1)

    return (main(w_in).astype(BF16), main(b_in)[..., None, :], small(w_in).astype(BF16),
            small(b_in)[..., None, :])


def kernel(x, c, norm1_g, ada_w, ada_b, w_in, b_in, conv_w, conv_b, conv_ln_g, conv_ln_b, gla_wa, gla_ba,
           gla_norm_g, w_branch_a, w_branch_b, w_branch_c, w_out, norm2_g, w_router, e_bias, w1, w3, w2,
           ws1, ws3, ws2, final_g):
    bsz, seq, d = x.shape
    t = bsz * seq
    depth = ada_w.shape[0]
    mod = _ada_mod(c, ada_w, ada_b).reshape(depth, bsz, 6, 1, d)
    x2 = x.reshape(t, d)
    row = lambda a: a.reshape(1, -1)
    w, b, ws, bs = _layout_inproj(w_in, b_in)
    for l in range(depth):
        sh1, sc1, g1, sh2, sc2, g2 = (mod[l, :, i] for i in range(6))
        z, zs = _inproj(l, x2, row(norm1_g[l]), sc1, sh1, w, b, ws, bs, seq)
        wa_pad = jnp.zeros((LANES, GLA_KDIM), F32).at[ZS_A:ZS_A + GLA_RANK].set(gla_wa[l])
        aq, ak, bc = _prep(zs, wa_pad, row(gla_ba[l]), seq)
        ya = _conv(z, conv_w[l].reshape(CONV_WIDTH, CONV_DIM), row(conv_b[l]), row(conv_ln_g[l]),
                   row(conv_ln_b[l]), seq)
        yb = _fox(z, aq, ak, seq)
        yc = _gla(z, bc, row(gla_norm_g[l]), seq)
        x2 = _merge(x2, g1, ya, yb, yc, z, w_branch_a[l].astype(BF16), w_branch_b[l].astype(BF16),
                    w_branch_c[l].astype(BF16), w_out[l].astype(BF16), seq)

        wr_t = w_router[l].T
        wrh = wr_t.astype(BF16)
        wrl = (wr_t - wrh.astype(F32)).astype(BF16)
        top_e, top_w, rank, counts = _route(x2, row(norm2_g[l]), sc2, sh2, wrh, wrl,
                                            e_bias[l].reshape(N_EXPERTS, 1), seq)
        block_e, n_used, pends, dest, n_rows = _dispatch_tables(top_e, rank, counts[:, 0], t)
        base, xs = _dispatch(x2, row(norm2_g[l]), sc2, sh2, g2, ws1[l].astype(BF16), ws3[l].astype(BF16),
                             ws2[l].astype(BF16), pends, dest, n_rows, seq)
        ys = _experts(l, block_e, n_used, xs, w1, w3, w2)
        x2 = _combine(base, g2, top_w.T, ys, dest, row(final_g), seq, final=(l == depth - 1))
    return x2.reshape(bsz, seq, d)
```

```python
import functools

import numpy as np
import jax
import jax.numpy as jnp
from jax import lax
from jax.experimental import pallas as pl
from jax.experimental.pallas import tpu as pltpu

F32 = jnp.float32
BF16 = jnp.bfloat16
U32 = jnp.uint32
I32 = jnp.int32

D_MODEL = 2048
CONV_DIM = 512
CONV_WIDTH = 31
FOX_HEADS = 8
FOX_HEAD_DIM = 128
FOX_DIM = FOX_HEADS * FOX_HEAD_DIM
GLA_HEADS = 4
GLA_DK = 64
GLA_DV = 128
GLA_KDIM = GLA_HEADS * GLA_DK
GLA_VDIM = GLA_HEADS * GLA_DV
GLA_RANK = 16
GLA_TAU = 16.0
GLA_CHUNK = 64
GLA_SUB = 8
N_EXPERTS = 64
N_GROUPS = 8
TOPK_GROUPS = 4
TOP_K = 8
EXPERT_DIM = 512
SHARED_DIM = 512
ROUTED_SCALE = 2.5
MOE_BLOCK = 512
EPS = 1e-6

LANES = 128
V7X_VMEM_LIMIT = 56 * 1024 * 1024

Z_GATE = 0
Z_CONV = 48
Z_QF = 56
Z_KF = 64
Z_VF = 72
Z_QG = 80
Z_KG = 82
Z_VG = 84
Z_RG = 88
Z_COLS = 92 * LANES
ZS_F = 0
ZS_A = 8
NEG = -1e30
LOG2E = 1.4426950408889634


def _cparams(n_axes, vmem=V7X_VMEM_LIMIT, flags=None):
    return pltpu.CompilerParams(dimension_semantics=("arbitrary",) * n_axes, vmem_limit_bytes=vmem, flags=flags)


def _log_sigmoid(x):
    return jnp.minimum(x, 0.0) - jnp.log(1.0 + jnp.exp(-jnp.abs(x)))


def _silu(x):
    return x * jax.nn.sigmoid(x)


def _split3(x):
    hi = x.astype(BF16)
    r1 = x - hi.astype(F32)
    mid = r1.astype(BF16)
    lo = (r1 - mid.astype(F32)).astype(BF16)
    return hi, mid, lo


def _dot(a, b):
    return jnp.dot(a, b, preferred_element_type=F32)


def _dot_nt(a, b):
    return lax.dot_general(a, b, (((1,), (1,)), ((), ())), preferred_element_type=F32)


def _dot_tn(a, b):
    return lax.dot_general(a, b, (((0,), (0,)), ((), ())), preferred_element_type=F32)


def _lmul_exact(l01, x):
    hi, mid, lo = _split3(x)
    return _dot(l01, hi) + _dot(l01, mid) + _dot(l01, lo)


def _dot_f32(a, b):
    a0, a1, a2 = _split3(a)
    b0, b1, b2 = _split3(b)
    return (_dot(a0, b0) + _dot(a0, b1) + _dot(a1, b0)) + (_dot(a0, b2) + _dot(a1, b1) + _dot(a2, b0))


def _ada_kernel(ct_ref, w_ref, b_ref, o_ref):
    ct = ct_ref[...]
    act = _silu(ct)
    w = w_ref[...]
    rows = [jnp.sum(w * act[:, b:b + 1], axis=0, keepdims=True) for b in range(ct.shape[1])]
    o_ref[...] = jnp.concatenate(rows, axis=0) + b_ref[...]


def _ada_mod(c, ada_w, ada_b):
    n_l, d, n = ada_w.shape
    bsz = c.shape[0]
    tn = 1024
    return pl.pallas_call(
        _ada_kernel,
        grid=(n_l, n // tn),
        in_specs=[pl.BlockSpec((d, bsz), lambda l, j: (0, 0)),
                  pl.BlockSpec((None, d, tn), lambda l, j: (l, 0, j)),
                  pl.BlockSpec((None, 1, tn), lambda l, j: (l, 0, j))],
        out_specs=pl.BlockSpec((None, bsz, tn), lambda l, j: (l, 0, j)),
        out_shape=jax.ShapeDtypeStruct((n_l, bsz, n), F32),
        compiler_params=_cparams(2),
        name="ada_mod",
    )(c.T, ada_w, ada_b.reshape(n_l, 1, n))


def _modulated_norm(x, g, sc, sh):
    ms = jnp.mean(x * x, axis=-1, keepdims=True)
    return (x * lax.rsqrt(ms + EPS) * g) * (1.0 + sc) + sh


def _inproj_kernel(x_ref, g_ref, sc_ref, sh_ref, w_ref, b_ref, ws_ref, bs_ref, z_ref, zs_ref, h_scr):
    @pl.when(pl.program_id(1) == 0)
    def _():
        h = _modulated_norm(x_ref[...], g_ref[...], sc_ref[...], sh_ref[...]).astype(BF16)
        h_scr[...] = h
        zs_ref[...] = _dot(h, ws_ref[...]) + bs_ref[...]

    z_ref[...] = (_dot(h_scr[...], w_ref[...]) + b_ref[...]).astype(BF16)


def _inproj(layer, x2, g, sc, sh, w, b, ws, bs, seq):
    t, d = x2.shape
    tm, tn = 512, Z_COLS // 4
    per_b = seq // tm
    vec = lambda: pl.BlockSpec((None, 1, d), lambda i, j: (i // per_b, 0, 0))
    return pl.pallas_call(
        _inproj_kernel,
        grid=(t // tm, Z_COLS // tn),
        in_specs=[pl.BlockSpec((tm, d), lambda i, j: (i, 0)),
                  pl.BlockSpec((1, d), lambda i, j: (0, 0)),
                  vec(), vec(),
                  pl.BlockSpec((None, d, tn), lambda i, j: (layer, 0, j)),
                  pl.BlockSpec((None, 1, tn), lambda i, j: (layer, 0, j)),
                  pl.BlockSpec((None, d, LANES), lambda i, j: (layer, 0, 0)),
                  pl.BlockSpec((None, 1, LANES), lambda i, j: (layer, 0, 0))],
        out_specs=[pl.BlockSpec((tm, tn), lambda i, j: (i, j)),
                   pl.BlockSpec((tm, LANES), lambda i, j: (i, 0))],
        out_shape=[jax.ShapeDtypeStruct((t, Z_COLS), BF16), jax.ShapeDtypeStruct((t, LANES), F32)],
        scratch_shapes=[pltpu.VMEM((tm, d), BF16)],
        compiler_params=_cparams(2),
        name="inproj",
    )(x2, g, sc, sh, w, b, ws, bs)


def _prep_kernel(zs_ref, ltri_ref, lblk_ref, pq_ref, pk_ref, cq_ref, ck_ref, wa_ref, ba_ref,
                 aq_ref, ak_ref, bc_ref, carry_ref):
    @pl.when(pl.program_id(1) == 0)
    def _():
        carry_ref[...] = jnp.zeros_like(carry_ref)

    zs = zs_ref[...]
    cum = _lmul_exact(ltri_ref[...], _log_sigmoid(zs)) + carry_ref[...]
    carry_ref[...] = cum[cum.shape[0] - 1:, :]
    hi, mid, lo = _split3(cum * LOG2E)
    aq = cq_ref[...] + _dot(hi, pq_ref[0]) + _dot(mid, pq_ref[1]) + _dot(lo, pq_ref[2])
    ak = ck_ref[...] + _dot(hi, pk_ref[0]) + _dot(mid, pk_ref[1]) + _dot(lo, pk_ref[2])
    aq_ref[...] = aq.astype(BF16)
    ak_ref[...] = ak.astype(BF16)
    la = _log_sigmoid(_dot_f32(zs, wa_ref[...]) + ba_ref[...]) * (1.0 / GLA_TAU)
    bc_ref[...] = _lmul_exact(lblk_ref[...], la)


def _prep_consts(ts):
    r = np.arange(ts)
    ltri = (r[:, None] >= r[None, :]).astype(np.float32)
    lblk = ltri * (r[:, None] // GLA_CHUNK == r[None, :] // GLA_CHUNK)
    pq = np.zeros((3, LANES, FOX_DIM), np.float32)
    pk = np.zeros((3, LANES, FOX_DIM), np.float32)
    cq = np.zeros((1, FOX_DIM), np.float32)
    ck = np.zeros((1, FOX_DIM), np.float32)
    for h in range(FOX_HEADS):
        for p in range(3):
            pq[p, ZS_F + h, h * FOX_HEAD_DIM + p] = 1.0
            pk[p, ZS_F + h, h * FOX_HEAD_DIM + 3 + p] = -1.0
            cq[0, h * FOX_HEAD_DIM + 3 + p] = 1.0
            ck[0, h * FOX_HEAD_DIM + p] = 1.0
    return (jnp.asarray(ltri, BF16), jnp.asarray(lblk, BF16), jnp.asarray(pq, BF16), jnp.asarray(pk, BF16),
            jnp.asarray(cq), jnp.asarray(ck))


def _prep(zs, wa_pad, ba, seq):
    t = zs.shape[0]
    ts = 256
    per_b = seq // ts
    ltri, lblk, pq, pk, cq, ck = _prep_consts(ts)
    full = lambda a: pl.BlockSpec(a.shape, lambda b, s: (0,) * a.ndim)
    row = lambda n: pl.BlockSpec((ts, n), lambda b, s: (b * per_b + s, 0))
    return pl.pallas_call(
        _prep_kernel,
        grid=(t // seq, per_b),
        in_specs=[row(LANES), full(ltri), full(lblk), full(pq), full(pk), full(cq), full(ck),
                  full(wa_pad), full(ba)],
        out_specs=[row(FOX_DIM), row(FOX_DIM), row(GLA_KDIM)],
        out_shape=[jax.ShapeDtypeStruct((t, FOX_DIM), BF16), jax.ShapeDtypeStruct((t, FOX_DIM), BF16),
                   jax.ShapeDtypeStruct((t, GLA_KDIM), F32)],
        scratch_shapes=[pltpu.VMEM((1, LANES), F32)],
        compiler_params=_cparams(2),
        name="prep",
    )(zs, ltri, lblk, pq, pk, cq, ck, wa_pad, ba)


CONV_HIST = 32


def _conv_kernel(u_ref, w_ref, cb_ref, lg_ref, lb_ref, o_ref, ypad_ref, shift_ref):
    ts = u_ref.shape[0]
    sub = 8

    @pl.when(pl.program_id(1) == 0)
    def _():
        ypad_ref[0:CONV_HIST, :] = jnp.zeros((CONV_HIST, CONV_DIM), F32)

    u = u_ref[...].astype(F32)
    ypad_ref[CONV_HIST:CONV_HIST + ts, :] = u[:, :CONV_DIM] * jax.nn.sigmoid(u[:, CONV_DIM:])
    span = ts + CONV_HIST - sub
    for r in range(1, sub):
        shift_ref[r - 1] = ypad_ref[r:r + span, :]
    acc = jnp.zeros((ts, CONV_DIM), F32) + cb_ref[...]
    first = CONV_HIST - (CONV_WIDTH - 1)
    for j in range(CONV_WIDTH):
        r = (first + j) % sub
        base = first + j - r
        rows = ypad_ref[base:base + ts, :] if r == 0 else shift_ref[r - 1, base:base + ts, :]
        acc = acc + w_ref[j:j + 1, :] * rows
    ypad_ref[0:CONV_HIST, :] = ypad_ref[ts:ts + CONV_HIST, :]
    mu = jnp.mean(acc, axis=-1, keepdims=True)
    cen = acc - mu
    var = jnp.mean(cen * cen, axis=-1, keepdims=True)
    y = cen * lax.rsqrt(var + EPS) * lg_ref[...] + lb_ref[...]
    o_ref[...] = _silu(y).astype(BF16)


def _conv(z, w, cb, lg, lb, seq):
    t = z.shape[0]
    ts = 512
    per_b = seq // ts
    full = lambda a: pl.BlockSpec(a.shape, lambda b, s: (0,) * a.ndim)
    return pl.pallas_call(
        _conv_kernel,
        grid=(t // seq, per_b),
        in_specs=[pl.BlockSpec((ts, 2 * CONV_DIM), lambda b, s: (b * per_b + s, Z_CONV * LANES // (2 * CONV_DIM))),
                  full(w), full(cb), full(lg), full(lb)],
        out_specs=pl.BlockSpec((ts, CONV_DIM), lambda b, s: (b * per_b + s, 0)),
        out_shape=jax.ShapeDtypeStruct((t, CONV_DIM), BF16),
        scratch_shapes=[pltpu.VMEM((ts + CONV_HIST, CONV_DIM), F32),
                        pltpu.VMEM((7, ts + CONV_HIST - 8, CONV_DIM), F32)],
        compiler_params=_cparams(2),
        name="conv",
    )(z, w, cb, lg, lb)


FOX_HEADS_PER_STEP = 2


def _fox_kernel(q_ref, aq_ref, k_ref, ak_ref, v_ref, o_ref, kk_ref, vt_ref, m_ref, acc_ref, *, blk):
    i = pl.program_id(2)
    dh = FOX_HEAD_DIM
    heads = range(FOX_HEADS_PER_STEP)
    col = lambda h: slice(h * dh, (h + 1) * dh)

    @pl.when(i == 0)
    def _():
        def fill(cb, carry):
            rows = pl.ds(pl.multiple_of(cb * blk, blk), blk)
            for h in heads:
                kk_ref[h, rows, 0:dh] = k_ref[rows, col(h)]
                kk_ref[h, rows, dh:2 * dh] = ak_ref[rows, col(h)]
                vt_ref[h, cb, 0:dh, :] = v_ref[rows, col(h)].astype(F32).T.astype(BF16)
                vt_ref[h, cb, dh:2 * dh, :] = jnp.ones((dh, blk), BF16)
            return carry
        lax.fori_loop(0, k_ref.shape[0] // blk, fill, 0)

    qq = [jnp.concatenate([q_ref[:, col(h)], aq_ref[:, col(h)]], axis=1) for h in heads]
    m_ref[...] = jnp.full(m_ref.shape, NEG, F32)
    acc_ref[...] = jnp.zeros(acc_ref.shape, F32)

    def step(j, masked):
        off = pl.multiple_of(j * blk, blk)
        parts = [(0, blk, blk)] if not masked else [(0, blk // 2, blk // 2), (blk // 2, blk // 2, blk)]
        chains = [(h, q0, nq, nk) for h in heads for q0, nq, nk in parts]
        sts = [_dot_nt(kk_ref[h, pl.ds(off, nk), :], qq[h][q0:q0 + nq, :]) for h, q0, nq, nk in chains]
        for (h, q0, nq, nk), st in zip(chains, sts):
            if masked:
                kv = lax.broadcasted_iota(I32, (nk, nq), 0)
                qi = lax.broadcasted_iota(I32, (nk, nq), 1) + q0
                st = jnp.where(kv <= qi, st, NEG)
            m_old = m_ref[h, :, q0:q0 + nq]
            m_new = jnp.maximum(m_old, jnp.max(st, axis=0, keepdims=True))
            p = jnp.exp2(st - m_new).astype(BF16)
            acc_ref[h, :, q0:q0 + nq] = (acc_ref[h, :, q0:q0 + nq] * jnp.exp2(m_old - m_new)
                                          + _dot(vt_ref[h, j, :, 0:nk], p))
            m_ref[h, :, q0:q0 + nq] = m_new

    def body(j, carry):
        step(j, False)
        return carry

    lax.fori_loop(0, i, body, 0)
    step(i, True)
    for h in heads:
        acc = acc_ref[h]
        o_ref[:, col(h)] = (acc[:dh, :] / acc[dh:, :]).T.astype(BF16)


def _fox(z, aq, ak, seq):
    t = z.shape[0]
    blk = min(1024, seq)
    nq = seq // blk
    hps = FOX_HEADS_PER_STEP
    width = hps * FOX_HEAD_DIM
    unit = lambda base: base * LANES // width
    qspec = lambda base: pl.BlockSpec((blk, width), lambda b, g, i: (b * nq + i, unit(base) + g))
    kspec = lambda base: pl.BlockSpec((seq, width), lambda b, g, i: (b, unit(base) + g),
                                      pipeline_mode=pl.Buffered(1))
    return pl.pallas_call(
        functools.partial(_fox_kernel, blk=blk),
        grid=(t // seq, FOX_HEADS // hps, nq),
        in_specs=[qspec(Z_QF), qspec(0), kspec(Z_KF), kspec(0), kspec(Z_VF)],
        out_specs=qspec(0),
        out_shape=jax.ShapeDtypeStruct((t, FOX_DIM), BF16),
        scratch_shapes=[pltpu.VMEM((hps, seq, 2 * FOX_HEAD_DIM), BF16),
                        pltpu.VMEM((hps, nq, 2 * FOX_HEAD_DIM, blk), BF16),
                        pltpu.VMEM((hps, 1, blk), F32), pltpu.VMEM((hps, 2 * FOX_HEAD_DIM, blk), F32)],
        compiler_params=_cparams(3),
        name="fox",
    )(z, aq, z, ak, z)


def _gla_chunk(q, k, v, b, st, e3):
    c, nsub = GLA_CHUNK, GLA_CHUNK // GLA_SUB
    row = lax.broadcasted_iota(I32, (c, GLA_KDIM), 0)
    rblk, rloc = row // GLA_SUB, row % GLA_SUB
    lane_k = lax.broadcasted_iota(I32, (c, GLA_KDIM), 1)
    khead, kpos = lane_k // GLA_DK, lane_k % GLA_DK
    b3 = b.reshape(nsub, GLA_SUB, GLA_KDIM)
    k3 = k.reshape(nsub, GLA_SUB, GLA_KDIM)
    last = b3[:, GLA_SUB - 1:GLA_SUB, :]
    bcast = lambda a3, i: jnp.broadcast_to(a3[:, i:i + 1, :], (nsub, GLA_SUB, GLA_KDIM)).reshape(c, GLA_KDIM)

    prev = jnp.concatenate([last[:1], last[:nsub - 1]], axis=0)
    refq = jnp.broadcast_to(prev, (nsub, GLA_SUB, GLA_KDIM)).reshape(c, GLA_KDIM)
    qt = q * jnp.exp(jnp.minimum(b - refq, 0.0))
    qcat = jnp.concatenate([jnp.where(rblk == i, qt, 0.0) for i in range(1, nsub)], axis=1).astype(BF16)
    kts = []
    for i in range(1, nsub):
        kt = k * jnp.exp(jnp.minimum(last[i - 1] - b, 0.0))
        kts.append(jnp.where(row < GLA_SUB * i, kt, 0.0))
    kcat = jnp.concatenate(
        [jnp.concatenate([jnp.where(khead == h, kt, 0.0) for kt in kts], axis=1) for h in range(GLA_HEADS)],
        axis=0).astype(BF16)
    a_all = _dot_nt(qcat, kcat)

    for sl in range(GLA_SUB):
        p = q * bcast(k3, sl) * jnp.exp(jnp.minimum(b - bcast(b3, sl), 0.0))
        w = _dot(p.astype(BF16), e3)
        hit = (kpos == rblk * GLA_SUB + sl) & (rloc >= sl)
        a_all = a_all + jnp.where(hit, w, 0.0)

    lane_v = lax.broadcasted_iota(I32, (c, GLA_VDIM), 1) // GLA_DV
    vbd = jnp.concatenate([jnp.where(lane_v == h, v, jnp.zeros_like(v)) for h in range(GLA_HEADS)], axis=0)
    qe = (q * jnp.exp(b)).astype(BF16)
    o = _dot(a_all.astype(BF16), vbd) + _dot_nt(qe, st.astype(BF16))

    blast = b[c - 1:c, :]
    kd = (k * jnp.exp(blast - b)).astype(BF16)
    shead_v = lax.broadcasted_iota(I32, (GLA_VDIM, GLA_KDIM), 0) // GLA_DV
    shead_k = lax.broadcasted_iota(I32, (GLA_VDIM, GLA_KDIM), 1) // GLA_DK
    st_new = jnp.where(shead_v == shead_k, st * jnp.exp(blast) + _dot_tn(v, kd), 0.0)
    return o, st_new


def _gla_kernel(q_ref, k_ref, v_ref, r_ref, b_ref, ng_ref, e3_ref, o_ref, st_ref):
    @pl.when(pl.program_id(1) == 0)
    def _():
        st_ref[...] = jnp.zeros_like(st_ref)

    e3 = e3_ref[...]
    ng = ng_ref[...]

    def body(ci, carry):
        rows = pl.ds(pl.multiple_of(ci * GLA_CHUNK, GLA_CHUNK), GLA_CHUNK)
        o, st_new = _gla_chunk(q_ref[rows, :].astype(F32), k_ref[rows, :].astype(F32), v_ref[rows, :],
                               b_ref[rows, :], st_ref[...], e3)
        st_ref[...] = st_new
        outs = []
        for h in range(GLA_HEADS):
            oh = o[:, h * GLA_DV:(h + 1) * GLA_DV]
            outs.append(oh * lax.rsqrt(jnp.mean(oh * oh, axis=-1, keepdims=True) + EPS))
        y = jnp.concatenate(outs, axis=1) * ng * _silu(r_ref[rows, :].astype(F32))
        o_ref[rows, :] = y.astype(BF16)
        return carry

    lax.fori_loop(0, q_ref.shape[0] // GLA_CHUNK, body, 0)


def _gla(z, bc, ng, seq):
    t = z.shape[0]
    ts = min(512, seq)
    per_b = seq // ts
    lane = np.arange(GLA_KDIM)
    e3 = jnp.asarray((lane[:, None] // GLA_DK == lane[None, :] // GLA_DK).astype(np.float32), BF16)
    spec = lambda n, unit: pl.BlockSpec((ts, n), lambda b, s: (b * per_b + s, unit))
    full = lambda a: pl.BlockSpec(a.shape, lambda b, s: (0,) * a.ndim)
    return pl.pallas_call(
        _gla_kernel,
        grid=(t // seq, per_b),
        in_specs=[spec(GLA_KDIM, Z_QG * LANES // GLA_KDIM), spec(GLA_KDIM, Z_KG * LANES // GLA_KDIM),
                  spec(GLA_VDIM, Z_VG * LANES // GLA_VDIM), spec(GLA_VDIM, Z_RG * LANES // GLA_VDIM),
                  spec(GLA_KDIM, 0), full(ng), full(e3)],
        out_specs=spec(GLA_VDIM, 0),
        out_shape=jax.ShapeDtypeStruct((t, GLA_VDIM), BF16),
        scratch_shapes=[pltpu.VMEM((GLA_VDIM, GLA_KDIM), F32)],
        compiler_params=_cparams(2),
        name="gla",
    )(z, z, z, z, bc, ng, e3)


def _merge_kernel(x_ref, g1_ref, ya_ref, yb_ref, yc_ref, ga_ref, gb_ref, gc_ref,
                  wa_ref, wb_ref, wc_ref, wo_ref, o_ref):
    gate = lambda r: jax.nn.sigmoid(r[...].astype(F32))
    merged = (gate(ga_ref) * _dot(ya_ref[...], wa_ref[...]) + gate(gb_ref) * _dot(yb_ref[...], wb_ref[...])
              + gate(gc_ref) * _dot(yc_ref[...], wc_ref[...]))
    o_ref[...] = x_ref[...] + g1_ref[...] * _dot(merged.astype(BF16), wo_ref[...])


def _merge(x2, g1, ya, yb, yc, z, wa, wb, wc, wo, seq):
    t, d = x2.shape
    tm = 256
    per_b = seq // tm
    full = lambda a: pl.BlockSpec(a.shape, lambda i: (0,) * a.ndim)
    row = lambda n, unit=0: pl.BlockSpec((tm, n), lambda i: (i, unit))
    return pl.pallas_call(
        _merge_kernel,
        grid=(t // tm,),
        in_specs=[row(d), pl.BlockSpec((None, 1, d), lambda i: (i // per_b, 0, 0)),
                  row(CONV_DIM), row(FOX_DIM), row(GLA_VDIM), row(d, 0), row(d, 1), row(d, 2),
                  full(wa), full(wb), full(wc), full(wo)],
        out_specs=row(d),
        out_shape=jax.ShapeDtypeStruct((t, d), F32),
        compiler_params=_cparams(1),
        name="merge",
    )(x2, g1, ya, yb, yc, z, z, z, wa, wb, wc, wo)


def _pack_halves(y):
    n = y.shape[1] // 2
    lo = pltpu.bitcast(y[:, :n].astype(BF16).astype(F32), U32)
    hi = pltpu.bitcast(y[:, n:].astype(BF16).astype(F32), U32)
    return (hi & jnp.uint32(0xFFFF0000)) | (lo >> 16)


def _unpack_halves(p):
    lo = pltpu.bitcast(p << 16, F32)
    hi = pltpu.bitcast(p & jnp.uint32(0xFFFF0000), F32)
    return lo, hi


def _first_index(hit, idx, n):
    return jnp.min(jnp.where(hit, idx, n), axis=0, keepdims=True)


def _route_kernel(x_ref, g_ref, sc_ref, sh_ref, wrh_ref, wrl_ref, eb_ref, u_ref,
                  te_ref, tw_ref, rk_ref, cnt_ref, carry_ref):
    @pl.when(pl.program_id(0) == 0)
    def _():
        carry_ref[...] = jnp.zeros_like(carry_ref)

    x = x_ref[...]
    h = _modulated_norm(x, g_ref[...], sc_ref[...], sh_ref[...])
    hb = h.astype(BF16)

    hlo = (h - hb.astype(F32)).astype(BF16)
    logits = _dot_nt(wrh_ref[...], hb) + _dot_nt(wrh_ref[...], hlo) + _dot_nt(wrl_ref[...], hb)
    scores = jax.nn.sigmoid(logits)
    biased = scores + eb_ref[...]
    tm = x.shape[0]
    per_group = N_EXPERTS // N_GROUPS
    sub = lax.broadcasted_iota(I32, (per_group, tm), 0).astype(F32)
    gscore = []
    for g in range(N_GROUPS):
        blk = biased[g * per_group:(g + 1) * per_group, :]
        m1 = jnp.max(blk, axis=0, keepdims=True)
        rest = jnp.where(sub == _first_index(blk == m1, sub, float(per_group)), -jnp.inf, blk)
        gscore.append(m1 + jnp.max(rest, axis=0, keepdims=True))
    gs = jnp.concatenate(gscore, axis=0)
    gidx = lax.broadcasted_iota(I32, (N_GROUPS, tm), 0).astype(F32)
    gsel = jnp.zeros((N_GROUPS, tm), F32)
    for _ in range(TOPK_GROUPS):
        m = jnp.max(gs, axis=0, keepdims=True)
        hit = gidx == _first_index(gs == m, gidx, float(N_GROUPS))
        gsel = jnp.where(hit, 1.0, gsel)
        gs = jnp.where(hit, -jnp.inf, gs)
    cand = jnp.concatenate(
        [jnp.where(gsel[g:g + 1, :] > 0.5, biased[g * per_group:(g + 1) * per_group, :], -jnp.inf)
         for g in range(N_GROUPS)], axis=0)
    eidx = lax.broadcasted_iota(I32, (N_EXPERTS, tm), 0).astype(F32)
    hits, tops, tws = [], [], []
    sel = jnp.zeros((N_EXPERTS, tm), F32)
    for _ in range(TOP_K):
        m = jnp.max(cand, axis=0, keepdims=True)
        e = _first_index(cand == m, eidx, float(N_EXPERTS))
        hit = eidx == e
        hits.append(hit)
        tops.append(e)
        tws.append(jnp.sum(jnp.where(hit, scores, 0.0), axis=0, keepdims=True))
        cand = jnp.where(hit, -jnp.inf, cand)
        sel = jnp.where(hit, 1.0, sel)
    te_ref[...] = jnp.concatenate(tops, axis=0).astype(I32)
    tw = jnp.concatenate(tws, axis=0)
    tw_ref[...] = tw / jnp.sum(tw, axis=0, keepdims=True) * ROUTED_SCALE

    before = _dot(sel.astype(BF16), u_ref[...]) + carry_ref[...]
    rk_ref[...] = jnp.concatenate(
        [jnp.sum(jnp.where(hit, before, 0.0), axis=0, keepdims=True) for hit in hits], axis=0).astype(I32)
    carry = carry_ref[...] + jnp.sum(sel, axis=1, keepdims=True)
    carry_ref[...] = carry
    cnt_ref[...] = jnp.broadcast_to(carry, cnt_ref.shape).astype(I32)


MOE_TM = 256


def _route(x2, g, sc, sh, wrh, wrl, eb, seq):
    t, d = x2.shape
    tm = MOE_TM
    per_b = seq // tm
    r = np.arange(tm)
    u = jnp.asarray((r[:, None] < r[None, :]).astype(np.float32), BF16)
    full = lambda a: pl.BlockSpec(a.shape, lambda i: (0,) * a.ndim)
    vec = lambda: pl.BlockSpec((None, 1, d), lambda i: (i // per_b, 0, 0))
    colk = lambda: pl.BlockSpec((TOP_K, tm), lambda i: (0, i))
    return pl.pallas_call(
        _route_kernel,
        grid=(t // tm,),
        in_specs=[pl.BlockSpec((tm, d), lambda i: (i, 0)), full(g), vec(), vec(),
                  full(wrh), full(wrl), full(eb), full(u)],
        out_specs=[colk(), colk(), colk(), pl.BlockSpec((N_EXPERTS, LANES), lambda i: (0, 0))],
        out_shape=[jax.ShapeDtypeStruct((TOP_K, t), I32), jax.ShapeDtypeStruct((TOP_K, t), F32),
                   jax.ShapeDtypeStruct((TOP_K, t), I32), jax.ShapeDtypeStruct((N_EXPERTS, LANES), I32)],
        scratch_shapes=[pltpu.VMEM((N_EXPERTS, 1), F32)],
        compiler_params=_cparams(1),
        name="route",
    )(x2, g, sc, sh, wrh, wrl, eb, u)


ROW_TILE = 8


def _tiles_to_rows(ref, lead, first, m):
    return jnp.concatenate(
        [ref[lead + (pl.ds(first * ROW_TILE + j, m, stride=ROW_TILE), slice(None))] for j in range(ROW_TILE)], axis=1)


def _rows_to_tiles(ref, lead, val):
    m = val.shape[0]
    for j in range(ROW_TILE):
        ref[lead + (pl.ds(j, m, stride=ROW_TILE), slice(None))] = val[:, j * LANES:(j + 1) * LANES]


def _tile_rows(ref, lead, row):
    return ref.at[lead + (pl.ds(pl.multiple_of(row, ROW_TILE) if not isinstance(row, int) else row, ROW_TILE),)]


def _dispatch_kernel(pend_ref, dst_ref, x_ref, g_ref, sc_ref, sh_ref, g2_ref, ws1_ref, ws3_ref, ws2_ref,
                     base_ref, xs_hbm, hbuf, zbuf, sem, zsem, *, n_tiles):
    i = pl.program_id(0)
    odd = i % 2 == 1
    even = jnp.logical_not(odd)
    tm = x_ref.shape[0]

    @pl.when(i == 0)
    def _():
        zbuf[...] = jnp.zeros_like(zbuf)
        def clear(e, start):
            first = jnp.maximum(pend_ref[e] - MOE_BLOCK, 0) * ROW_TILE
            cp = pltpu.make_async_copy(zbuf, xs_hbm.at[pl.ds(pl.multiple_of(first, ROW_TILE), zbuf.shape[0])], zsem)
            cp.start() if start else cp.wait()
        for e in range(N_EXPERTS):
            clear(e, True)
        for e in range(N_EXPERTS):
            clear(e, False)

    n_parts = 4

    def copies(sl, start, part=None):
        for k in range(TOP_K):
            for t in range(tm):
                n = k * tm + t
                if part is not None and n * n_parts // (TOP_K * tm) != part:
                    continue
                cp = pltpu.make_async_copy(_tile_rows(hbuf, (sl,), ROW_TILE * t),
                                           _tile_rows(xs_hbm, (), dst_ref[k, t] if start else 0), sem.at[sl])
                cp.start(priority=n % 2) if start else cp.wait()

    pl.when((i >= 2) & even)(functools.partial(copies, 0, False))
    pl.when((i >= 2) & odd)(functools.partial(copies, 1, False))

    def main(sl):
        x = x_ref[...]
        h = _modulated_norm(x, g_ref[...], sc_ref[...], sh_ref[...])
        _rows_to_tiles(hbuf, (sl,), _pack_halves(h))
        copies(sl, True, part=0)
        hb = h.astype(BF16)
        up = _dot(hb, ws1_ref[...])
        copies(sl, True, part=1)
        gate = _dot(hb, ws3_ref[...])
        copies(sl, True, part=2)
        hs = (_silu(up) * gate).astype(BF16)
        base_ref[...] = x + g2_ref[...] * _dot(hs, ws2_ref[...])
        copies(sl, True, part=3)

    pl.when(even)(functools.partial(main, 0))
    pl.when(odd)(functools.partial(main, 1))

    @pl.when(i == n_tiles - 1)
    def _():
        if n_tiles >= 2:
            copies(n_tiles % 2, False)
        copies((n_tiles - 1) % 2, False)


def _dispatch(x2, g, sc, sh, g2, ws1, ws3, ws2, pends, dest, n_rows, seq):
    t, d = x2.shape
    tm = MOE_TM
    per_b = seq // tm
    full = lambda a: pl.BlockSpec(a.shape, lambda i, pe: (0,) * a.ndim)
    vec = lambda: pl.BlockSpec((None, 1, d), lambda i, pe: (i // per_b, 0, 0))
    return pl.pallas_call(
        functools.partial(_dispatch_kernel, n_tiles=t // tm),
        grid_spec=pltpu.PrefetchScalarGridSpec(
            num_scalar_prefetch=1,
            grid=(t // tm,),
            in_specs=[pl.BlockSpec((TOP_K, tm), lambda i, pe: (0, i), memory_space=pltpu.SMEM),
                      pl.BlockSpec((tm, d), lambda i, pe: (i, 0)), full(g), vec(), vec(), vec(),
                      full(ws1), full(ws3), full(ws2)],
            out_specs=[pl.BlockSpec((tm, d), lambda i, pe: (i, 0)), pl.BlockSpec(memory_space=pl.ANY)],
            scratch_shapes=[pltpu.VMEM((2, tm * ROW_TILE, LANES), U32),
                            pltpu.VMEM((MOE_BLOCK * ROW_TILE, LANES), U32),
                            pltpu.SemaphoreType.DMA((2,)), pltpu.SemaphoreType.DMA(())]),
        out_shape=[jax.ShapeDtypeStruct((t, d), F32), jax.ShapeDtypeStruct((n_rows * ROW_TILE, LANES), U32)],
        compiler_params=_cparams(1),
        name="dispatch",
    )(pends, dest, x2, g, sc, sh, g2, ws1, ws3, ws2)


def _experts_kernel(be_ref, nu_ref, nv_ref, xs_ref, w1_ref, w3_ref, w2_ref, ys_ref, w1b, w3b, w2b):
    s = pl.program_id(0)
    half = MOE_BLOCK // 2

    @pl.when((s == 0) | (be_ref[s] != be_ref[jnp.maximum(s - 1, 0)]))
    def _():
        w1b[...] = w1_ref[...].astype(BF16)
        w3b[...] = w3_ref[...].astype(BF16)
        w2b[...] = w2_ref[...].astype(BF16)

    def compute(n_halves):
        xbs = []
        for p in range(n_halves):
            lo, hi = _unpack_halves(_tiles_to_rows(xs_ref, (), p * half, half))
            xbs.append(jnp.concatenate([lo.astype(BF16), hi.astype(BF16)], axis=1))
        ups = [_dot(xb, w1b[...]) for xb in xbs]
        gates = [_dot(xb, w3b[...]) for xb in xbs]
        hids = [(_silu(u) * g).astype(BF16) for u, g in zip(ups, gates)]
        packed = _pack_halves(jnp.concatenate([_dot(hd, w2b[...]) for hd in hids], axis=0))
        for j in range(ROW_TILE):
            ys_ref[pl.ds(j, n_halves * half, stride=ROW_TILE), :] = packed[:, j * LANES:(j + 1) * LANES]

    nv = nv_ref[s]
    pl.when(nv > half)(functools.partial(compute, 2))
    pl.when((nv > 0) & (nv <= half))(functools.partial(compute, 1))


def _experts(layer, block_e, n_used, n_valid, xs, w1, w3, w2):
    nb = block_e.shape[0]
    rows = pl.BlockSpec((MOE_BLOCK * ROW_TILE, LANES), lambda s, be, nu, nv: (jnp.minimum(s, nu[0] - 1), 0))
    wspec = lambda a: pl.BlockSpec((None, None) + a.shape[2:], lambda s, be, nu, nv: (layer, be[s], 0, 0))
    return pl.pallas_call(
        _experts_kernel,
        grid_spec=pltpu.PrefetchScalarGridSpec(
            num_scalar_prefetch=3,
            grid=(nb,),
            in_specs=[rows, wspec(w1), wspec(w3), wspec(w2)],
            out_specs=rows,
            scratch_shapes=[pltpu.VMEM(w1.shape[2:], BF16), pltpu.VMEM(w3.shape[2:], BF16),
                            pltpu.VMEM(w2.shape[2:], BF16)]),
        out_shape=jax.ShapeDtypeStruct(xs.shape, U32),
        compiler_params=_cparams(1),
        name="experts",
    )(block_e, n_used, n_valid, xs, w1, w3, w2)


def _combine_kernel(dst_ref, dstn_ref, base_ref, g2_ref, w_ref, fg_ref, ys_hbm, o_ref, gbuf, sem, *, n_tiles, final):
    i = pl.program_id(0)
    odd = i % 2 == 1
    even = jnp.logical_not(odd)
    tm = base_ref.shape[0]

    rb = 32
    n_rb = tm // rb

    def fetch(idx_ref, sl, start, part=None):
        for k in range(TOP_K):
            for t in range(tm):
                n = k * tm + t
                if part is not None and n * n_rb // (TOP_K * tm) != part:
                    continue
                cp = pltpu.make_async_copy(_tile_rows(ys_hbm, (), idx_ref[k, t] if start else 0),
                                           _tile_rows(gbuf, (sl,), ROW_TILE * n), sem.at[sl])
                cp.start(priority=n % 2) if start else cp.wait()

    @pl.when(i == 0)
    def _():
        fetch(dst_ref, 0, True)

    def main(sl):
        fetch(dst_ref, sl, False)
        half = o_ref.shape[1] // 2

        def rows_step(ib):
            r0 = ib * rb
            rows = pl.ds(r0, rb)
            w = w_ref[rows, :]
            wb = [jnp.broadcast_to(w[:, k:k + 1], (rb, LANES)) for k in range(TOP_K)]
            ssq = jnp.zeros((rb, 1), F32)
            for j in range(ROW_TILE):
                acc_lo = acc_hi = None
                for k in range(TOP_K):
                    lo, hi = _unpack_halves(gbuf[sl, pl.ds((k * tm + r0) * ROW_TILE + j, rb, stride=ROW_TILE), :])
                    acc_lo = wb[k] * lo if acc_lo is None else acc_lo + wb[k] * lo
                    acc_hi = wb[k] * hi if acc_hi is None else acc_hi + wb[k] * hi
                for acc, c0 in ((acc_lo, j * LANES), (acc_hi, half + j * LANES)):
                    x = base_ref[rows, c0:c0 + LANES] + g2_ref[:, c0:c0 + LANES] * acc
                    o_ref[rows, c0:c0 + LANES] = x
                    ssq = ssq + jnp.sum(x * x, axis=-1, keepdims=True)
            if final:
                scale = lax.rsqrt(ssq * (1.0 / o_ref.shape[1]) + EPS)
                o_ref[rows, :] = o_ref[rows, :] * scale * fg_ref[...]

        for ib in range(n_rb):
            rows_step(ib)
            fetch(dstn_ref, 1 - sl, True, part=ib)

    pl.when(even)(functools.partial(main, 0))
    pl.when(odd)(functools.partial(main, 1))

    @pl.when(i == n_tiles - 1)
    def _():
        fetch(dst_ref, n_tiles % 2, False)


def _combine(base, g2, w, ys, dest, fg, seq, final):
    t, d = base.shape
    tm = MOE_TM
    per_b = seq // tm
    n_tiles = t // tm
    idx = lambda f: pl.BlockSpec((TOP_K, tm), f, memory_space=pltpu.SMEM)
    return pl.pallas_call(
        functools.partial(_combine_kernel, n_tiles=n_tiles, final=final),
        grid=(n_tiles,),
        in_specs=[idx(lambda i: (0, i)), idx(lambda i: (0, jnp.minimum(i + 1, n_tiles - 1))),
                  pl.BlockSpec((tm, d), lambda i: (i, 0)),
                  pl.BlockSpec((None, 1, d), lambda i: (i // per_b, 0, 0)),
                  pl.BlockSpec((tm, TOP_K), lambda i: (i, 0)),
                  pl.BlockSpec((1, d), lambda i: (0, 0)),
                  pl.BlockSpec(memory_space=pl.ANY)],
        out_specs=pl.BlockSpec((tm, d), lambda i: (i, 0)),
        out_shape=jax.ShapeDtypeStruct((t, d), F32),
        scratch_shapes=[pltpu.VMEM((2, TOP_K * tm * ROW_TILE, LANES), U32), pltpu.SemaphoreType.DMA((2,))],
        compiler_params=_cparams(1),
        name="combine",
    )(dest, dest, base, g2, w, fg, ys)


def _dispatch_tables(top_e, rank, counts, t):
    n_blocks = TOP_K * t // MOE_BLOCK + N_EXPERTS
    padded = (counts + MOE_BLOCK - 1) // MOE_BLOCK * MOE_BLOCK
    pends = jnp.cumsum(padded)
    pstarts = pends - padded
    experts = jnp.arange(N_EXPERTS, dtype=I32)
    pstart_of = jnp.sum(jnp.where(top_e[..., None] == experts, pstarts, 0), axis=-1)
    dest = (pstart_of + rank) * ROW_TILE
    starts = jnp.arange(n_blocks, dtype=I32) * MOE_BLOCK
    block_e = jnp.sum((pends[None, :] <= starts[:, None]).astype(I32), axis=1)
    block_e = jnp.minimum(block_e, jnp.max(jnp.where(starts < pends[-1], block_e, 0)))
    n_used = (pends[-1:] // MOE_BLOCK).astype(I32)
    seg_end = jnp.sum(jnp.where(block_e[:, None] == experts, pstarts + counts, 0), axis=-1)
    n_valid = jnp.clip(seg_end - starts, 0, MOE_BLOCK).astype(I32)
    return block_e, n_used, n_valid, pends.astype(I32), dest, n_blocks * MOE_BLOCK


def _layout_inproj(w_in, b_in):
    o = np.cumsum([0, 2 * CONV_DIM, FOX_DIM, FOX_DIM, FOX_DIM, FOX_HEADS, GLA_KDIM, GLA_KDIM, GLA_VDIM, GLA_VDIM,
                   GLA_RANK, 3 * D_MODEL])
    seg = lambda a, i: a[..., int(o[i]):int(o[i + 1])]

    def main(a):
        return jnp.concatenate([seg(a, 10), seg(a, 0), seg(a, 1) * (FOX_HEAD_DIM ** -0.5 * LOG2E), seg(a, 2), seg(a, 3),
                                seg(a, 5) * GLA_DK ** -0.5, seg(a, 6), seg(a, 7), seg(a, 8)], axis=-1)

    def small(a):
        pad = jnp.zeros(a.shape[:-1] + (LANES - FOX_HEADS - GLA_RANK,), a.dtype)
        return jnp.concatenate([seg(a, 4), seg(a, 9), pad], axis=-1)

    return (main(w_in).astype(BF16), main(b_in)[..., None, :], small(w_in).astype(BF16),
            small(b_in)[..., None, :])


def kernel(x, c, norm1_g, ada_w, ada_b, w_in, b_in, conv_w, conv_b, conv_ln_g, conv_ln_b, gla_wa, gla_ba,
           gla_norm_g, w_branch_a, w_branch_b, w_branch_c, w_out, norm2_g, w_router, e_bias, w1, w3, w2,
           ws1, ws3, ws2, final_g):
    bsz, seq, d = x.shape
    t = bsz * seq
    depth = ada_w.shape[0]
    mod = _ada_mod(c, ada_w, ada_b).reshape(depth, bsz, 6, 1, d)
    x2 = x.reshape(t, d)
    row = lambda a: a.reshape(1, -1)
    w, b, ws, bs = _layout_inproj(w_in, b_in)
    for l in range(depth):
        sh1, sc1, g1, sh2, sc2, g2 = (mod[l, :, i] for i in range(6))
        z, zs = _inproj(l, x2, row(norm1_g[l]), sc1, sh1, w, b, ws, bs, seq)
        wa_pad = jnp.zeros((LANES, GLA_KDIM), F32).at[ZS_A:ZS_A + GLA_RANK].set(gla_wa[l])
        aq, ak, bc = _prep(zs, wa_pad, row(gla_ba[l]), seq)
        ya = _conv(z, conv_w[l].reshape(CONV_WIDTH, CONV_DIM), row(conv_b[l]), row(conv_ln_g[l]),
                   row(conv_ln_b[l]), seq)
        yb = _fox(z, aq, ak, seq)
        yc = _gla(z, bc, row(gla_norm_g[l]), seq)
        x2 = _merge(x2, g1, ya, yb, yc, z, w_branch_a[l].astype(BF16), w_branch_b[l].astype(BF16),
                    w_branch_c[l].astype(BF16), w_out[l].astype(BF16), seq)

        wr_t = w_router[l].T
        wrh = wr_t.astype(BF16)
        wrl = (wr_t - wrh.astype(F32)).astype(BF16)
        top_e, top_w, rank, counts = _route(x2, row(norm2_g[l]), sc2, sh2, wrh, wrl,
                                            e_bias[l].reshape(N_EXPERTS, 1), seq)
        block_e, n_used, n_valid, pends, dest, n_rows = _dispatch_tables(top_e, rank, counts[:, 0], t)
        base, xs = _dispatch(x2, row(norm2_g[l]), sc2, sh2, g2, ws1[l].astype(BF16), ws3[l].astype(BF16),
                             ws2[l].astype(BF16), pends, dest, n_rows, seq)
        ys = _experts(l, block_e, n_used, n_valid, xs, w1, w3, w2)
        x2 = _combine(base, g2, top_w.T, ys, dest, row(final_g), seq, final=(l == depth - 1))
    return x2.reshape(bsz, seq, d)
```

```python
import functools

import numpy as np
import jax
import jax.numpy as jnp
from jax import lax
from jax.experimental import pallas as pl
from jax.experimental.pallas import tpu as pltpu

F32 = jnp.float32
BF16 = jnp.bfloat16
U32 = jnp.uint32
I32 = jnp.int32

D_MODEL = 2048
CONV_DIM = 512
CONV_WIDTH = 31
FOX_HEADS = 8
FOX_HEAD_DIM = 128
FOX_DIM = FOX_HEADS * FOX_HEAD_DIM
GLA_HEADS = 4
GLA_DK = 64
GLA_DV = 128
GLA_KDIM = GLA_HEADS * GLA_DK
GLA_VDIM = GLA_HEADS * GLA_DV
GLA_RANK = 16
GLA_TAU = 16.0
GLA_CHUNK = 64
GLA_SUB = 8
N_EXPERTS = 64
N_GROUPS = 8
TOPK_GROUPS = 4
TOP_K = 8
EXPERT_DIM = 512
SHARED_DIM = 512
ROUTED_SCALE = 2.5
MOE_BLOCK = 512
EPS = 1e-6

LANES = 128
V7X_VMEM_LIMIT = 56 * 1024 * 1024

Z_GATE = 0
Z_CONV = 48
Z_QF = 56
Z_KF = 64
Z_VF = 72
Z_QG = 80
Z_KG = 82
Z_VG = 84
Z_RG = 88
Z_COLS = 92 * LANES
ZS_F = 0
ZS_A = 8
NEG = -1e30
LOG2E = 1.4426950408889634


def _cparams(n_axes, vmem=V7X_VMEM_LIMIT, flags=None):
    return pltpu.CompilerParams(dimension_semantics=("arbitrary",) * n_axes, vmem_limit_bytes=vmem, flags=flags)


def _log_sigmoid(x):
    return jnp.minimum(x, 0.0) - jnp.log(1.0 + jnp.exp(-jnp.abs(x)))


def _silu(x):
    return x * jax.nn.sigmoid(x)


def _split3(x):
    hi = x.astype(BF16)
    r1 = x - hi.astype(F32)
    mid = r1.astype(BF16)
    lo = (r1 - mid.astype(F32)).astype(BF16)
    return hi, mid, lo


def _dot(a, b):
    return jnp.dot(a, b, preferred_element_type=F32)


def _dot_nt(a, b):
    return lax.dot_general(a, b, (((1,), (1,)), ((), ())), preferred_element_type=F32)


def _dot_tn(a, b):
    return lax.dot_general(a, b, (((0,), (0,)), ((), ())), preferred_element_type=F32)


def _lmul_exact(l01, x):
    hi, mid, lo = _split3(x)
    return _dot(l01, hi) + _dot(l01, mid) + _dot(l01, lo)


def _dot_f32(a, b):
    a0, a1, a2 = _split3(a)
    b0, b1, b2 = _split3(b)
    return (_dot(a0, b0) + _dot(a0, b1) + _dot(a1, b0)) + (_dot(a0, b2) + _dot(a1, b1) + _dot(a2, b0))


def _ada_kernel(ct_ref, w_ref, b_ref, o_ref):
    ct = ct_ref[...]
    act = _silu(ct)
    w = w_ref[...]
    rows = [jnp.sum(w * act[:, b:b + 1], axis=0, keepdims=True) for b in range(ct.shape[1])]
    o_ref[...] = jnp.concatenate(rows, axis=0) + b_ref[...]


def _ada_mod(c, ada_w, ada_b):
    n_l, d, n = ada_w.shape
    bsz = c.shape[0]
    tn = 1024
    return pl.pallas_call(
        _ada_kernel,
        grid=(n_l, n // tn),
        in_specs=[pl.BlockSpec((d, bsz), lambda l, j: (0, 0)),
                  pl.BlockSpec((None, d, tn), lambda l, j: (l, 0, j)),
                  pl.BlockSpec((None, 1, tn), lambda l, j: (l, 0, j))],
        out_specs=pl.BlockSpec((None, bsz, tn), lambda l, j: (l, 0, j)),
        out_shape=jax.ShapeDtypeStruct((n_l, bsz, n), F32),
        compiler_params=_cparams(2),
        name="ada_mod",
    )(c.T, ada_w, ada_b.reshape(n_l, 1, n))


def _modulated_norm(x, g, sc, sh):
    ms = jnp.mean(x * x, axis=-1, keepdims=True)
    return (x * lax.rsqrt(ms + EPS) * g) * (1.0 + sc) + sh


def _inproj_kernel(x_ref, g_ref, sc_ref, sh_ref, w_ref, b_ref, ws_ref, bs_ref, z_ref, zs_ref, h_scr):
    @pl.when(pl.program_id(1) == 0)
    def _():
        h = _modulated_norm(x_ref[...], g_ref[...], sc_ref[...], sh_ref[...]).astype(BF16)
        h_scr[...] = h
        zs_ref[...] = _dot(h, ws_ref[...]) + bs_ref[...]

    z_ref[...] = (_dot(h_scr[...], w_ref[...]) + b_ref[...]).astype(BF16)


def _inproj(layer, x2, g, sc, sh, w, b, ws, bs, seq):
    t, d = x2.shape
    tm, tn = 512, Z_COLS // 4
    per_b = seq // tm
    vec = lambda: pl.BlockSpec((None, 1, d), lambda i, j: (i // per_b, 0, 0))
    return pl.pallas_call(
        _inproj_kernel,
        grid=(t // tm, Z_COLS // tn),
        in_specs=[pl.BlockSpec((tm, d), lambda i, j: (i, 0)),
                  pl.BlockSpec((1, d), lambda i, j: (0, 0)),
                  vec(), vec(),
                  pl.BlockSpec((None, d, tn), lambda i, j: (layer, 0, j)),
                  pl.BlockSpec((None, 1, tn), lambda i, j: (layer, 0, j)),
                  pl.BlockSpec((None, d, LANES), lambda i, j: (layer, 0, 0)),
                  pl.BlockSpec((None, 1, LANES), lambda i, j: (layer, 0, 0))],
        out_specs=[pl.BlockSpec((tm, tn), lambda i, j: (i, j)),
                   pl.BlockSpec((tm, LANES), lambda i, j: (i, 0))],
        out_shape=[jax.ShapeDtypeStruct((t, Z_COLS), BF16), jax.ShapeDtypeStruct((t, LANES), F32)],
        scratch_shapes=[pltpu.VMEM((tm, d), BF16)],
        compiler_params=_cparams(2),
        name="inproj",
    )(x2, g, sc, sh, w, b, ws, bs)


def _prep_kernel(zs_ref, ltri_ref, lblk_ref, pq_ref, pk_ref, cq_ref, ck_ref, wa_ref, ba_ref,
                 aq_ref, ak_ref, bc_ref, carry_ref):
    @pl.when(pl.program_id(1) == 0)
    def _():
        carry_ref[...] = jnp.zeros_like(carry_ref)

    zs = zs_ref[...]
    cum = _lmul_exact(ltri_ref[...], _log_sigmoid(zs)) + carry_ref[...]
    carry_ref[...] = cum[cum.shape[0] - 1:, :]
    hi, mid, lo = _split3(cum * LOG2E)
    aq = cq_ref[...] + _dot(hi, pq_ref[0]) + _dot(mid, pq_ref[1]) + _dot(lo, pq_ref[2])
    ak = ck_ref[...] + _dot(hi, pk_ref[0]) + _dot(mid, pk_ref[1]) + _dot(lo, pk_ref[2])
    aq_ref[...] = aq.astype(BF16)
    ak_ref[...] = ak.astype(BF16)
    la = _log_sigmoid(_dot_f32(zs, wa_ref[...]) + ba_ref[...]) * (1.0 / GLA_TAU)
    bc_ref[...] = _lmul_exact(lblk_ref[...], la)


def _prep_consts(ts):
    r = np.arange(ts)
    ltri = (r[:, None] >= r[None, :]).astype(np.float32)
    lblk = ltri * (r[:, None] // GLA_CHUNK == r[None, :] // GLA_CHUNK)
    pq = np.zeros((3, LANES, FOX_DIM), np.float32)
    pk = np.zeros((3, LANES, FOX_DIM), np.float32)
    cq = np.zeros((1, FOX_DIM), np.float32)
    ck = np.zeros((1, FOX_DIM), np.float32)
    for h in range(FOX_HEADS):
        for p in range(3):
            pq[p, ZS_F + h, h * FOX_HEAD_DIM + p] = 1.0
            pk[p, ZS_F + h, h * FOX_HEAD_DIM + 3 + p] = -1.0
            cq[0, h * FOX_HEAD_DIM + 3 + p] = 1.0
            ck[0, h * FOX_HEAD_DIM + p] = 1.0
    return (jnp.asarray(ltri, BF16), jnp.asarray(lblk, BF16), jnp.asarray(pq, BF16), jnp.asarray(pk, BF16),
            jnp.asarray(cq), jnp.asarray(ck))


def _prep(zs, wa_pad, ba, seq):
    t = zs.shape[0]
    ts = 256
    per_b = seq // ts
    ltri, lblk, pq, pk, cq, ck = _prep_consts(ts)
    full = lambda a: pl.BlockSpec(a.shape, lambda b, s: (0,) * a.ndim)
    row = lambda n: pl.BlockSpec((ts, n), lambda b, s: (b * per_b + s, 0))
    return pl.pallas_call(
        _prep_kernel,
        grid=(t // seq, per_b),
        in_specs=[row(LANES), full(ltri), full(lblk), full(pq), full(pk), full(cq), full(ck),
                  full(wa_pad), full(ba)],
        out_specs=[row(FOX_DIM), row(FOX_DIM), row(GLA_KDIM)],
        out_shape=[jax.ShapeDtypeStruct((t, FOX_DIM), BF16), jax.ShapeDtypeStruct((t, FOX_DIM), BF16),
                   jax.ShapeDtypeStruct((t, GLA_KDIM), F32)],
        scratch_shapes=[pltpu.VMEM((1, LANES), F32)],
        compiler_params=_cparams(2),
        name="prep",
    )(zs, ltri, lblk, pq, pk, cq, ck, wa_pad, ba)


CONV_HIST = 32


def _conv_kernel(u_ref, w_ref, cb_ref, lg_ref, lb_ref, o_ref, ypad_ref, shift_ref):
    ts = u_ref.shape[0]
    sub = 8

    @pl.when(pl.program_id(1) == 0)
    def _():
        ypad_ref[0:CONV_HIST, :] = jnp.zeros((CONV_HIST, CONV_DIM), F32)

    u = u_ref[...].astype(F32)
    ypad_ref[CONV_HIST:CONV_HIST + ts, :] = u[:, :CONV_DIM] * jax.nn.sigmoid(u[:, CONV_DIM:])
    span = ts + CONV_HIST - sub
    for r in range(1, sub):
        shift_ref[r - 1] = ypad_ref[r:r + span, :]
    acc = jnp.zeros((ts, CONV_DIM), F32) + cb_ref[...]
    first = CONV_HIST - (CONV_WIDTH - 1)
    for j in range(CONV_WIDTH):
        r = (first + j) % sub
        base = first + j - r
        rows = ypad_ref[base:base + ts, :] if r == 0 else shift_ref[r - 1, base:base + ts, :]
        acc = acc + w_ref[j:j + 1, :] * rows
    ypad_ref[0:CONV_HIST, :] = ypad_ref[ts:ts + CONV_HIST, :]
    mu = jnp.mean(acc, axis=-1, keepdims=True)
    cen = acc - mu
    var = jnp.mean(cen * cen, axis=-1, keepdims=True)
    y = cen * lax.rsqrt(var + EPS) * lg_ref[...] + lb_ref[...]
    o_ref[...] = _silu(y).astype(BF16)


def _conv(z, w, cb, lg, lb, seq):
    t = z.shape[0]
    ts = 512
    per_b = seq // ts
    full = lambda a: pl.BlockSpec(a.shape, lambda b, s: (0,) * a.ndim)
    return pl.pallas_call(
        _conv_kernel,
        grid=(t // seq, per_b),
        in_specs=[pl.BlockSpec((ts, 2 * CONV_DIM), lambda b, s: (b * per_b + s, Z_CONV * LANES // (2 * CONV_DIM))),
                  full(w), full(cb), full(lg), full(lb)],
        out_specs=pl.BlockSpec((ts, CONV_DIM), lambda b, s: (b * per_b + s, 0)),
        out_shape=jax.ShapeDtypeStruct((t, CONV_DIM), BF16),
        scratch_shapes=[pltpu.VMEM((ts + CONV_HIST, CONV_DIM), F32),
                        pltpu.VMEM((7, ts + CONV_HIST - 8, CONV_DIM), F32)],
        compiler_params=_cparams(2),
        name="conv",
    )(z, w, cb, lg, lb)


FOX_HEADS_PER_STEP = 2


def _fox_kernel(q_ref, aq_ref, k_ref, ak_ref, v_ref, o_ref, kk_ref, vt_ref, m_ref, acc_ref, *, blk):
    i = pl.program_id(2)
    dh = FOX_HEAD_DIM
    heads = range(FOX_HEADS_PER_STEP)
    col = lambda h: slice(h * dh, (h + 1) * dh)

    @pl.when(i == 0)
    def _():
        def fill(cb, carry):
            rows = pl.ds(pl.multiple_of(cb * blk, blk), blk)
            for h in heads:
                kk_ref[h, rows, 0:dh] = k_ref[rows, col(h)]
                kk_ref[h, rows, dh:2 * dh] = ak_ref[rows, col(h)]
                vt_ref[h, cb, 0:dh, :] = v_ref[rows, col(h)].astype(F32).T.astype(BF16)
                vt_ref[h, cb, dh:2 * dh, :] = jnp.ones((dh, blk), BF16)
            return carry
        lax.fori_loop(0, k_ref.shape[0] // blk, fill, 0)

    qq = [jnp.concatenate([q_ref[:, col(h)], aq_ref[:, col(h)]], axis=1) for h in heads]
    m_ref[...] = jnp.full(m_ref.shape, NEG, F32)
    acc_ref[...] = jnp.zeros(acc_ref.shape, F32)

    def step(j, masked):
        off = pl.multiple_of(j * blk, blk)
        parts = [(0, blk, blk)] if not masked else [(0, blk // 2, blk // 2), (blk // 2, blk // 2, blk)]
        chains = [(h, q0, nq, nk) for h in heads for q0, nq, nk in parts]
        sts = [_dot_nt(kk_ref[h, pl.ds(off, nk), :], qq[h][q0:q0 + nq, :]) for h, q0, nq, nk in chains]
        for (h, q0, nq, nk), st in zip(chains, sts):
            if masked:
                kv = lax.broadcasted_iota(I32, (nk, nq), 0)
                qi = lax.broadcasted_iota(I32, (nk, nq), 1) + q0
                st = jnp.where(kv <= qi, st, NEG)
            m_old = m_ref[h, :, q0:q0 + nq]
            m_new = jnp.maximum(m_old, jnp.max(st, axis=0, keepdims=True))
            p = jnp.exp2(st - m_new).astype(BF16)
            acc_ref[h, :, q0:q0 + nq] = (acc_ref[h, :, q0:q0 + nq] * jnp.exp2(m_old - m_new)
                                          + _dot(vt_ref[h, j, :, 0:nk], p))
            m_ref[h, :, q0:q0 + nq] = m_new

    def body(j, carry):
        step(j, False)
        return carry

    lax.fori_loop(0, i, body, 0)
    step(i, True)
    for h in heads:
        acc = acc_ref[h]
        o_ref[:, col(h)] = (acc[:dh, :] / acc[dh:, :]).T.astype(BF16)


def _fox(z, aq, ak, seq):
    t = z.shape[0]
    blk = min(1024, seq)
    nq = seq // blk
    hps = FOX_HEADS_PER_STEP
    width = hps * FOX_HEAD_DIM
    unit = lambda base: base * LANES // width
    qspec = lambda base: pl.BlockSpec((blk, width), lambda b, g, i: (b * nq + i, unit(base) + g))
    kspec = lambda base: pl.BlockSpec((seq, width), lambda b, g, i: (b, unit(base) + g),
                                      pipeline_mode=pl.Buffered(1))
    return pl.pallas_call(
        functools.partial(_fox_kernel, blk=blk),
        grid=(t // seq, FOX_HEADS // hps, nq),
        in_specs=[qspec(Z_QF), qspec(0), kspec(Z_KF), kspec(0), kspec(Z_VF)],
        out_specs=qspec(0),
        out_shape=jax.ShapeDtypeStruct((t, FOX_DIM), BF16),
        scratch_shapes=[pltpu.VMEM((hps, seq, 2 * FOX_HEAD_DIM), BF16),
                        pltpu.VMEM((hps, nq, 2 * FOX_HEAD_DIM, blk), BF16),
                        pltpu.VMEM((hps, 1, blk), F32), pltpu.VMEM((hps, 2 * FOX_HEAD_DIM, blk), F32)],
        compiler_params=_cparams(3),
        name="fox",
    )(z, aq, z, ak, z)


def _gla_chunk(q, k, v, b, st, e3):
    c, nsub = GLA_CHUNK, GLA_CHUNK // GLA_SUB
    row = lax.broadcasted_iota(I32, (c, GLA_KDIM), 0)
    rblk, rloc = row // GLA_SUB, row % GLA_SUB
    lane_k = lax.broadcasted_iota(I32, (c, GLA_KDIM), 1)
    khead, kpos = lane_k // GLA_DK, lane_k % GLA_DK
    b3 = b.reshape(nsub, GLA_SUB, GLA_KDIM)
    k3 = k.reshape(nsub, GLA_SUB, GLA_KDIM)
    last = b3[:, GLA_SUB - 1:GLA_SUB, :]
    bcast = lambda a3, i: jnp.broadcast_to(a3[:, i:i + 1, :], (nsub, GLA_SUB, GLA_KDIM)).reshape(c, GLA_KDIM)

    prev = jnp.concatenate([last[:1], last[:nsub - 1]], axis=0)
    refq = jnp.broadcast_to(prev, (nsub, GLA_SUB, GLA_KDIM)).reshape(c, GLA_KDIM)
    qt = q * jnp.exp(jnp.minimum(b - refq, 0.0))
    qcat = jnp.concatenate([jnp.where(rblk == i, qt, 0.0) for i in range(1, nsub)], axis=1).astype(BF16)
    kts = []
    for i in range(1, nsub):
        kt = k * jnp.exp(jnp.minimum(last[i - 1] - b, 0.0))
        kts.append(jnp.where(row < GLA_SUB * i, kt, 0.0))
    kcat = jnp.concatenate(
        [jnp.concatenate([jnp.where(khead == h, kt, 0.0) for kt in kts], axis=1) for h in range(GLA_HEADS)],
        axis=0).astype(BF16)
    a_all = _dot_nt(qcat, kcat)

    for sl in range(GLA_SUB):
        p = q * bcast(k3, sl) * jnp.exp(jnp.minimum(b - bcast(b3, sl), 0.0))
        w = _dot(p.astype(BF16), e3)
        hit = (kpos == rblk * GLA_SUB + sl) & (rloc >= sl)
        a_all = a_all + jnp.where(hit, w, 0.0)

    lane_v = lax.broadcasted_iota(I32, (c, GLA_VDIM), 1) // GLA_DV
    vbd = jnp.concatenate([jnp.where(lane_v == h, v, jnp.zeros_like(v)) for h in range(GLA_HEADS)], axis=0)
    qe = (q * jnp.exp(b)).astype(BF16)
    o = _dot(a_all.astype(BF16), vbd) + _dot_nt(qe, st.astype(BF16))

    blast = b[c - 1:c, :]
    kd = (k * jnp.exp(blast - b)).astype(BF16)
    shead_v = lax.broadcasted_iota(I32, (GLA_VDIM, GLA_KDIM), 0) // GLA_DV
    shead_k = lax.broadcasted_iota(I32, (GLA_VDIM, GLA_KDIM), 1) // GLA_DK
    st_new = jnp.where(shead_v == shead_k, st * jnp.exp(blast) + _dot_tn(v, kd), 0.0)
    return o, st_new


def _gla_kernel(q_ref, k_ref, v_ref, r_ref, b_ref, ng_ref, e3_ref, o_ref, st_ref):
    @pl.when(pl.program_id(1) == 0)
    def _():
        st_ref[...] = jnp.zeros_like(st_ref)

    e3 = e3_ref[...]
    ng = ng_ref[...]

    def body(ci, carry):
        rows = pl.ds(pl.multiple_of(ci * GLA_CHUNK, GLA_CHUNK), GLA_CHUNK)
        o, st_new = _gla_chunk(q_ref[rows, :].astype(F32), k_ref[rows, :].astype(F32), v_ref[rows, :],
                               b_ref[rows, :], st_ref[...], e3)
        st_ref[...] = st_new
        outs = []
        for h in range(GLA_HEADS):
            oh = o[:, h * GLA_DV:(h + 1) * GLA_DV]
            outs.append(oh * lax.rsqrt(jnp.mean(oh * oh, axis=-1, keepdims=True) + EPS))
        y = jnp.concatenate(outs, axis=1) * ng * _silu(r_ref[rows, :].astype(F32))
        o_ref[rows, :] = y.astype(BF16)
        return carry

    lax.fori_loop(0, q_ref.shape[0] // GLA_CHUNK, body, 0)


def _gla(z, bc, ng, seq):
    t = z.shape[0]
    ts = min(512, seq)
    per_b = seq // ts
    lane = np.arange(GLA_KDIM)
    e3 = jnp.asarray((lane[:, None] // GLA_DK == lane[None, :] // GLA_DK).astype(np.float32), BF16)
    spec = lambda n, unit: pl.BlockSpec((ts, n), lambda b, s: (b * per_b + s, unit))
    full = lambda a: pl.BlockSpec(a.shape, lambda b, s: (0,) * a.ndim)
    return pl.pallas_call(
        _gla_kernel,
        grid=(t // seq, per_b),
        in_specs=[spec(GLA_KDIM, Z_QG * LANES // GLA_KDIM), spec(GLA_KDIM, Z_KG * LANES // GLA_KDIM),
                  spec(GLA_VDIM, Z_VG * LANES // GLA_VDIM), spec(GLA_VDIM, Z_RG * LANES // GLA_VDIM),
                  spec(GLA_KDIM, 0), full(ng), full(e3)],
        out_specs=spec(GLA_VDIM, 0),
        out_shape=jax.ShapeDtypeStruct((t, GLA_VDIM), BF16),
        scratch_shapes=[pltpu.VMEM((GLA_VDIM, GLA_KDIM), F32)],
        compiler_params=_cparams(2),
        name="gla",
    )(z, z, z, z, bc, ng, e3)


def _merge_kernel(x_ref, g1_ref, ya_ref, yb_ref, yc_ref, ga_ref, gb_ref, gc_ref,
                  wa_ref, wb_ref, wc_ref, wo_ref, o_ref):
    gate = lambda r: jax.nn.sigmoid(r[...].astype(F32))
    merged = (gate(ga_ref) * _dot(ya_ref[...], wa_ref[...]) + gate(gb_ref) * _dot(yb_ref[...], wb_ref[...])
              + gate(gc_ref) * _dot(yc_ref[...], wc_ref[...]))
    o_ref[...] = x_ref[...] + g1_ref[...] * _dot(merged.astype(BF16), wo_ref[...])


def _merge(x2, g1, ya, yb, yc, z, wa, wb, wc, wo, seq):
    t, d = x2.shape
    tm = 256
    per_b = seq // tm
    full = lambda a: pl.BlockSpec(a.shape, lambda i: (0,) * a.ndim)
    row = lambda n, unit=0: pl.BlockSpec((tm, n), lambda i: (i, unit))
    return pl.pallas_call(
        _merge_kernel,
        grid=(t // tm,),
        in_specs=[row(d), pl.BlockSpec((None, 1, d), lambda i: (i // per_b, 0, 0)),
                  row(CONV_DIM), row(FOX_DIM), row(GLA_VDIM), row(d, 0), row(d, 1), row(d, 2),
                  full(wa), full(wb), full(wc), full(wo)],
        out_specs=row(d),
        out_shape=jax.ShapeDtypeStruct((t, d), F32),
        compiler_params=_cparams(1),
        name="merge",
    )(x2, g1, ya, yb, yc, z, z, z, wa, wb, wc, wo)


def _pack_halves(y):
    n = y.shape[1] // 2
    lo = pltpu.bitcast(y[:, :n].astype(BF16).astype(F32), U32)
    hi = pltpu.bitcast(y[:, n:].astype(BF16).astype(F32), U32)
    return (hi & jnp.uint32(0xFFFF0000)) | (lo >> 16)


def _unpack_halves(p):
    lo = pltpu.bitcast(p << 16, F32)
    hi = pltpu.bitcast(p & jnp.uint32(0xFFFF0000), F32)
    return lo, hi


def _first_index(hit, idx, n):
    return jnp.min(jnp.where(hit, idx, n), axis=0, keepdims=True)


def _route_kernel(x_ref, g_ref, sc_ref, sh_ref, wrh_ref, wrl_ref, eb_ref, u_ref,
                  te_ref, tw_ref, rk_ref, cnt_ref, carry_ref):
    @pl.when(pl.program_id(0) == 0)
    def _():
        carry_ref[...] = jnp.zeros_like(carry_ref)

    x = x_ref[...]
    h = _modulated_norm(x, g_ref[...], sc_ref[...], sh_ref[...])
    hb = h.astype(BF16)

    hlo = (h - hb.astype(F32)).astype(BF16)
    logits = _dot_nt(wrh_ref[...], hb) + _dot_nt(wrh_ref[...], hlo) + _dot_nt(wrl_ref[...], hb)
    scores = jax.nn.sigmoid(logits)
    biased = scores + eb_ref[...]
    tm = x.shape[0]
    per_group = N_EXPERTS // N_GROUPS
    sub = lax.broadcasted_iota(I32, (per_group, tm), 0).astype(F32)
    gscore = []
    for g in range(N_GROUPS):
        blk = biased[g * per_group:(g + 1) * per_group, :]
        m1 = jnp.max(blk, axis=0, keepdims=True)
        rest = jnp.where(sub == _first_index(blk == m1, sub, float(per_group)), -jnp.inf, blk)
        gscore.append(m1 + jnp.max(rest, axis=0, keepdims=True))
    gs = jnp.concatenate(gscore, axis=0)
    gidx = lax.broadcasted_iota(I32, (N_GROUPS, tm), 0).astype(F32)
    gsel = jnp.zeros((N_GROUPS, tm), F32)
    for _ in range(TOPK_GROUPS):
        m = jnp.max(gs, axis=0, keepdims=True)
        hit = gidx == _first_index(gs == m, gidx, float(N_GROUPS))
        gsel = jnp.where(hit, 1.0, gsel)
        gs = jnp.where(hit, -jnp.inf, gs)
    cand = jnp.concatenate(
        [jnp.where(gsel[g:g + 1, :] > 0.5, biased[g * per_group:(g + 1) * per_group, :], -jnp.inf)
         for g in range(N_GROUPS)], axis=0)
    eidx = lax.broadcasted_iota(I32, (N_EXPERTS, tm), 0).astype(F32)
    hits, tops, tws = [], [], []
    sel = jnp.zeros((N_EXPERTS, tm), F32)
    for _ in range(TOP_K):
        m = jnp.max(cand, axis=0, keepdims=True)
        e = _first_index(cand == m, eidx, float(N_EXPERTS))
        hit = eidx == e
        hits.append(hit)
        tops.append(e)
        tws.append(jnp.sum(jnp.where(hit, scores, 0.0), axis=0, keepdims=True))
        cand = jnp.where(hit, -jnp.inf, cand)
        sel = jnp.where(hit, 1.0, sel)
    te_ref[...] = jnp.concatenate(tops, axis=0).astype(I32)
    tw = jnp.concatenate(tws, axis=0)
    tw_ref[...] = tw / jnp.sum(tw, axis=0, keepdims=True) * ROUTED_SCALE

    before = _dot(sel.astype(BF16), u_ref[...]) + carry_ref[...]
    rk_ref[...] = jnp.concatenate(
        [jnp.sum(jnp.where(hit, before, 0.0), axis=0, keepdims=True) for hit in hits], axis=0).astype(I32)
    carry = carry_ref[...] + jnp.sum(sel, axis=1, keepdims=True)
    carry_ref[...] = carry
    cnt_ref[...] = jnp.broadcast_to(carry, cnt_ref.shape).astype(I32)


MOE_TM = 256


def _route(x2, g, sc, sh, wrh, wrl, eb, seq):
    t, d = x2.shape
    tm = MOE_TM
    per_b = seq // tm
    r = np.arange(tm)
    u = jnp.asarray((r[:, None] < r[None, :]).astype(np.float32), BF16)
    full = lambda a: pl.BlockSpec(a.shape, lambda i: (0,) * a.ndim)
    vec = lambda: pl.BlockSpec((None, 1, d), lambda i: (i // per_b, 0, 0))
    colk = lambda: pl.BlockSpec((TOP_K, tm), lambda i: (0, i))
    return pl.pallas_call(
        _route_kernel,
        grid=(t // tm,),
        in_specs=[pl.BlockSpec((tm, d), lambda i: (i, 0)), full(g), vec(), vec(),
                  full(wrh), full(wrl), full(eb), full(u)],
        out_specs=[colk(), colk(), colk(), pl.BlockSpec((N_EXPERTS, LANES), lambda i: (0, 0))],
        out_shape=[jax.ShapeDtypeStruct((TOP_K, t), I32), jax.ShapeDtypeStruct((TOP_K, t), F32),
                   jax.ShapeDtypeStruct((TOP_K, t), I32), jax.ShapeDtypeStruct((N_EXPERTS, LANES), I32)],
        scratch_shapes=[pltpu.VMEM((N_EXPERTS, 1), F32)],
        compiler_params=_cparams(1),
        name="route",
    )(x2, g, sc, sh, wrh, wrl, eb, u)


ROW_TILE = 8


def _tiles_to_rows(ref, lead, first, m):
    return jnp.concatenate(
        [ref[lead + (pl.ds(first * ROW_TILE + j, m, stride=ROW_TILE), slice(None))] for j in range(ROW_TILE)], axis=1)


def _rows_to_tiles(ref, lead, val):
    m = val.shape[0]
    for j in range(ROW_TILE):
        ref[lead + (pl.ds(j, m, stride=ROW_TILE), slice(None))] = val[:, j * LANES:(j + 1) * LANES]


def _tile_rows(ref, lead, row):
    return ref.at[lead + (pl.ds(pl.multiple_of(row, ROW_TILE) if not isinstance(row, int) else row, ROW_TILE),)]


def _dispatch_kernel(pend_ref, dst_ref, x_ref, g_ref, sc_ref, sh_ref, g2_ref, ws1_ref, ws3_ref, ws2_ref,
                     base_ref, xs_hbm, hbuf, zbuf, sem, zsem, *, n_tiles):
    i = pl.program_id(0)
    odd = i % 2 == 1
    even = jnp.logical_not(odd)
    tm = x_ref.shape[0]

    @pl.when(i == 0)
    def _():
        zbuf[...] = jnp.zeros_like(zbuf)
        def clear(e, start):
            first = jnp.maximum(pend_ref[e] - MOE_BLOCK, 0) * ROW_TILE
            cp = pltpu.make_async_copy(zbuf, xs_hbm.at[pl.ds(pl.multiple_of(first, ROW_TILE), zbuf.shape[0])], zsem)
            cp.start() if start else cp.wait()
        for e in range(N_EXPERTS):
            clear(e, True)
        for e in range(N_EXPERTS):
            clear(e, False)

    n_parts = 4

    def copies(sl, start, part=None):
        for k in range(TOP_K):
            for t in range(tm):
                n = k * tm + t
                if part is not None and n * n_parts // (TOP_K * tm) != part:
                    continue
                cp = pltpu.make_async_copy(_tile_rows(hbuf, (sl,), ROW_TILE * t),
                                           _tile_rows(xs_hbm, (), dst_ref[k, t] if start else 0), sem.at[sl])
                cp.start(priority=n % 2) if start else cp.wait()

    pl.when((i >= 2) & even)(functools.partial(copies, 0, False))
    pl.when((i >= 2) & odd)(functools.partial(copies, 1, False))

    def main(sl):
        x = x_ref[...]
        h = _modulated_norm(x, g_ref[...], sc_ref[...], sh_ref[...])
        _rows_to_tiles(hbuf, (sl,), _pack_halves(h))
        copies(sl, True, part=0)
        hb = h.astype(BF16)
        up = _dot(hb, ws1_ref[...])
        copies(sl, True, part=1)
        gate = _dot(hb, ws3_ref[...])
        copies(sl, True, part=2)
        hs = (_silu(up) * gate).astype(BF16)
        base_ref[...] = x + g2_ref[...] * _dot(hs, ws2_ref[...])
        copies(sl, True, part=3)

    pl.when(even)(functools.partial(main, 0))
    pl.when(odd)(functools.partial(main, 1))

    @pl.when(i == n_tiles - 1)
    def _():
        if n_tiles >= 2:
            copies(n_tiles % 2, False)
        copies((n_tiles - 1) % 2, False)


def _dispatch(x2, g, sc, sh, g2, ws1, ws3, ws2, pends, dest, n_rows, seq):
    t, d = x2.shape
    tm = MOE_TM
    per_b = seq // tm
    full = lambda a: pl.BlockSpec(a.shape, lambda i, pe: (0,) * a.ndim)
    vec = lambda: pl.BlockSpec((None, 1, d), lambda i, pe: (i // per_b, 0, 0))
    return pl.pallas_call(
        functools.partial(_dispatch_kernel, n_tiles=t // tm),
        grid_spec=pltpu.PrefetchScalarGridSpec(
            num_scalar_prefetch=1,
            grid=(t // tm,),
            in_specs=[pl.BlockSpec((TOP_K, tm), lambda i, pe: (0, i), memory_space=pltpu.SMEM),
                      pl.BlockSpec((tm, d), lambda i, pe: (i, 0)), full(g), vec(), vec(), vec(),
                      full(ws1), full(ws3), full(ws2)],
            out_specs=[pl.BlockSpec((tm, d), lambda i, pe: (i, 0)), pl.BlockSpec(memory_space=pl.ANY)],
            scratch_shapes=[pltpu.VMEM((2, tm * ROW_TILE, LANES), U32),
                            pltpu.VMEM((MOE_BLOCK * ROW_TILE, LANES), U32),
                            pltpu.SemaphoreType.DMA((2,)), pltpu.SemaphoreType.DMA(())]),
        out_shape=[jax.ShapeDtypeStruct((t, d), F32), jax.ShapeDtypeStruct((n_rows * ROW_TILE, LANES), U32)],
        compiler_params=_cparams(1),
        name="dispatch",
    )(pends, dest, x2, g, sc, sh, g2, ws1, ws3, ws2)


def _experts_kernel(be_ref, nu_ref, nv_ref, fi_ref, od_ref, nx_ref, xs_ref, w1_hbm, w3_hbm, w2_hbm, ys_ref,
                    wf1, wf3, wf2, w1b, w3b, w2b, wsem, *, layer):
    s = pl.program_id(0)
    half = MOE_BLOCK // 2

    def fetch(e, slot, start):
        for src, dst in ((w1_hbm, wf1), (w3_hbm, wf3), (w2_hbm, wf2)):
            cp = pltpu.make_async_copy(src.at[layer, e], dst.at[slot], wsem.at[slot])
            cp.start() if start else cp.wait()

    @pl.when(s == 0)
    def _():
        fetch(be_ref[0], 0, True)

    @pl.when(fi_ref[s] == 1)
    def _():
        slot = od_ref[s] % 2
        fetch(be_ref[s], slot, False)
        w1b[...] = wf1[slot].astype(BF16)
        w3b[...] = wf3[slot].astype(BF16)
        w2b[...] = wf2[slot].astype(BF16)

        @pl.when(nx_ref[s] != be_ref[s])
        def _():
            fetch(nx_ref[s], 1 - slot, True)

    def compute(n_halves):
        xbs = []
        for p in range(n_halves):
            lo, hi = _unpack_halves(_tiles_to_rows(xs_ref, (), p * half, half))
            xbs.append(jnp.concatenate([lo.astype(BF16), hi.astype(BF16)], axis=1))
        ups = [_dot(xb, w1b[...]) for xb in xbs]
        gates = [_dot(xb, w3b[...]) for xb in xbs]
        hids = [(_silu(u) * g).astype(BF16) for u, g in zip(ups, gates)]
        packed = _pack_halves(jnp.concatenate([_dot(hd, w2b[...]) for hd in hids], axis=0))
        for j in range(ROW_TILE):
            ys_ref[pl.ds(j, n_halves * half, stride=ROW_TILE), :] = packed[:, j * LANES:(j + 1) * LANES]

    nv = nv_ref[s]
    pl.when(nv > half)(functools.partial(compute, 2))
    pl.when((nv > 0) & (nv <= half))(functools.partial(compute, 1))


def _experts(layer, block_e, n_used, n_valid, xs, w1, w3, w2):
    nb = block_e.shape[0]
    steps = jnp.arange(nb, dtype=I32)
    first = ((steps == 0) | (block_e != jnp.roll(block_e, 1))).astype(I32)
    ordinal = (jnp.cumsum(first) - 1).astype(I32)
    nxt_step = jnp.min(jnp.where((first[None, :] == 1) & (steps[None, :] > steps[:, None]), steps[None, :], nb), axis=1)
    nxt = jnp.sum(jnp.where(steps[None, :] == nxt_step[:, None], block_e[None, :], 0), axis=1)
    nxt = jnp.where(nxt_step < nb, nxt, block_e).astype(I32)
    rows = pl.BlockSpec((MOE_BLOCK * ROW_TILE, LANES), lambda s, be, nu, *_: (jnp.minimum(s, nu[0] - 1), 0))
    hbm = pl.BlockSpec(memory_space=pl.ANY)
    slot2 = lambda a: pltpu.VMEM((2,) + a.shape[2:], F32)
    return pl.pallas_call(
        functools.partial(_experts_kernel, layer=layer),
        grid_spec=pltpu.PrefetchScalarGridSpec(
            num_scalar_prefetch=6,
            grid=(nb,),
            in_specs=[rows, hbm, hbm, hbm],
            out_specs=rows,
            scratch_shapes=[slot2(w1), slot2(w3), slot2(w2),
                            pltpu.VMEM(w1.shape[2:], BF16), pltpu.VMEM(w3.shape[2:], BF16),
                            pltpu.VMEM(w2.shape[2:], BF16), pltpu.SemaphoreType.DMA((2,))]),
        out_shape=jax.ShapeDtypeStruct(xs.shape, U32),
        compiler_params=_cparams(1),
        name="experts",
    )(block_e, n_used, n_valid, first, ordinal, nxt, xs, w1, w3, w2)


def _combine_kernel(dst_ref, dstn_ref, base_ref, g2_ref, w_ref, fg_ref, ys_hbm, o_ref, gbuf, sem, *, n_tiles, final):
    i = pl.program_id(0)
    odd = i % 2 == 1
    even = jnp.logical_not(odd)
    tm = base_ref.shape[0]

    rb = 32
    n_rb = tm // rb

    def fetch(idx_ref, sl, start, part=None):
        for k in range(TOP_K):
            for t in range(tm):
                n = k * tm + t
                if part is not None and n * n_rb // (TOP_K * tm) != part:
                    continue
                cp = pltpu.make_async_copy(_tile_rows(ys_hbm, (), idx_ref[k, t] if start else 0),
                                           _tile_rows(gbuf, (sl,), ROW_TILE * n), sem.at[sl])
                cp.start(priority=n % 2) if start else cp.wait()

    @pl.when(i == 0)
    def _():
        fetch(dst_ref, 0, True)

    def main(sl):
        fetch(dst_ref, sl, False)
        half = o_ref.shape[1] // 2

        def rows_step(ib):
            r0 = ib * rb
            rows = pl.ds(r0, rb)
            w = w_ref[rows, :]
            wb = [jnp.broadcast_to(w[:, k:k + 1], (rb, LANES)) for k in range(TOP_K)]
            ssq = jnp.zeros((rb, 1), F32)
            for j in range(ROW_TILE):
                acc_lo = acc_hi = None
                for k in range(TOP_K):
                    lo, hi = _unpack_halves(gbuf[sl, pl.ds((k * tm + r0) * ROW_TILE + j, rb, stride=ROW_TILE), :])
                    acc_lo = wb[k] * lo if acc_lo is None else acc_lo + wb[k] * lo
                    acc_hi = wb[k] * hi if acc_hi is None else acc_hi + wb[k] * hi
                for acc, c0 in ((acc_lo, j * LANES), (acc_hi, half + j * LANES)):
                    x = base_ref[rows, c0:c0 + LANES] + g2_ref[:, c0:c0 + LANES] * acc
                    o_ref[rows, c0:c0 + LANES] = x
                    ssq = ssq + jnp.sum(x * x, axis=-1, keepdims=True)
            if final:
                scale = lax.rsqrt(ssq * (1.0 / o_ref.shape[1]) + EPS)
                o_ref[rows, :] = o_ref[rows, :] * scale * fg_ref[...]

        for ib in range(n_rb):
            rows_step(ib)
            fetch(dstn_ref, 1 - sl, True, part=ib)

    pl.when(even)(functools.partial(main, 0))
    pl.when(odd)(functools.partial(main, 1))

    @pl.when(i == n_tiles - 1)
    def _():
        fetch(dst_ref, n_tiles % 2, False)


def _combine(base, g2, w, ys, dest, fg, seq, final):
    t, d = base.shape
    tm = MOE_TM
    per_b = seq // tm
    n_tiles = t // tm
    idx = lambda f: pl.BlockSpec((TOP_K, tm), f, memory_space=pltpu.SMEM)
    return pl.pallas_call(
        functools.partial(_combine_kernel, n_tiles=n_tiles, final=final),
        grid=(n_tiles,),
        in_specs=[idx(lambda i: (0, i)), idx(lambda i: (0, jnp.minimum(i + 1, n_tiles - 1))),
                  pl.BlockSpec((tm, d), lambda i: (i, 0)),
                  pl.BlockSpec((None, 1, d), lambda i: (i // per_b, 0, 0)),
                  pl.BlockSpec((tm, TOP_K), lambda i: (i, 0)),
                  pl.BlockSpec((1, d), lambda i: (0, 0)),
                  pl.BlockSpec(memory_space=pl.ANY)],
        out_specs=pl.BlockSpec((tm, d), lambda i: (i, 0)),
        out_shape=jax.ShapeDtypeStruct((t, d), F32),
        scratch_shapes=[pltpu.VMEM((2, TOP_K * tm * ROW_TILE, LANES), U32), pltpu.SemaphoreType.DMA((2,))],
        compiler_params=_cparams(1),
        name="combine",
    )(dest, dest, base, g2, w, fg, ys)


def _dispatch_tables(top_e, rank, counts, t):
    n_blocks = TOP_K * t // MOE_BLOCK + N_EXPERTS
    padded = (counts + MOE_BLOCK - 1) // MOE_BLOCK * MOE_BLOCK
    pends = jnp.cumsum(padded)
    pstarts = pends - padded
    experts = jnp.arange(N_EXPERTS, dtype=I32)
    pstart_of = jnp.sum(jnp.where(top_e[..., None] == experts, pstarts, 0), axis=-1)
    dest = (pstart_of + rank) * ROW_TILE
    starts = jnp.arange(n_blocks, dtype=I32) * MOE_BLOCK
    block_e = jnp.sum((pends[None, :] <= starts[:, None]).astype(I32), axis=1)
    block_e = jnp.minimum(block_e, jnp.max(jnp.where(starts < pends[-1], block_e, 0)))
    n_used = (pends[-1:] // MOE_BLOCK).astype(I32)
    seg_end = jnp.sum(jnp.where(block_e[:, None] == experts, pstarts + counts, 0), axis=-1)
    n_valid = jnp.clip(seg_end - starts, 0, MOE_BLOCK).astype(I32)
    return block_e, n_used, n_valid, pends.astype(I32), dest, n_blocks * MOE_BLOCK


def _layout_inproj(w_in, b_in):
    o = np.cumsum([0, 2 * CONV_DIM, FOX_DIM, FOX_DIM, FOX_DIM, FOX_HEADS, GLA_KDIM, GLA_KDIM, GLA_VDIM, GLA_VDIM,
                   GLA_RANK, 3 * D_MODEL])
    seg = lambda a, i: a[..., int(o[i]):int(o[i + 1])]

    def main(a):
        return jnp.concatenate([seg(a, 10), seg(a, 0), seg(a, 1) * (FOX_HEAD_DIM ** -0.5 * LOG2E), seg(a, 2), seg(a, 3),
                                seg(a, 5) * GLA_DK ** -0.5, seg(a, 6), seg(a, 7), seg(a, 8)], axis=-1)

    def small(a):
        pad = jnp.zeros(a.shape[:-1] + (LANES - FOX_HEADS - GLA_RANK,), a.dtype)
        return jnp.concatenate([seg(a, 4), seg(a, 9), pad], axis=-1)

    return (main(w_in).astype(BF16), main(b_in)[..., None, :], small(w_in).astype(BF16),
            small(b_in)[..., None, :])


def kernel(x, c, norm1_g, ada_w, ada_b, w_in, b_in, conv_w, conv_b, conv_ln_g, conv_ln_b, gla_wa, gla_ba,
           gla_norm_g, w_branch_a, w_branch_b, w_branch_c, w_out, norm2_g, w_router, e_bias, w1, w3, w2,
           ws1, ws3, ws2, final_g):
    bsz, seq, d = x.shape
    t = bsz * seq
    depth = ada_w.shape[0]
    mod = _ada_mod(c, ada_w, ada_b).reshape(depth, bsz, 6, 1, d)
    x2 = x.reshape(t, d)
    row = lambda a: a.reshape(1, -1)
    w, b, ws, bs = _layout_inproj(w_in, b_in)
    for l in range(depth):
        sh1, sc1, g1, sh2, sc2, g2 = (mod[l, :, i] for i in range(6))
        z, zs = _inproj(l, x2, row(norm1_g[l]), sc1, sh1, w, b, ws, bs, seq)
        wa_pad = jnp.zeros((LANES, GLA_KDIM), F32).at[ZS_A:ZS_A + GLA_RANK].set(gla_wa[l])
        aq, ak, bc = _prep(zs, wa_pad, row(gla_ba[l]), seq)
        ya = _conv(z, conv_w[l].reshape(CONV_WIDTH, CONV_DIM), row(conv_b[l]), row(conv_ln_g[l]),
                   row(conv_ln_b[l]), seq)
        yb = _fox(z, aq, ak, seq)
        yc = _gla(z, bc, row(gla_norm_g[l]), seq)
        x2 = _merge(x2, g1, ya, yb, yc, z, w_branch_a[l].astype(BF16), w_branch_b[l].astype(BF16),
                    w_branch_c[l].astype(BF16), w_out[l].astype(BF16), seq)

        wr_t = w_router[l].T
        wrh = wr_t.astype(BF16)
        wrl = (wr_t - wrh.astype(F32)).astype(BF16)
        top_e, top_w, rank, counts = _route(x2, row(norm2_g[l]), sc2, sh2, wrh, wrl,
                                            e_bias[l].reshape(N_EXPERTS, 1), seq)
        block_e, n_used, n_valid, pends, dest, n_rows = _dispatch_tables(top_e, rank, counts[:, 0], t)
        base, xs = _dispatch(x2, row(norm2_g[l]), sc2, sh2, g2, ws1[l].astype(BF16), ws3[l].astype(BF16),
                             ws2[l].astype(BF16), pends, dest, n_rows, seq)
        ys = _experts(l, block_e, n_used, n_valid, xs, w1, w3, w2)
        x2 = _combine(base, g2, top_w.T, ys, dest, row(final_g), seq, final=(l == depth - 1))
    return x2.reshape(bsz, seq, d)
```

```python
import functools

import numpy as np
import jax
import jax.numpy as jnp
from jax import lax
from jax.experimental import pallas as pl
from jax.experimental.pallas import tpu as pltpu

F32 = jnp.float32
BF16 = jnp.bfloat16
U32 = jnp.uint32
I32 = jnp.int32

D_MODEL = 2048
CONV_DIM = 512
CONV_WIDTH = 31
FOX_HEADS = 8
FOX_HEAD_DIM = 128
FOX_DIM = FOX_HEADS * FOX_HEAD_DIM
GLA_HEADS = 4
GLA_DK = 64
GLA_DV = 128
GLA_KDIM = GLA_HEADS * GLA_DK
GLA_VDIM = GLA_HEADS * GLA_DV
GLA_RANK = 16
GLA_TAU = 16.0
GLA_CHUNK = 64
GLA_SUB = 8
N_EXPERTS = 64
N_GROUPS = 8
TOPK_GROUPS = 4
TOP_K = 8
EXPERT_DIM = 512
SHARED_DIM = 512
ROUTED_SCALE = 2.5
MOE_BLOCK = 512
EPS = 1e-6

LANES = 128
V7X_VMEM_LIMIT = 56 * 1024 * 1024

Z_GATE = 0
Z_CONV = 48
Z_QF = 56
Z_KF = 64
Z_VF = 72
Z_QG = 80
Z_KG = 82
Z_VG = 84
Z_RG = 88
Z_COLS = 92 * LANES
ZS_F = 0
ZS_A = 8
NEG = -1e30
LOG2E = 1.4426950408889634


def _cparams(n_axes, vmem=V7X_VMEM_LIMIT, flags=None):
    return pltpu.CompilerParams(dimension_semantics=("arbitrary",) * n_axes, vmem_limit_bytes=vmem, flags=flags)


def _log_sigmoid(x):
    return jnp.minimum(x, 0.0) - jnp.log(1.0 + jnp.exp(-jnp.abs(x)))


def _silu(x):
    return x * jax.nn.sigmoid(x)


def _split3(x):
    hi = x.astype(BF16)
    r1 = x - hi.astype(F32)
    mid = r1.astype(BF16)
    lo = (r1 - mid.astype(F32)).astype(BF16)
    return hi, mid, lo


def _dot(a, b):
    return jnp.dot(a, b, preferred_element_type=F32)


def _dot_nt(a, b):
    return lax.dot_general(a, b, (((1,), (1,)), ((), ())), preferred_element_type=F32)


def _dot_tn(a, b):
    return lax.dot_general(a, b, (((0,), (0,)), ((), ())), preferred_element_type=F32)


def _lmul_exact(l01, x):
    hi, mid, lo = _split3(x)
    return _dot(l01, hi) + _dot(l01, mid) + _dot(l01, lo)


def _dot_f32(a, b):
    a0, a1, a2 = _split3(a)
    b0, b1, b2 = _split3(b)
    return (_dot(a0, b0) + _dot(a0, b1) + _dot(a1, b0)) + (_dot(a0, b2) + _dot(a1, b1) + _dot(a2, b0))


def _ada_kernel(ct_ref, w_ref, b_ref, o_ref):
    ct = ct_ref[...]
    act = _silu(ct)
    w = w_ref[...]
    rows = [jnp.sum(w * act[:, b:b + 1], axis=0, keepdims=True) for b in range(ct.shape[1])]
    o_ref[...] = jnp.concatenate(rows, axis=0) + b_ref[...]


def _ada_mod(c, ada_w, ada_b):
    n_l, d, n = ada_w.shape
    bsz = c.shape[0]
    tn = 1024
    return pl.pallas_call(
        _ada_kernel,
        grid=(n_l, n // tn),
        in_specs=[pl.BlockSpec((d, bsz), lambda l, j: (0, 0)),
                  pl.BlockSpec((None, d, tn), lambda l, j: (l, 0, j)),
                  pl.BlockSpec((None, 1, tn), lambda l, j: (l, 0, j))],
        out_specs=pl.BlockSpec((None, bsz, tn), lambda l, j: (l, 0, j)),
        out_shape=jax.ShapeDtypeStruct((n_l, bsz, n), F32),
        compiler_params=_cparams(2),
        name="ada_mod",
    )(c.T, ada_w, ada_b.reshape(n_l, 1, n))


def _modulated_norm(x, g, sc, sh):
    ms = jnp.mean(x * x, axis=-1, keepdims=True)
    return (x * lax.rsqrt(ms + EPS) * g) * (1.0 + sc) + sh


def _inproj_kernel(x_ref, g_ref, sc_ref, sh_ref, w_ref, b_ref, ws_ref, bs_ref, z_ref, zs_ref, h_scr):
    @pl.when(pl.program_id(1) == 0)
    def _():
        h = _modulated_norm(x_ref[...], g_ref[...], sc_ref[...], sh_ref[...]).astype(BF16)
        h_scr[...] = h
        zs_ref[...] = _dot(h, ws_ref[...]) + bs_ref[...]

    z_ref[...] = (_dot(h_scr[...], w_ref[...]) + b_ref[...]).astype(BF16)


def _inproj(layer, x2, g, sc, sh, w, b, ws, bs, seq):
    t, d = x2.shape
    tm, tn = 512, Z_COLS // 4
    per_b = seq // tm
    vec = lambda: pl.BlockSpec((None, 1, d), lambda i, j: (i // per_b, 0, 0))
    return pl.pallas_call(
        _inproj_kernel,
        grid=(t // tm, Z_COLS // tn),
        in_specs=[pl.BlockSpec((tm, d), lambda i, j: (i, 0)),
                  pl.BlockSpec((1, d), lambda i, j: (0, 0)),
                  vec(), vec(),
                  pl.BlockSpec((None, d, tn), lambda i, j: (layer, 0, j)),
                  pl.BlockSpec((None, 1, tn), lambda i, j: (layer, 0, j)),
                  pl.BlockSpec((None, d, LANES), lambda i, j: (layer, 0, 0)),
                  pl.BlockSpec((None, 1, LANES), lambda i, j: (layer, 0, 0))],
        out_specs=[pl.BlockSpec((tm, tn), lambda i, j: (i, j)),
                   pl.BlockSpec((tm, LANES), lambda i, j: (i, 0))],
        out_shape=[jax.ShapeDtypeStruct((t, Z_COLS), BF16), jax.ShapeDtypeStruct((t, LANES), F32)],
        scratch_shapes=[pltpu.VMEM((tm, d), BF16)],
        compiler_params=_cparams(2),
        name="inproj",
    )(x2, g, sc, sh, w, b, ws, bs)


def _prep_kernel(zs_ref, ltri_ref, lblk_ref, pq_ref, pk_ref, cq_ref, ck_ref, wa_ref, ba_ref,
                 aq_ref, ak_ref, bc_ref, carry_ref):
    @pl.when(pl.program_id(1) == 0)
    def _():
        carry_ref[...] = jnp.zeros_like(carry_ref)

    zs = zs_ref[...]
    cum = _lmul_exact(ltri_ref[...], _log_sigmoid(zs)) + carry_ref[...]
    carry_ref[...] = cum[cum.shape[0] - 1:, :]
    hi, mid, lo = _split3(cum * LOG2E)
    aq = cq_ref[...] + _dot(hi, pq_ref[0]) + _dot(mid, pq_ref[1]) + _dot(lo, pq_ref[2])
    ak = ck_ref[...] + _dot(hi, pk_ref[0]) + _dot(mid, pk_ref[1]) + _dot(lo, pk_ref[2])
    aq_ref[...] = aq.astype(BF16)
    ak_ref[...] = ak.astype(BF16)
    la = _log_sigmoid(_dot_f32(zs, wa_ref[...]) + ba_ref[...]) * (1.0 / GLA_TAU)
    bc_ref[...] = _lmul_exact(lblk_ref[...], la)


def _prep_consts(ts):
    r = np.arange(ts)
    ltri = (r[:, None] >= r[None, :]).astype(np.float32)
    lblk = ltri * (r[:, None] // GLA_CHUNK == r[None, :] // GLA_CHUNK)
    pq = np.zeros((3, LANES, FOX_DIM), np.float32)
    pk = np.zeros((3, LANES, FOX_DIM), np.float32)
    cq = np.zeros((1, FOX_DIM), np.float32)
    ck = np.zeros((1, FOX_DIM), np.float32)
    for h in range(FOX_HEADS):
        for p in range(3):
            pq[p, ZS_F + h, h * FOX_HEAD_DIM + p] = 1.0
            pk[p, ZS_F + h, h * FOX_HEAD_DIM + 3 + p] = -1.0
            cq[0, h * FOX_HEAD_DIM + 3 + p] = 1.0
            ck[0, h * FOX_HEAD_DIM + p] = 1.0
    return (jnp.asarray(ltri, BF16), jnp.asarray(lblk, BF16), jnp.asarray(pq, BF16), jnp.asarray(pk, BF16),
            jnp.asarray(cq), jnp.asarray(ck))


def _prep(zs, wa_pad, ba, seq):
    t = zs.shape[0]
    ts = 256
    per_b = seq // ts
    ltri, lblk, pq, pk, cq, ck = _prep_consts(ts)
    full = lambda a: pl.BlockSpec(a.shape, lambda b, s: (0,) * a.ndim)
    row = lambda n: pl.BlockSpec((ts, n), lambda b, s: (b * per_b + s, 0))
    return pl.pallas_call(
        _prep_kernel,
        grid=(t // seq, per_b),
        in_specs=[row(LANES), full(ltri), full(lblk), full(pq), full(pk), full(cq), full(ck),
                  full(wa_pad), full(ba)],
        out_specs=[row(FOX_DIM), row(FOX_DIM), row(GLA_KDIM)],
        out_shape=[jax.ShapeDtypeStruct((t, FOX_DIM), BF16), jax.ShapeDtypeStruct((t, FOX_DIM), BF16),
                   jax.ShapeDtypeStruct((t, GLA_KDIM), F32)],
        scratch_shapes=[pltpu.VMEM((1, LANES), F32)],
        compiler_params=_cparams(2),
        name="prep",
    )(zs, ltri, lblk, pq, pk, cq, ck, wa_pad, ba)


CONV_HIST = 32


def _conv_kernel(u_ref, w_ref, cb_ref, lg_ref, lb_ref, o_ref, ypad_ref, shift_ref):
    ts = u_ref.shape[0]
    sub = 8

    @pl.when(pl.program_id(1) == 0)
    def _():
        ypad_ref[0:CONV_HIST, :] = jnp.zeros((CONV_HIST, CONV_DIM), F32)

    u = u_ref[...].astype(F32)
    ypad_ref[CONV_HIST:CONV_HIST + ts, :] = u[:, :CONV_DIM] * jax.nn.sigmoid(u[:, CONV_DIM:])
    span = ts + CONV_HIST - sub
    for r in range(1, sub):
        shift_ref[r - 1] = ypad_ref[r:r + span, :]
    acc = jnp.zeros((ts, CONV_DIM), F32) + cb_ref[...]
    first = CONV_HIST - (CONV_WIDTH - 1)
    for j in range(CONV_WIDTH):
        r = (first + j) % sub
        base = first + j - r
        rows = ypad_ref[base:base + ts, :] if r == 0 else shift_ref[r - 1, base:base + ts, :]
        acc = acc + w_ref[j:j + 1, :] * rows
    ypad_ref[0:CONV_HIST, :] = ypad_ref[ts:ts + CONV_HIST, :]
    mu = jnp.mean(acc, axis=-1, keepdims=True)
    cen = acc - mu
    var = jnp.mean(cen * cen, axis=-1, keepdims=True)
    y = cen * lax.rsqrt(var + EPS) * lg_ref[...] + lb_ref[...]
    o_ref[...] = _silu(y).astype(BF16)


def _conv(z, w, cb, lg, lb, seq):
    t = z.shape[0]
    ts = 512
    per_b = seq // ts
    full = lambda a: pl.BlockSpec(a.shape, lambda b, s: (0,) * a.ndim)
    return pl.pallas_call(
        _conv_kernel,
        grid=(t // seq, per_b),
        in_specs=[pl.BlockSpec((ts, 2 * CONV_DIM), lambda b, s: (b * per_b + s, Z_CONV * LANES // (2 * CONV_DIM))),
                  full(w), full(cb), full(lg), full(lb)],
        out_specs=pl.BlockSpec((ts, CONV_DIM), lambda b, s: (b * per_b + s, 0)),
        out_shape=jax.ShapeDtypeStruct((t, CONV_DIM), BF16),
        scratch_shapes=[pltpu.VMEM((ts + CONV_HIST, CONV_DIM), F32),
                        pltpu.VMEM((7, ts + CONV_HIST - 8, CONV_DIM), F32)],
        compiler_params=_cparams(2),
        name="conv",
    )(z, w, cb, lg, lb)


FOX_HEADS_PER_STEP = 2


def _fox_kernel(q_ref, aq_ref, k_ref, ak_ref, v_ref, o_ref, kk_ref, vt_ref, m_ref, acc_ref, *, blk):
    i = pl.program_id(2)
    dh = FOX_HEAD_DIM
    heads = range(FOX_HEADS_PER_STEP)
    col = lambda h: slice(h * dh, (h + 1) * dh)

    @pl.when(i == 0)
    def _():
        def fill(cb, carry):
            rows = pl.ds(pl.multiple_of(cb * blk, blk), blk)
            for h in heads:
                kk_ref[h, rows, 0:dh] = k_ref[rows, col(h)]
                kk_ref[h, rows, dh:2 * dh] = ak_ref[rows, col(h)]
                vt_ref[h, cb, 0:dh, :] = v_ref[rows, col(h)].astype(F32).T.astype(BF16)
                vt_ref[h, cb, dh:2 * dh, :] = jnp.ones((dh, blk), BF16)
            return carry
        lax.fori_loop(0, k_ref.shape[0] // blk, fill, 0)

    qq = [jnp.concatenate([q_ref[:, col(h)], aq_ref[:, col(h)]], axis=1) for h in heads]
    m_ref[...] = jnp.full(m_ref.shape, NEG, F32)
    acc_ref[...] = jnp.zeros(acc_ref.shape, F32)

    def step(j, masked):
        off = pl.multiple_of(j * blk, blk)
        parts = [(0, blk, blk)] if not masked else [(0, blk // 2, blk // 2), (blk // 2, blk // 2, blk)]
        chains = [(h, q0, nq, nk) for h in heads for q0, nq, nk in parts]
        sts = [_dot_nt(kk_ref[h, pl.ds(off, nk), :], qq[h][q0:q0 + nq, :]) for h, q0, nq, nk in chains]
        for (h, q0, nq, nk), st in zip(chains, sts):
            if masked:
                kv = lax.broadcasted_iota(I32, (nk, nq), 0)
                qi = lax.broadcasted_iota(I32, (nk, nq), 1) + q0
                st = jnp.where(kv <= qi, st, NEG)
            m_old = m_ref[h, :, q0:q0 + nq]
            m_new = jnp.maximum(m_old, jnp.max(st, axis=0, keepdims=True))
            p = jnp.exp2(st - m_new).astype(BF16)
            acc_ref[h, :, q0:q0 + nq] = (acc_ref[h, :, q0:q0 + nq] * jnp.exp2(m_old - m_new)
                                          + _dot(vt_ref[h, j, :, 0:nk], p))
            m_ref[h, :, q0:q0 + nq] = m_new

    def body(j, carry):
        step(j, False)
        return carry

    lax.fori_loop(0, i, body, 0)
    step(i, True)
    for h in heads:
        acc = acc_ref[h]
        o_ref[:, col(h)] = (acc[:dh, :] / acc[dh:, :]).T.astype(BF16)


def _fox(z, aq, ak, seq):
    t = z.shape[0]
    blk = min(1024, seq)
    nq = seq // blk
    hps = FOX_HEADS_PER_STEP
    width = hps * FOX_HEAD_DIM
    unit = lambda base: base * LANES // width
    qspec = lambda base: pl.BlockSpec((blk, width), lambda b, g, i: (b * nq + i, unit(base) + g))
    kspec = lambda base: pl.BlockSpec((seq, width), lambda b, g, i: (b, unit(base) + g),
                                      pipeline_mode=pl.Buffered(1))
    return pl.pallas_call(
        functools.partial(_fox_kernel, blk=blk),
        grid=(t // seq, FOX_HEADS // hps, nq),
        in_specs=[qspec(Z_QF), qspec(0), kspec(Z_KF), kspec(0), kspec(Z_VF)],
        out_specs=qspec(0),
        out_shape=jax.ShapeDtypeStruct((t, FOX_DIM), BF16),
        scratch_shapes=[pltpu.VMEM((hps, seq, 2 * FOX_HEAD_DIM), BF16),
                        pltpu.VMEM((hps, nq, 2 * FOX_HEAD_DIM, blk), BF16),
                        pltpu.VMEM((hps, 1, blk), F32), pltpu.VMEM((hps, 2 * FOX_HEAD_DIM, blk), F32)],
        compiler_params=_cparams(3),
        name="fox",
    )(z, aq, z, ak, z)


def _gla_chunk(q, k, v, b, st, e3):
    c, nsub = GLA_CHUNK, GLA_CHUNK // GLA_SUB
    row = lax.broadcasted_iota(I32, (c, GLA_KDIM), 0)
    rblk, rloc = row // GLA_SUB, row % GLA_SUB
    lane_k = lax.broadcasted_iota(I32, (c, GLA_KDIM), 1)
    khead, kpos = lane_k // GLA_DK, lane_k % GLA_DK
    b3 = b.reshape(nsub, GLA_SUB, GLA_KDIM)
    k3 = k.reshape(nsub, GLA_SUB, GLA_KDIM)
    last = b3[:, GLA_SUB - 1:GLA_SUB, :]
    bcast = lambda a3, i: jnp.broadcast_to(a3[:, i:i + 1, :], (nsub, GLA_SUB, GLA_KDIM)).reshape(c, GLA_KDIM)

    prev = jnp.concatenate([last[:1], last[:nsub - 1]], axis=0)
    refq = jnp.broadcast_to(prev, (nsub, GLA_SUB, GLA_KDIM)).reshape(c, GLA_KDIM)
    qt = q * jnp.exp(jnp.minimum(b - refq, 0.0))
    qcat = jnp.concatenate([jnp.where(rblk == i, qt, 0.0) for i in range(1, nsub)], axis=1).astype(BF16)
    kts = []
    for i in range(1, nsub):
        kt = k * jnp.exp(jnp.minimum(last[i - 1] - b, 0.0))
        kts.append(jnp.where(row < GLA_SUB * i, kt, 0.0))
    kcat = jnp.concatenate(
        [jnp.concatenate([jnp.where(khead == h, kt, 0.0) for kt in kts], axis=1) for h in range(GLA_HEADS)],
        axis=0).astype(BF16)
    a_all = _dot_nt(qcat, kcat)

    for sl in range(GLA_SUB):
        p = q * bcast(k3, sl) * jnp.exp(jnp.minimum(b - bcast(b3, sl), 0.0))
        w = _dot(p.astype(BF16), e3)
        hit = (kpos == rblk * GLA_SUB + sl) & (rloc >= sl)
        a_all = a_all + jnp.where(hit, w, 0.0)

    lane_v = lax.broadcasted_iota(I32, (c, GLA_VDIM), 1) // GLA_DV
    vbd = jnp.concatenate([jnp.where(lane_v == h, v, jnp.zeros_like(v)) for h in range(GLA_HEADS)], axis=0)
    qe = (q * jnp.exp(b)).astype(BF16)
    o = _dot(a_all.astype(BF16), vbd) + _dot_nt(qe, st.astype(BF16))

    blast = b[c - 1:c, :]
    kd = (k * jnp.exp(blast - b)).astype(BF16)
    shead_v = lax.broadcasted_iota(I32, (GLA_VDIM, GLA_KDIM), 0) // GLA_DV
    shead_k = lax.broadcasted_iota(I32, (GLA_VDIM, GLA_KDIM), 1) // GLA_DK
    st_new = jnp.where(shead_v == shead_k, st * jnp.exp(blast) + _dot_tn(v, kd), 0.0)
    return o, st_new


def _gla_kernel(q_ref, k_ref, v_ref, r_ref, b_ref, ng_ref, e3_ref, o_ref, st_ref):
    @pl.when(pl.program_id(1) == 0)
    def _():
        st_ref[...] = jnp.zeros_like(st_ref)

    e3 = e3_ref[...]
    ng = ng_ref[...]

    def body(ci, carry):
        rows = pl.ds(pl.multiple_of(ci * GLA_CHUNK, GLA_CHUNK), GLA_CHUNK)
        o, st_new = _gla_chunk(q_ref[rows, :].astype(F32), k_ref[rows, :].astype(F32), v_ref[rows, :],
                               b_ref[rows, :], st_ref[...], e3)
        st_ref[...] = st_new
        outs = []
        for h in range(GLA_HEADS):
            oh = o[:, h * GLA_DV:(h + 1) * GLA_DV]
            outs.append(oh * lax.rsqrt(jnp.mean(oh * oh, axis=-1, keepdims=True) + EPS))
        y = jnp.concatenate(outs, axis=1) * ng * _silu(r_ref[rows, :].astype(F32))
        o_ref[rows, :] = y.astype(BF16)
        return carry

    lax.fori_loop(0, q_ref.shape[0] // GLA_CHUNK, body, 0)


def _gla(z, bc, ng, seq):
    t = z.shape[0]
    ts = min(512, seq)
    per_b = seq // ts
    lane = np.arange(GLA_KDIM)
    e3 = jnp.asarray((lane[:, None] // GLA_DK == lane[None, :] // GLA_DK).astype(np.float32), BF16)
    spec = lambda n, unit: pl.BlockSpec((ts, n), lambda b, s: (b * per_b + s, unit))
    full = lambda a: pl.BlockSpec(a.shape, lambda b, s: (0,) * a.ndim)
    return pl.pallas_call(
        _gla_kernel,
        grid=(t // seq, per_b),
        in_specs=[spec(GLA_KDIM, Z_QG * LANES // GLA_KDIM), spec(GLA_KDIM, Z_KG * LANES // GLA_KDIM),
                  spec(GLA_VDIM, Z_VG * LANES // GLA_VDIM), spec(GLA_VDIM, Z_RG * LANES // GLA_VDIM),
                  spec(GLA_KDIM, 0), full(ng), full(e3)],
        out_specs=spec(GLA_VDIM, 0),
        out_shape=jax.ShapeDtypeStruct((t, GLA_VDIM), BF16),
        scratch_shapes=[pltpu.VMEM((GLA_VDIM, GLA_KDIM), F32)],
        compiler_params=_cparams(2),
        name="gla",
    )(z, z, z, z, bc, ng, e3)


def _merge_kernel(x_ref, g1_ref, ya_ref, yb_ref, yc_ref, ga_ref, gb_ref, gc_ref,
                  wa_ref, wb_ref, wc_ref, wo_ref, o_ref):
    gate = lambda r: jax.nn.sigmoid(r[...].astype(F32))
    merged = (gate(ga_ref) * _dot(ya_ref[...], wa_ref[...]) + gate(gb_ref) * _dot(yb_ref[...], wb_ref[...])
              + gate(gc_ref) * _dot(yc_ref[...], wc_ref[...]))
    o_ref[...] = x_ref[...] + g1_ref[...] * _dot(merged.astype(BF16), wo_ref[...])


def _merge(x2, g1, ya, yb, yc, z, wa, wb, wc, wo, seq):
    t, d = x2.shape
    tm = 256
    per_b = seq // tm
    full = lambda a: pl.BlockSpec(a.shape, lambda i: (0,) * a.ndim)
    row = lambda n, unit=0: pl.BlockSpec((tm, n), lambda i: (i, unit))
    return pl.pallas_call(
        _merge_kernel,
        grid=(t // tm,),
        in_specs=[row(d), pl.BlockSpec((None, 1, d), lambda i: (i // per_b, 0, 0)),
                  row(CONV_DIM), row(FOX_DIM), row(GLA_VDIM), row(d, 0), row(d, 1), row(d, 2),
                  full(wa), full(wb), full(wc), full(wo)],
        out_specs=row(d),
        out_shape=jax.ShapeDtypeStruct((t, d), F32),
        compiler_params=_cparams(1),
        name="merge",
    )(x2, g1, ya, yb, yc, z, z, z, wa, wb, wc, wo)


def _pack_halves(y):
    n = y.shape[1] // 2
    lo = pltpu.bitcast(y[:, :n].astype(BF16).astype(F32), U32)
    hi = pltpu.bitcast(y[:, n:].astype(BF16).astype(F32), U32)
    return (hi & jnp.uint32(0xFFFF0000)) | (lo >> 16)


def _unpack_halves(p):
    lo = pltpu.bitcast(p << 16, F32)
    hi = pltpu.bitcast(p & jnp.uint32(0xFFFF0000), F32)
    return lo, hi


def _first_index(hit, idx, n):
    return jnp.min(jnp.where(hit, idx, n), axis=0, keepdims=True)


def _route_kernel(x_ref, g_ref, sc_ref, sh_ref, wrh_ref, wrl_ref, eb_ref, u_ref,
                  te_ref, tw_ref, rk_ref, cnt_ref, carry_ref):
    @pl.when(pl.program_id(0) == 0)
    def _():
        carry_ref[...] = jnp.zeros_like(carry_ref)

    x = x_ref[...]
    h = _modulated_norm(x, g_ref[...], sc_ref[...], sh_ref[...])
    hb = h.astype(BF16)

    hlo = (h - hb.astype(F32)).astype(BF16)
    logits = _dot_nt(wrh_ref[...], hb) + _dot_nt(wrh_ref[...], hlo) + _dot_nt(wrl_ref[...], hb)
    scores = jax.nn.sigmoid(logits)
    biased = scores + eb_ref[...]
    tm = x.shape[0]
    per_group = N_EXPERTS // N_GROUPS
    sub = lax.broadcasted_iota(I32, (per_group, tm), 0).astype(F32)
    gscore = []
    for g in range(N_GROUPS):
        blk = biased[g * per_group:(g + 1) * per_group, :]
        m1 = jnp.max(blk, axis=0, keepdims=True)
        rest = jnp.where(sub == _first_index(blk == m1, sub, float(per_group)), -jnp.inf, blk)
        gscore.append(m1 + jnp.max(rest, axis=0, keepdims=True))
    gs = jnp.concatenate(gscore, axis=0)
    gidx = lax.broadcasted_iota(I32, (N_GROUPS, tm), 0).astype(F32)
    gsel = jnp.zeros((N_GROUPS, tm), F32)
    for _ in range(TOPK_GROUPS):
        m = jnp.max(gs, axis=0, keepdims=True)
        hit = gidx == _first_index(gs == m, gidx, float(N_GROUPS))
        gsel = jnp.where(hit, 1.0, gsel)
        gs = jnp.where(hit, -jnp.inf, gs)
    cand = jnp.concatenate(
        [jnp.where(gsel[g:g + 1, :] > 0.5, biased[g * per_group:(g + 1) * per_group, :], -jnp.inf)
         for g in range(N_GROUPS)], axis=0)
    eidx = lax.broadcasted_iota(I32, (N_EXPERTS, tm), 0).astype(F32)
    hits, tops, tws = [], [], []
    sel = jnp.zeros((N_EXPERTS, tm), F32)
    for _ in range(TOP_K):
        m = jnp.max(cand, axis=0, keepdims=True)
        e = _first_index(cand == m, eidx, float(N_EXPERTS))
        hit = eidx == e
        hits.append(hit)
        tops.append(e)
        tws.append(jnp.sum(jnp.where(hit, scores, 0.0), axis=0, keepdims=True))
        cand = jnp.where(hit, -jnp.inf, cand)
        sel = jnp.where(hit, 1.0, sel)
    te_ref[...] = jnp.concatenate(tops, axis=0).astype(I32)
    tw = jnp.concatenate(tws, axis=0)
    tw_ref[...] = tw / jnp.sum(tw, axis=0, keepdims=True) * ROUTED_SCALE

    before = _dot(sel.astype(BF16), u_ref[...]) + carry_ref[...]
    rk_ref[...] = jnp.concatenate(
        [jnp.sum(jnp.where(hit, before, 0.0), axis=0, keepdims=True) for hit in hits], axis=0).astype(I32)
    carry = carry_ref[...] + jnp.sum(sel, axis=1, keepdims=True)
    carry_ref[...] = carry
    cnt_ref[...] = jnp.broadcast_to(carry, cnt_ref.shape).astype(I32)


MOE_TM = 256


def _route(x2, g, sc, sh, wrh, wrl, eb, seq):
    t, d = x2.shape
    tm = MOE_TM
    per_b = seq // tm
    r = np.arange(tm)
    u = jnp.asarray((r[:, None] < r[None, :]).astype(np.float32), BF16)
    full = lambda a: pl.BlockSpec(a.shape, lambda i: (0,) * a.ndim)
    vec = lambda: pl.BlockSpec((None, 1, d), lambda i: (i // per_b, 0, 0))
    colk = lambda: pl.BlockSpec((TOP_K, tm), lambda i: (0, i))
    return pl.pallas_call(
        _route_kernel,
        grid=(t // tm,),
        in_specs=[pl.BlockSpec((tm, d), lambda i: (i, 0)), full(g), vec(), vec(),
                  full(wrh), full(wrl), full(eb), full(u)],
        out_specs=[colk(), colk(), colk(), pl.BlockSpec((N_EXPERTS, LANES), lambda i: (0, 0))],
        out_shape=[jax.ShapeDtypeStruct((TOP_K, t), I32), jax.ShapeDtypeStruct((TOP_K, t), F32),
                   jax.ShapeDtypeStruct((TOP_K, t), I32), jax.ShapeDtypeStruct((N_EXPERTS, LANES), I32)],
        scratch_shapes=[pltpu.VMEM((N_EXPERTS, 1), F32)],
        compiler_params=_cparams(1),
        name="route",
    )(x2, g, sc, sh, wrh, wrl, eb, u)


ROW_TILE = 8


def _tiles_to_rows(ref, lead, first, m):
    return jnp.concatenate(
        [ref[lead + (pl.ds(first * ROW_TILE + j, m, stride=ROW_TILE), slice(None))] for j in range(ROW_TILE)], axis=1)


def _rows_to_tiles(ref, lead, val):
    m = val.shape[0]
    for j in range(ROW_TILE):
        ref[lead + (pl.ds(j, m, stride=ROW_TILE), slice(None))] = val[:, j * LANES:(j + 1) * LANES]


def _tile_rows(ref, lead, row):
    return ref.at[lead + (pl.ds(pl.multiple_of(row, ROW_TILE) if not isinstance(row, int) else row, ROW_TILE),)]


def _dispatch_kernel(pend_ref, dst_ref, x_ref, g_ref, sc_ref, sh_ref, g2_ref, ws1_ref, ws3_ref, ws2_ref,
                     base_ref, xs_hbm, hbuf, zbuf, sem, zsem, *, n_tiles):
    i = pl.program_id(0)
    odd = i % 2 == 1
    even = jnp.logical_not(odd)
    tm = x_ref.shape[0]

    @pl.when(i == 0)
    def _():
        zbuf[...] = jnp.zeros_like(zbuf)
        def clear(e, start):
            first = jnp.maximum(pend_ref[e] - MOE_BLOCK, 0) * ROW_TILE
            cp = pltpu.make_async_copy(zbuf, xs_hbm.at[pl.ds(pl.multiple_of(first, ROW_TILE), zbuf.shape[0])], zsem)
            cp.start() if start else cp.wait()
        for e in range(N_EXPERTS):
            clear(e, True)
        for e in range(N_EXPERTS):
            clear(e, False)

    n_parts = 4

    def copies(sl, start, part=None):
        for k in range(TOP_K):
            for t in range(tm):
                n = k * tm + t
                if part is not None and n * n_parts // (TOP_K * tm) != part:
                    continue
                cp = pltpu.make_async_copy(_tile_rows(hbuf, (sl,), ROW_TILE * t),
                                           _tile_rows(xs_hbm, (), dst_ref[k, t] if start else 0), sem.at[sl])
                cp.start(priority=n % 2) if start else cp.wait()

    pl.when((i >= 2) & even)(functools.partial(copies, 0, False))
    pl.when((i >= 2) & odd)(functools.partial(copies, 1, False))

    def main(sl):
        x = x_ref[...]
        h = _modulated_norm(x, g_ref[...], sc_ref[...], sh_ref[...])
        _rows_to_tiles(hbuf, (sl,), _pack_halves(h))
        copies(sl, True, part=0)
        hb = h.astype(BF16)
        up = _dot(hb, ws1_ref[...])
        copies(sl, True, part=1)
        gate = _dot(hb, ws3_ref[...])
        copies(sl, True, part=2)
        hs = (_silu(up) * gate).astype(BF16)
        base_ref[...] = x + g2_ref[...] * _dot(hs, ws2_ref[...])
        copies(sl, True, part=3)

    pl.when(even)(functools.partial(main, 0))
    pl.when(odd)(functools.partial(main, 1))

    @pl.when(i == n_tiles - 1)
    def _():
        if n_tiles >= 2:
            copies(n_tiles % 2, False)
        copies((n_tiles - 1) % 2, False)


def _dispatch(x2, g, sc, sh, g2, ws1, ws3, ws2, pends, dest, n_rows, seq):
    t, d = x2.shape
    tm = MOE_TM
    per_b = seq // tm
    full = lambda a: pl.BlockSpec(a.shape, lambda i, pe: (0,) * a.ndim)
    vec = lambda: pl.BlockSpec((None, 1, d), lambda i, pe: (i // per_b, 0, 0))
    return pl.pallas_call(
        functools.partial(_dispatch_kernel, n_tiles=t // tm),
        grid_spec=pltpu.PrefetchScalarGridSpec(
            num_scalar_prefetch=1,
            grid=(t // tm,),
            in_specs=[pl.BlockSpec((TOP_K, tm), lambda i, pe: (0, i), memory_space=pltpu.SMEM),
                      pl.BlockSpec((tm, d), lambda i, pe: (i, 0)), full(g), vec(), vec(), vec(),
                      full(ws1), full(ws3), full(ws2)],
            out_specs=[pl.BlockSpec((tm, d), lambda i, pe: (i, 0)), pl.BlockSpec(memory_space=pl.ANY)],
            scratch_shapes=[pltpu.VMEM((2, tm * ROW_TILE, LANES), U32),
                            pltpu.VMEM((MOE_BLOCK * ROW_TILE, LANES), U32),
                            pltpu.SemaphoreType.DMA((2,)), pltpu.SemaphoreType.DMA(())]),
        out_shape=[jax.ShapeDtypeStruct((t, d), F32), jax.ShapeDtypeStruct((n_rows * ROW_TILE, LANES), U32)],
        compiler_params=_cparams(1),
        name="dispatch",
    )(pends, dest, x2, g, sc, sh, g2, ws1, ws3, ws2)


def _experts_kernel(be_ref, nu_ref, nv_ref, fi_ref, od_ref, nx_ref, xs_ref, w1_hbm, w3_hbm, w2_hbm, ys_ref,
                    wf1, wf3, wf2, w1b, w3b, w2b, wsem, *, layer):
    s = pl.program_id(0)
    half = MOE_BLOCK // 2

    def fetch(e, slot, start):
        for src, dst in ((w1_hbm, wf1), (w3_hbm, wf3), (w2_hbm, wf2)):
            cp = pltpu.make_async_copy(src.at[layer, e], dst.at[slot], wsem.at[slot])
            cp.start() if start else cp.wait()

    @pl.when(s == 0)
    def _():
        fetch(be_ref[0], 0, True)

    def change(slot):
        fetch(be_ref[s], slot, False)
        w1b[...] = wf1[slot].astype(BF16)
        w3b[...] = wf3[slot].astype(BF16)
        w2b[...] = wf2[slot].astype(BF16)

        @pl.when(nx_ref[s] != be_ref[s])
        def _():
            fetch(nx_ref[s], 1 - slot, True)

    pl.when((fi_ref[s] == 1) & (od_ref[s] % 2 == 0))(functools.partial(change, 0))
    pl.when((fi_ref[s] == 1) & (od_ref[s] % 2 == 1))(functools.partial(change, 1))

    def compute(n_halves):
        xbs = []
        for p in range(n_halves):
            lo, hi = _unpack_halves(_tiles_to_rows(xs_ref, (), p * half, half))
            xbs.append(jnp.concatenate([lo.astype(BF16), hi.astype(BF16)], axis=1))
        ups = [_dot(xb, w1b[...]) for xb in xbs]
        gates = [_dot(xb, w3b[...]) for xb in xbs]
        hids = [(_silu(u) * g).astype(BF16) for u, g in zip(ups, gates)]
        packed = _pack_halves(jnp.concatenate([_dot(hd, w2b[...]) for hd in hids], axis=0))
        for j in range(ROW_TILE):
            ys_ref[pl.ds(j, n_halves * half, stride=ROW_TILE), :] = packed[:, j * LANES:(j + 1) * LANES]

    nv = nv_ref[s]
    pl.when(nv > half)(functools.partial(compute, 2))
    pl.when((nv > 0) & (nv <= half))(functools.partial(compute, 1))


def _experts(layer, block_e, n_used, n_valid, xs, w1, w3, w2):
    nb = block_e.shape[0]
    steps = jnp.arange(nb, dtype=I32)
    first = ((steps == 0) | (block_e != jnp.roll(block_e, 1))).astype(I32)
    ordinal = (jnp.cumsum(first) - 1).astype(I32)
    nxt_step = jnp.min(jnp.where((first[None, :] == 1) & (steps[None, :] > steps[:, None]), steps[None, :], nb), axis=1)
    nxt = jnp.sum(jnp.where(steps[None, :] == nxt_step[:, None], block_e[None, :], 0), axis=1)
    nxt = jnp.where(nxt_step < nb, nxt, block_e).astype(I32)
    rows = pl.BlockSpec((MOE_BLOCK * ROW_TILE, LANES), lambda s, be, nu, *_: (jnp.minimum(s, nu[0] - 1), 0))
    hbm = pl.BlockSpec(memory_space=pl.ANY)
    slot2 = lambda a: pltpu.VMEM((2,) + a.shape[2:], F32)
    return pl.pallas_call(
        functools.partial(_experts_kernel, layer=layer),
        grid_spec=pltpu.PrefetchScalarGridSpec(
            num_scalar_prefetch=6,
            grid=(nb,),
            in_specs=[rows, hbm, hbm, hbm],
            out_specs=rows,
            scratch_shapes=[slot2(w1), slot2(w3), slot2(w2),
                            pltpu.VMEM(w1.shape[2:], BF16), pltpu.VMEM(w3.shape[2:], BF16),
                            pltpu.VMEM(w2.shape[2:], BF16), pltpu.SemaphoreType.DMA((2,))]),
        out_shape=jax.ShapeDtypeStruct(xs.shape, U32),
        compiler_params=_cparams(1),
        name="experts",
    )(block_e, n_used, n_valid, first, ordinal, nxt, xs, w1, w3, w2)


def _combine_kernel(dst_ref, dstn_ref, base_ref, g2_ref, w_ref, fg_ref, ys_hbm, o_ref, gbuf, sem, *, n_tiles, final):
    i = pl.program_id(0)
    odd = i % 2 == 1
    even = jnp.logical_not(odd)
    tm = base_ref.shape[0]

    rb = 32
    n_rb = tm // rb

    def fetch(idx_ref, sl, start, part=None):
        for k in range(TOP_K):
            for t in range(tm):
                n = k * tm + t
                if part is not None and n * n_rb // (TOP_K * tm) != part:
                    continue
                cp = pltpu.make_async_copy(_tile_rows(ys_hbm, (), idx_ref[k, t] if start else 0),
                                           _tile_rows(gbuf, (sl,), ROW_TILE * n), sem.at[sl])
                cp.start(priority=n % 2) if start else cp.wait()

    @pl.when(i == 0)
    def _():
        fetch(dst_ref, 0, True)

    def main(sl):
        fetch(dst_ref, sl, False)
        half = o_ref.shape[1] // 2

        def rows_step(ib):
            r0 = ib * rb
            rows = pl.ds(r0, rb)
            w = w_ref[rows, :]
            wb = [jnp.broadcast_to(w[:, k:k + 1], (rb, LANES)) for k in range(TOP_K)]
            ssq = jnp.zeros((rb, 1), F32)
            for j in range(ROW_TILE):
                acc_lo = acc_hi = None
                for k in range(TOP_K):
                    lo, hi = _unpack_halves(gbuf[sl, pl.ds((k * tm + r0) * ROW_TILE + j, rb, stride=ROW_TILE), :])
                    acc_lo = wb[k] * lo if acc_lo is None else acc_lo + wb[k] * lo
                    acc_hi = wb[k] * hi if acc_hi is None else acc_hi + wb[k] * hi
                for acc, c0 in ((acc_lo, j * LANES), (acc_hi, half + j * LANES)):
                    x = base_ref[rows, c0:c0 + LANES] + g2_ref[:, c0:c0 + LANES] * acc
                    o_ref[rows, c0:c0 + LANES] = x
                    ssq = ssq + jnp.sum(x * x, axis=-1, keepdims=True)
            if final:
                scale = lax.rsqrt(ssq * (1.0 / o_ref.shape[1]) + EPS)
                o_ref[rows, :] = o_ref[rows, :] * scale * fg_ref[...]

        for ib in range(n_rb):
            rows_step(ib)
            fetch(dstn_ref, 1 - sl, True, part=ib)

    pl.when(even)(functools.partial(main, 0))
    pl.when(odd)(functools.partial(main, 1))

    @pl.when(i == n_tiles - 1)
    def _():
        fetch(dst_ref, n_tiles % 2, False)


def _combine(base, g2, w, ys, dest, fg, seq, final):
    t, d = base.shape
    tm = MOE_TM
    per_b = seq // tm
    n_tiles = t // tm
    idx = lambda f: pl.BlockSpec((TOP_K, tm), f, memory_space=pltpu.SMEM)
    return pl.pallas_call(
        functools.partial(_combine_kernel, n_tiles=n_tiles, final=final),
        grid=(n_tiles,),
        in_specs=[idx(lambda i: (0, i)), idx(lambda i: (0, jnp.minimum(i + 1, n_tiles - 1))),
                  pl.BlockSpec((tm, d), lambda i: (i, 0)),
                  pl.BlockSpec((None, 1, d), lambda i: (i // per_b, 0, 0)),
                  pl.BlockSpec((tm, TOP_K), lambda i: (i, 0)),
                  pl.BlockSpec((1, d), lambda i: (0, 0)),
                  pl.BlockSpec(memory_space=pl.ANY)],
        out_specs=pl.BlockSpec((tm, d), lambda i: (i, 0)),
        out_shape=jax.ShapeDtypeStruct((t, d), F32),
        scratch_shapes=[pltpu.VMEM((2, TOP_K * tm * ROW_TILE, LANES), U32), pltpu.SemaphoreType.DMA((2,))],
        compiler_params=_cparams(1),
        name="combine",
    )(dest, dest, base, g2, w, fg, ys)


def _dispatch_tables(top_e, rank, counts, t):
    n_blocks = TOP_K * t // MOE_BLOCK + N_EXPERTS
    padded = (counts + MOE_BLOCK - 1) // MOE_BLOCK * MOE_BLOCK
    pends = jnp.cumsum(padded)
    pstarts = pends - padded
    experts = jnp.arange(N_EXPERTS, dtype=I32)
    pstart_of = jnp.sum(jnp.where(top_e[..., None] == experts, pstarts, 0), axis=-1)
    dest = (pstart_of + rank) * ROW_TILE
    starts = jnp.arange(n_blocks, dtype=I32) * MOE_BLOCK
    block_e = jnp.sum((pends[None, :] <= starts[:, None]).astype(I32), axis=1)
    block_e = jnp.minimum(block_e, jnp.max(jnp.where(starts < pends[-1], block_e, 0)))
    n_used = (pends[-1:] // MOE_BLOCK).astype(I32)
    seg_end = jnp.sum(jnp.where(block_e[:, None] == experts, pstarts + counts, 0), axis=-1)
    n_valid = jnp.clip(seg_end - starts, 0, MOE_BLOCK).astype(I32)
    return block_e, n_used, n_valid, pends.astype(I32), dest, n_blocks * MOE_BLOCK


def _layout_inproj(w_in, b_in):
    o = np.cumsum([0, 2 * CONV_DIM, FOX_DIM, FOX_DIM, FOX_DIM, FOX_HEADS, GLA_KDIM, GLA_KDIM, GLA_VDIM, GLA_VDIM,
                   GLA_RANK, 3 * D_MODEL])
    seg = lambda a, i: a[..., int(o[i]):int(o[i + 1])]

    def main(a):
        return jnp.concatenate([seg(a, 10), seg(a, 0), seg(a, 1) * (FOX_HEAD_DIM ** -0.5 * LOG2E), seg(a, 2), seg(a, 3),
                                seg(a, 5) * GLA_DK ** -0.5, seg(a, 6), seg(a, 7), seg(a, 8)], axis=-1)

    def small(a):
        pad = jnp.zeros(a.shape[:-1] + (LANES - FOX_HEADS - GLA_RANK,), a.dtype)
        return jnp.concatenate([seg(a, 4), seg(a, 9), pad], axis=-1)

    return (main(w_in).astype(BF16), main(b_in)[..., None, :], small(w_in).astype(BF16),
            small(b_in)[..., None, :])


def kernel(x, c, norm1_g, ada_w, ada_b, w_in, b_in, conv_w, conv_b, conv_ln_g, conv_ln_b, gla_wa, gla_ba,
           gla_norm_g, w_branch_a, w_branch_b, w_branch_c, w_out, norm2_g, w_router, e_bias, w1, w3, w2,
           ws1, ws3, ws2, final_g):
    bsz, seq, d = x.shape
    t = bsz * seq
    depth = ada_w.shape[0]
    mod = _ada_mod(c, ada_w, ada_b).reshape(depth, bsz, 6, 1, d)
    x2 = x.reshape(t, d)
    row = lambda a: a.reshape(1, -1)
    w, b, ws, bs = _layout_inproj(w_in, b_in)
    for l in range(depth):
        sh1, sc1, g1, sh2, sc2, g2 = (mod[l, :, i] for i in range(6))
        z, zs = _inproj(l, x2, row(norm1_g[l]), sc1, sh1, w, b, ws, bs, seq)
        wa_pad = jnp.zeros((LANES, GLA_KDIM), F32).at[ZS_A:ZS_A + GLA_RANK].set(gla_wa[l])
        aq, ak, bc = _prep(zs, wa_pad, row(gla_ba[l]), seq)
        ya = _conv(z, conv_w[l].reshape(CONV_WIDTH, CONV_DIM), row(conv_b[l]), row(conv_ln_g[l]),
                   row(conv_ln_b[l]), seq)
        yb = _fox(z, aq, ak, seq)
        yc = _gla(z, bc, row(gla_norm_g[l]), seq)
        x2 = _merge(x2, g1, ya, yb, yc, z, w_branch_a[l].astype(BF16), w_branch_b[l].astype(BF16),
                    w_branch_c[l].astype(BF16), w_out[l].astype(BF16), seq)

        wr_t = w_router[l].T
        wrh = wr_t.astype(BF16)
        wrl = (wr_t - wrh.astype(F32)).astype(BF16)
        top_e, top_w, rank, counts = _route(x2, row(norm2_g[l]), sc2, sh2, wrh, wrl,
                                            e_bias[l].reshape(N_EXPERTS, 1), seq)
        block_e, n_used, n_valid, pends, dest, n_rows = _dispatch_tables(top_e, rank, counts[:, 0], t)
        base, xs = _dispatch(x2, row(norm2_g[l]), sc2, sh2, g2, ws1[l].astype(BF16), ws3[l].astype(BF16),
                             ws2[l].astype(BF16), pends, dest, n_rows, seq)
        ys = _experts(l, block_e, n_used, n_valid, xs, w1, w3, w2)
        x2 = _combine(base, g2, top_w.T, ys, dest, row(final_g), seq, final=(l == depth - 1))
    return x2.reshape(bsz, seq, d)
```
